```python
import jax, jax.numpy as jnp
from jax import lax
import numpy as np

D_MODEL = 1024
BATCH = 16
SEQ = 2048
DEPTH = 1
DEC_BATCH = 16
DEC_SEQ = 64
PAST_LEN = 4096

CHUNK = 64
RET_HEADS = 4
RET_DK = 128
RET_DV = 128
RET_WIDTH = RET_HEADS * RET_DV
SB_HEADS = 8
SB_HEAD_DIM = 64
SB_WIDTH = SB_HEADS * SB_HEAD_DIM
MIX_WIDTH = RET_WIDTH + SB_WIDTH
IN_WIDTH = 2 * RET_HEADS * RET_DK + 2 * RET_WIDTH + 3 * SB_WIDTH
SB_BLOCK = 128
ROPE_BASE = 10000.0
N_GROUPS = 4
EXPERTS_PER_GROUP = 4
N_EXPERTS = N_GROUPS * EXPERTS_PER_GROUP
TOP_K_IN_GROUP = 2
D_EXPERT = 512
DEEPNORM_ALPHA = (2.0 * DEPTH) ** 0.25
DEEPNORM_BETA = (8.0 * DEPTH) ** -0.25
LN_EPS = 1e-5

kernel_name = 'hymba_retention_stickbreaking_hmoe_stream_step'


def layer_norm_plain(x):
    xf = x.astype(jnp.float32)
    mu = jnp.mean(xf, axis=-1, keepdims=True)
    var = jnp.mean(jnp.square(xf - mu), axis=-1, keepdims=True)
    return (xf - mu) * lax.rsqrt(var + LN_EPS)


def layer_norm(x, g, b):
    return (layer_norm_plain(x) * g.astype(jnp.float32) + b.astype(jnp.float32)).astype(x.dtype)


def rms_norm_plain(x):
    xf = x.astype(jnp.float32)
    return xf * lax.rsqrt(jnp.mean(jnp.square(xf), axis=-1, keepdims=True) + LN_EPS)


def adaln_modulation(c, w, b):
    m = jnp.einsum('bd,de->be', jax.nn.silu(c), w) + b
    shift, scale, gate = jnp.split(m, 3, axis=-1)
    return shift[:, None, :], scale[:, None, :], gate[:, None, :]


def modulate(x, shift, scale):
    return (layer_norm_plain(x) * (1.0 + scale.astype(jnp.float32)) + shift.astype(jnp.float32)).astype(x.dtype)


def split_heads(t, n_heads):
    B, T, _ = t.shape
    return t.reshape(B, T, n_heads, -1).transpose(0, 2, 1, 3)


def merge_heads(t):
    B, H, T, d = t.shape
    return t.transpose(0, 2, 1, 3).reshape(B, T, H * d)


def rotary(x, pos):
    half = x.shape[-1] // 2
    inv = ROPE_BASE ** (-jnp.arange(half, dtype=jnp.float32) / half)
    ang = pos.astype(jnp.float32)[:, None] * inv[None, :]
    cos, sin = jnp.cos(ang), jnp.sin(ang)
    x1 = x[..., :half].astype(jnp.float32)
    x2 = x[..., half:].astype(jnp.float32)
    return jnp.concatenate([x1 * cos - x2 * sin, x2 * cos + x1 * sin], axis=-1).astype(x.dtype)


def retention(q, k, v, state0):
    B, H, T, dk = q.shape
    dv = v.shape[-1]
    C = min(CHUNK, T)
    N = T // C
    log_gamma = jnp.log1p(-jnp.exp2(-5.0 - jnp.arange(H, dtype=jnp.float32)))
    idx = jnp.arange(C, dtype=jnp.float32)
    rel = idx[:, None] - idx[None, :]
    decay = jnp.where(rel >= 0, jnp.exp(log_gamma[:, None, None] * jnp.maximum(rel, 0.0)), 0.0)
    q_decay = jnp.exp(log_gamma[:, None] * (idx + 1.0))[None, :, :, None]
    k_decay = jnp.exp(log_gamma[:, None] * (C - 1.0 - idx))[None, :, :, None]
    chunk_decay = jnp.exp(log_gamma * C)[None, :, None, None]
    qc = q.astype(jnp.float32).reshape(B, H, N, C, dk)
    kc = (k.astype(jnp.float32) * dk ** -0.5).reshape(B, H, N, C, dk)
    vc = v.astype(jnp.float32).reshape(B, H, N, C, dv)
    scores = jnp.einsum('bhncd,bhnmd->bhncm', qc, kc) * decay[None, :, None]
    intra = jnp.einsum('bhncm,bhnme->bhnce', scores, vc)

    def step(state, xs):
        qn, kn, vn = xs
        inter = jnp.einsum('bhcd,bhde->bhce', qn * q_decay, state)
        state = state * chunk_decay + jnp.einsum('bhcd,bhce->bhde', kn * k_decay, vn)
        return state, inter

    state, inter = lax.scan(step, state0.astype(jnp.float32),
                            (jnp.moveaxis(qc, 2, 0), jnp.moveaxis(kc, 2, 0), jnp.moveaxis(vc, 2, 0)))
    out = intra + jnp.moveaxis(inter, 0, 2)
    return out.reshape(B, H, T, dv), state


def stick_breaking(q, k, v, q_offset):
    B, H, Tq, d = q.shape
    Tk = k.shape[2]
    blk = min(SB_BLOCK, Tq)
    nb = Tq // blk
    qb = jnp.moveaxis(q.reshape(B, H, nb, blk, d), 2, 0)
    kpos = jnp.arange(Tk)

    def one_block(args):
        qi, i = args
        tpos = q_offset + i * blk + jnp.arange(blk)
        z = jnp.einsum('bhqd,bhkd->bhqk', qi, k).astype(jnp.float32) * (d ** -0.5)
        mask = (kpos[None, :] < tpos[:, None])[None, None]
        log_beta = jax.nn.log_sigmoid(z)
        log_one_minus = jnp.where(mask, jax.nn.log_sigmoid(-z), 0.0)
        later = lax.cumsum(log_one_minus, axis=3, reverse=True) - log_one_minus
        weights = jnp.where(mask, jnp.exp(log_beta + later), 0.0)
        return jnp.einsum('bhqk,bhkd->bhqd', weights.astype(v.dtype), v)

    out = lax.map(one_block, (qb, jnp.arange(nb)))
    return jnp.moveaxis(out, 0, 2).reshape(B, H, Tq, d)


def hybrid_mixer(h, pos0, ret_state0, k_past, v_past, w_in, ret_norm_g, sb_norm_g, w_out):
    T = h.shape[1]
    proj = jnp.einsum('btd,de->bte', h, w_in)
    sizes = (RET_HEADS * RET_DK, RET_HEADS * RET_DK, RET_WIDTH, RET_WIDTH, SB_WIDTH, SB_WIDTH, SB_WIDTH)
    rq, rk, rv, rg, sq, sk, sv = jnp.split(proj, np.cumsum(sizes)[:-1].tolist(), axis=-1)
    pos = pos0 + jnp.arange(T)
    ret_o, ret_state = retention(rotary(split_heads(rq, RET_HEADS), pos), rotary(split_heads(rk, RET_HEADS), pos),
                                 split_heads(rv, RET_HEADS), ret_state0)
    ret_o = merge_heads(layer_norm_plain(ret_o)) * ret_norm_g.astype(jnp.float32) * jax.nn.silu(rg.astype(jnp.float32))
    sq = split_heads(sq, SB_HEADS)
    sk = split_heads(sk, SB_HEADS)
    sv = split_heads(sv, SB_HEADS)
    if k_past is None:
        k_all, v_all = sk, sv
    else:
        k_all = jnp.concatenate([k_past.astype(sk.dtype), sk], axis=2)
        v_all = jnp.concatenate([v_past.astype(sv.dtype), sv], axis=2)
    sb_o = stick_breaking(sq, k_all, v_all, k_all.shape[2] - T)
    sb_o = merge_heads(rms_norm_plain(sb_o)) * sb_norm_g.astype(jnp.float32)
    mixed = jnp.concatenate([ret_o, sb_o], axis=-1).astype(h.dtype)
    return jnp.einsum('bte,ed->btd', mixed, w_out), sk, sv, ret_state


def hier_moe(h, w_group, b_group, w_router, b_router, w_e_gate, w_e_up, w_e_down):
    B, T, D = h.shape
    hf = h.reshape(B * T, D)
    g_logits = (hf @ w_group + b_group).astype(jnp.float32)
    g_prob = jax.nn.softmax(g_logits, axis=-1)
    g_idx = jnp.argmax(g_logits, axis=-1)
    g_p = jnp.take_along_axis(g_prob, g_idx[:, None], axis=-1)
    e_logits = (hf @ w_router + b_router).astype(jnp.float32).reshape(-1, N_GROUPS, EXPERTS_PER_GROUP)
    e_in = jnp.take_along_axis(e_logits, g_idx[:, None, None], axis=1)[:, 0]
    top_v, top_i = lax.top_k(e_in, TOP_K_IN_GROUP)
    top_w = jax.nn.softmax(top_v, axis=-1) * g_p
    expert_id = g_idx[:, None] * EXPERTS_PER_GROUP + top_i
    combine = jnp.sum(jax.nn.one_hot(expert_id, N_EXPERTS, dtype=jnp.float32) * top_w[..., None], axis=1)
    combine = combine.astype(h.dtype)
    y = jnp.zeros_like(hf)
    for e in range(N_EXPERTS):
        a = jax.nn.silu(hf @ w_e_gate[e]) * (hf @ w_e_up[e])
        y = y + combine[:, e:e + 1] * (a @ w_e_down[e])
    return y.reshape(B, T, D)


def trunk(x, c, pos0, ret_state0, past_k, past_v, weights):
    (w_in, w_out, ret_norm_g, sb_norm_g, w_ada_mix, b_ada_mix, ln_mix_g, ln_mix_b,
     w_ada_ffn, b_ada_ffn, ln_ffn_g, ln_ffn_b, w_group, b_group, w_router, b_router,
     w_e_gate, w_e_up, w_e_down) = weights
    new_k, new_v, new_r = [], [], []
    for l in range(DEPTH):
        shift, scale, gate = adaln_modulation(c, w_ada_mix[l], b_ada_mix[l])
        h = modulate(x, shift, scale)
        kp = None if past_k is None else past_k[l]
        vp = None if past_v is None else past_v[l]
        mix, k_l, v_l, r_l = hybrid_mixer(h, pos0, ret_state0[l], kp, vp, w_in[l], ret_norm_g[l], sb_norm_g[l], w_out[l])
        x = layer_norm(DEEPNORM_ALPHA * x + gate * mix, ln_mix_g[l], ln_mix_b[l])
        shift, scale, gate = adaln_modulation(c, w_ada_ffn[l], b_ada_ffn[l])
        h = modulate(x, shift, scale)
        ffn = hier_moe(h, w_group[l], b_group[l], w_router[l], b_router[l], w_e_gate[l], w_e_up[l], w_e_down[l])
        x = layer_norm(DEEPNORM_ALPHA * x + gate * ffn, ln_ffn_g[l], ln_ffn_b[l])
        new_k.append(k_l)
        new_v.append(v_l)
        new_r.append(r_l)
    return x, jnp.stack(new_k, 0), jnp.stack(new_v, 0), jnp.stack(new_r, 0)


def setup_inputs(seed: int = 0) -> dict:
    key = jax.random.key(seed)
    ks = jax.random.split(key, 32)

    def nrm(k, shape, scale):
        return jax.random.normal(k, shape, jnp.float32) * scale

    col_scale = jnp.concatenate([
        jnp.ones((2 * RET_HEADS * RET_DK,), jnp.float32),
        jnp.full((RET_WIDTH,), DEEPNORM_BETA, jnp.float32),
        jnp.ones((RET_WIDTH,), jnp.float32),
        jnp.ones((2 * SB_WIDTH,), jnp.float32),
        jnp.full((SB_WIDTH,), DEEPNORM_BETA, jnp.float32),
    ])
    return {
        'x_prompt': nrm(ks[0], (BATCH, SEQ, D_MODEL), 1.0),
        'x_sample': nrm(ks[1], (DEC_BATCH, DEC_SEQ, D_MODEL), 1.0),
        'cache_sb_k': nrm(ks[2], (DEPTH, DEC_BATCH, SB_HEADS, PAST_LEN, SB_HEAD_DIM), 1.0),
        'cache_sb_v': nrm(ks[3], (DEPTH, DEC_BATCH, SB_HEADS, PAST_LEN, SB_HEAD_DIM), DEEPNORM_BETA),
        'state_ret': nrm(ks[4], (DEPTH, DEC_BATCH, RET_HEADS, RET_DK, RET_DV), 1.0),
        'c_prompt': nrm(ks[5], (BATCH, D_MODEL), 1.0),
        'c_sample': nrm(ks[6], (DEC_BATCH, D_MODEL), 1.0),
        'w_in': nrm(ks[7], (DEPTH, D_MODEL, IN_WIDTH), D_MODEL ** -0.5) * col_scale,
        'w_out': nrm(ks[8], (DEPTH, MIX_WIDTH, D_MODEL), MIX_WIDTH ** -0.5 * DEEPNORM_BETA),
        'ret_norm_g': 1.0 + nrm(ks[9], (DEPTH, RET_WIDTH), 0.02),
        'sb_norm_g': 1.0 + nrm(ks[10], (DEPTH, SB_WIDTH), 0.02),
        'w_ada_mix': nrm(ks[11], (DEPTH, D_MODEL, 3 * D_MODEL), 0.5 * D_MODEL ** -0.5),
        'b_ada_mix': nrm(ks[12], (DEPTH, 3 * D_MODEL), 0.02),
        'ln_mix_g': 1.0 + nrm(ks[13], (DEPTH, D_MODEL), 0.02),
        'ln_mix_b': nrm(ks[14], (DEPTH, D_MODEL), 0.02),
        'w_ada_ffn': nrm(ks[15], (DEPTH, D_MODEL, 3 * D_MODEL), 0.5 * D_MODEL ** -0.5),
        'b_ada_ffn': nrm(ks[16], (DEPTH, 3 * D_MODEL), 0.02),
        'ln_ffn_g': 1.0 + nrm(ks[17], (DEPTH, D_MODEL), 0.02),
        'ln_ffn_b': nrm(ks[18], (DEPTH, D_MODEL), 0.02),
        'w_group': nrm(ks[19], (DEPTH, D_MODEL, N_GROUPS), D_MODEL ** -0.5),
        'b_group': nrm(ks[20], (DEPTH, N_GROUPS), 0.01),
        'w_router': nrm(ks[21], (DEPTH, D_MODEL, N_EXPERTS), D_MODEL ** -0.5),
        'b_router': nrm(ks[22], (DEPTH, N_EXPERTS), 0.01),
        'w_e_gate': nrm(ks[23], (DEPTH, N_EXPERTS, D_MODEL, D_EXPERT), D_MODEL ** -0.5),
        'w_e_up': nrm(ks[24], (DEPTH, N_EXPERTS, D_MODEL, D_EXPERT), D_MODEL ** -0.5),
        'w_e_down': nrm(ks[25], (DEPTH, N_EXPERTS, D_EXPERT, D_MODEL), D_EXPERT ** -0.5 * DEEPNORM_BETA),
    }


def reference(x_prompt, x_sample, cache_sb_k, cache_sb_v, state_ret, c_prompt, c_sample,
              w_in, w_out, ret_norm_g, sb_norm_g, w_ada_mix, b_ada_mix, ln_mix_g, ln_mix_b,
              w_ada_ffn, b_ada_ffn, ln_ffn_g, ln_ffn_b, w_group, b_group, w_router, b_router,
              w_e_gate, w_e_up, w_e_down):
    weights = (w_in, w_out, ret_norm_g, sb_norm_g, w_ada_mix, b_ada_mix, ln_mix_g, ln_mix_b,
               w_ada_ffn, b_ada_ffn, ln_ffn_g, ln_ffn_b, w_group, b_group, w_router, b_router,
               w_e_gate, w_e_up, w_e_down)
    ret_zero = jnp.zeros((DEPTH, x_prompt.shape[0], RET_HEADS, RET_DK, RET_DV), jnp.float32)
    y_prompt, sb_k_prompt, sb_v_prompt, ret_state_prompt = trunk(
        x_prompt, c_prompt, 0, ret_zero, None, None, weights)
    y_sample, sb_k_sample, sb_v_sample, ret_state_sample = trunk(
        x_sample, c_sample, cache_sb_k.shape[3], state_ret, cache_sb_k, cache_sb_v, weights)
    return (y_prompt, y_sample, sb_k_prompt, sb_v_prompt, ret_state_prompt, sb_k_sample, sb_v_sample, ret_state_sample)
```

```python
import functools
import math

import numpy as np
import jax
import jax.numpy as jnp
from jax import lax
from jax.experimental import pallas as pl
from jax.experimental.pallas import tpu as pltpu

D_MODEL = 1024
RET_HEADS = 4
RET_DK = 128
RET_DV = 128
RET_WIDTH = RET_HEADS * RET_DV
SB_HEADS = 8
SB_HEAD_DIM = 64
SB_WIDTH = SB_HEADS * SB_HEAD_DIM
IN_WIDTH = 2 * RET_HEADS * RET_DK + 2 * RET_WIDTH + 3 * SB_WIDTH
ROPE_BASE = 10000.0
N_GROUPS = 4
EXPERTS_PER_GROUP = 4
N_EXPERTS = N_GROUPS * EXPERTS_PER_GROUP
D_EXPERT = 512
DEPTH = 1
DEEPNORM_ALPHA = (2.0 * DEPTH) ** 0.25
LN_EPS = 1e-5

LANES = 128
VMEM_LIMIT = 48 * 1024 * 1024

F32 = jnp.float32
BF16 = jnp.bfloat16


def _cparams(sem):
    return pltpu.CompilerParams(dimension_semantics=sem, vmem_limit_bytes=VMEM_LIMIT)


def _dot(a, b):
    return jnp.dot(a, b, preferred_element_type=F32)


def _dot_nt(a, b):
    return lax.dot_general(a, b, (((1,), (1,)), ((), ())), preferred_element_type=F32)


def _dot_tn(a, b):
    return lax.dot_general(a, b, (((0,), (0,)), ((), ())), preferred_element_type=F32)


def _split_dot(a, w_hi, w_lo):
    a_hi = a.astype(BF16)
    a_lo = (a - a_hi.astype(F32)).astype(BF16)
    return _dot(a_hi, w_hi) + (_dot(a_hi, w_lo) + _dot(a_lo, w_hi))


def _ln_plain(x):
    mu = jnp.mean(x, axis=-1, keepdims=True)
    xc = x - mu
    var = jnp.mean(xc * xc, axis=-1, keepdims=True)
    return xc * lax.rsqrt(var + LN_EPS)


def _silu(x):
    return x * (1.0 / (1.0 + jnp.exp(-x)))


def _ada_kernel(c_ref, w_ref, b_ref, o_ref):
    c = c_ref[...]
    w = w_ref[...]
    w_hi = w.astype(BF16)
    w_lo = (w - w_hi.astype(F32)).astype(BF16)
    o_ref[...] = _split_dot(_silu(c), w_hi, w_lo) + b_ref[...]


def _ada(c, w, b):
    r = c.shape[0]
    tn = 768
    return pl.pallas_call(
        _ada_kernel,
        grid=(3 * D_MODEL // tn,),
        in_specs=[pl.BlockSpec((r, D_MODEL), lambda j: (0, 0)),
                  pl.BlockSpec((D_MODEL, tn), lambda j: (0, j)),
                  pl.BlockSpec((1, tn), lambda j: (0, j))],
        out_specs=pl.BlockSpec((r, tn), lambda j: (0, j)),
        out_shape=jax.ShapeDtypeStruct((r, 3 * D_MODEL), F32),
        compiler_params=_cparams(("arbitrary",)),
        name="ada_mod",
    )(c, w, b.reshape(1, -1))


def _in_proj_kernel(x_ref, mod_ref, w_ref, tab_ref, ret_ref, q_ref, k_ref, v_ref):
    m = mod_ref[0]
    h = _ln_plain(x_ref[0]) * (1.0 + m[:, D_MODEL:2 * D_MODEL]) + m[:, :D_MODEL]
    h = h.astype(BF16)
    tab = tab_ref[...]
    for c in range(4):
        p = _dot(h, w_ref[:, c * RET_WIDTH:(c + 1) * RET_WIDTH])
        if c < 2:
            cs = tab[:, (2 * c) * LANES:(2 * c + 1) * LANES]
            sn = tab[:, (2 * c + 1) * LANES:(2 * c + 2) * LANES]
            for hh in range(RET_HEADS):
                ph = p[:, hh * RET_DK:(hh + 1) * RET_DK]
                ret_ref[0, :, c * RET_WIDTH + hh * RET_DK:c * RET_WIDTH + (hh + 1) * RET_DK] = (
                    ph * cs + pltpu.roll(ph, RET_DK // 2, 1) * sn)
        else:
            ret_ref[0, :, c * RET_WIDTH:(c + 1) * RET_WIDTH] = p
    base = 4 * RET_WIDTH
    for c, ref in enumerate((q_ref, k_ref, v_ref)):
        p = _dot(h, w_ref[:, base + c * SB_WIDTH:base + (c + 1) * SB_WIDTH])
        if c == 0:
            p = p * (SB_HEAD_DIM ** -0.5)
        for hh in range(SB_HEADS):
            ref[0, hh] = p[:, hh * SB_HEAD_DIM:(hh + 1) * SB_HEAD_DIM].astype(ref.dtype)


def _rope_table(pos0, t):
    half = RET_DK // 2
    inv = ROPE_BASE ** (-np.arange(half, dtype=np.float64) / half)
    ang = (pos0 + np.arange(t, dtype=np.float64))[:, None] * inv[None, :]
    cos, sin = np.cos(ang), np.sin(ang)
    cs = np.concatenate([cos, cos], axis=1)
    sn = np.concatenate([-sin, sin], axis=1)
    ks = RET_DK ** -0.5
    return jnp.asarray(np.concatenate([cs, sn, cs * ks, sn * ks], axis=1), dtype=F32)


def _in_proj(x, mod, w_in_bf, pos0, tm):
    b, t, _ = x.shape
    tab = _rope_table(pos0, t)
    hs = jax.ShapeDtypeStruct((b, SB_HEADS, t, SB_HEAD_DIM), F32)
    head_spec = pl.BlockSpec((1, SB_HEADS, tm, SB_HEAD_DIM), lambda i, j: (i, 0, j, 0))
    return pl.pallas_call(
        _in_proj_kernel,
        grid=(b, t // tm),
        in_specs=[pl.BlockSpec((1, tm, D_MODEL), lambda i, j: (i, j, 0)),
                  pl.BlockSpec((1, 1, 3 * D_MODEL), lambda i, j: (i, 0, 0)),
                  pl.BlockSpec((D_MODEL, IN_WIDTH), lambda i, j: (0, 0)),
                  pl.BlockSpec((tm, 4 * LANES), lambda i, j: (j, 0))],
        out_specs=[pl.BlockSpec((1, tm, 4 * RET_WIDTH), lambda i, j: (i, j, 0)),
                   head_spec, head_spec, head_spec],
        out_shape=[jax.ShapeDtypeStruct((b, t, 4 * RET_WIDTH), F32),
                   jax.ShapeDtypeStruct((b, SB_HEADS, t, SB_HEAD_DIM), BF16), hs, hs],
        compiler_params=_cparams(("parallel", "arbitrary")),
        name="in_proj",
    )(x, mod.reshape(b, 1, -1), w_in_bf, tab)


def _ret_kernel(q_ref, k_ref, v_ref, g_ref, s0_ref, dec_ref, qd_ref, kd_ref, ng_ref,
                o_ref, so_ref, st_ref, *, chunk_decay):
    j = pl.program_id(1)

    @pl.when(j == 0)
    def _():
        st_ref[...] = s0_ref[0]

    for hh in range(RET_HEADS):
        sl = slice(hh * RET_DK, (hh + 1) * RET_DK)
        q = q_ref[0, :, sl]
        k = k_ref[0, :, sl]
        v = v_ref[0, :, sl]
        vb = v.astype(BF16)
        st = st_ref[hh]
        scores = _dot_nt(q.astype(BF16), k.astype(BF16)) * dec_ref[hh]
        o = _dot(scores.astype(BF16), vb) + _dot((q * qd_ref[hh]).astype(BF16), st.astype(BF16))
        st_ref[hh] = st * chunk_decay[hh] + _dot_tn((k * kd_ref[hh]).astype(BF16), vb)
        o = _ln_plain(o) * ng_ref[:, sl] * _silu(g_ref[0, :, sl])
        o_ref[0, :, sl] = o.astype(o_ref.dtype)

    @pl.when(j == pl.num_programs(1) - 1)
    def _():
        so_ref[0] = st_ref[...]


def _retention(ret_in, state0, norm_g, chunk):
    b, t, _ = ret_in.shape
    lg = np.log1p(-np.exp2(-5.0 - np.arange(RET_HEADS, dtype=np.float64)))
    idx = np.arange(chunk, dtype=np.float64)
    rel = idx[:, None] - idx[None, :]
    dec = np.where(rel >= 0, np.exp(lg[:, None, None] * np.maximum(rel, 0.0)), 0.0)
    qd = np.broadcast_to(np.exp(lg[:, None] * (idx + 1.0))[:, :, None], (RET_HEADS, chunk, RET_DK))
    kd = np.broadcast_to(np.exp(lg[:, None] * (chunk - 1.0 - idx))[:, :, None], (RET_HEADS, chunk, RET_DK))
    chunk_decay = tuple(float(v) for v in np.exp(lg * chunk))

    def col(c):
        return pl.BlockSpec((1, chunk, RET_WIDTH), lambda i, j, c=c: (i, j, c))

    const3 = lambda shape: pl.BlockSpec(shape, lambda i, j: (0, 0, 0))
    state_spec = pl.BlockSpec((1, RET_HEADS, RET_DK, RET_DV), lambda i, j: (i, 0, 0, 0))
    return pl.pallas_call(
        functools.partial(_ret_kernel, chunk_decay=chunk_decay),
        grid=(b, t // chunk),
        in_specs=[col(0), col(1), col(2), col(3), state_spec,
                  const3((RET_HEADS, chunk, chunk)), const3((RET_HEADS, chunk, RET_DK)),
                  const3((RET_HEADS, chunk, RET_DK)),
                  pl.BlockSpec((1, RET_WIDTH), lambda i, j: (0, 0))],
        out_specs=[pl.BlockSpec((1, chunk, RET_WIDTH), lambda i, j: (i, j, 0)), state_spec],
        out_shape=[jax.ShapeDtypeStruct((b, t, RET_WIDTH), BF16),
                   jax.ShapeDtypeStruct((b, RET_HEADS, RET_DK, RET_DV), F32)],
        scratch_shapes=[pltpu.VMEM((RET_HEADS, RET_DK, RET_DV), F32)],
        compiler_params=_cparams(("parallel", "arbitrary")),
        name="retention",
    )(ret_in, ret_in, ret_in, ret_in, state0,
      jnp.asarray(dec, F32), jnp.asarray(qd, F32), jnp.asarray(kd, F32), norm_g.reshape(1, -1))


def _sb_step(q, k, v, acc_ref, car_ref, masked):
    tq, kb = q.shape[0], k.shape[0]
    z = _dot_nt(q, k.astype(BF16))
    l = jnp.log(1.0 + jnp.exp(-jnp.abs(z)))
    lb = jnp.minimum(z, 0.0) - l
    lom = lb - z
    if masked:
        valid = (lax.broadcasted_iota(jnp.int32, (tq, kb), 1)
                 < lax.broadcasted_iota(jnp.int32, (tq, kb), 0))
        lom = jnp.where(valid, lom, 0.0)
    tri = (lax.broadcasted_iota(jnp.int32, (kb, kb), 0)
           > lax.broadcasted_iota(jnp.int32, (kb, kb), 1)).astype(BF16)
    hi = lom.astype(BF16)
    lo = (lom - hi.astype(F32)).astype(BF16)
    later = _dot(hi, tri) + _dot(lo, tri)
    car = car_ref[...]
    w = jnp.exp(lb + later + car)
    if masked:
        w = jnp.where(valid, w, 0.0)
    acc_ref[...] += _dot(w.astype(BF16), v.astype(BF16))
    car_ref[...] = car + later[:, 0:1] + lom[:, 0:1]


def _sb_finish(acc_ref, g):
    o = acc_ref[...]
    return o * lax.rsqrt(jnp.mean(o * o, axis=-1, keepdims=True) + LN_EPS) * g


def _sb_prompt_kernel(q_ref, k_ref, v_ref, g_ref, o_ref, acc_ref, car_ref, *, tq):
    i = pl.program_id(2)
    hp = pl.program_id(1)
    outs = []
    for hh in range(2):
        q = q_ref[0, hh]
        acc_ref[...] = jnp.zeros_like(acc_ref)
        car_ref[...] = jnp.zeros_like(car_ref)
        start = pl.multiple_of(i * tq, tq)
        _sb_step(q, k_ref[0, hh, pl.ds(start, tq), :], v_ref[0, hh, pl.ds(start, tq), :],
                 acc_ref, car_ref, True)

        def body(t, _):
            s = pl.multiple_of((i - 1 - t) * tq, tq)
            _sb_step(q, k_ref[0, hh, pl.ds(s, tq), :], v_ref[0, hh, pl.ds(s, tq), :],
                     acc_ref, car_ref, False)
            return 0

        lax.fori_loop(0, i, body, 0)
        outs.append(_sb_finish(acc_ref, g_ref[0, :, hh * SB_HEAD_DIM:(hh + 1) * SB_HEAD_DIM]))
    del hp
    o_ref[0] = jnp.concatenate(outs, axis=-1).astype(o_ref.dtype)


def _sb_prompt(q, k, v, norm_g, tq):
    b, _, t, d = q.shape
    kv_spec = pl.BlockSpec((1, 2, t, d), lambda i, h, j: (i, h, 0, 0))
    return pl.pallas_call(
        functools.partial(_sb_prompt_kernel, tq=tq),
        grid=(b, SB_HEADS // 2, t // tq),
        in_specs=[pl.BlockSpec((1, 2, tq, d), lambda i, h, j: (i, h, j, 0)), kv_spec, kv_spec,
                  pl.BlockSpec((1, 1, 2 * d), lambda i, h, j: (h, 0, 0))],
        out_specs=pl.BlockSpec((1, tq, 2 * d), lambda i, h, j: (i, j, h)),
        out_shape=jax.ShapeDtypeStruct((b, t, SB_WIDTH), BF16),
        scratch_shapes=[pltpu.VMEM((tq, d), F32), pltpu.VMEM((tq, 1), F32)],
        compiler_params=_cparams(("parallel", "parallel", "arbitrary")),
        name="sb_prompt",
    )(q, k, v, norm_g.reshape(SB_HEADS // 2, 1, 2 * d))


def _sb_sample_kernel(q_ref, k_ref, v_ref, kp_ref, vp_ref, g_ref, o_ref, acc_ref, car_ref, *, kb):
    n_past = kp_ref.shape[2] // kb
    outs = []
    for hh in range(2):
        q = q_ref[0, hh]
        acc_ref[...] = jnp.zeros_like(acc_ref)
        car_ref[...] = jnp.zeros_like(car_ref)
        _sb_step(q, k_ref[0, hh], v_ref[0, hh], acc_ref, car_ref, True)

        def body(t, _):
            s = pl.multiple_of((n_past - 1 - t) * kb, kb)
            _sb_step(q, kp_ref[0, hh, pl.ds(s, kb), :], vp_ref[0, hh, pl.ds(s, kb), :],
                     acc_ref, car_ref, False)
            return 0

        lax.fori_loop(0, n_past, body, 0)
        outs.append(_sb_finish(acc_ref, g_ref[0, :, hh * SB_HEAD_DIM:(hh + 1) * SB_HEAD_DIM]))
    o_ref[0] = jnp.concatenate(outs, axis=-1).astype(o_ref.dtype)


def _sb_sample(q, k, v, k_past, v_past, norm_g, kb):
    b, _, t, d = q.shape
    p = k_past.shape[2]
    new_spec = pl.BlockSpec((1, 2, t, d), lambda i, h: (i, h, 0, 0))
    past_spec = pl.BlockSpec((1, 2, p, d), lambda i, h: (i, h, 0, 0))
    return pl.pallas_call(
        functools.partial(_sb_sample_kernel, kb=kb),
        grid=(b, SB_HEADS // 2),
        in_specs=[new_spec, new_spec, new_spec, past_spec, past_spec,
                  pl.BlockSpec((1, 1, 2 * d), lambda i, h: (h, 0, 0))],
        out_specs=pl.BlockSpec((1, t, 2 * d), lambda i, h: (i, 0, h)),
        out_shape=jax.ShapeDtypeStruct((b, t, SB_WIDTH), BF16),
        scratch_shapes=[pltpu.VMEM((t, d), F32), pltpu.VMEM((t, 1), F32)],
        compiler_params=_cparams(("parallel", "arbitrary")),
        name="sb_sample",
    )(q, k, v, k_past, v_past, norm_g.reshape(SB_HEADS // 2, 1, 2 * d))


def _route(logits):
    lane = lax.broadcasted_iota(jnp.int32, logits.shape, 1)
    neg = -jnp.inf
    big = jnp.int32(2 * LANES)
    gl = jnp.where(lane < N_GROUPS, logits, neg)
    gmax = jnp.max(gl, axis=-1, keepdims=True)
    g_idx = jnp.min(jnp.where(gl == gmax, lane, big), axis=-1, keepdims=True)
    g_p = 1.0 / jnp.sum(jnp.exp(gl - gmax), axis=-1, keepdims=True)
    lo = N_GROUPS + g_idx * EXPERTS_PER_GROUP
    el = jnp.where((lane >= lo) & (lane < lo + EXPERTS_PER_GROUP), logits, neg)
    v1 = jnp.max(el, axis=-1, keepdims=True)
    i1 = jnp.min(jnp.where(el == v1, lane, big), axis=-1, keepdims=True)
    el2 = jnp.where(lane == i1, neg, el)
    v2 = jnp.max(el2, axis=-1, keepdims=True)
    i2 = jnp.min(jnp.where(el2 == v2, lane, big), axis=-1, keepdims=True)
    e21 = jnp.exp(v2 - v1)
    p1 = 1.0 / (1.0 + e21)
    p2 = e21 * p1
    return jnp.where(lane == i1, p1 * g_p, 0.0) + jnp.where(lane == i2, p2 * g_p, 0.0)


def _out_proj_kernel(x_ref, mr_ref, ms_ref, wo_ref, mm_ref, mf_ref, lg_ref, lb_ref, wr_ref, br_ref,
                     x1_ref, h2_ref, cmb_ref):
    mix = _dot(mr_ref[0], wo_ref[:MIX_HALF]) + _dot(ms_ref[0], wo_ref[MIX_HALF:])
    gate = mm_ref[0][:, 2 * D_MODEL:]
    x1 = _ln_plain(DEEPNORM_ALPHA * x_ref[0] + gate * mix) * lg_ref[...] + lb_ref[...]
    x1_ref[0] = x1
    mf = mf_ref[0]
    h2 = (_ln_plain(x1) * (1.0 + mf[:, D_MODEL:2 * D_MODEL]) + mf[:, :D_MODEL]).astype(BF16)
    h2_ref[0] = h2
    cmb_ref[0] = _route(_dot(h2, wr_ref[...]) + br_ref[...])


MIX_HALF = RET_WIDTH


def _out_proj(x, mr, ms, w_out_bf, mod_mix, mod_ffn, ln_g, ln_b, w_route_bf, b_route, tm):
    b, t, _ = x.shape
    tok = lambda w: pl.BlockSpec((1, tm, w), lambda i, j: (i, j, 0))
    modspec = pl.BlockSpec((1, 1, 3 * D_MODEL), lambda i, j: (i, 0, 0))
    row = lambda w: pl.BlockSpec((1, w), lambda i, j: (0, 0))
    return pl.pallas_call(
        _out_proj_kernel,
        grid=(b, t // tm),
        in_specs=[tok(D_MODEL), tok(RET_WIDTH), tok(SB_WIDTH),
                  pl.BlockSpec((D_MODEL, D_MODEL), lambda i, j: (0, 0)),
                  modspec, modspec, row(D_MODEL), row(D_MODEL),
                  pl.BlockSpec((D_MODEL, LANES), lambda i, j: (0, 0)), row(LANES)],
        out_specs=[tok(D_MODEL), tok(D_MODEL), tok(LANES)],
        out_shape=[jax.ShapeDtypeStruct((b, t, D_MODEL), F32),
                   jax.ShapeDtypeStruct((b, t, D_MODEL), BF16),
                   jax.ShapeDtypeStruct((b, t, LANES), F32)],
        compiler_params=_cparams(("parallel", "arbitrary")),
        name="out_proj",
    )(x, mr, ms, w_out_bf, mod_mix.reshape(b, 1, -1), mod_ffn.reshape(b, 1, -1),
      ln_g.reshape(1, -1), ln_b.reshape(1, -1), w_route_bf, b_route)


def _moe_kernel(h_ref, c_ref, wg_ref, wu_ref, wd_ref, y_ref):
    e = pl.program_id(1)

    @pl.when(e == 0)
    def _():
        y_ref[...] = jnp.zeros_like(y_ref)

    h = h_ref[...]
    a = _silu(_dot(h, wg_ref[0])) * _dot(h, wu_ref[0])
    r = _dot(a.astype(BF16), wd_ref[0])
    cmb = c_ref[...]
    lane = lax.broadcasted_iota(jnp.int32, cmb.shape, 1)
    c = jnp.sum(jnp.where(lane == N_GROUPS + e, cmb, 0.0), axis=-1, keepdims=True)
    y_ref[...] += c * r


def _moe(h2, cmb, wg_bf, wu_bf, wd_bf, tm):
    n = h2.shape[0]
    return pl.pallas_call(
        _moe_kernel,
        grid=(n // tm, N_EXPERTS),
        in_specs=[pl.BlockSpec((tm, D_MODEL), lambda i, e: (i, 0)),
                  pl.BlockSpec((tm, LANES), lambda i, e: (i, 0)),
                  pl.BlockSpec((1, D_MODEL, D_EXPERT), lambda i, e: (e, 0, 0)),
                  pl.BlockSpec((1, D_MODEL, D_EXPERT), lambda i, e: (e, 0, 0)),
                  pl.BlockSpec((1, D_EXPERT, D_MODEL), lambda i, e: (e, 0, 0))],
        out_specs=pl.BlockSpec((tm, D_MODEL), lambda i, e: (i, 0)),
        out_shape=jax.ShapeDtypeStruct((n, D_MODEL), F32),
        compiler_params=_cparams(("parallel", "arbitrary")),
        name="moe_dense",
    )(h2, cmb, wg_bf, wu_bf, wd_bf)


def _final_kernel(x1_ref, y_ref, mf_ref, lg_ref, lb_ref, o_ref):
    gate = mf_ref[0][:, 2 * D_MODEL:]
    o_ref[0] = _ln_plain(DEEPNORM_ALPHA * x1_ref[0] + gate * y_ref[0]) * lg_ref[...] + lb_ref[...]


def _final(x1, y, mod_ffn, ln_g, ln_b, tm):
    b, t, _ = x1.shape
    tok = pl.BlockSpec((1, tm, D_MODEL), lambda i, j: (i, j, 0))
    row = pl.BlockSpec((1, D_MODEL), lambda i, j: (0, 0))
    return pl.pallas_call(
        _final_kernel,
        grid=(b, t // tm),
        in_specs=[tok, tok, pl.BlockSpec((1, 1, 3 * D_MODEL), lambda i, j: (i, 0, 0)), row, row],
        out_specs=tok,
        out_shape=jax.ShapeDtypeStruct((b, t, D_MODEL), F32),
        compiler_params=_cparams(("parallel", "arbitrary")),
        name="final_ln",
    )(x1, y, mod_ffn.reshape(b, 1, -1), ln_g.reshape(1, -1), ln_b.reshape(1, -1))


def _trunk(x, mod_mix, mod_ffn, pos0, state0, k_past, v_past, wts):
    b, t, _ = x.shape
    tm = min(256, t)
    ret_in, sq, sk, sv = _in_proj(x, mod_mix, wts["w_in"], pos0, tm)
    mr, state = _retention(ret_in, state0, wts["ret_norm_g"], min(256, t))
    if k_past is None:
        ms = _sb_prompt(sq, sk, sv, wts["sb_norm_g"], min(256, t))
    else:
        ms = _sb_sample(sq, sk, sv, k_past, v_past, wts["sb_norm_g"], 256)
    x1, h2, cmb = _out_proj(x, mr, ms, wts["w_out"], mod_mix, mod_ffn, wts["ln_mix_g"], wts["ln_mix_b"],
                            wts["w_route"], wts["b_route"], tm)
    n = b * t
    y = _moe(h2.reshape(n, D_MODEL), cmb.reshape(n, LANES), wts["w_e_gate"], wts["w_e_up"],
             wts["w_e_down"], min(512, n))
    out = _final(x1, y.reshape(b, t, D_MODEL), mod_ffn, wts["ln_ffn_g"], wts["ln_ffn_b"], tm)
    return out, sk[None], sv[None], state[None]


def kernel(x_prompt, x_sample, cache_sb_k, cache_sb_v, state_ret, c_prompt, c_sample, w_in, w_out, ret_norm_g, sb_norm_g, w_ada_mix, b_ada_mix, ln_mix_g, ln_mix_b, w_ada_ffn, b_ada_ffn, ln_ffn_g, ln_ffn_b, w_group, b_group, w_router, b_router, w_e_gate, w_e_up, w_e_down):
    bp = x_prompt.shape[0]
    c_all = jnp.concatenate([c_prompt, c_sample], axis=0)
    mod_mix = _ada(c_all, w_ada_mix[0], b_ada_mix[0])
    mod_ffn = _ada(c_all, w_ada_ffn[0], b_ada_ffn[0])
    pad = LANES - N_GROUPS - N_EXPERTS
    w_route = jnp.concatenate([w_group[0], w_router[0], jnp.zeros((D_MODEL, pad), F32)], axis=1)
    b_route = jnp.concatenate([b_group[0], b_router[0], jnp.zeros((pad,), F32)]).reshape(1, LANES)
    wts = dict(w_in=w_in[0].astype(BF16), w_out=w_out[0].astype(BF16),
               ret_norm_g=ret_norm_g[0], sb_norm_g=sb_norm_g[0],
               ln_mix_g=ln_mix_g[0], ln_mix_b=ln_mix_b[0], ln_ffn_g=ln_ffn_g[0], ln_ffn_b=ln_ffn_b[0],
               w_route=w_route.astype(BF16), b_route=b_route,
               w_e_gate=w_e_gate[0].astype(BF16), w_e_up=w_e_up[0].astype(BF16),
               w_e_down=w_e_down[0].astype(BF16))
    ret_zero = jnp.zeros((bp, RET_HEADS, RET_DK, RET_DV), F32)
    y_p, k_p, v_p, r_p = _trunk(x_prompt, mod_mix[:bp], mod_ffn[:bp], 0, ret_zero, None, None, wts)
    y_s, k_s, v_s, r_s = _trunk(x_sample, mod_mix[bp:], mod_ffn[bp:], cache_sb_k.shape[3], state_ret[0],
                                cache_sb_k[0], cache_sb_v[0], wts)
    return (y_p, y_s, k_p, v_p, r_p, k_s, v_s, r_s)
```

```python
import functools
import math

import numpy as np
import jax
import jax.numpy as jnp
from jax import lax
from jax.experimental import pallas as pl
from jax.experimental.pallas import tpu as pltpu

D_MODEL = 1024
RET_HEADS = 4
RET_DK = 128
RET_DV = 128
RET_WIDTH = RET_HEADS * RET_DV
SB_HEADS = 8
SB_HEAD_DIM = 64
SB_WIDTH = SB_HEADS * SB_HEAD_DIM
IN_WIDTH = 2 * RET_HEADS * RET_DK + 2 * RET_WIDTH + 3 * SB_WIDTH
ROPE_BASE = 10000.0
N_GROUPS = 4
EXPERTS_PER_GROUP = 4
N_EXPERTS = N_GROUPS * EXPERTS_PER_GROUP
D_EXPERT = 512
DEPTH = 1
DEEPNORM_ALPHA = (2.0 * DEPTH) ** 0.25
LN_EPS = 1e-5

LANES = 128
VMEM_LIMIT = 48 * 1024 * 1024

F32 = jnp.float32
BF16 = jnp.bfloat16


def _cparams(sem):
    return pltpu.CompilerParams(dimension_semantics=sem, vmem_limit_bytes=VMEM_LIMIT)


def _dot(a, b):
    return jnp.dot(a, b, preferred_element_type=F32)


def _dot_nt(a, b):
    return lax.dot_general(a, b, (((1,), (1,)), ((), ())), preferred_element_type=F32)


def _dot_tn(a, b):
    return lax.dot_general(a, b, (((0,), (0,)), ((), ())), preferred_element_type=F32)


def _split_dot(a, w_hi, w_lo):
    a_hi = a.astype(BF16)
    a_lo = (a - a_hi.astype(F32)).astype(BF16)
    return _dot(a_hi, w_hi) + (_dot(a_hi, w_lo) + _dot(a_lo, w_hi))


def _ln_plain(x):
    mu = jnp.mean(x, axis=-1, keepdims=True)
    xc = x - mu
    var = jnp.mean(xc * xc, axis=-1, keepdims=True)
    return xc * lax.rsqrt(var + LN_EPS)


def _silu(x):
    return x * (1.0 / (1.0 + jnp.exp(-x)))


def _ada_kernel(c_ref, w_ref, b_ref, o_ref):
    c = c_ref[...]
    w = w_ref[...]
    w_hi = w.astype(BF16)
    w_lo = (w - w_hi.astype(F32)).astype(BF16)
    o_ref[...] = _split_dot(_silu(c), w_hi, w_lo) + b_ref[...]


def _ada(c, w, b):
    r = c.shape[0]
    tn = 768
    return pl.pallas_call(
        _ada_kernel,
        grid=(3 * D_MODEL // tn,),
        in_specs=[pl.BlockSpec((r, D_MODEL), lambda j: (0, 0)),
                  pl.BlockSpec((D_MODEL, tn), lambda j: (0, j)),
                  pl.BlockSpec((1, tn), lambda j: (0, j))],
        out_specs=pl.BlockSpec((r, tn), lambda j: (0, j)),
        out_shape=jax.ShapeDtypeStruct((r, 3 * D_MODEL), F32),
        compiler_params=_cparams(("arbitrary",)),
        name="ada_mod",
    )(c, w, b.reshape(1, -1))


def _in_proj_kernel(x_ref, mod_ref, w_ref, wkv_ref, tab_ref, ret_ref, q_ref, k_ref, v_ref):
    m = mod_ref[0]
    h = _ln_plain(x_ref[0]) * (1.0 + m[:, D_MODEL:2 * D_MODEL]) + m[:, :D_MODEL]
    h = h.astype(BF16)
    tab = tab_ref[...]
    for c in range(4):
        p = _dot(h, w_ref[:, c * RET_WIDTH:(c + 1) * RET_WIDTH])
        if c < 2:
            cs = tab[:, (2 * c) * LANES:(2 * c + 1) * LANES]
            sn = tab[:, (2 * c + 1) * LANES:(2 * c + 2) * LANES]
            for hh in range(RET_HEADS):
                ph = p[:, hh * RET_DK:(hh + 1) * RET_DK]
                ret_ref[0, :, c * RET_WIDTH + hh * RET_DK:c * RET_WIDTH + (hh + 1) * RET_DK] = (
                    ph * cs + pltpu.roll(ph, RET_DK // 2, 1) * sn)
        else:
            ret_ref[0, :, c * RET_WIDTH:(c + 1) * RET_WIDTH] = p
    base = 4 * RET_WIDTH
    p = _dot(h, w_ref[:, base:base + SB_WIDTH]) * (LOG2E * SB_HEAD_DIM ** -0.5)
    for hh in range(SB_HEADS):
        q_ref[0, hh] = p[:, hh * SB_HEAD_DIM:(hh + 1) * SB_HEAD_DIM].astype(q_ref.dtype)
    pt = _dot_nt(wkv_ref[...], h)
    for c, ref in enumerate((k_ref, v_ref)):
        for hh in range(SB_HEADS):
            r0 = c * SB_WIDTH + hh * SB_HEAD_DIM
            ref[0, hh] = pt[r0:r0 + SB_HEAD_DIM, :]


def _rope_table(pos0, t):
    half = RET_DK // 2
    inv = ROPE_BASE ** (-np.arange(half, dtype=np.float64) / half)
    ang = (pos0 + np.arange(t, dtype=np.float64))[:, None] * inv[None, :]
    cos, sin = np.cos(ang), np.sin(ang)
    cs = np.concatenate([cos, cos], axis=1)
    sn = np.concatenate([-sin, sin], axis=1)
    ks = RET_DK ** -0.5
    return jnp.asarray(np.concatenate([cs, sn, cs * ks, sn * ks], axis=1), dtype=F32)


def _in_proj(x, mod, w_a_bf, w_kvt_bf, pos0, tm):
    b, t, _ = x.shape
    tab = _rope_table(pos0, t)
    wa = w_a_bf.shape[1]
    hs = jax.ShapeDtypeStruct((b, SB_HEADS, SB_HEAD_DIM, t), F32)
    q_spec = pl.BlockSpec((1, SB_HEADS, tm, SB_HEAD_DIM), lambda i, j: (i, 0, j, 0))
    kv_spec = pl.BlockSpec((1, SB_HEADS, SB_HEAD_DIM, tm), lambda i, j: (i, 0, 0, j))
    return pl.pallas_call(
        _in_proj_kernel,
        grid=(b, t // tm),
        in_specs=[pl.BlockSpec((1, tm, D_MODEL), lambda i, j: (i, j, 0)),
                  pl.BlockSpec((1, 1, 3 * D_MODEL), lambda i, j: (i, 0, 0)),
                  pl.BlockSpec((D_MODEL, wa), lambda i, j: (0, 0)),
                  pl.BlockSpec((2 * SB_WIDTH, D_MODEL), lambda i, j: (0, 0)),
                  pl.BlockSpec((tm, 4 * LANES), lambda i, j: (j, 0))],
        out_specs=[pl.BlockSpec((1, tm, 4 * RET_WIDTH), lambda i, j: (i, j, 0)),
                   q_spec, kv_spec, kv_spec],
        out_shape=[jax.ShapeDtypeStruct((b, t, 4 * RET_WIDTH), F32),
                   jax.ShapeDtypeStruct((b, SB_HEADS, t, SB_HEAD_DIM), BF16), hs, hs],
        compiler_params=_cparams(("parallel", "arbitrary")),
        name="in_proj",
    )(x, mod.reshape(b, 1, -1), w_a_bf, w_kvt_bf, tab)


def _ret_kernel(q_ref, k_ref, v_ref, g_ref, s0_ref, dec_ref, qd_ref, kd_ref, ng_ref,
                o_ref, so_ref, st_ref, *, chunk_decay):
    j = pl.program_id(1)

    @pl.when(j == 0)
    def _():
        st_ref[...] = s0_ref[0]

    for hh in range(RET_HEADS):
        sl = slice(hh * RET_DK, (hh + 1) * RET_DK)
        q = q_ref[0, :, sl]
        k = k_ref[0, :, sl]
        v = v_ref[0, :, sl]
        vb = v.astype(BF16)
        st = st_ref[hh]
        scores = _dot_nt(q.astype(BF16), k.astype(BF16)) * dec_ref[hh]
        o = _dot(scores.astype(BF16), vb) + _dot((q * qd_ref[hh]).astype(BF16), st.astype(BF16))
        st_ref[hh] = st * chunk_decay[hh] + _dot_tn((k * kd_ref[hh]).astype(BF16), vb)
        o = _ln_plain(o) * ng_ref[:, sl] * _silu(g_ref[0, :, sl])
        o_ref[0, :, sl] = o.astype(o_ref.dtype)

    @pl.when(j == pl.num_programs(1) - 1)
    def _():
        so_ref[0] = st_ref[...]


def _retention(ret_in, state0, norm_g, chunk):
    b, t, _ = ret_in.shape
    lg = np.log1p(-np.exp2(-5.0 - np.arange(RET_HEADS, dtype=np.float64)))
    idx = np.arange(chunk, dtype=np.float64)
    rel = idx[:, None] - idx[None, :]
    dec = np.where(rel >= 0, np.exp(lg[:, None, None] * np.maximum(rel, 0.0)), 0.0)
    qd = np.broadcast_to(np.exp(lg[:, None] * (idx + 1.0))[:, :, None], (RET_HEADS, chunk, RET_DK))
    kd = np.broadcast_to(np.exp(lg[:, None] * (chunk - 1.0 - idx))[:, :, None], (RET_HEADS, chunk, RET_DK))
    chunk_decay = tuple(float(v) for v in np.exp(lg * chunk))

    def col(c):
        return pl.BlockSpec((1, chunk, RET_WIDTH), lambda i, j, c=c: (i, j, c))

    const3 = lambda shape: pl.BlockSpec(shape, lambda i, j: (0, 0, 0))
    state_spec = pl.BlockSpec((1, RET_HEADS, RET_DK, RET_DV), lambda i, j: (i, 0, 0, 0))
    return pl.pallas_call(
        functools.partial(_ret_kernel, chunk_decay=chunk_decay),
        grid=(b, t // chunk),
        in_specs=[col(0), col(1), col(2), col(3), state_spec,
                  const3((RET_HEADS, chunk, chunk)), const3((RET_HEADS, chunk, RET_DK)),
                  const3((RET_HEADS, chunk, RET_DK)),
                  pl.BlockSpec((1, RET_WIDTH), lambda i, j: (0, 0))],
        out_specs=[pl.BlockSpec((1, chunk, RET_WIDTH), lambda i, j: (i, j, 0)), state_spec],
        out_shape=[jax.ShapeDtypeStruct((b, t, RET_WIDTH), BF16),
                   jax.ShapeDtypeStruct((b, RET_HEADS, RET_DK, RET_DV), F32)],
        scratch_shapes=[pltpu.VMEM((RET_HEADS, RET_DK, RET_DV), F32)],
        compiler_params=_cparams(("parallel", "arbitrary")),
        name="retention",
    )(ret_in, ret_in, ret_in, ret_in, state0,
      jnp.asarray(dec, F32), jnp.asarray(qd, F32), jnp.asarray(kd, F32), norm_g.reshape(1, -1))


SB_KEY_BLOCK = 256


LOG2E = 1.4426950408889634
SIGN_BIT = -2 ** 31


def _sb_block(qs, kts, vts, tri2, acc_ref, car_ref, masked):
    n = len(qs)
    tq, kb = qs[0].shape[0], kts[0].shape[1]
    if masked:
        valid = (lax.broadcasted_iota(jnp.int32, (tq, kb), 1)
                 < lax.broadcasted_iota(jnp.int32, (tq, kb), 0))
    zs, parts = [], []
    for q, kt in zip(qs, kts):
        z = _dot(q, kt.astype(BF16))
        neg_abs = lax.bitcast_convert_type(
            lax.bitcast_convert_type(z, jnp.int32) | jnp.int32(SIGN_BIT), F32)
        p = jnp.maximum(z, 0.0) + jnp.log2(1.0 + jnp.exp2(neg_abs))
        if masked:
            p = jnp.where(valid, p, 0.0)
        hi = p.astype(BF16)
        zs.append(z)
        parts.append(jnp.concatenate([hi, (p - hi.astype(F32)).astype(BF16)], axis=1))
    r = _dot(jnp.concatenate(parts, axis=0), tri2)
    for c in range(n):
        incl = r[c * tq:(c + 1) * tq]
        car = car_ref[c]
        w = jnp.exp2(zs[c] + incl + car)
        if masked:
            w = jnp.where(valid, w, 0.0)
        acc_ref[c] += _dot_nt(w.astype(BF16), vts[c].astype(BF16))
        car_ref[c] = car + incl[:, 0:1]


def _sb_finish(o, g):
    return o * lax.rsqrt(jnp.mean(o * o, axis=-1, keepdims=True) + LN_EPS) * g


def _tri_matrix():
    idx = np.arange(SB_KEY_BLOCK)
    t = -(idx[:, None] >= idx[None, :]).astype(np.float32)
    return jnp.asarray(np.concatenate([t, t], axis=0), dtype=BF16)


def _sb_prompt_kernel(q_ref, kt_ref, vt_ref, tri_ref, g_ref, o_ref, acc_ref, car_ref, *, tq, nh):
    i = pl.program_id(2)
    qs = [q_ref[0, hh] for hh in range(nh)]
    acc_ref[...] = jnp.zeros_like(acc_ref)
    car_ref[...] = jnp.zeros_like(car_ref)

    def block(start, masked):
        kts = [kt_ref[0, hh, :, pl.ds(start, tq)] for hh in range(nh)]
        vts = [vt_ref[0, hh, :, pl.ds(start, tq)] for hh in range(nh)]
        _sb_block(qs, kts, vts, tri_ref[...], acc_ref, car_ref, masked)

    block(pl.multiple_of(i * tq, tq), True)

    def body(t, _):
        block(pl.multiple_of((i - 1 - t) * tq, tq), False)
        return 0

    lax.fori_loop(0, i, body, 0)
    d = SB_HEAD_DIM
    outs = [_sb_finish(acc_ref[hh], g_ref[0, :, hh * d:(hh + 1) * d]) for hh in range(nh)]
    o_ref[0] = jnp.concatenate(outs, axis=-1).astype(o_ref.dtype)


def _sb_prompt(q, kt, vt, norm_g):
    b, _, t, d = q.shape
    tq = SB_KEY_BLOCK
    nh = 4
    kv_spec = pl.BlockSpec((1, nh, d, t), lambda i, h, j: (i, h, 0, 0))
    return pl.pallas_call(
        functools.partial(_sb_prompt_kernel, tq=tq, nh=nh),
        grid=(b, SB_HEADS // nh, t // tq),
        in_specs=[pl.BlockSpec((1, nh, tq, d), lambda i, h, j: (i, h, j, 0)), kv_spec, kv_spec,
                  pl.BlockSpec((2 * tq, tq), lambda i, h, j: (0, 0)),
                  pl.BlockSpec((1, 1, nh * d), lambda i, h, j: (h, 0, 0))],
        out_specs=pl.BlockSpec((1, tq, nh * d), lambda i, h, j: (i, j, h)),
        out_shape=jax.ShapeDtypeStruct((b, t, SB_WIDTH), BF16),
        scratch_shapes=[pltpu.VMEM((nh, tq, d), F32), pltpu.VMEM((nh, tq, 1), F32)],
        compiler_params=_cparams(("parallel", "parallel", "arbitrary")),
        name="sb_prompt",
    )(q, kt, vt, _tri_matrix(), norm_g.reshape(SB_HEADS // nh, 1, nh * d))


def _sb_sample_kernel(q_ref, kt_ref, vt_ref, ktp_ref, vtp_ref, tri_ref, g_ref, o_ref, acc_ref, car_ref,
                      *, nh):
    kb = SB_KEY_BLOCK
    n_past = ktp_ref.shape[3] // kb
    t = q_ref.shape[2]
    qs = [q_ref[0, hh] for hh in range(nh)]
    acc_ref[...] = jnp.zeros_like(acc_ref)
    car_ref[...] = jnp.zeros_like(car_ref)
    _sb_block(qs, [kt_ref[0, hh] for hh in range(nh)], [vt_ref[0, hh] for hh in range(nh)],
              jnp.concatenate([tri_ref[:t, :t], tri_ref[:t, :t]], axis=0), acc_ref, car_ref, True)

    def body(j, _):
        s = pl.multiple_of((n_past - 1 - j) * kb, kb)
        _sb_block(qs, [ktp_ref[0, hh, :, pl.ds(s, kb)] for hh in range(nh)],
                  [vtp_ref[0, hh, :, pl.ds(s, kb)] for hh in range(nh)],
                  tri_ref[...], acc_ref, car_ref, False)
        return 0

    lax.fori_loop(0, n_past, body, 0)
    d = SB_HEAD_DIM
    outs = [_sb_finish(acc_ref[hh], g_ref[0, :, hh * d:(hh + 1) * d]) for hh in range(nh)]
    o_ref[0] = jnp.concatenate(outs, axis=-1).astype(o_ref.dtype)


def _sb_sample(q, kt, vt, kt_past, vt_past, norm_g):
    b, _, t, d = q.shape
    p = kt_past.shape[3]
    nh = 4
    kb = SB_KEY_BLOCK
    new_spec = pl.BlockSpec((1, nh, d, t), lambda i, h: (i, h, 0, 0))
    past_spec = pl.BlockSpec((1, nh, d, p), lambda i, h: (i, h, 0, 0))
    return pl.pallas_call(
        functools.partial(_sb_sample_kernel, nh=nh),
        grid=(b, SB_HEADS // nh),
        in_specs=[pl.BlockSpec((1, nh, t, d), lambda i, h: (i, h, 0, 0)), new_spec, new_spec,
                  past_spec, past_spec,
                  pl.BlockSpec((2 * kb, kb), lambda i, h: (0, 0)),
                  pl.BlockSpec((1, 1, nh * d), lambda i, h: (h, 0, 0))],
        out_specs=pl.BlockSpec((1, t, nh * d), lambda i, h: (i, 0, h)),
        out_shape=jax.ShapeDtypeStruct((b, t, SB_WIDTH), BF16),
        scratch_shapes=[pltpu.VMEM((nh, t, d), F32), pltpu.VMEM((nh, t, 1), F32)],
        compiler_params=_cparams(("parallel", "arbitrary")),
        name="sb_sample",
    )(q, kt, vt, kt_past, vt_past, _tri_matrix(), norm_g.reshape(SB_HEADS // nh, 1, nh * d))


def _route(logits):
    lane = lax.broadcasted_iota(jnp.int32, logits.shape, 1)
    neg = -jnp.inf
    big = jnp.int32(2 * LANES)
    gl = jnp.where(lane < N_GROUPS, logits, neg)
    gmax = jnp.max(gl, axis=-1, keepdims=True)
    g_idx = jnp.min(jnp.where(gl == gmax, lane, big), axis=-1, keepdims=True)
    g_p = 1.0 / jnp.sum(jnp.exp(gl - gmax), axis=-1, keepdims=True)
    lo = N_GROUPS + g_idx * EXPERTS_PER_GROUP
    el = jnp.where((lane >= lo) & (lane < lo + EXPERTS_PER_GROUP), logits, neg)
    v1 = jnp.max(el, axis=-1, keepdims=True)
    i1 = jnp.min(jnp.where(el == v1, lane, big), axis=-1, keepdims=True)
    el2 = jnp.where(lane == i1, neg, el)
    v2 = jnp.max(el2, axis=-1, keepdims=True)
    i2 = jnp.min(jnp.where(el2 == v2, lane, big), axis=-1, keepdims=True)
    e21 = jnp.exp(v2 - v1)
    p1 = 1.0 / (1.0 + e21)
    p2 = e21 * p1
    return jnp.where(lane == i1, p1 * g_p, 0.0) + jnp.where(lane == i2, p2 * g_p, 0.0)


def _out_proj_kernel(x_ref, mr_ref, ms_ref, wo_ref, mm_ref, mf_ref, lg_ref, lb_ref, wr_ref, br_ref,
                     x1_ref, h2_ref, cmb_ref):
    mix = _dot(mr_ref[0], wo_ref[:MIX_HALF]) + _dot(ms_ref[0], wo_ref[MIX_HALF:])
    gate = mm_ref[0][:, 2 * D_MODEL:]
    x1 = _ln_plain(DEEPNORM_ALPHA * x_ref[0] + gate * mix) * lg_ref[...] + lb_ref[...]
    x1_ref[0] = x1
    mf = mf_ref[0]
    h2 = (_ln_plain(x1) * (1.0 + mf[:, D_MODEL:2 * D_MODEL]) + mf[:, :D_MODEL]).astype(BF16)
    h2_ref[0] = h2
    cmb_ref[0] = _route(_dot(h2, wr_ref[...]) + br_ref[...])


MIX_HALF = RET_WIDTH


def _out_proj(x, mr, ms, w_out_bf, mod_mix, mod_ffn, ln_g, ln_b, w_route_bf, b_route, tm):
    b, t, _ = x.shape
    tok = lambda w: pl.BlockSpec((1, tm, w), lambda i, j: (i, j, 0))
    modspec = pl.BlockSpec((1, 1, 3 * D_MODEL), lambda i, j: (i, 0, 0))
    row = lambda w: pl.BlockSpec((1, w), lambda i, j: (0, 0))
    return pl.pallas_call(
        _out_proj_kernel,
        grid=(b, t // tm),
        in_specs=[tok(D_MODEL), tok(RET_WIDTH), tok(SB_WIDTH),
                  pl.BlockSpec((D_MODEL, D_MODEL), lambda i, j: (0, 0)),
                  modspec, modspec, row(D_MODEL), row(D_MODEL),
                  pl.BlockSpec((D_MODEL, LANES), lambda i, j: (0, 0)), row(LANES)],
        out_specs=[tok(D_MODEL), tok(D_MODEL), tok(LANES)],
        out_shape=[jax.ShapeDtypeStruct((b, t, D_MODEL), F32),
                   jax.ShapeDtypeStruct((b, t, D_MODEL), BF16),
                   jax.ShapeDtypeStruct((b, t, LANES), F32)],
        compiler_params=_cparams(("parallel", "arbitrary")),
        name="out_proj",
    )(x, mr, ms, w_out_bf, mod_mix.reshape(b, 1, -1), mod_ffn.reshape(b, 1, -1),
      ln_g.reshape(1, -1), ln_b.reshape(1, -1), w_route_bf, b_route)


def _moe_kernel(h_ref, c_ref, wg_ref, wu_ref, wd_ref, y_ref):
    e = pl.program_id(1)

    @pl.when(e == 0)
    def _():
        y_ref[...] = jnp.zeros_like(y_ref)

    h = h_ref[...]
    a = _silu(_dot(h, wg_ref[0])) * _dot(h, wu_ref[0])
    r = _dot(a.astype(BF16), wd_ref[0])
    cmb = c_ref[...]
    lane = lax.broadcasted_iota(jnp.int32, cmb.shape, 1)
    c = jnp.sum(jnp.where(lane == N_GROUPS + e, cmb, 0.0), axis=-1, keepdims=True)
    y_ref[...] += c * r


def _moe(h2, cmb, wg_bf, wu_bf, wd_bf, tm):
    n = h2.shape[0]
    return pl.pallas_call(
        _moe_kernel,
        grid=(n // tm, N_EXPERTS),
        in_specs=[pl.BlockSpec((tm, D_MODEL), lambda i, e: (i, 0)),
                  pl.BlockSpec((tm, LANES), lambda i, e: (i, 0)),
                  pl.BlockSpec((1, D_MODEL, D_EXPERT), lambda i, e: (e, 0, 0)),
                  pl.BlockSpec((1, D_MODEL, D_EXPERT), lambda i, e: (e, 0, 0)),
                  pl.BlockSpec((1, D_EXPERT, D_MODEL), lambda i, e: (e, 0, 0))],
        out_specs=pl.BlockSpec((tm, D_MODEL), lambda i, e: (i, 0)),
        out_shape=jax.ShapeDtypeStruct((n, D_MODEL), F32),
        compiler_params=_cparams(("parallel", "arbitrary")),
        name="moe_dense",
    )(h2, cmb, wg_bf, wu_bf, wd_bf)


def _final_kernel(x1_ref, y_ref, mf_ref, lg_ref, lb_ref, o_ref):
    gate = mf_ref[0][:, 2 * D_MODEL:]
    o_ref[0] = _ln_plain(DEEPNORM_ALPHA * x1_ref[0] + gate * y_ref[0]) * lg_ref[...] + lb_ref[...]


def _final(x1, y, mod_ffn, ln_g, ln_b, tm):
    b, t, _ = x1.shape
    tok = pl.BlockSpec((1, tm, D_MODEL), lambda i, j: (i, j, 0))
    row = pl.BlockSpec((1, D_MODEL), lambda i, j: (0, 0))
    return pl.pallas_call(
        _final_kernel,
        grid=(b, t // tm),
        in_specs=[tok, tok, pl.BlockSpec((1, 1, 3 * D_MODEL), lambda i, j: (i, 0, 0)), row, row],
        out_specs=tok,
        out_shape=jax.ShapeDtypeStruct((b, t, D_MODEL), F32),
        compiler_params=_cparams(("parallel", "arbitrary")),
        name="final_ln",
    )(x1, y, mod_ffn.reshape(b, 1, -1), ln_g.reshape(1, -1), ln_b.reshape(1, -1))


def _trunk(x, mod_mix, mod_ffn, pos0, state0, k_past, v_past, wts):
    b, t, _ = x.shape
    tm = min(256, t)
    ret_in, sq, skt, svt = _in_proj(x, mod_mix, wts["w_in_a"], wts["w_in_kvt"], pos0, tm)
    mr, state = _retention(ret_in, state0, wts["ret_norm_g"], min(256, t))
    if k_past is None:
        ms = _sb_prompt(sq, skt, svt, wts["sb_norm_g"])
    else:
        ms = _sb_sample(sq, skt, svt, jnp.swapaxes(k_past, 2, 3), jnp.swapaxes(v_past, 2, 3),
                        wts["sb_norm_g"])
    sk, sv = jnp.swapaxes(skt, 2, 3), jnp.swapaxes(svt, 2, 3)
    x1, h2, cmb = _out_proj(x, mr, ms, wts["w_out"], mod_mix, mod_ffn, wts["ln_mix_g"], wts["ln_mix_b"],
                            wts["w_route"], wts["b_route"], tm)
    n = b * t
    y = _moe(h2.reshape(n, D_MODEL), cmb.reshape(n, LANES), wts["w_e_gate"], wts["w_e_up"],
             wts["w_e_down"], min(512, n))
    out = _final(x1, y.reshape(b, t, D_MODEL), mod_ffn, wts["ln_ffn_g"], wts["ln_ffn_b"], tm)
    return out, sk[None], sv[None], state[None]


def kernel(x_prompt, x_sample, cache_sb_k, cache_sb_v, state_ret, c_prompt, c_sample, w_in, w_out, ret_norm_g, sb_norm_g, w_ada_mix, b_ada_mix, ln_mix_g, ln_mix_b, w_ada_ffn, b_ada_ffn, ln_ffn_g, ln_ffn_b, w_group, b_group, w_router, b_router, w_e_gate, w_e_up, w_e_down):
    bp = x_prompt.shape[0]
    c_all = jnp.concatenate([c_prompt, c_sample], axis=0)
    mod_mix = _ada(c_all, w_ada_mix[0], b_ada_mix[0])
    mod_ffn = _ada(c_all, w_ada_ffn[0], b_ada_ffn[0])
    pad = LANES - N_GROUPS - N_EXPERTS
    w_route = jnp.concatenate([w_group[0], w_router[0], jnp.zeros((D_MODEL, pad), F32)], axis=1)
    b_route = jnp.concatenate([b_group[0], b_router[0], jnp.zeros((pad,), F32)]).reshape(1, LANES)
    n_a = 4 * RET_WIDTH + SB_WIDTH
    wts = dict(w_in_a=w_in[0, :, :n_a].astype(BF16), w_in_kvt=w_in[0, :, n_a:].T.astype(BF16),
               w_out=w_out[0].astype(BF16),
               ret_norm_g=ret_norm_g[0], sb_norm_g=sb_norm_g[0],
               ln_mix_g=ln_mix_g[0], ln_mix_b=ln_mix_b[0], ln_ffn_g=ln_ffn_g[0], ln_ffn_b=ln_ffn_b[0],
               w_route=w_route.astype(BF16), b_route=b_route,
               w_e_gate=w_e_gate[0].astype(BF16), w_e_up=w_e_up[0].astype(BF16),
               w_e_down=w_e_down[0].astype(BF16))
    ret_zero = jnp.zeros((bp, RET_HEADS, RET_DK, RET_DV), F32)
    y_p, k_p, v_p, r_p = _trunk(x_prompt, mod_mix[:bp], mod_ffn[:bp], 0, ret_zero, None, None, wts)
    y_s, k_s, v_s, r_s = _trunk(x_sample, mod_mix[bp:], mod_ffn[bp:], cache_sb_k.shape[3], state_ret[0],
                                cache_sb_k[0], cache_sb_v[0], wts)
    return (y_p, y_s, k_p, v_p, r_p, k_s, v_s, r_s)
```

```python
import functools
import math

import numpy as np
import jax
import jax.numpy as jnp
from jax import lax
from jax.experimental import pallas as pl
from jax.experimental.pallas import tpu as pltpu

D_MODEL = 1024
RET_HEADS = 4
RET_DK = 128
RET_DV = 128
RET_WIDTH = RET_HEADS * RET_DV
SB_HEADS = 8
SB_HEAD_DIM = 64
SB_WIDTH = SB_HEADS * SB_HEAD_DIM
IN_WIDTH = 2 * RET_HEADS * RET_DK + 2 * RET_WIDTH + 3 * SB_WIDTH
ROPE_BASE = 10000.0
N_GROUPS = 4
EXPERTS_PER_GROUP = 4
N_EXPERTS = N_GROUPS * EXPERTS_PER_GROUP
D_EXPERT = 512
DEPTH = 1
DEEPNORM_ALPHA = (2.0 * DEPTH) ** 0.25
LN_EPS = 1e-5

LANES = 128
VMEM_LIMIT = 48 * 1024 * 1024

F32 = jnp.float32
BF16 = jnp.bfloat16


def _cparams(sem):
    return pltpu.CompilerParams(dimension_semantics=sem, vmem_limit_bytes=VMEM_LIMIT)


def _dot(a, b):
    return jnp.dot(a, b, preferred_element_type=F32)


def _dot_nt(a, b):
    return lax.dot_general(a, b, (((1,), (1,)), ((), ())), preferred_element_type=F32)


def _dot_tn(a, b):
    return lax.dot_general(a, b, (((0,), (0,)), ((), ())), preferred_element_type=F32)


def _split_dot(a, w_hi, w_lo):
    a_hi = a.astype(BF16)
    a_lo = (a - a_hi.astype(F32)).astype(BF16)
    return _dot(a_hi, w_hi) + (_dot(a_hi, w_lo) + _dot(a_lo, w_hi))


def _ln_plain(x):
    mu = jnp.mean(x, axis=-1, keepdims=True)
    xc = x - mu
    var = jnp.mean(xc * xc, axis=-1, keepdims=True)
    return xc * lax.rsqrt(var + LN_EPS)


def _silu(x):
    return x * (1.0 / (1.0 + jnp.exp(-x)))


def _ada_kernel(c_ref, w_ref, b_ref, o_ref):
    c = c_ref[...]
    w = w_ref[...]
    w_hi = w.astype(BF16)
    w_lo = (w - w_hi.astype(F32)).astype(BF16)
    o_ref[...] = _split_dot(_silu(c), w_hi, w_lo) + b_ref[...]


def _ada(c, w, b):
    r = c.shape[0]
    tn = 768
    return pl.pallas_call(
        _ada_kernel,
        grid=(3 * D_MODEL // tn,),
        in_specs=[pl.BlockSpec((r, D_MODEL), lambda j: (0, 0)),
                  pl.BlockSpec((D_MODEL, tn), lambda j: (0, j)),
                  pl.BlockSpec((1, tn), lambda j: (0, j))],
        out_specs=pl.BlockSpec((r, tn), lambda j: (0, j)),
        out_shape=jax.ShapeDtypeStruct((r, 3 * D_MODEL), F32),
        compiler_params=_cparams(("arbitrary",)),
        name="ada_mod",
    )(c, w, b.reshape(1, -1))


def _in_proj_kernel(x_ref, mod_ref, w_ref, wkv_ref, tab_ref, ret_ref, q_ref, k_ref, v_ref):
    m = mod_ref[0]
    h = _ln_plain(x_ref[0]) * (1.0 + m[:, D_MODEL:2 * D_MODEL]) + m[:, :D_MODEL]
    h = h.astype(BF16)
    tab = tab_ref[...]
    for c in range(4):
        p = _dot(h, w_ref[:, c * RET_WIDTH:(c + 1) * RET_WIDTH])
        if c < 2:
            cs = tab[:, (2 * c) * LANES:(2 * c + 1) * LANES]
            sn = tab[:, (2 * c + 1) * LANES:(2 * c + 2) * LANES]
            for hh in range(RET_HEADS):
                ph = p[:, hh * RET_DK:(hh + 1) * RET_DK]
                ret_ref[0, :, c * RET_WIDTH + hh * RET_DK:c * RET_WIDTH + (hh + 1) * RET_DK] = (
                    ph * cs + pltpu.roll(ph, RET_DK // 2, 1) * sn)
        else:
            ret_ref[0, :, c * RET_WIDTH:(c + 1) * RET_WIDTH] = p
    base = 4 * RET_WIDTH
    p = _dot(h, w_ref[:, base:base + SB_WIDTH]) * (LOG2E * SB_HEAD_DIM ** -0.5)
    for hh in range(SB_HEADS):
        q_ref[0, hh] = p[:, hh * SB_HEAD_DIM:(hh + 1) * SB_HEAD_DIM].astype(q_ref.dtype)
    pt = _dot_nt(wkv_ref[...], h)
    for c, ref in enumerate((k_ref, v_ref)):
        for hh in range(SB_HEADS):
            r0 = c * SB_WIDTH + hh * SB_HEAD_DIM
            ref[0, hh] = pt[r0:r0 + SB_HEAD_DIM, :]


def _rope_table(pos0, t):
    half = RET_DK // 2
    inv = ROPE_BASE ** (-np.arange(half, dtype=np.float64) / half)
    ang = (pos0 + np.arange(t, dtype=np.float64))[:, None] * inv[None, :]
    cos, sin = np.cos(ang), np.sin(ang)
    cs = np.concatenate([cos, cos], axis=1)
    sn = np.concatenate([-sin, sin], axis=1)
    ks = RET_DK ** -0.5
    return jnp.asarray(np.concatenate([cs, sn, cs * ks, sn * ks], axis=1), dtype=F32)


def _in_proj(x, mod, w_a_bf, w_kvt_bf, pos0, tm):
    b, t, _ = x.shape
    tab = _rope_table(pos0, t)
    wa = w_a_bf.shape[1]
    hs = jax.ShapeDtypeStruct((b, SB_HEADS, SB_HEAD_DIM, t), F32)
    q_spec = pl.BlockSpec((1, SB_HEADS, tm, SB_HEAD_DIM), lambda i, j: (i, 0, j, 0))
    kv_spec = pl.BlockSpec((1, SB_HEADS, SB_HEAD_DIM, tm), lambda i, j: (i, 0, 0, j))
    return pl.pallas_call(
        _in_proj_kernel,
        grid=(b, t // tm),
        in_specs=[pl.BlockSpec((1, tm, D_MODEL), lambda i, j: (i, j, 0)),
                  pl.BlockSpec((1, 1, 3 * D_MODEL), lambda i, j: (i, 0, 0)),
                  pl.BlockSpec((D_MODEL, wa), lambda i, j: (0, 0)),
                  pl.BlockSpec((2 * SB_WIDTH, D_MODEL), lambda i, j: (0, 0)),
                  pl.BlockSpec((tm, 4 * LANES), lambda i, j: (j, 0))],
        out_specs=[pl.BlockSpec((1, tm, 4 * RET_WIDTH), lambda i, j: (i, j, 0)),
                   q_spec, kv_spec, kv_spec],
        out_shape=[jax.ShapeDtypeStruct((b, t, 4 * RET_WIDTH), F32),
                   jax.ShapeDtypeStruct((b, SB_HEADS, t, SB_HEAD_DIM), BF16), hs, hs],
        compiler_params=_cparams(("parallel", "arbitrary")),
        name="in_proj",
    )(x, mod.reshape(b, 1, -1), w_a_bf, w_kvt_bf, tab)


def _ret_kernel(q_ref, k_ref, v_ref, g_ref, s0_ref, dec_ref, qd_ref, kd_ref, ng_ref,
                o_ref, so_ref, st_ref, *, chunk_decay):
    j = pl.program_id(1)

    @pl.when(j == 0)
    def _():
        st_ref[...] = s0_ref[0]

    for hh in range(RET_HEADS):
        sl = slice(hh * RET_DK, (hh + 1) * RET_DK)
        q = q_ref[0, :, sl]
        k = k_ref[0, :, sl]
        v = v_ref[0, :, sl]
        vb = v.astype(BF16)
        st = st_ref[hh]
        scores = _dot_nt(q.astype(BF16), k.astype(BF16)) * dec_ref[hh]
        o = _dot(scores.astype(BF16), vb) + _dot((q * qd_ref[hh]).astype(BF16), st.astype(BF16))
        st_ref[hh] = st * chunk_decay[hh] + _dot_tn((k * kd_ref[hh]).astype(BF16), vb)
        o = _ln_plain(o) * ng_ref[:, sl] * _silu(g_ref[0, :, sl])
        o_ref[0, :, sl] = o.astype(o_ref.dtype)

    @pl.when(j == pl.num_programs(1) - 1)
    def _():
        so_ref[0] = st_ref[...]


def _retention(ret_in, state0, norm_g, chunk):
    b, t, _ = ret_in.shape
    lg = np.log1p(-np.exp2(-5.0 - np.arange(RET_HEADS, dtype=np.float64)))
    idx = np.arange(chunk, dtype=np.float64)
    rel = idx[:, None] - idx[None, :]
    dec = np.where(rel >= 0, np.exp(lg[:, None, None] * np.maximum(rel, 0.0)), 0.0)
    qd = np.broadcast_to(np.exp(lg[:, None] * (idx + 1.0))[:, :, None], (RET_HEADS, chunk, RET_DK))
    kd = np.broadcast_to(np.exp(lg[:, None] * (chunk - 1.0 - idx))[:, :, None], (RET_HEADS, chunk, RET_DK))
    chunk_decay = tuple(float(v) for v in np.exp(lg * chunk))

    def col(c):
        return pl.BlockSpec((1, chunk, RET_WIDTH), lambda i, j, c=c: (i, j, c))

    const3 = lambda shape: pl.BlockSpec(shape, lambda i, j: (0, 0, 0))
    state_spec = pl.BlockSpec((1, RET_HEADS, RET_DK, RET_DV), lambda i, j: (i, 0, 0, 0))
    return pl.pallas_call(
        functools.partial(_ret_kernel, chunk_decay=chunk_decay),
        grid=(b, t // chunk),
        in_specs=[col(0), col(1), col(2), col(3), state_spec,
                  const3((RET_HEADS, chunk, chunk)), const3((RET_HEADS, chunk, RET_DK)),
                  const3((RET_HEADS, chunk, RET_DK)),
                  pl.BlockSpec((1, RET_WIDTH), lambda i, j: (0, 0))],
        out_specs=[pl.BlockSpec((1, chunk, RET_WIDTH), lambda i, j: (i, j, 0)), state_spec],
        out_shape=[jax.ShapeDtypeStruct((b, t, RET_WIDTH), BF16),
                   jax.ShapeDtypeStruct((b, RET_HEADS, RET_DK, RET_DV), F32)],
        scratch_shapes=[pltpu.VMEM((RET_HEADS, RET_DK, RET_DV), F32)],
        compiler_params=_cparams(("parallel", "arbitrary")),
        name="retention",
    )(ret_in, ret_in, ret_in, ret_in, state0,
      jnp.asarray(dec, F32), jnp.asarray(qd, F32), jnp.asarray(kd, F32), norm_g.reshape(1, -1))


SB_KEY_BLOCK = 256


LOG2E = 1.4426950408889634


def _sb_block(qs, kts, vts, tri2, acc_ref, car_ref, masked):
    n = len(qs)
    tq, kb = qs[0].shape[0], kts[0].shape[1]
    if masked:
        valid = (lax.broadcasted_iota(jnp.int32, (tq, kb), 1)
                 < lax.broadcasted_iota(jnp.int32, (tq, kb), 0))
    zs, parts = [], []
    for q, kt in zip(qs, kts):
        z = _dot(q, kt.astype(BF16))
        p = jnp.maximum(z, 0.0) + jnp.log2(1.0 + jnp.exp2(-jnp.abs(z)))
        if masked:
            p = jnp.where(valid, p, 0.0)
        hi = p.astype(BF16)
        zs.append(z)
        parts.append(jnp.concatenate([hi, (p - hi.astype(F32)).astype(BF16)], axis=1))
    r = _dot(jnp.concatenate(parts, axis=0), tri2)
    for c in range(n):
        incl = r[c * tq:(c + 1) * tq]
        car = car_ref[c]
        w = jnp.exp2(zs[c] + incl + car)
        if masked:
            w = jnp.where(valid, w, 0.0)
        acc_ref[c] += _dot_nt(w.astype(BF16), vts[c].astype(BF16))
        car_ref[c] = car + incl[:, 0:1]


def _sb_finish(o, g):
    return o * lax.rsqrt(jnp.mean(o * o, axis=-1, keepdims=True) + LN_EPS) * g


def _tri_matrix():
    idx = np.arange(SB_KEY_BLOCK)
    t = -(idx[:, None] >= idx[None, :]).astype(np.float32)
    return jnp.asarray(np.concatenate([t, t], axis=0), dtype=BF16)


def _sb_prompt_kernel(q_ref, kt_ref, vt_ref, tri_ref, g_ref, o_ref, acc_ref, car_ref, *, tq, nh):
    i = pl.program_id(2)
    qs = [q_ref[0, hh] for hh in range(nh)]
    acc_ref[...] = jnp.zeros_like(acc_ref)
    car_ref[...] = jnp.zeros_like(car_ref)

    def block(start, masked):
        kts = [kt_ref[0, hh, :, pl.ds(start, tq)] for hh in range(nh)]
        vts = [vt_ref[0, hh, :, pl.ds(start, tq)] for hh in range(nh)]
        _sb_block(qs, kts, vts, tri_ref[...], acc_ref, car_ref, masked)

    block(pl.multiple_of(i * tq, tq), True)

    def body(t, _):
        block(pl.multiple_of((i - 1 - t) * tq, tq), False)
        return 0

    lax.fori_loop(0, i, body, 0)
    d = SB_HEAD_DIM
    outs = [_sb_finish(acc_ref[hh], g_ref[0, :, hh * d:(hh + 1) * d]) for hh in range(nh)]
    o_ref[0] = jnp.concatenate(outs, axis=-1).astype(o_ref.dtype)


def _sb_prompt(q, kt, vt, norm_g):
    b, _, t, d = q.shape
    tq = SB_KEY_BLOCK
    nh = 4
    kv_spec = pl.BlockSpec((1, nh, d, t), lambda i, h, j: (i, h, 0, 0))
    return pl.pallas_call(
        functools.partial(_sb_prompt_kernel, tq=tq, nh=nh),
        grid=(b, SB_HEADS // nh, t // tq),
        in_specs=[pl.BlockSpec((1, nh, tq, d), lambda i, h, j: (i, h, j, 0)), kv_spec, kv_spec,
                  pl.BlockSpec((2 * tq, tq), lambda i, h, j: (0, 0)),
                  pl.BlockSpec((1, 1, nh * d), lambda i, h, j: (h, 0, 0))],
        out_specs=pl.BlockSpec((1, tq, nh * d), lambda i, h, j: (i, j, h)),
        out_shape=jax.ShapeDtypeStruct((b, t, SB_WIDTH), BF16),
        scratch_shapes=[pltpu.VMEM((nh, tq, d), F32), pltpu.VMEM((nh, tq, 1), F32)],
        compiler_params=_cparams(("parallel", "parallel", "arbitrary")),
        name="sb_prompt",
    )(q, kt, vt, _tri_matrix(), norm_g.reshape(SB_HEADS // nh, 1, nh * d))


def _sb_sample_kernel(q_ref, kt_ref, vt_ref, ktp_ref, vtp_ref, tri_ref, g_ref, o_ref, acc_ref, car_ref,
                      *, nh):
    kb = SB_KEY_BLOCK
    n_past = ktp_ref.shape[3] // kb
    t = q_ref.shape[2]
    qs = [q_ref[0, hh] for hh in range(nh)]
    acc_ref[...] = jnp.zeros_like(acc_ref)
    car_ref[...] = jnp.zeros_like(car_ref)
    _sb_block(qs, [kt_ref[0, hh] for hh in range(nh)], [vt_ref[0, hh] for hh in range(nh)],
              jnp.concatenate([tri_ref[:t, :t], tri_ref[:t, :t]], axis=0), acc_ref, car_ref, True)

    def body(j, _):
        s = pl.multiple_of((n_past - 1 - j) * kb, kb)
        _sb_block(qs, [ktp_ref[0, hh, :, pl.ds(s, kb)] for hh in range(nh)],
                  [vtp_ref[0, hh, :, pl.ds(s, kb)] for hh in range(nh)],
                  tri_ref[...], acc_ref, car_ref, False)
        return 0

    lax.fori_loop(0, n_past, body, 0)
    d = SB_HEAD_DIM
    outs = [_sb_finish(acc_ref[hh], g_ref[0, :, hh * d:(hh + 1) * d]) for hh in range(nh)]
    o_ref[0] = jnp.concatenate(outs, axis=-1).astype(o_ref.dtype)


def _sb_sample(q, kt, vt, kt_past, vt_past, norm_g):
    b, _, t, d = q.shape
    p = kt_past.shape[3]
    nh = 4
    kb = SB_KEY_BLOCK
    new_spec = pl.BlockSpec((1, nh, d, t), lambda i, h: (i, h, 0, 0))
    past_spec = pl.BlockSpec((1, nh, d, p), lambda i, h: (i, h, 0, 0))
    return pl.pallas_call(
        functools.partial(_sb_sample_kernel, nh=nh),
        grid=(b, SB_HEADS // nh),
        in_specs=[pl.BlockSpec((1, nh, t, d), lambda i, h: (i, h, 0, 0)), new_spec, new_spec,
                  past_spec, past_spec,
                  pl.BlockSpec((2 * kb, kb), lambda i, h: (0, 0)),
                  pl.BlockSpec((1, 1, nh * d), lambda i, h: (h, 0, 0))],
        out_specs=pl.BlockSpec((1, t, nh * d), lambda i, h: (i, 0, h)),
        out_shape=jax.ShapeDtypeStruct((b, t, SB_WIDTH), BF16),
        scratch_shapes=[pltpu.VMEM((nh, t, d), F32), pltpu.VMEM((nh, t, 1), F32)],
        compiler_params=_cparams(("parallel", "arbitrary")),
        name="sb_sample",
    )(q, kt, vt, kt_past, vt_past, _tri_matrix(), norm_g.reshape(SB_HEADS // nh, 1, nh * d))


N_PAIRS = 6
N_BUCKETS = N_GROUPS * N_PAIRS
MOE_TM = 256
XW = D_MODEL + LANES
_PAIRS = [(a, b) for a in range(EXPERTS_PER_GROUP) for b in range(a + 1, EXPERTS_PER_GROUP)]
BUCKET_E0 = np.array([g * EXPERTS_PER_GROUP + a for g in range(N_GROUPS) for a, _ in _PAIRS], np.int32)
BUCKET_E1 = np.array([g * EXPERTS_PER_GROUP + b for g in range(N_GROUPS) for _, b in _PAIRS], np.int32)


def _route(logits):
    lane = lax.broadcasted_iota(jnp.int32, logits.shape, 1)
    neg = -jnp.inf
    big = jnp.int32(2 * LANES)
    gl = jnp.where(lane < N_GROUPS, logits, neg)
    gmax = jnp.max(gl, axis=-1, keepdims=True)
    g_idx = jnp.min(jnp.where(gl == gmax, lane, big), axis=-1, keepdims=True)
    g_p = 1.0 / jnp.sum(jnp.exp(gl - gmax), axis=-1, keepdims=True)
    lo = N_GROUPS + g_idx * EXPERTS_PER_GROUP
    el = jnp.where((lane >= lo) & (lane < lo + EXPERTS_PER_GROUP), logits, neg)
    v1 = jnp.max(el, axis=-1, keepdims=True)
    i1 = jnp.min(jnp.where(el == v1, lane, big), axis=-1, keepdims=True)
    el2 = jnp.where(lane == i1, neg, el)
    v2 = jnp.max(el2, axis=-1, keepdims=True)
    i2 = jnp.min(jnp.where(el2 == v2, lane, big), axis=-1, keepdims=True)
    e21 = jnp.exp(v2 - v1)
    p1 = 1.0 / (1.0 + e21)
    p2 = e21 * p1
    first_lo = i1 < i2
    w_lo = jnp.where(first_lo, p1, p2) * g_p
    w_hi = jnp.where(first_lo, p2, p1) * g_p
    a = jnp.minimum(i1, i2) - lo
    b = jnp.maximum(i1, i2) - lo
    pair = jnp.where(a == 0, b - 1, jnp.where(a == 1, b + 1, 5))
    return g_idx * N_PAIRS + pair, w_lo, w_hi


def _out_proj_kernel(x_ref, mr_ref, ms_ref, wo_ref, mm_ref, mf_ref, lg_ref, lb_ref, wr_ref, br_ref,
                     tril_ref, x1_ref, hx_ref, rt_ref, cnt_ref, run_ref):
    @pl.when((pl.program_id(0) == 0) & (pl.program_id(1) == 0))
    def _():
        run_ref[...] = jnp.zeros_like(run_ref)

    mix = _dot(mr_ref[0], wo_ref[:MIX_HALF]) + _dot(ms_ref[0], wo_ref[MIX_HALF:])
    gate = mm_ref[0][:, 2 * D_MODEL:]
    x1 = _ln_plain(DEEPNORM_ALPHA * x_ref[0] + gate * mix) * lg_ref[...] + lb_ref[...]
    x1_ref[0] = x1
    mf = mf_ref[0]
    h2 = _ln_plain(x1) * (1.0 + mf[:, D_MODEL:2 * D_MODEL]) + mf[:, :D_MODEL]
    bucket, w_lo, w_hi = _route(_dot(h2.astype(BF16), wr_ref[...]) + br_ref[...])
    tm = h2.shape[0]
    lane = lax.broadcasted_iota(jnp.int32, (tm, LANES), 1)
    hx_ref[0, :, :D_MODEL] = h2
    hx_ref[0, :, D_MODEL:] = jnp.where(lane == 0, w_lo, jnp.where(lane == 1, w_hi, 0.0))
    hit = lane == bucket
    onehot = hit.astype(BF16)
    before = _dot(tril_ref[...], onehot)
    run = run_ref[...]
    rank = jnp.sum(jnp.where(hit, before + run, 0.0), axis=-1, keepdims=True)
    run = run + before[tm - 1:tm] + onehot[tm - 1:tm].astype(F32)
    run_ref[...] = run
    cnt_ref[...] = run
    rt_ref[0] = jnp.where(lane == 0, bucket.astype(F32), jnp.where(lane == 1, rank, 0.0))


MIX_HALF = RET_WIDTH


def _out_proj(x, mr, ms, w_out_bf, mod_mix, mod_ffn, ln_g, ln_b, w_route_bf, b_route, tm):
    b, t, _ = x.shape
    tok = lambda w: pl.BlockSpec((1, tm, w), lambda i, j: (i, j, 0))
    modspec = pl.BlockSpec((1, 1, 3 * D_MODEL), lambda i, j: (i, 0, 0))
    row = lambda w: pl.BlockSpec((1, w), lambda i, j: (0, 0))
    idx = np.arange(tm)
    tril = jnp.asarray(idx[:, None] > idx[None, :], dtype=BF16)
    return pl.pallas_call(
        _out_proj_kernel,
        grid=(b, t // tm),
        in_specs=[tok(D_MODEL), tok(RET_WIDTH), tok(SB_WIDTH),
                  pl.BlockSpec((D_MODEL, D_MODEL), lambda i, j: (0, 0)),
                  modspec, modspec, row(D_MODEL), row(D_MODEL),
                  pl.BlockSpec((D_MODEL, LANES), lambda i, j: (0, 0)), row(LANES),
                  pl.BlockSpec((tm, tm), lambda i, j: (0, 0))],
        out_specs=[tok(D_MODEL), tok(XW), tok(LANES), row(LANES)],
        out_shape=[jax.ShapeDtypeStruct((b, t, D_MODEL), F32),
                   jax.ShapeDtypeStruct((b, t, XW), F32),
                   jax.ShapeDtypeStruct((b, t, LANES), F32),
                   jax.ShapeDtypeStruct((1, LANES), F32)],
        scratch_shapes=[pltpu.VMEM((1, LANES), F32)],
        compiler_params=_cparams(("arbitrary", "arbitrary")),
        name="out_proj",
    )(x, mr, ms, w_out_bf, mod_mix.reshape(b, 1, -1), mod_ffn.reshape(b, 1, -1),
      ln_g.reshape(1, -1), ln_b.reshape(1, -1), w_route_bf, b_route, tril)


def _route_plan(route, counts, n_tiles):
    n = route.shape[0] * route.shape[1]
    bucket = route[..., 0].astype(jnp.int32).reshape(n)
    rank = route[..., 1].astype(jnp.int32).reshape(n)
    cnt = counts[0, :N_BUCKETS].astype(jnp.int32)
    padded = ((cnt + MOE_TM - 1) // MOE_TM) * MOE_TM
    ends = jnp.cumsum(padded)
    starts = ends - padded
    n_used = ends[-1] // MOE_TM
    tile = jnp.arange(n_tiles, dtype=jnp.int32)
    last = jnp.maximum(n_used - 1, 0)
    tile_idx = jnp.minimum(tile, last)
    tile_bucket = jnp.sum((ends[None, :] <= (tile_idx * MOE_TM)[:, None]).astype(jnp.int32), axis=1)
    tile_bucket = jnp.minimum(tile_bucket, N_BUCKETS - 1)
    e0 = jnp.asarray(BUCKET_E0)[tile_bucket]
    e1 = jnp.asarray(BUCKET_E1)[tile_bucket]
    return bucket, rank, starts.astype(jnp.int32), e0, e1, tile_idx, n_used.reshape(1).astype(jnp.int32)


def _row_copy(src_ref, s, dst_ref, d, sem):
    return pltpu.make_async_copy(src_ref.at[pl.ds(s, 1)], dst_ref.at[pl.ds(d, 1)], sem)


def _scatter_kernel(bkt_ref, rnk_ref, st_ref, src_ref, zin_ref, out_ref, sem, *, rows):
    del zin_ref
    i = pl.program_id(0)
    slot = i % 2

    def issue(r, _):
        tok = i * rows + r
        _row_copy(src_ref, tok, out_ref, st_ref[bkt_ref[tok]] + rnk_ref[tok], sem.at[slot]).start()
        return 0

    lax.fori_loop(0, rows, issue, 0)

    def drain(s):
        def wait_one(r, _):
            _row_copy(src_ref, 0, out_ref, 0, sem.at[s]).wait()
            return 0
        lax.fori_loop(0, rows, wait_one, 0)

    @pl.when(i > 0)
    def _():
        drain(1 - slot)

    @pl.when(i == pl.num_programs(0) - 1)
    def _():
        drain(slot)


def _scatter_rows(hx, bucket, rank, starts, n_rows):
    n, w = hx.shape
    rows = min(256, n)
    any_spec = pl.BlockSpec(memory_space=pl.ANY)
    return pl.pallas_call(
        functools.partial(_scatter_kernel, rows=rows),
        grid_spec=pltpu.PrefetchScalarGridSpec(
            num_scalar_prefetch=3, grid=(n // rows,),
            in_specs=[any_spec, any_spec], out_specs=any_spec,
            scratch_shapes=[pltpu.SemaphoreType.DMA((2,))]),
        out_shape=jax.ShapeDtypeStruct((n_rows, w), F32),
        input_output_aliases={4: 0},
        compiler_params=_cparams(("arbitrary",)),
        name="scatter_rows",
    )(bucket, rank, starts, hx, jnp.zeros((n_rows, w), F32))


def _moe_kernel(e0_ref, e1_ref, ti_ref, nu_ref, x_ref, wg0, wu0, wd0, wg1, wu1, wd1, y_ref):
    del e0_ref, e1_ref, ti_ref

    @pl.when(pl.program_id(0) < nu_ref[0])
    def _():
        x = x_ref[...]
        h = x[:, :D_MODEL].astype(BF16)
        wx = x[:, D_MODEL:]
        lane = lax.broadcasted_iota(jnp.int32, wx.shape, 1)
        w_lo = jnp.sum(jnp.where(lane == 0, wx, 0.0), axis=-1, keepdims=True)
        w_hi = jnp.sum(jnp.where(lane == 1, wx, 0.0), axis=-1, keepdims=True)

        def expert(wg, wu, wd):
            a = _silu(_dot(h, wg[0])) * _dot(h, wu[0])
            return _dot(a.astype(BF16), wd[0])

        y_ref[...] = w_lo * expert(wg0, wu0, wd0) + w_hi * expert(wg1, wu1, wd1)

    @pl.when(pl.program_id(0) >= nu_ref[0])
    def _():
        y_ref[...] = jnp.zeros_like(y_ref)


def _moe(xs, e0, e1, tile_idx, n_used, wg_bf, wu_bf, wd_bf):
    n_rows = xs.shape[0]
    up = lambda sel: pl.BlockSpec((1, D_MODEL, D_EXPERT), lambda t, e0, e1, ti, nu: ((e0, e1)[sel][t], 0, 0))
    down = lambda sel: pl.BlockSpec((1, D_EXPERT, D_MODEL), lambda t, e0, e1, ti, nu: ((e0, e1)[sel][t], 0, 0))
    return pl.pallas_call(
        _moe_kernel,
        grid_spec=pltpu.PrefetchScalarGridSpec(
            num_scalar_prefetch=4, grid=(n_rows // MOE_TM,),
            in_specs=[pl.BlockSpec((MOE_TM, XW), lambda t, e0, e1, ti, nu: (ti[t], 0)),
                      up(0), up(0), down(0), up(1), up(1), down(1)],
            out_specs=pl.BlockSpec((MOE_TM, D_MODEL), lambda t, e0, e1, ti, nu: (t, 0))),
        out_shape=jax.ShapeDtypeStruct((n_rows, D_MODEL), F32),
        compiler_params=_cparams(("arbitrary",)),
        name="moe_routed",
    )(e0, e1, tile_idx, n_used, xs, wg_bf, wu_bf, wd_bf, wg_bf, wu_bf, wd_bf)


def _final_kernel(bkt_ref, rnk_ref, st_ref, x1_ref, ys_ref, mf_ref, lg_ref, lb_ref, o_ref, buf, sem, *, tm):
    nj = pl.num_programs(1)
    step = pl.program_id(0) * nj + pl.program_id(1)
    n_steps = pl.num_programs(0) * nj
    slot = step % 2

    def fetch(s, sl):
        def issue(r, _):
            tok = s * tm + r
            _row_copy(ys_ref, st_ref[bkt_ref[tok]] + rnk_ref[tok], buf.at[sl], r, sem.at[sl]).start()
            return 0
        lax.fori_loop(0, tm, issue, 0)

    @pl.when(step == 0)
    def _():
        fetch(step, slot)

    @pl.when(step + 1 < n_steps)
    def _():
        fetch(step + 1, 1 - slot)

    def wait_one(r, _):
        _row_copy(ys_ref, 0, buf.at[slot], 0, sem.at[slot]).wait()
        return 0

    lax.fori_loop(0, tm, wait_one, 0)
    gate = mf_ref[0][:, 2 * D_MODEL:]
    o_ref[0] = _ln_plain(DEEPNORM_ALPHA * x1_ref[0] + gate * buf[slot]) * lg_ref[...] + lb_ref[...]


def _final(x1, ys, bucket, rank, starts, mod_ffn, ln_g, ln_b, tm):
    b, t, _ = x1.shape
    tok = pl.BlockSpec((1, tm, D_MODEL), lambda i, j, *_: (i, j, 0))
    row = pl.BlockSpec((1, D_MODEL), lambda i, j, *_: (0, 0))
    return pl.pallas_call(
        functools.partial(_final_kernel, tm=tm),
        grid_spec=pltpu.PrefetchScalarGridSpec(
            num_scalar_prefetch=3, grid=(b, t // tm),
            in_specs=[tok, pl.BlockSpec(memory_space=pl.ANY),
                      pl.BlockSpec((1, 1, 3 * D_MODEL), lambda i, j, *_: (i, 0, 0)), row, row],
            out_specs=tok,
            scratch_shapes=[pltpu.VMEM((2, tm, D_MODEL), F32), pltpu.SemaphoreType.DMA((2,))]),
        out_shape=jax.ShapeDtypeStruct((b, t, D_MODEL), F32),
        compiler_params=_cparams(("arbitrary", "arbitrary")),
        name="final_ln",
    )(bucket, rank, starts, x1, ys, mod_ffn.reshape(b, 1, -1), ln_g.reshape(1, -1), ln_b.reshape(1, -1))


def _trunk(x, mod_mix, mod_ffn, pos0, state0, k_past, v_past, wts):
    b, t, _ = x.shape
    tm = min(256, t)
    ret_in, sq, skt, svt = _in_proj(x, mod_mix, wts["w_in_a"], wts["w_in_kvt"], pos0, tm)
    mr, state = _retention(ret_in, state0, wts["ret_norm_g"], min(256, t))
    if k_past is None:
        ms = _sb_prompt(sq, skt, svt, wts["sb_norm_g"])
    else:
        ms = _sb_sample(sq, skt, svt, jnp.swapaxes(k_past, 2, 3), jnp.swapaxes(v_past, 2, 3),
                        wts["sb_norm_g"])
    sk, sv = jnp.swapaxes(skt, 2, 3), jnp.swapaxes(svt, 2, 3)
    x1, hx, route, counts = _out_proj(x, mr, ms, wts["w_out"], mod_mix, mod_ffn, wts["ln_mix_g"],
                                      wts["ln_mix_b"], wts["w_route"], wts["b_route"], tm)
    n = b * t
    n_tiles = n // MOE_TM + N_BUCKETS
    bucket, rank, starts, e0, e1, tile_idx, n_used = _route_plan(route, counts, n_tiles)
    xs = _scatter_rows(hx.reshape(n, XW), bucket, rank, starts, n_tiles * MOE_TM)
    ys = _moe(xs, e0, e1, tile_idx, n_used, wts["w_e_gate"], wts["w_e_up"], wts["w_e_down"])
    out = _final(x1, ys, bucket, rank, starts, mod_ffn, wts["ln_ffn_g"], wts["ln_ffn_b"], tm)
    return out, sk[None], sv[None], state[None]


def kernel(x_prompt, x_sample, cache_sb_k, cache_sb_v, state_ret, c_prompt, c_sample, w_in, w_out, ret_norm_g, sb_norm_g, w_ada_mix, b_ada_mix, ln_mix_g, ln_mix_b, w_ada_ffn, b_ada_ffn, ln_ffn_g, ln_ffn_b, w_group, b_group, w_router, b_router, w_e_gate, w_e_up, w_e_down):
    bp = x_prompt.shape[0]
    c_all = jnp.concatenate([c_prompt, c_sample], axis=0)
    mod_mix = _ada(c_all, w_ada_mix[0], b_ada_mix[0])
    mod_ffn = _ada(c_all, w_ada_ffn[0], b_ada_ffn[0])
    pad = LANES - N_GROUPS - N_EXPERTS
    w_route = jnp.concatenate([w_group[0], w_router[0], jnp.zeros((D_MODEL, pad), F32)], axis=1)
    b_route = jnp.concatenate([b_group[0], b_router[0], jnp.zeros((pad,), F32)]).reshape(1, LANES)
    n_a = 4 * RET_WIDTH + SB_WIDTH
    wts = dict(w_in_a=w_in[0, :, :n_a].astype(BF16), w_in_kvt=w_in[0, :, n_a:].T.astype(BF16),
               w_out=w_out[0].astype(BF16),
               ret_norm_g=ret_norm_g[0], sb_norm_g=sb_norm_g[0],
               ln_mix_g=ln_mix_g[0], ln_mix_b=ln_mix_b[0], ln_ffn_g=ln_ffn_g[0], ln_ffn_b=ln_ffn_b[0],
               w_route=w_route.astype(BF16), b_route=b_route,
               w_e_gate=w_e_gate[0].astype(BF16), w_e_up=w_e_up[0].astype(BF16),
               w_e_down=w_e_down[0].astype(BF16))
    ret_zero = jnp.zeros((bp, RET_HEADS, RET_DK, RET_DV), F32)
    y_p, k_p, v_p, r_p = _trunk(x_prompt, mod_mix[:bp], mod_ffn[:bp], 0, ret_zero, None, None, wts)
    y_s, k_s, v_s, r_s = _trunk(x_sample, mod_mix[bp:], mod_ffn[bp:], cache_sb_k.shape[3], state_ret[0],
                                cache_sb_k[0], cache_sb_v[0], wts)
    return (y_p, y_s, k_p, v_p, r_p, k_s, v_s, r_s)
```

```python
import functools
import math

import numpy as np
import jax
import jax.numpy as jnp
from jax import lax
from jax.experimental import pallas as pl
from jax.experimental.pallas import tpu as pltpu

D_MODEL = 1024
RET_HEADS = 4
RET_DK = 128
RET_DV = 128
RET_WIDTH = RET_HEADS * RET_DV
SB_HEADS = 8
SB_HEAD_DIM = 64
SB_WIDTH = SB_HEADS * SB_HEAD_DIM
IN_WIDTH = 2 * RET_HEADS * RET_DK + 2 * RET_WIDTH + 3 * SB_WIDTH
ROPE_BASE = 10000.0
N_GROUPS = 4
EXPERTS_PER_GROUP = 4
N_EXPERTS = N_GROUPS * EXPERTS_PER_GROUP
D_EXPERT = 512
DEPTH = 1
DEEPNORM_ALPHA = (2.0 * DEPTH) ** 0.25
LN_EPS = 1e-5

LANES = 128
VMEM_LIMIT = 48 * 1024 * 1024

F32 = jnp.float32
BF16 = jnp.bfloat16


def _cparams(sem):
    return pltpu.CompilerParams(dimension_semantics=sem, vmem_limit_bytes=VMEM_LIMIT)


def _dot(a, b):
    return jnp.dot(a, b, preferred_element_type=F32)


def _dot_nt(a, b):
    return lax.dot_general(a, b, (((1,), (1,)), ((), ())), preferred_element_type=F32)


def _dot_tn(a, b):
    return lax.dot_general(a, b, (((0,), (0,)), ((), ())), preferred_element_type=F32)


def _split_dot(a, w_hi, w_lo):
    a_hi = a.astype(BF16)
    a_lo = (a - a_hi.astype(F32)).astype(BF16)
    return _dot(a_hi, w_hi) + (_dot(a_hi, w_lo) + _dot(a_lo, w_hi))


def _ln_plain(x):
    mu = jnp.mean(x, axis=-1, keepdims=True)
    xc = x - mu
    var = jnp.mean(xc * xc, axis=-1, keepdims=True)
    return xc * lax.rsqrt(var + LN_EPS)


def _silu(x):
    return x * (1.0 / (1.0 + jnp.exp(-x)))


def _ada_kernel(c_ref, w_ref, b_ref, o_ref):
    c = c_ref[...]
    w = w_ref[...]
    w_hi = w.astype(BF16)
    w_lo = (w - w_hi.astype(F32)).astype(BF16)
    o_ref[...] = _split_dot(_silu(c), w_hi, w_lo) + b_ref[...]


def _ada(c, w, b):
    r = c.shape[0]
    tn = 768
    return pl.pallas_call(
        _ada_kernel,
        grid=(3 * D_MODEL // tn,),
        in_specs=[pl.BlockSpec((r, D_MODEL), lambda j: (0, 0)),
                  pl.BlockSpec((D_MODEL, tn), lambda j: (0, j)),
                  pl.BlockSpec((1, tn), lambda j: (0, j))],
        out_specs=pl.BlockSpec((r, tn), lambda j: (0, j)),
        out_shape=jax.ShapeDtypeStruct((r, 3 * D_MODEL), F32),
        compiler_params=_cparams(("arbitrary",)),
        name="ada_mod",
    )(c, w, b.reshape(1, -1))


def _in_proj_kernel(x_ref, mod_ref, w_ref, wkv_ref, tab_ref, ret_ref, q_ref, k_ref, v_ref):
    m = mod_ref[0]
    h = _ln_plain(x_ref[0]) * (1.0 + m[:, D_MODEL:2 * D_MODEL]) + m[:, :D_MODEL]
    h = h.astype(BF16)
    tab = tab_ref[...]
    for c in range(4):
        p = _dot(h, w_ref[:, c * RET_WIDTH:(c + 1) * RET_WIDTH])
        if c < 2:
            cs = tab[:, (2 * c) * LANES:(2 * c + 1) * LANES]
            sn = tab[:, (2 * c + 1) * LANES:(2 * c + 2) * LANES]
            for hh in range(RET_HEADS):
                ph = p[:, hh * RET_DK:(hh + 1) * RET_DK]
                ret_ref[0, :, c * RET_WIDTH + hh * RET_DK:c * RET_WIDTH + (hh + 1) * RET_DK] = (
                    ph * cs + pltpu.roll(ph, RET_DK // 2, 1) * sn)
        else:
            ret_ref[0, :, c * RET_WIDTH:(c + 1) * RET_WIDTH] = p
    base = 4 * RET_WIDTH
    p = _dot(h, w_ref[:, base:base + SB_WIDTH]) * (LOG2E * SB_HEAD_DIM ** -0.5)
    for hh in range(SB_HEADS):
        q_ref[0, hh] = p[:, hh * SB_HEAD_DIM:(hh + 1) * SB_HEAD_DIM].astype(q_ref.dtype)
    pt = _dot_nt(wkv_ref[...], h)
    for c, ref in enumerate((k_ref, v_ref)):
        for hh in range(SB_HEADS):
            r0 = c * SB_WIDTH + hh * SB_HEAD_DIM
            ref[0, hh] = pt[r0:r0 + SB_HEAD_DIM, :]


def _rope_table(pos0, t):
    half = RET_DK // 2
    inv = ROPE_BASE ** (-np.arange(half, dtype=np.float64) / half)
    ang = (pos0 + np.arange(t, dtype=np.float64))[:, None] * inv[None, :]
    cos, sin = np.cos(ang), np.sin(ang)
    cs = np.concatenate([cos, cos], axis=1)
    sn = np.concatenate([-sin, sin], axis=1)
    ks = RET_DK ** -0.5
    return jnp.asarray(np.concatenate([cs, sn, cs * ks, sn * ks], axis=1), dtype=F32)


def _in_proj(x, mod, w_a_bf, w_kvt_bf, pos0, tm):
    b, t, _ = x.shape
    tab = _rope_table(pos0, t)
    wa = w_a_bf.shape[1]
    hs = jax.ShapeDtypeStruct((b, SB_HEADS, SB_HEAD_DIM, t), F32)
    q_spec = pl.BlockSpec((1, SB_HEADS, tm, SB_HEAD_DIM), lambda i, j: (i, 0, j, 0))
    kv_spec = pl.BlockSpec((1, SB_HEADS, SB_HEAD_DIM, tm), lambda i, j: (i, 0, 0, j))
    return pl.pallas_call(
        _in_proj_kernel,
        grid=(b, t // tm),
        in_specs=[pl.BlockSpec((1, tm, D_MODEL), lambda i, j: (i, j, 0)),
                  pl.BlockSpec((1, 1, 3 * D_MODEL), lambda i, j: (i, 0, 0)),
                  pl.BlockSpec((D_MODEL, wa), lambda i, j: (0, 0)),
                  pl.BlockSpec((2 * SB_WIDTH, D_MODEL), lambda i, j: (0, 0)),
                  pl.BlockSpec((tm, 4 * LANES), lambda i, j: (j, 0))],
        out_specs=[pl.BlockSpec((1, tm, 4 * RET_WIDTH), lambda i, j: (i, j, 0)),
                   q_spec, kv_spec, kv_spec],
        out_shape=[jax.ShapeDtypeStruct((b, t, 4 * RET_WIDTH), F32),
                   jax.ShapeDtypeStruct((b, SB_HEADS, t, SB_HEAD_DIM), BF16), hs, hs],
        compiler_params=_cparams(("parallel", "arbitrary")),
        name="in_proj",
    )(x, mod.reshape(b, 1, -1), w_a_bf, w_kvt_bf, tab)


def _ret_kernel(q_ref, k_ref, v_ref, g_ref, s0_ref, dec_ref, qd_ref, kd_ref, ng_ref,
                o_ref, so_ref, st_ref, *, chunk_decay):
    j = pl.program_id(1)

    @pl.when(j == 0)
    def _():
        st_ref[...] = s0_ref[0]

    for hh in range(RET_HEADS):
        sl = slice(hh * RET_DK, (hh + 1) * RET_DK)
        q = q_ref[0, :, sl]
        k = k_ref[0, :, sl]
        v = v_ref[0, :, sl]
        vb = v.astype(BF16)
        st = st_ref[hh]
        scores = _dot_nt(q.astype(BF16), k.astype(BF16)) * dec_ref[hh]
        o = _dot(scores.astype(BF16), vb) + _dot((q * qd_ref[hh]).astype(BF16), st.astype(BF16))
        st_ref[hh] = st * chunk_decay[hh] + _dot_tn((k * kd_ref[hh]).astype(BF16), vb)
        o = _ln_plain(o) * ng_ref[:, sl] * _silu(g_ref[0, :, sl])
        o_ref[0, :, sl] = o.astype(o_ref.dtype)

    @pl.when(j == pl.num_programs(1) - 1)
    def _():
        so_ref[0] = st_ref[...]


def _retention(ret_in, state0, norm_g, chunk):
    b, t, _ = ret_in.shape
    lg = np.log1p(-np.exp2(-5.0 - np.arange(RET_HEADS, dtype=np.float64)))
    idx = np.arange(chunk, dtype=np.float64)
    rel = idx[:, None] - idx[None, :]
    dec = np.where(rel >= 0, np.exp(lg[:, None, None] * np.maximum(rel, 0.0)), 0.0)
    qd = np.broadcast_to(np.exp(lg[:, None] * (idx + 1.0))[:, :, None], (RET_HEADS, chunk, RET_DK))
    kd = np.broadcast_to(np.exp(lg[:, None] * (chunk - 1.0 - idx))[:, :, None], (RET_HEADS, chunk, RET_DK))
    chunk_decay = tuple(float(v) for v in np.exp(lg * chunk))

    def col(c):
        return pl.BlockSpec((1, chunk, RET_WIDTH), lambda i, j, c=c: (i, j, c))

    const3 = lambda shape: pl.BlockSpec(shape, lambda i, j: (0, 0, 0))
    state_spec = pl.BlockSpec((1, RET_HEADS, RET_DK, RET_DV), lambda i, j: (i, 0, 0, 0))
    return pl.pallas_call(
        functools.partial(_ret_kernel, chunk_decay=chunk_decay),
        grid=(b, t // chunk),
        in_specs=[col(0), col(1), col(2), col(3), state_spec,
                  const3((RET_HEADS, chunk, chunk)), const3((RET_HEADS, chunk, RET_DK)),
                  const3((RET_HEADS, chunk, RET_DK)),
                  pl.BlockSpec((1, RET_WIDTH), lambda i, j: (0, 0))],
        out_specs=[pl.BlockSpec((1, chunk, RET_WIDTH), lambda i, j: (i, j, 0)), state_spec],
        out_shape=[jax.ShapeDtypeStruct((b, t, RET_WIDTH), BF16),
                   jax.ShapeDtypeStruct((b, RET_HEADS, RET_DK, RET_DV), F32)],
        scratch_shapes=[pltpu.VMEM((RET_HEADS, RET_DK, RET_DV), F32)],
        compiler_params=_cparams(("parallel", "arbitrary")),
        name="retention",
    )(ret_in, ret_in, ret_in, ret_in, state0,
      jnp.asarray(dec, F32), jnp.asarray(qd, F32), jnp.asarray(kd, F32), norm_g.reshape(1, -1))


SB_KEY_BLOCK = 256


LOG2E = 1.4426950408889634


def _sb_block(qs, kts, vts, tri2, acc_ref, car_ref, masked):
    n = len(qs)
    tq, kb = qs[0].shape[0], kts[0].shape[1]
    if masked:
        valid = (lax.broadcasted_iota(jnp.int32, (tq, kb), 1)
                 < lax.broadcasted_iota(jnp.int32, (tq, kb), 0))
    zs, parts = [], []
    for q, kt in zip(qs, kts):
        z = _dot(q, kt.astype(BF16))
        p = jnp.maximum(z, 0.0) + jnp.log2(1.0 + jnp.exp2(-jnp.abs(z)))
        if masked:
            p = jnp.where(valid, p, 0.0)
        hi = p.astype(BF16)
        zs.append(z)
        parts.append(jnp.concatenate([hi, (p - hi.astype(F32)).astype(BF16)], axis=1))
    r = _dot(jnp.concatenate(parts, axis=0), tri2)
    for c in range(n):
        incl = r[c * tq:(c + 1) * tq]
        car = car_ref[c]
        w = jnp.exp2(zs[c] + incl + car)
        if masked:
            w = jnp.where(valid, w, 0.0)
        acc_ref[c] += _dot_nt(w.astype(BF16), vts[c].astype(BF16))
        car_ref[c] = car + incl[:, 0:1]


def _sb_finish(o, g):
    return o * lax.rsqrt(jnp.mean(o * o, axis=-1, keepdims=True) + LN_EPS) * g


def _tri_matrix():
    idx = np.arange(SB_KEY_BLOCK)
    t = -(idx[:, None] >= idx[None, :]).astype(np.float32)
    return jnp.asarray(np.concatenate([t, t], axis=0), dtype=BF16)


def _sb_prompt_kernel(q_ref, kt_ref, vt_ref, tri_ref, g_ref, o_ref, acc_ref, car_ref, *, tq, nh):
    i = pl.program_id(2)
    qs = [q_ref[0, hh] for hh in range(nh)]
    acc_ref[...] = jnp.zeros_like(acc_ref)
    car_ref[...] = jnp.zeros_like(car_ref)

    def block(start, masked):
        kts = [kt_ref[0, hh, :, pl.ds(start, tq)] for hh in range(nh)]
        vts = [vt_ref[0, hh, :, pl.ds(start, tq)] for hh in range(nh)]
        _sb_block(qs, kts, vts, tri_ref[...], acc_ref, car_ref, masked)

    block(pl.multiple_of(i * tq, tq), True)

    def body(t, _):
        block(pl.multiple_of((i - 1 - t) * tq, tq), False)
        return 0

    lax.fori_loop(0, i, body, 0)
    d = SB_HEAD_DIM
    outs = [_sb_finish(acc_ref[hh], g_ref[0, :, hh * d:(hh + 1) * d]) for hh in range(nh)]
    o_ref[0] = jnp.concatenate(outs, axis=-1).astype(o_ref.dtype)


def _sb_prompt(q, kt, vt, norm_g):
    b, _, t, d = q.shape
    tq = SB_KEY_BLOCK
    nh = 4
    kv_spec = pl.BlockSpec((1, nh, d, t), lambda i, h, j: (i, h, 0, 0))
    return pl.pallas_call(
        functools.partial(_sb_prompt_kernel, tq=tq, nh=nh),
        grid=(b, SB_HEADS // nh, t // tq),
        in_specs=[pl.BlockSpec((1, nh, tq, d), lambda i, h, j: (i, h, j, 0)), kv_spec, kv_spec,
                  pl.BlockSpec((2 * tq, tq), lambda i, h, j: (0, 0)),
                  pl.BlockSpec((1, 1, nh * d), lambda i, h, j: (h, 0, 0))],
        out_specs=pl.BlockSpec((1, tq, nh * d), lambda i, h, j: (i, j, h)),
        out_shape=jax.ShapeDtypeStruct((b, t, SB_WIDTH), BF16),
        scratch_shapes=[pltpu.VMEM((nh, tq, d), F32), pltpu.VMEM((nh, tq, 1), F32)],
        compiler_params=_cparams(("parallel", "parallel", "arbitrary")),
        name="sb_prompt",
    )(q, kt, vt, _tri_matrix(), norm_g.reshape(SB_HEADS // nh, 1, nh * d))


def _sb_sample_kernel(q_ref, kt_ref, vt_ref, ktp_ref, vtp_ref, tri_ref, g_ref, o_ref, acc_ref, car_ref,
                      *, nh):
    kb = SB_KEY_BLOCK
    n_past = ktp_ref.shape[3] // kb
    t = q_ref.shape[2]
    qs = [q_ref[0, hh] for hh in range(nh)]
    acc_ref[...] = jnp.zeros_like(acc_ref)
    car_ref[...] = jnp.zeros_like(car_ref)
    _sb_block(qs, [kt_ref[0, hh] for hh in range(nh)], [vt_ref[0, hh] for hh in range(nh)],
              jnp.concatenate([tri_ref[:t, :t], tri_ref[:t, :t]], axis=0), acc_ref, car_ref, True)

    def body(j, _):
        s = pl.multiple_of((n_past - 1 - j) * kb, kb)
        _sb_block(qs, [ktp_ref[0, hh, :, pl.ds(s, kb)] for hh in range(nh)],
                  [vtp_ref[0, hh, :, pl.ds(s, kb)] for hh in range(nh)],
                  tri_ref[...], acc_ref, car_ref, False)
        return 0

    lax.fori_loop(0, n_past, body, 0)
    d = SB_HEAD_DIM
    outs = [_sb_finish(acc_ref[hh], g_ref[0, :, hh * d:(hh + 1) * d]) for hh in range(nh)]
    o_ref[0] = jnp.concatenate(outs, axis=-1).astype(o_ref.dtype)


def _sb_sample(q, kt, vt, kt_past, vt_past, norm_g):
    b, _, t, d = q.shape
    p = kt_past.shape[3]
    nh = 4
    kb = SB_KEY_BLOCK
    new_spec = pl.BlockSpec((1, nh, d, t), lambda i, h: (i, h, 0, 0))
    past_spec = pl.BlockSpec((1, nh, d, p), lambda i, h: (i, h, 0, 0))
    return pl.pallas_call(
        functools.partial(_sb_sample_kernel, nh=nh),
        grid=(b, SB_HEADS // nh),
        in_specs=[pl.BlockSpec((1, nh, t, d), lambda i, h: (i, h, 0, 0)), new_spec, new_spec,
                  past_spec, past_spec,
                  pl.BlockSpec((2 * kb, kb), lambda i, h: (0, 0)),
                  pl.BlockSpec((1, 1, nh * d), lambda i, h: (h, 0, 0))],
        out_specs=pl.BlockSpec((1, t, nh * d), lambda i, h: (i, 0, h)),
        out_shape=jax.ShapeDtypeStruct((b, t, SB_WIDTH), BF16),
        scratch_shapes=[pltpu.VMEM((nh, t, d), F32), pltpu.VMEM((nh, t, 1), F32)],
        compiler_params=_cparams(("parallel", "arbitrary")),
        name="sb_sample",
    )(q, kt, vt, kt_past, vt_past, _tri_matrix(), norm_g.reshape(SB_HEADS // nh, 1, nh * d))


N_PAIRS = 6
N_BUCKETS = N_GROUPS * N_PAIRS
MOE_TM = 256
XW = D_MODEL + LANES
_PAIRS = [(a, b) for a in range(EXPERTS_PER_GROUP) for b in range(a + 1, EXPERTS_PER_GROUP)]
BUCKET_E0 = np.array([g * EXPERTS_PER_GROUP + a for g in range(N_GROUPS) for a, _ in _PAIRS], np.int32)
BUCKET_E1 = np.array([g * EXPERTS_PER_GROUP + b for g in range(N_GROUPS) for _, b in _PAIRS], np.int32)


def _route(logits):
    lane = lax.broadcasted_iota(jnp.int32, logits.shape, 1)
    neg = -jnp.inf
    big = jnp.int32(2 * LANES)
    gl = jnp.where(lane < N_GROUPS, logits, neg)
    gmax = jnp.max(gl, axis=-1, keepdims=True)
    g_idx = jnp.min(jnp.where(gl == gmax, lane, big), axis=-1, keepdims=True)
    g_p = 1.0 / jnp.sum(jnp.exp(gl - gmax), axis=-1, keepdims=True)
    lo = N_GROUPS + g_idx * EXPERTS_PER_GROUP
    el = jnp.where((lane >= lo) & (lane < lo + EXPERTS_PER_GROUP), logits, neg)
    v1 = jnp.max(el, axis=-1, keepdims=True)
    i1 = jnp.min(jnp.where(el == v1, lane, big), axis=-1, keepdims=True)
    el2 = jnp.where(lane == i1, neg, el)
    v2 = jnp.max(el2, axis=-1, keepdims=True)
    i2 = jnp.min(jnp.where(el2 == v2, lane, big), axis=-1, keepdims=True)
    e21 = jnp.exp(v2 - v1)
    p1 = 1.0 / (1.0 + e21)
    p2 = e21 * p1
    first_lo = i1 < i2
    w_lo = jnp.where(first_lo, p1, p2) * g_p
    w_hi = jnp.where(first_lo, p2, p1) * g_p
    a = jnp.minimum(i1, i2) - lo
    b = jnp.maximum(i1, i2) - lo
    pair = jnp.where(a == 0, b - 1, jnp.where(a == 1, b + 1, 5))
    return g_idx * N_PAIRS + pair, w_lo, w_hi


def _out_proj_kernel(x_ref, mr_ref, ms_ref, wo_ref, mm_ref, mf_ref, lg_ref, lb_ref, wr_ref, br_ref,
                     tril_ref, x1_ref, hx_ref, rt_ref, cnt_ref, run_ref):
    @pl.when((pl.program_id(0) == 0) & (pl.program_id(1) == 0))
    def _():
        run_ref[...] = jnp.zeros_like(run_ref)

    mix = _dot(mr_ref[0], wo_ref[:MIX_HALF]) + _dot(ms_ref[0], wo_ref[MIX_HALF:])
    gate = mm_ref[0][:, 2 * D_MODEL:]
    x1 = _ln_plain(DEEPNORM_ALPHA * x_ref[0] + gate * mix) * lg_ref[...] + lb_ref[...]
    x1_ref[0] = x1
    mf = mf_ref[0]
    h2 = _ln_plain(x1) * (1.0 + mf[:, D_MODEL:2 * D_MODEL]) + mf[:, :D_MODEL]
    bucket, w_lo, w_hi = _route(_dot(h2.astype(BF16), wr_ref[...]) + br_ref[...])
    tm = h2.shape[0]
    lane = lax.broadcasted_iota(jnp.int32, (tm, LANES), 1)
    hx_ref[0, :, :D_MODEL] = h2
    hx_ref[0, :, D_MODEL:] = jnp.where(lane == 0, w_lo, jnp.where(lane == 1, w_hi, 0.0))
    hit = lane == bucket
    onehot = hit.astype(BF16)
    before = _dot(tril_ref[...], onehot)
    run = run_ref[...]
    rank = jnp.sum(jnp.where(hit, before + run, 0.0), axis=-1, keepdims=True)
    run = run + before[tm - 1:tm] + onehot[tm - 1:tm].astype(F32)
    run_ref[...] = run
    cnt_ref[...] = run
    rt_ref[0] = jnp.where(lane == 0, bucket.astype(F32), jnp.where(lane == 1, rank, 0.0))


MIX_HALF = RET_WIDTH


def _out_proj(x, mr, ms, w_out_bf, mod_mix, mod_ffn, ln_g, ln_b, w_route_bf, b_route, tm):
    b, t, _ = x.shape
    tok = lambda w: pl.BlockSpec((1, tm, w), lambda i, j: (i, j, 0))
    modspec = pl.BlockSpec((1, 1, 3 * D_MODEL), lambda i, j: (i, 0, 0))
    row = lambda w: pl.BlockSpec((1, w), lambda i, j: (0, 0))
    idx = np.arange(tm)
    tril = jnp.asarray(idx[:, None] > idx[None, :], dtype=BF16)
    return pl.pallas_call(
        _out_proj_kernel,
        grid=(b, t // tm),
        in_specs=[tok(D_MODEL), tok(RET_WIDTH), tok(SB_WIDTH),
                  pl.BlockSpec((D_MODEL, D_MODEL), lambda i, j: (0, 0)),
                  modspec, modspec, row(D_MODEL), row(D_MODEL),
                  pl.BlockSpec((D_MODEL, LANES), lambda i, j: (0, 0)), row(LANES),
                  pl.BlockSpec((tm, tm), lambda i, j: (0, 0))],
        out_specs=[tok(D_MODEL), tok(XW), tok(LANES), row(LANES)],
        out_shape=[jax.ShapeDtypeStruct((b, t, D_MODEL), F32),
                   jax.ShapeDtypeStruct((b, t, XW), F32),
                   jax.ShapeDtypeStruct((b, t, LANES), F32),
                   jax.ShapeDtypeStruct((1, LANES), F32)],
        scratch_shapes=[pltpu.VMEM((1, LANES), F32)],
        compiler_params=_cparams(("arbitrary", "arbitrary")),
        name="out_proj",
    )(x, mr, ms, w_out_bf, mod_mix.reshape(b, 1, -1), mod_ffn.reshape(b, 1, -1),
      ln_g.reshape(1, -1), ln_b.reshape(1, -1), w_route_bf, b_route, tril)


def _route_plan(route, counts, n_tiles):
    n = route.shape[0] * route.shape[1]
    bucket = route[..., 0].astype(jnp.int32).reshape(n)
    rank = route[..., 1].astype(jnp.int32).reshape(n)
    cnt = counts[0, :N_BUCKETS].astype(jnp.int32)
    padded = ((cnt + MOE_TM - 1) // MOE_TM) * MOE_TM
    ends = jnp.cumsum(padded)
    starts = ends - padded
    n_used = ends[-1] // MOE_TM
    tile = jnp.arange(n_tiles, dtype=jnp.int32)
    last = jnp.maximum(n_used - 1, 0)
    tile_idx = jnp.minimum(tile, last)
    tile_bucket = jnp.sum((ends[None, :] <= (tile_idx * MOE_TM)[:, None]).astype(jnp.int32), axis=1)
    tile_bucket = jnp.minimum(tile_bucket, N_BUCKETS - 1)
    e0 = jnp.asarray(BUCKET_E0)[tile_bucket]
    e1 = jnp.asarray(BUCKET_E1)[tile_bucket]
    in_bucket = bucket[:, None] == jnp.arange(N_BUCKETS, dtype=jnp.int32)[None, :]
    dest = rank + jnp.sum(jnp.where(in_bucket, starts[None, :], 0), axis=1)
    return dest.astype(jnp.int32), e0, e1, tile_idx, n_used.reshape(1).astype(jnp.int32)


def _row_copy(src_ref, s, dst_ref, d, sem):
    return pltpu.make_async_copy(src_ref.at[pl.ds(s, 1)], dst_ref.at[pl.ds(d, 1)], sem)


DMA_UNROLL = 8


def _scatter_kernel(dest_ref, src_ref, zin_ref, out_ref, sem, *, rows):
    del zin_ref
    base = pl.program_id(0) * rows

    def issue(r, _):
        _row_copy(src_ref, r, out_ref, dest_ref[base + r], sem).start()
        return 0

    lax.fori_loop(0, rows, issue, 0, unroll=DMA_UNROLL)
    pltpu.make_async_copy(src_ref, out_ref.at[pl.ds(0, rows)], sem).wait()


def _scatter_rows(hx, dest, n_rows):
    n, w = hx.shape
    rows = min(256, n)
    any_spec = pl.BlockSpec(memory_space=pl.ANY)
    return pl.pallas_call(
        functools.partial(_scatter_kernel, rows=rows),
        grid_spec=pltpu.PrefetchScalarGridSpec(
            num_scalar_prefetch=1, grid=(n // rows,),
            in_specs=[pl.BlockSpec((rows, w), lambda i, d: (i, 0)), any_spec], out_specs=any_spec,
            scratch_shapes=[pltpu.SemaphoreType.DMA(())]),
        out_shape=jax.ShapeDtypeStruct((n_rows, w), F32),
        input_output_aliases={2: 0},
        compiler_params=_cparams(("arbitrary",)),
        name="scatter_rows",
    )(dest, hx, jnp.zeros((n_rows, w), F32))


def _moe_kernel(e0_ref, e1_ref, ti_ref, nu_ref, x_ref, wg0, wu0, wd0, wg1, wu1, wd1, y_ref):
    del e0_ref, e1_ref, ti_ref

    @pl.when(pl.program_id(0) < nu_ref[0])
    def _():
        x = x_ref[...]
        h = x[:, :D_MODEL].astype(BF16)
        wx = x[:, D_MODEL:]
        lane = lax.broadcasted_iota(jnp.int32, wx.shape, 1)
        w_lo = jnp.sum(jnp.where(lane == 0, wx, 0.0), axis=-1, keepdims=True)
        w_hi = jnp.sum(jnp.where(lane == 1, wx, 0.0), axis=-1, keepdims=True)

        def expert(wg, wu, wd):
            a = _silu(_dot(h, wg[0])) * _dot(h, wu[0])
            return _dot(a.astype(BF16), wd[0])

        y_ref[...] = w_lo * expert(wg0, wu0, wd0) + w_hi * expert(wg1, wu1, wd1)

    @pl.when(pl.program_id(0) >= nu_ref[0])
    def _():
        y_ref[...] = jnp.zeros_like(y_ref)


def _moe(xs, e0, e1, tile_idx, n_used, wg_bf, wu_bf, wd_bf):
    n_rows = xs.shape[0]
    up = lambda sel: pl.BlockSpec((1, D_MODEL, D_EXPERT), lambda t, e0, e1, ti, nu: ((e0, e1)[sel][t], 0, 0))
    down = lambda sel: pl.BlockSpec((1, D_EXPERT, D_MODEL), lambda t, e0, e1, ti, nu: ((e0, e1)[sel][t], 0, 0))
    return pl.pallas_call(
        _moe_kernel,
        grid_spec=pltpu.PrefetchScalarGridSpec(
            num_scalar_prefetch=4, grid=(n_rows // MOE_TM,),
            in_specs=[pl.BlockSpec((MOE_TM, XW), lambda t, e0, e1, ti, nu: (ti[t], 0)),
                      up(0), up(0), down(0), up(1), up(1), down(1)],
            out_specs=pl.BlockSpec((MOE_TM, D_MODEL), lambda t, e0, e1, ti, nu: (t, 0))),
        out_shape=jax.ShapeDtypeStruct((n_rows, D_MODEL), F32),
        compiler_params=_cparams(("arbitrary",)),
        name="moe_routed",
    )(e0, e1, tile_idx, n_used, xs, wg_bf, wu_bf, wd_bf, wg_bf, wu_bf, wd_bf)


def _final_kernel(dest_ref, x1_ref, ys_ref, mf_ref, lg_ref, lb_ref, o_ref, buf, sem, *, tm):
    nj = pl.num_programs(1)
    step = pl.program_id(0) * nj + pl.program_id(1)
    n_steps = pl.num_programs(0) * nj
    slot = step % 2

    def fetch(s, sl):
        def issue(r, _):
            _row_copy(ys_ref, dest_ref[s * tm + r], buf.at[sl], r, sem.at[sl]).start()
            return 0
        lax.fori_loop(0, tm, issue, 0, unroll=DMA_UNROLL)

    @pl.when(step == 0)
    def _():
        fetch(step, slot)

    @pl.when(step + 1 < n_steps)
    def _():
        fetch(step + 1, 1 - slot)

    pltpu.make_async_copy(ys_ref.at[pl.ds(0, tm)], buf.at[slot], sem.at[slot]).wait()
    gate = mf_ref[0][:, 2 * D_MODEL:]
    o_ref[0] = _ln_plain(DEEPNORM_ALPHA * x1_ref[0] + gate * buf[slot]) * lg_ref[...] + lb_ref[...]


def _final(x1, ys, dest, mod_ffn, ln_g, ln_b, tm):
    b, t, _ = x1.shape
    tok = pl.BlockSpec((1, tm, D_MODEL), lambda i, j, *_: (i, j, 0))
    row = pl.BlockSpec((1, D_MODEL), lambda i, j, *_: (0, 0))
    return pl.pallas_call(
        functools.partial(_final_kernel, tm=tm),
        grid_spec=pltpu.PrefetchScalarGridSpec(
            num_scalar_prefetch=1, grid=(b, t // tm),
            in_specs=[tok, pl.BlockSpec(memory_space=pl.ANY),
                      pl.BlockSpec((1, 1, 3 * D_MODEL), lambda i, j, *_: (i, 0, 0)), row, row],
            out_specs=tok,
            scratch_shapes=[pltpu.VMEM((2, tm, D_MODEL), F32), pltpu.SemaphoreType.DMA((2,))]),
        out_shape=jax.ShapeDtypeStruct((b, t, D_MODEL), F32),
        compiler_params=_cparams(("arbitrary", "arbitrary")),
        name="final_ln",
    )(dest, x1, ys, mod_ffn.reshape(b, 1, -1), ln_g.reshape(1, -1), ln_b.reshape(1, -1))


def _trunk(x, mod_mix, mod_ffn, pos0, state0, k_past, v_past, wts):
    b, t, _ = x.shape
    tm = min(256, t)
    ret_in, sq, skt, svt = _in_proj(x, mod_mix, wts["w_in_a"], wts["w_in_kvt"], pos0, tm)
    mr, state = _retention(ret_in, state0, wts["ret_norm_g"], min(256, t))
    if k_past is None:
        ms = _sb_prompt(sq, skt, svt, wts["sb_norm_g"])
    else:
        ms = _sb_sample(sq, skt, svt, jnp.swapaxes(k_past, 2, 3), jnp.swapaxes(v_past, 2, 3),
                        wts["sb_norm_g"])
    sk, sv = jnp.swapaxes(skt, 2, 3), jnp.swapaxes(svt, 2, 3)
    x1, hx, route, counts = _out_proj(x, mr, ms, wts["w_out"], mod_mix, mod_ffn, wts["ln_mix_g"],
                                      wts["ln_mix_b"], wts["w_route"], wts["b_route"], tm)
    n = b * t
    n_tiles = n // MOE_TM + N_BUCKETS
    dest, e0, e1, tile_idx, n_used = _route_plan(route, counts, n_tiles)
    xs = _scatter_rows(hx.reshape(n, XW), dest, n_tiles * MOE_TM)
    ys = _moe(xs, e0, e1, tile_idx, n_used, wts["w_e_gate"], wts["w_e_up"], wts["w_e_down"])
    out = _final(x1, ys, dest, mod_ffn, wts["ln_ffn_g"], wts["ln_ffn_b"], tm)
    return out, sk[None], sv[None], state[None]


def kernel(x_prompt, x_sample, cache_sb_k, cache_sb_v, state_ret, c_prompt, c_sample, w_in, w_out, ret_norm_g, sb_norm_g, w_ada_mix, b_ada_mix, ln_mix_g, ln_mix_b, w_ada_ffn, b_ada_ffn, ln_ffn_g, ln_ffn_b, w_group, b_group, w_router, b_router, w_e_gate, w_e_up, w_e_down):
    bp = x_prompt.shape[0]
    c_all = jnp.concatenate([c_prompt, c_sample], axis=0)
    mod_mix = _ada(c_all, w_ada_mix[0], b_ada_mix[0])
    mod_ffn = _ada(c_all, w_ada_ffn[0], b_ada_ffn[0])
    pad = LANES - N_GROUPS - N_EXPERTS
    w_route = jnp.concatenate([w_group[0], w_router[0], jnp.zeros((D_MODEL, pad), F32)], axis=1)
    b_route = jnp.concatenate([b_group[0], b_router[0], jnp.zeros((pad,), F32)]).reshape(1, LANES)
    n_a = 4 * RET_WIDTH + SB_WIDTH
    wts = dict(w_in_a=w_in[0, :, :n_a].astype(BF16), w_in_kvt=w_in[0, :, n_a:].T.astype(BF16),
               w_out=w_out[0].astype(BF16),
               ret_norm_g=ret_norm_g[0], sb_norm_g=sb_norm_g[0],
               ln_mix_g=ln_mix_g[0], ln_mix_b=ln_mix_b[0], ln_ffn_g=ln_ffn_g[0], ln_ffn_b=ln_ffn_b[0],
               w_route=w_route.astype(BF16), b_route=b_route,
               w_e_gate=w_e_gate[0].astype(BF16), w_e_up=w_e_up[0].astype(BF16),
               w_e_down=w_e_down[0].astype(BF16))
    ret_zero = jnp.zeros((bp, RET_HEADS, RET_DK, RET_DV), F32)
    y_p, k_p, v_p, r_p = _trunk(x_prompt, mod_mix[:bp], mod_ffn[:bp], 0, ret_zero, None, None, wts)
    y_s, k_s, v_s, r_s = _trunk(x_sample, mod_mix[bp:], mod_ffn[bp:], cache_sb_k.shape[3], state_ret[0],
                                cache_sb_k[0], cache_sb_v[0], wts)
    return (y_p, y_s, k_p, v_p, r_p, k_s, v_s, r_s)
```

```python
import functools
import math

import numpy as np
import jax
import jax.numpy as jnp
from jax import lax
from jax.experimental import pallas as pl
from jax.experimental.pallas import tpu as pltpu

D_MODEL = 1024
RET_HEADS = 4
RET_DK = 128
RET_DV = 128
RET_WIDTH = RET_HEADS * RET_DV
SB_HEADS = 8
SB_HEAD_DIM = 64
SB_WIDTH = SB_HEADS * SB_HEAD_DIM
IN_WIDTH = 2 * RET_HEADS * RET_DK + 2 * RET_WIDTH + 3 * SB_WIDTH
ROPE_BASE = 10000.0
N_GROUPS = 4
EXPERTS_PER_GROUP = 4
N_EXPERTS = N_GROUPS * EXPERTS_PER_GROUP
D_EXPERT = 512
DEPTH = 1
DEEPNORM_ALPHA = (2.0 * DEPTH) ** 0.25
LN_EPS = 1e-5

LANES = 128
VMEM_LIMIT = 48 * 1024 * 1024

F32 = jnp.float32
BF16 = jnp.bfloat16


def _cparams(sem, flags=None):
    return pltpu.CompilerParams(dimension_semantics=sem, vmem_limit_bytes=VMEM_LIMIT, flags=flags)


SB_FLAGS = None


def _dot(a, b):
    return jnp.dot(a, b, preferred_element_type=F32)


def _dot_nt(a, b):
    return lax.dot_general(a, b, (((1,), (1,)), ((), ())), preferred_element_type=F32)


def _dot_tn(a, b):
    return lax.dot_general(a, b, (((0,), (0,)), ((), ())), preferred_element_type=F32)


def _split_dot(a, w_hi, w_lo):
    a_hi = a.astype(BF16)
    a_lo = (a - a_hi.astype(F32)).astype(BF16)
    return _dot(a_hi, w_hi) + (_dot(a_hi, w_lo) + _dot(a_lo, w_hi))


def _ln_plain(x):
    mu = jnp.mean(x, axis=-1, keepdims=True)
    xc = x - mu
    var = jnp.mean(xc * xc, axis=-1, keepdims=True)
    return xc * lax.rsqrt(var + LN_EPS)


def _silu(x):
    return x * (1.0 / (1.0 + jnp.exp(-x)))


def _ada_kernel(c_ref, w_ref, b_ref, o_ref):
    c = c_ref[...]
    w = w_ref[...]
    w_hi = w.astype(BF16)
    w_lo = (w - w_hi.astype(F32)).astype(BF16)
    o_ref[...] = _split_dot(_silu(c), w_hi, w_lo) + b_ref[...]


def _ada(c, w, b):
    r = c.shape[0]
    tn = 768
    return pl.pallas_call(
        _ada_kernel,
        grid=(3 * D_MODEL // tn,),
        in_specs=[pl.BlockSpec((r, D_MODEL), lambda j: (0, 0)),
                  pl.BlockSpec((D_MODEL, tn), lambda j: (0, j)),
                  pl.BlockSpec((1, tn), lambda j: (0, j))],
        out_specs=pl.BlockSpec((r, tn), lambda j: (0, j)),
        out_shape=jax.ShapeDtypeStruct((r, 3 * D_MODEL), F32),
        compiler_params=_cparams(("arbitrary",)),
        name="ada_mod",
    )(c, w, b.reshape(1, -1))


def _in_proj_kernel(x_ref, mod_ref, w_ref, wkv_ref, tab_ref, ret_ref, q_ref, k_ref, v_ref):
    m = mod_ref[0]
    h = _ln_plain(x_ref[0]) * (1.0 + m[:, D_MODEL:2 * D_MODEL]) + m[:, :D_MODEL]
    h = h.astype(BF16)
    tab = tab_ref[...]
    for c in range(4):
        p = _dot(h, w_ref[:, c * RET_WIDTH:(c + 1) * RET_WIDTH])
        if c < 2:
            cs = tab[:, (2 * c) * LANES:(2 * c + 1) * LANES]
            sn = tab[:, (2 * c + 1) * LANES:(2 * c + 2) * LANES]
            for hh in range(RET_HEADS):
                ph = p[:, hh * RET_DK:(hh + 1) * RET_DK]
                ret_ref[0, :, c * RET_WIDTH + hh * RET_DK:c * RET_WIDTH + (hh + 1) * RET_DK] = (
                    ph * cs + pltpu.roll(ph, RET_DK // 2, 1) * sn)
        else:
            ret_ref[0, :, c * RET_WIDTH:(c + 1) * RET_WIDTH] = p
    base = 4 * RET_WIDTH
    p = _dot(h, w_ref[:, base:base + SB_WIDTH]) * (LOG2E * SB_HEAD_DIM ** -0.5)
    for hh in range(SB_HEADS):
        q_ref[0, hh] = p[:, hh * SB_HEAD_DIM:(hh + 1) * SB_HEAD_DIM].astype(q_ref.dtype)
    pt = _dot_nt(wkv_ref[...], h)
    for c, ref in enumerate((k_ref, v_ref)):
        for hh in range(SB_HEADS):
            r0 = c * SB_WIDTH + hh * SB_HEAD_DIM
            ref[0, hh] = pt[r0:r0 + SB_HEAD_DIM, :]


def _rope_table(pos0, t):
    half = RET_DK // 2
    inv = ROPE_BASE ** (-np.arange(half, dtype=np.float64) / half)
    ang = (pos0 + np.arange(t, dtype=np.float64))[:, None] * inv[None, :]
    cos, sin = np.cos(ang), np.sin(ang)
    cs = np.concatenate([cos, cos], axis=1)
    sn = np.concatenate([-sin, sin], axis=1)
    ks = RET_DK ** -0.5
    return jnp.asarray(np.concatenate([cs, sn, cs * ks, sn * ks], axis=1), dtype=F32)


def _in_proj(x, mod, w_a_bf, w_kvt_bf, pos0, tm):
    b, t, _ = x.shape
    tab = _rope_table(pos0, t)
    wa = w_a_bf.shape[1]
    hs = jax.ShapeDtypeStruct((b, SB_HEADS, SB_HEAD_DIM, t), F32)
    q_spec = pl.BlockSpec((1, SB_HEADS, tm, SB_HEAD_DIM), lambda i, j: (i, 0, j, 0))
    kv_spec = pl.BlockSpec((1, SB_HEADS, SB_HEAD_DIM, tm), lambda i, j: (i, 0, 0, j))
    return pl.pallas_call(
        _in_proj_kernel,
        grid=(b, t // tm),
        in_specs=[pl.BlockSpec((1, tm, D_MODEL), lambda i, j: (i, j, 0)),
                  pl.BlockSpec((1, 1, 3 * D_MODEL), lambda i, j: (i, 0, 0)),
                  pl.BlockSpec((D_MODEL, wa), lambda i, j: (0, 0)),
                  pl.BlockSpec((2 * SB_WIDTH, D_MODEL), lambda i, j: (0, 0)),
                  pl.BlockSpec((tm, 4 * LANES), lambda i, j: (j, 0))],
        out_specs=[pl.BlockSpec((1, tm, 4 * RET_WIDTH), lambda i, j: (i, j, 0)),
                   q_spec, kv_spec, kv_spec],
        out_shape=[jax.ShapeDtypeStruct((b, t, 4 * RET_WIDTH), F32),
                   jax.ShapeDtypeStruct((b, SB_HEADS, t, SB_HEAD_DIM), BF16), hs, hs],
        compiler_params=_cparams(("parallel", "arbitrary")),
        name="in_proj",
    )(x, mod.reshape(b, 1, -1), w_a_bf, w_kvt_bf, tab)


def _ret_kernel(q_ref, k_ref, v_ref, g_ref, s0_ref, dec_ref, qd_ref, kd_ref, ng_ref,
                o_ref, so_ref, st_ref, *, chunk_decay):
    j = pl.program_id(1)

    @pl.when(j == 0)
    def _():
        st_ref[...] = s0_ref[0]

    for hh in range(RET_HEADS):
        sl = slice(hh * RET_DK, (hh + 1) * RET_DK)
        q = q_ref[0, :, sl]
        k = k_ref[0, :, sl]
        v = v_ref[0, :, sl]
        vb = v.astype(BF16)
        st = st_ref[hh]
        scores = _dot_nt(q.astype(BF16), k.astype(BF16)) * dec_ref[hh]
        o = _dot(scores.astype(BF16), vb) + _dot((q * qd_ref[hh]).astype(BF16), st.astype(BF16))
        st_ref[hh] = st * chunk_decay[hh] + _dot_tn((k * kd_ref[hh]).astype(BF16), vb)
        o = _ln_plain(o) * ng_ref[:, sl] * _silu(g_ref[0, :, sl])
        o_ref[0, :, sl] = o.astype(o_ref.dtype)

    @pl.when(j == pl.num_programs(1) - 1)
    def _():
        so_ref[0] = st_ref[...]


def _retention(ret_in, state0, norm_g, chunk):
    b, t, _ = ret_in.shape
    lg = np.log1p(-np.exp2(-5.0 - np.arange(RET_HEADS, dtype=np.float64)))
    idx = np.arange(chunk, dtype=np.float64)
    rel = idx[:, None] - idx[None, :]
    dec = np.where(rel >= 0, np.exp(lg[:, None, None] * np.maximum(rel, 0.0)), 0.0)
    qd = np.broadcast_to(np.exp(lg[:, None] * (idx + 1.0))[:, :, None], (RET_HEADS, chunk, RET_DK))
    kd = np.broadcast_to(np.exp(lg[:, None] * (chunk - 1.0 - idx))[:, :, None], (RET_HEADS, chunk, RET_DK))
    chunk_decay = tuple(float(v) for v in np.exp(lg * chunk))

    def col(c):
        return pl.BlockSpec((1, chunk, RET_WIDTH), lambda i, j, c=c: (i, j, c))

    const3 = lambda shape: pl.BlockSpec(shape, lambda i, j: (0, 0, 0))
    state_spec = pl.BlockSpec((1, RET_HEADS, RET_DK, RET_DV), lambda i, j: (i, 0, 0, 0))
    return pl.pallas_call(
        functools.partial(_ret_kernel, chunk_decay=chunk_decay),
        grid=(b, t // chunk),
        in_specs=[col(0), col(1), col(2), col(3), state_spec,
                  const3((RET_HEADS, chunk, chunk)), const3((RET_HEADS, chunk, RET_DK)),
                  const3((RET_HEADS, chunk, RET_DK)),
                  pl.BlockSpec((1, RET_WIDTH), lambda i, j: (0, 0))],
        out_specs=[pl.BlockSpec((1, chunk, RET_WIDTH), lambda i, j: (i, j, 0)), state_spec],
        out_shape=[jax.ShapeDtypeStruct((b, t, RET_WIDTH), BF16),
                   jax.ShapeDtypeStruct((b, RET_HEADS, RET_DK, RET_DV), F32)],
        scratch_shapes=[pltpu.VMEM((RET_HEADS, RET_DK, RET_DV), F32)],
        compiler_params=_cparams(("parallel", "arbitrary")),
        name="retention",
    )(ret_in, ret_in, ret_in, ret_in, state0,
      jnp.asarray(dec, F32), jnp.asarray(qd, F32), jnp.asarray(kd, F32), norm_g.reshape(1, -1))


SB_KEY_BLOCK = 256


LOG2E = 1.4426950408889634


MASKED_LOGIT = -1e30


def _sb_stage1(qs, kts, z_ref, hl_ref, masked):
    tq, kb = qs[0].shape[0], kts[0].shape[1]
    if masked:
        valid = (lax.broadcasted_iota(jnp.int32, (tq, kb), 1)
                 < lax.broadcasted_iota(jnp.int32, (tq, kb), 0))
    for c, (q, kt) in enumerate(zip(qs, kts)):
        z = _dot(q, kt.astype(BF16))
        p = jnp.maximum(z, 0.0) + jnp.log2(1.0 + jnp.exp2(-jnp.abs(z)))
        if masked:
            p = jnp.where(valid, p, 0.0)
            z = jnp.where(valid, z, MASKED_LOGIT)
        hi = p.astype(BF16)
        z_ref[c] = z
        hl_ref[c, :, :kb] = hi
        hl_ref[c, :, kb:] = (p - hi.astype(F32)).astype(BF16)


def _sb_stage2(z_ref, hl_ref, vts, tri2, acc_ref, car_ref):
    n, tq, kb = z_ref.shape
    r = _dot(hl_ref[...].reshape(n * tq, 2 * kb), tri2)
    for c in range(n):
        incl = r[c * tq:(c + 1) * tq]
        car = car_ref[c]
        w = jnp.exp2(z_ref[c] + incl + car)
        acc_ref[c] += _dot_nt(w.astype(BF16), vts[c].astype(BF16))
        car_ref[c] = car + incl[:, 0:1]


def _sb_finish(o, g):
    return o * lax.rsqrt(jnp.mean(o * o, axis=-1, keepdims=True) + LN_EPS) * g


def _tri_matrix():
    idx = np.arange(SB_KEY_BLOCK)
    t = -(idx[:, None] >= idx[None, :]).astype(np.float32)
    return jnp.asarray(np.concatenate([t, t], axis=0), dtype=BF16)


def _sb_prompt_kernel(q_ref, kt_ref, vt_ref, tri_ref, g_ref, o_ref, acc_ref, car_ref,
                      za_ref, ha_ref, zb_ref, hb_ref, *, tq, nh):
    i = pl.program_id(2)
    qs = [q_ref[0, hh] for hh in range(nh)]
    acc_ref[...] = jnp.zeros_like(acc_ref)
    car_ref[...] = jnp.zeros_like(car_ref)

    def s1(blk, z_ref, hl_ref, masked=False):
        start = pl.multiple_of(blk * tq, tq)
        _sb_stage1(qs, [kt_ref[0, hh, :, pl.ds(start, tq)] for hh in range(nh)], z_ref, hl_ref, masked)

    def s2(blk, z_ref, hl_ref):
        start = pl.multiple_of(blk * tq, tq)
        _sb_stage2(z_ref, hl_ref, [vt_ref[0, hh, :, pl.ds(start, tq)] for hh in range(nh)],
                   tri_ref[...], acc_ref, car_ref)

    s1(i, za_ref, ha_ref, masked=True)

    def pair(p, _):
        blk = i - 2 * p
        s1(blk - 1, zb_ref, hb_ref)
        s2(blk, za_ref, ha_ref)
        s1(blk - 2, za_ref, ha_ref)
        s2(blk - 1, zb_ref, hb_ref)
        return 0

    lax.fori_loop(0, i // 2, pair, 0)

    @pl.when(i % 2 == 0)
    def _():
        s2(0, za_ref, ha_ref)

    @pl.when(i % 2 == 1)
    def _():
        s1(0, zb_ref, hb_ref)
        s2(1, za_ref, ha_ref)
        s2(0, zb_ref, hb_ref)

    d = SB_HEAD_DIM
    outs = [_sb_finish(acc_ref[hh], g_ref[0, :, hh * d:(hh + 1) * d]) for hh in range(nh)]
    o_ref[0] = jnp.concatenate(outs, axis=-1).astype(o_ref.dtype)


def _sb_prompt(q, kt, vt, norm_g):
    b, _, t, d = q.shape
    tq = SB_KEY_BLOCK
    nh = 4
    kv_spec = pl.BlockSpec((1, nh, d, t), lambda i, h, j: (i, h, 0, 0))
    return pl.pallas_call(
        functools.partial(_sb_prompt_kernel, tq=tq, nh=nh),
        grid=(b, SB_HEADS // nh, t // tq),
        in_specs=[pl.BlockSpec((1, nh, tq, d), lambda i, h, j: (i, h, j, 0)), kv_spec, kv_spec,
                  pl.BlockSpec((2 * tq, tq), lambda i, h, j: (0, 0)),
                  pl.BlockSpec((1, 1, nh * d), lambda i, h, j: (h, 0, 0))],
        out_specs=pl.BlockSpec((1, tq, nh * d), lambda i, h, j: (i, j, h)),
        out_shape=jax.ShapeDtypeStruct((b, t, SB_WIDTH), BF16),
        scratch_shapes=[pltpu.VMEM((nh, tq, d), F32), pltpu.VMEM((nh, tq, 1), F32),
                        pltpu.VMEM((nh, tq, tq), F32), pltpu.VMEM((nh, tq, 2 * tq), BF16),
                        pltpu.VMEM((nh, tq, tq), F32), pltpu.VMEM((nh, tq, 2 * tq), BF16)],
        compiler_params=_cparams(("parallel", "parallel", "arbitrary"), SB_FLAGS),
        name="sb_prompt",
    )(q, kt, vt, _tri_matrix(), norm_g.reshape(SB_HEADS // nh, 1, nh * d))


def _sb_sample_kernel(q_ref, kt_ref, vt_ref, ktp_ref, vtp_ref, tri_ref, g_ref, o_ref, acc_ref, car_ref,
                      zd_ref, hd_ref, za_ref, ha_ref, zb_ref, hb_ref, *, nh):
    kb = SB_KEY_BLOCK
    n_past = ktp_ref.shape[3] // kb
    t = q_ref.shape[2]
    qs = [q_ref[0, hh] for hh in range(nh)]
    acc_ref[...] = jnp.zeros_like(acc_ref)
    car_ref[...] = jnp.zeros_like(car_ref)

    def s1(blk, z_ref, hl_ref):
        start = pl.multiple_of(blk * kb, kb)
        _sb_stage1(qs, [ktp_ref[0, hh, :, pl.ds(start, kb)] for hh in range(nh)], z_ref, hl_ref, False)

    def s2(blk, z_ref, hl_ref):
        start = pl.multiple_of(blk * kb, kb)
        _sb_stage2(z_ref, hl_ref, [vtp_ref[0, hh, :, pl.ds(start, kb)] for hh in range(nh)],
                   tri_ref[...], acc_ref, car_ref)

    _sb_stage1(qs, [kt_ref[0, hh] for hh in range(nh)], zd_ref, hd_ref, True)
    s1(n_past - 1, za_ref, ha_ref)
    _sb_stage2(zd_ref, hd_ref, [vt_ref[0, hh] for hh in range(nh)],
               jnp.concatenate([tri_ref[:t, :t], tri_ref[:t, :t]], axis=0), acc_ref, car_ref)

    def pair(p, _):
        blk = n_past - 1 - 2 * p
        s1(blk - 1, zb_ref, hb_ref)
        s2(blk, za_ref, ha_ref)
        s1(blk - 2, za_ref, ha_ref)
        s2(blk - 1, zb_ref, hb_ref)
        return 0

    lax.fori_loop(0, (n_past - 1) // 2, pair, 0)
    if (n_past - 1) % 2 == 0:
        s2(0, za_ref, ha_ref)
    else:
        s1(0, zb_ref, hb_ref)
        s2(1, za_ref, ha_ref)
        s2(0, zb_ref, hb_ref)
    d = SB_HEAD_DIM
    outs = [_sb_finish(acc_ref[hh], g_ref[0, :, hh * d:(hh + 1) * d]) for hh in range(nh)]
    o_ref[0] = jnp.concatenate(outs, axis=-1).astype(o_ref.dtype)


def _sb_sample(q, kt, vt, kt_past, vt_past, norm_g):
    b, _, t, d = q.shape
    p = kt_past.shape[3]
    nh = 4
    kb = SB_KEY_BLOCK
    new_spec = pl.BlockSpec((1, nh, d, t), lambda i, h: (i, h, 0, 0))
    past_spec = pl.BlockSpec((1, nh, d, p), lambda i, h: (i, h, 0, 0))
    return pl.pallas_call(
        functools.partial(_sb_sample_kernel, nh=nh),
        grid=(b, SB_HEADS // nh),
        in_specs=[pl.BlockSpec((1, nh, t, d), lambda i, h: (i, h, 0, 0)), new_spec, new_spec,
                  past_spec, past_spec,
                  pl.BlockSpec((2 * kb, kb), lambda i, h: (0, 0)),
                  pl.BlockSpec((1, 1, nh * d), lambda i, h: (h, 0, 0))],
        out_specs=pl.BlockSpec((1, t, nh * d), lambda i, h: (i, 0, h)),
        out_shape=jax.ShapeDtypeStruct((b, t, SB_WIDTH), BF16),
        scratch_shapes=[pltpu.VMEM((nh, t, d), F32), pltpu.VMEM((nh, t, 1), F32),
                        pltpu.VMEM((nh, t, t), F32), pltpu.VMEM((nh, t, 2 * t), BF16),
                        pltpu.VMEM((nh, t, kb), F32), pltpu.VMEM((nh, t, 2 * kb), BF16),
                        pltpu.VMEM((nh, t, kb), F32), pltpu.VMEM((nh, t, 2 * kb), BF16)],
        compiler_params=_cparams(("parallel", "arbitrary"), SB_FLAGS),
        name="sb_sample",
    )(q, kt, vt, kt_past, vt_past, _tri_matrix(), norm_g.reshape(SB_HEADS // nh, 1, nh * d))


N_PAIRS = 6
N_BUCKETS = N_GROUPS * N_PAIRS
MOE_TM = 256
XW = D_MODEL + LANES
_PAIRS = [(a, b) for a in range(EXPERTS_PER_GROUP) for b in range(a + 1, EXPERTS_PER_GROUP)]
BUCKET_E0 = np.array([g * EXPERTS_PER_GROUP + a for g in range(N_GROUPS) for a, _ in _PAIRS], np.int32)
BUCKET_E1 = np.array([g * EXPERTS_PER_GROUP + b for g in range(N_GROUPS) for _, b in _PAIRS], np.int32)


def _route(logits):
    lane = lax.broadcasted_iota(jnp.int32, logits.shape, 1)
    neg = -jnp.inf
    big = jnp.int32(2 * LANES)
    gl = jnp.where(lane < N_GROUPS, logits, neg)
    gmax = jnp.max(gl, axis=-1, keepdims=True)
    g_idx = jnp.min(jnp.where(gl == gmax, lane, big), axis=-1, keepdims=True)
    g_p = 1.0 / jnp.sum(jnp.exp(gl - gmax), axis=-1, keepdims=True)
    lo = N_GROUPS + g_idx * EXPERTS_PER_GROUP
    el = jnp.where((lane >= lo) & (lane < lo + EXPERTS_PER_GROUP), logits, neg)
    v1 = jnp.max(el, axis=-1, keepdims=True)
    i1 = jnp.min(jnp.where(el == v1, lane, big), axis=-1, keepdims=True)
    el2 = jnp.where(lane == i1, neg, el)
    v2 = jnp.max(el2, axis=-1, keepdims=True)
    i2 = jnp.min(jnp.where(el2 == v2, lane, big), axis=-1, keepdims=True)
    e21 = jnp.exp(v2 - v1)
    p1 = 1.0 / (1.0 + e21)
    p2 = e21 * p1
    first_lo = i1 < i2
    w_lo = jnp.where(first_lo, p1, p2) * g_p
    w_hi = jnp.where(first_lo, p2, p1) * g_p
    a = jnp.minimum(i1, i2) - lo
    b = jnp.maximum(i1, i2) - lo
    pair = jnp.where(a == 0, b - 1, jnp.where(a == 1, b + 1, 5))
    return g_idx * N_PAIRS + pair, w_lo, w_hi


def _out_proj_kernel(x_ref, mr_ref, ms_ref, wo_ref, mm_ref, mf_ref, lg_ref, lb_ref, wr_ref, br_ref,
                     tril_ref, x1_ref, hx_ref, rt_ref, cnt_ref, run_ref):
    @pl.when((pl.program_id(0) == 0) & (pl.program_id(1) == 0))
    def _():
        run_ref[...] = jnp.zeros_like(run_ref)

    mix = _dot(mr_ref[0], wo_ref[:MIX_HALF]) + _dot(ms_ref[0], wo_ref[MIX_HALF:])
    gate = mm_ref[0][:, 2 * D_MODEL:]
    x1 = _ln_plain(DEEPNORM_ALPHA * x_ref[0] + gate * mix) * lg_ref[...] + lb_ref[...]
    x1_ref[0] = x1
    mf = mf_ref[0]
    h2 = _ln_plain(x1) * (1.0 + mf[:, D_MODEL:2 * D_MODEL]) + mf[:, :D_MODEL]
    bucket, w_lo, w_hi = _route(_dot(h2.astype(BF16), wr_ref[...]) + br_ref[...])
    tm = h2.shape[0]
    lane = lax.broadcasted_iota(jnp.int32, (tm, LANES), 1)
    hx_ref[0, :, :D_MODEL] = h2
    hx_ref[0, :, D_MODEL:] = jnp.where(lane == 0, w_lo, jnp.where(lane == 1, w_hi, 0.0))
    hit = lane == bucket
    onehot = hit.astype(BF16)
    before = _dot(tril_ref[...], onehot)
    run = run_ref[...]
    rank = jnp.sum(jnp.where(hit, before + run, 0.0), axis=-1, keepdims=True)
    run = run + before[tm - 1:tm] + onehot[tm - 1:tm].astype(F32)
    run_ref[...] = run
    cnt_ref[...] = run
    rt_ref[0] = jnp.where(lane == 0, bucket.astype(F32), jnp.where(lane == 1, rank, 0.0))


MIX_HALF = RET_WIDTH


def _out_proj(x, mr, ms, w_out_bf, mod_mix, mod_ffn, ln_g, ln_b, w_route_bf, b_route, tm):
    b, t, _ = x.shape
    tok = lambda w: pl.BlockSpec((1, tm, w), lambda i, j: (i, j, 0))
    modspec = pl.BlockSpec((1, 1, 3 * D_MODEL), lambda i, j: (i, 0, 0))
    row = lambda w: pl.BlockSpec((1, w), lambda i, j: (0, 0))
    idx = np.arange(tm)
    tril = jnp.asarray(idx[:, None] > idx[None, :], dtype=BF16)
    return pl.pallas_call(
        _out_proj_kernel,
        grid=(b, t // tm),
        in_specs=[tok(D_MODEL), tok(RET_WIDTH), tok(SB_WIDTH),
                  pl.BlockSpec((D_MODEL, D_MODEL), lambda i, j: (0, 0)),
                  modspec, modspec, row(D_MODEL), row(D_MODEL),
                  pl.BlockSpec((D_MODEL, LANES), lambda i, j: (0, 0)), row(LANES),
                  pl.BlockSpec((tm, tm), lambda i, j: (0, 0))],
        out_specs=[tok(D_MODEL), tok(XW), tok(LANES), row(LANES)],
        out_shape=[jax.ShapeDtypeStruct((b, t, D_MODEL), F32),
                   jax.ShapeDtypeStruct((b, t, XW), F32),
                   jax.ShapeDtypeStruct((b, t, LANES), F32),
                   jax.ShapeDtypeStruct((1, LANES), F32)],
        scratch_shapes=[pltpu.VMEM((1, LANES), F32)],
        compiler_params=_cparams(("arbitrary", "arbitrary")),
        name="out_proj",
    )(x, mr, ms, w_out_bf, mod_mix.reshape(b, 1, -1), mod_ffn.reshape(b, 1, -1),
      ln_g.reshape(1, -1), ln_b.reshape(1, -1), w_route_bf, b_route, tril)


def _route_plan(route, counts, n_tiles):
    n = route.shape[0] * route.shape[1]
    bucket = route[..., 0].astype(jnp.int32).reshape(n)
    rank = route[..., 1].astype(jnp.int32).reshape(n)
    cnt = counts[0, :N_BUCKETS].astype(jnp.int32)
    padded = ((cnt + MOE_TM - 1) // MOE_TM) * MOE_TM
    ends = jnp.cumsum(padded)
    starts = ends - padded
    n_used = ends[-1] // MOE_TM
    tile = jnp.arange(n_tiles, dtype=jnp.int32)
    last = jnp.maximum(n_used - 1, 0)
    tile_idx = jnp.minimum(tile, last)
    tile_bucket = jnp.sum((ends[None, :] <= (tile_idx * MOE_TM)[:, None]).astype(jnp.int32), axis=1)
    tile_bucket = jnp.minimum(tile_bucket, N_BUCKETS - 1)
    e0 = jnp.asarray(BUCKET_E0)[tile_bucket]
    e1 = jnp.asarray(BUCKET_E1)[tile_bucket]
    in_bucket = bucket[:, None] == jnp.arange(N_BUCKETS, dtype=jnp.int32)[None, :]
    dest = rank + jnp.sum(jnp.where(in_bucket, starts[None, :], 0), axis=1)
    trailing = n_used + jnp.arange(N_BUCKETS, dtype=jnp.int32)
    ztiles = jnp.concatenate([jnp.where(padded > 0, ends - MOE_TM, -1),
                              jnp.where(trailing < n_tiles, trailing * MOE_TM, -1)])
    return (dest.astype(jnp.int32), ztiles.astype(jnp.int32), e0, e1, tile_idx,
            n_used.reshape(1).astype(jnp.int32))


def _row_copy(src_ref, s, dst_ref, d, sem):
    return pltpu.make_async_copy(src_ref.at[pl.ds(s, 1)], dst_ref.at[pl.ds(d, 1)], sem)


DMA_UNROLL = 8


def _scatter_kernel(dest_ref, ztile_ref, src_ref, out_ref, zbuf, sem, zsem, *, rows):
    @pl.when(pl.program_id(0) == 0)
    def _():
        zbuf[...] = jnp.zeros_like(zbuf)

        def fill(k, _):
            @pl.when(ztile_ref[k] >= 0)
            def _():
                row0 = pl.multiple_of(ztile_ref[k], MOE_TM)
                pltpu.make_async_copy(zbuf, out_ref.at[pl.ds(row0, MOE_TM)], zsem).start()
            return 0

        def drain(k, _):
            @pl.when(ztile_ref[k] >= 0)
            def _():
                pltpu.make_async_copy(zbuf, out_ref.at[pl.ds(0, MOE_TM)], zsem).wait()
            return 0

        lax.fori_loop(0, ztile_ref.shape[0], fill, 0)
        lax.fori_loop(0, ztile_ref.shape[0], drain, 0)

    base = pl.program_id(0) * rows

    def issue(r, _):
        _row_copy(src_ref, r, out_ref, dest_ref[base + r], sem).start()
        return 0

    lax.fori_loop(0, rows, issue, 0, unroll=DMA_UNROLL)
    pltpu.make_async_copy(src_ref, out_ref.at[pl.ds(0, rows)], sem).wait()


def _scatter_rows(hx, dest, ztiles, n_rows):
    n, w = hx.shape
    rows = min(256, n)
    return pl.pallas_call(
        functools.partial(_scatter_kernel, rows=rows),
        grid_spec=pltpu.PrefetchScalarGridSpec(
            num_scalar_prefetch=2, grid=(n // rows,),
            in_specs=[pl.BlockSpec((rows, w), lambda i, d, z: (i, 0))],
            out_specs=pl.BlockSpec(memory_space=pl.ANY),
            scratch_shapes=[pltpu.VMEM((MOE_TM, w), F32), pltpu.SemaphoreType.DMA(()),
                            pltpu.SemaphoreType.DMA(())]),
        out_shape=jax.ShapeDtypeStruct((n_rows, w), F32),
        compiler_params=_cparams(("arbitrary",)),
        name="scatter_rows",
    )(dest, ztiles, hx)


def _moe_kernel(e0_ref, e1_ref, ti_ref, nu_ref, x_ref, wg0, wu0, wd0, wg1, wu1, wd1, y_ref):
    del e0_ref, e1_ref, ti_ref

    @pl.when(pl.program_id(0) < nu_ref[0])
    def _():
        x = x_ref[...]
        h = x[:, :D_MODEL].astype(BF16)
        wx = x[:, D_MODEL:]
        lane = lax.broadcasted_iota(jnp.int32, wx.shape, 1)
        w_lo = jnp.sum(jnp.where(lane == 0, wx, 0.0), axis=-1, keepdims=True)
        w_hi = jnp.sum(jnp.where(lane == 1, wx, 0.0), axis=-1, keepdims=True)

        def expert(wg, wu, wd):
            a = _silu(_dot(h, wg[0])) * _dot(h, wu[0])
            return _dot(a.astype(BF16), wd[0])

        y_ref[...] = w_lo * expert(wg0, wu0, wd0) + w_hi * expert(wg1, wu1, wd1)

    @pl.when(pl.program_id(0) >= nu_ref[0])
    def _():
        y_ref[...] = jnp.zeros_like(y_ref)


def _moe(xs, e0, e1, tile_idx, n_used, wg_bf, wu_bf, wd_bf):
    n_rows = xs.shape[0]
    up = lambda sel: pl.BlockSpec((1, D_MODEL, D_EXPERT), lambda t, e0, e1, ti, nu: ((e0, e1)[sel][t], 0, 0))
    down = lambda sel: pl.BlockSpec((1, D_EXPERT, D_MODEL), lambda t, e0, e1, ti, nu: ((e0, e1)[sel][t], 0, 0))
    return pl.pallas_call(
        _moe_kernel,
        grid_spec=pltpu.PrefetchScalarGridSpec(
            num_scalar_prefetch=4, grid=(n_rows // MOE_TM,),
            in_specs=[pl.BlockSpec((MOE_TM, XW), lambda t, e0, e1, ti, nu: (ti[t], 0)),
                      up(0), up(0), down(0), up(1), up(1), down(1)],
            out_specs=pl.BlockSpec((MOE_TM, D_MODEL), lambda t, e0, e1, ti, nu: (t, 0))),
        out_shape=jax.ShapeDtypeStruct((n_rows, D_MODEL), F32),
        compiler_params=_cparams(("arbitrary",)),
        name="moe_routed",
    )(e0, e1, tile_idx, n_used, xs, wg_bf, wu_bf, wd_bf, wg_bf, wu_bf, wd_bf)


def _final_kernel(dest_ref, x1_ref, ys_ref, mf_ref, lg_ref, lb_ref, o_ref, buf, sem, *, tm):
    nj = pl.num_programs(1)
    step = pl.program_id(0) * nj + pl.program_id(1)
    n_steps = pl.num_programs(0) * nj
    slot = step % 2

    def fetch(s, sl):
        def issue(r, _):
            _row_copy(ys_ref, dest_ref[s * tm + r], buf.at[sl], r, sem.at[sl]).start()
            return 0
        lax.fori_loop(0, tm, issue, 0, unroll=DMA_UNROLL)

    @pl.when(step == 0)
    def _():
        fetch(step, slot)

    @pl.when(step + 1 < n_steps)
    def _():
        fetch(step + 1, 1 - slot)

    pltpu.make_async_copy(ys_ref.at[pl.ds(0, tm)], buf.at[slot], sem.at[slot]).wait()
    gate = mf_ref[0][:, 2 * D_MODEL:]
    o_ref[0] = _ln_plain(DEEPNORM_ALPHA * x1_ref[0] + gate * buf[slot]) * lg_ref[...] + lb_ref[...]


def _final(x1, ys, dest, mod_ffn, ln_g, ln_b, tm):
    b, t, _ = x1.shape
    tok = pl.BlockSpec((1, tm, D_MODEL), lambda i, j, *_: (i, j, 0))
    row = pl.BlockSpec((1, D_MODEL), lambda i, j, *_: (0, 0))
    return pl.pallas_call(
        functools.partial(_final_kernel, tm=tm),
        grid_spec=pltpu.PrefetchScalarGridSpec(
            num_scalar_prefetch=1, grid=(b, t // tm),
            in_specs=[tok, pl.BlockSpec(memory_space=pl.ANY),
                      pl.BlockSpec((1, 1, 3 * D_MODEL), lambda i, j, *_: (i, 0, 0)), row, row],
            out_specs=tok,
            scratch_shapes=[pltpu.VMEM((2, tm, D_MODEL), F32), pltpu.SemaphoreType.DMA((2,))]),
        out_shape=jax.ShapeDtypeStruct((b, t, D_MODEL), F32),
        compiler_params=_cparams(("arbitrary", "arbitrary")),
        name="final_ln",
    )(dest, x1, ys, mod_ffn.reshape(b, 1, -1), ln_g.reshape(1, -1), ln_b.reshape(1, -1))


def _trunk(x, mod_mix, mod_ffn, pos0, state0, k_past, v_past, wts):
    b, t, _ = x.shape
    tm = min(256, t)
    ret_in, sq, skt, svt = _in_proj(x, mod_mix, wts["w_in_a"], wts["w_in_kvt"], pos0, tm)
    mr, state = _retention(ret_in, state0, wts["ret_norm_g"], min(256, t))
    if k_past is None:
        ms = _sb_prompt(sq, skt, svt, wts["sb_norm_g"])
    else:
        ms = _sb_sample(sq, skt, svt, jnp.swapaxes(k_past, 2, 3), jnp.swapaxes(v_past, 2, 3),
                        wts["sb_norm_g"])
    sk, sv = jnp.swapaxes(skt, 2, 3), jnp.swapaxes(svt, 2, 3)
    x1, hx, route, counts = _out_proj(x, mr, ms, wts["w_out"], mod_mix, mod_ffn, wts["ln_mix_g"],
                                      wts["ln_mix_b"], wts["w_route"], wts["b_route"], min(512, t))
    n = b * t
    n_tiles = n // MOE_TM + N_BUCKETS
    dest, ztiles, e0, e1, tile_idx, n_used = _route_plan(route, counts, n_tiles)
    xs = _scatter_rows(hx.reshape(n, XW), dest, ztiles, n_tiles * MOE_TM)
    ys = _moe(xs, e0, e1, tile_idx, n_used, wts["w_e_gate"], wts["w_e_up"], wts["w_e_down"])
    out = _final(x1, ys, dest, mod_ffn, wts["ln_ffn_g"], wts["ln_ffn_b"], tm)
    return out, sk[None], sv[None], state[None]


def kernel(x_prompt, x_sample, cache_sb_k, cache_sb_v, state_ret, c_prompt, c_sample, w_in, w_out, ret_norm_g, sb_norm_g, w_ada_mix, b_ada_mix, ln_mix_g, ln_mix_b, w_ada_ffn, b_ada_ffn, ln_ffn_g, ln_ffn_b, w_group, b_group, w_router, b_router, w_e_gate, w_e_up, w_e_down):
    bp = x_prompt.shape[0]
    c_all = jnp.concatenate([c_prompt, c_sample], axis=0)
    mod_mix = _ada(c_all, w_ada_mix[0], b_ada_mix[0])
    mod_ffn = _ada(c_all, w_ada_ffn[0], b_ada_ffn[0])
    pad = LANES - N_GROUPS - N_EXPERTS
    w_route = jnp.concatenate([w_group[0], w_router[0], jnp.zeros((D_MODEL, pad), F32)], axis=1)
    b_route = jnp.concatenate([b_group[0], b_router[0], jnp.zeros((pad,), F32)]).reshape(1, LANES)
    n_a = 4 * RET_WIDTH + SB_WIDTH
    wts = dict(w_in_a=w_in[0, :, :n_a].astype(BF16), w_in_kvt=w_in[0, :, n_a:].T.astype(BF16),
               w_out=w_out[0].astype(BF16),
               ret_norm_g=ret_norm_g[0], sb_norm_g=sb_norm_g[0],
               ln_mix_g=ln_mix_g[0], ln_mix_b=ln_mix_b[0], ln_ffn_g=ln_ffn_g[0], ln_ffn_b=ln_ffn_b[0],
               w_route=w_route.astype(BF16), b_route=b_route,
               w_e_gate=w_e_gate[0].astype(BF16), w_e_up=w_e_up[0].astype(BF16),
               w_e_down=w_e_down[0].astype(BF16))
    ret_zero = jnp.zeros((bp, RET_HEADS, RET_DK, RET_DV), F32)
    y_p, k_p, v_p, r_p = _trunk(x_prompt, mod_mix[:bp], mod_ffn[:bp], 0, ret_zero, None, None, wts)
    y_s, k_s, v_s, r_s = _trunk(x_sample, mod_mix[bp:], mod_ffn[bp:], cache_sb_k.shape[3], state_ret[0],
                                cache_sb_k[0], cache_sb_v[0], wts)
    return (y_p, y_s, k_p, v_p, r_p, k_s, v_s, r_s)
```

```python
import functools
import math

import numpy as np
import jax
import jax.numpy as jnp
from jax import lax
from jax.experimental import pallas as pl
from jax.experimental.pallas import tpu as pltpu

D_MODEL = 1024
RET_HEADS = 4
RET_DK = 128
RET_DV = 128
RET_WIDTH = RET_HEADS * RET_DV
SB_HEADS = 8
SB_HEAD_DIM = 64
SB_WIDTH = SB_HEADS * SB_HEAD_DIM
IN_WIDTH = 2 * RET_HEADS * RET_DK + 2 * RET_WIDTH + 3 * SB_WIDTH
ROPE_BASE = 10000.0
N_GROUPS = 4
EXPERTS_PER_GROUP = 4
N_EXPERTS = N_GROUPS * EXPERTS_PER_GROUP
D_EXPERT = 512
DEPTH = 1
DEEPNORM_ALPHA = (2.0 * DEPTH) ** 0.25
LN_EPS = 1e-5

LANES = 128
VMEM_LIMIT = 48 * 1024 * 1024

F32 = jnp.float32
BF16 = jnp.bfloat16


def _cparams(sem, flags=None):
    return pltpu.CompilerParams(dimension_semantics=sem, vmem_limit_bytes=VMEM_LIMIT, flags=flags)


SB_FLAGS = None


def _dot(a, b):
    return jnp.dot(a, b, preferred_element_type=F32)


def _dot_nt(a, b):
    return lax.dot_general(a, b, (((1,), (1,)), ((), ())), preferred_element_type=F32)


def _dot_tn(a, b):
    return lax.dot_general(a, b, (((0,), (0,)), ((), ())), preferred_element_type=F32)


def _split_dot(a, w_hi, w_lo):
    a_hi = a.astype(BF16)
    a_lo = (a - a_hi.astype(F32)).astype(BF16)
    return _dot(a_hi, w_hi) + (_dot(a_hi, w_lo) + _dot(a_lo, w_hi))


def _ln_plain(x):
    mu = jnp.mean(x, axis=-1, keepdims=True)
    xc = x - mu
    var = jnp.mean(xc * xc, axis=-1, keepdims=True)
    return xc * lax.rsqrt(var + LN_EPS)


def _silu(x):
    return x * (1.0 / (1.0 + jnp.exp(-x)))


def _ada_kernel(c_ref, w_ref, b_ref, o_ref):
    c = c_ref[...]
    w = w_ref[...]
    w_hi = w.astype(BF16)
    w_lo = (w - w_hi.astype(F32)).astype(BF16)
    o_ref[...] = _split_dot(_silu(c), w_hi, w_lo) + b_ref[...]


def _ada(c, w, b):
    r = c.shape[0]
    tn = 768
    return pl.pallas_call(
        _ada_kernel,
        grid=(3 * D_MODEL // tn,),
        in_specs=[pl.BlockSpec((r, D_MODEL), lambda j: (0, 0)),
                  pl.BlockSpec((D_MODEL, tn), lambda j: (0, j)),
                  pl.BlockSpec((1, tn), lambda j: (0, j))],
        out_specs=pl.BlockSpec((r, tn), lambda j: (0, j)),
        out_shape=jax.ShapeDtypeStruct((r, 3 * D_MODEL), F32),
        compiler_params=_cparams(("arbitrary",)),
        name="ada_mod",
    )(c, w, b.reshape(1, -1))


def _in_proj_kernel(x_ref, mod_ref, w_ref, wkv_ref, tab_ref, ret_ref, q_ref, k_ref, v_ref):
    m = mod_ref[0]
    h = _ln_plain(x_ref[0]) * (1.0 + m[:, D_MODEL:2 * D_MODEL]) + m[:, :D_MODEL]
    h = h.astype(BF16)
    tab = tab_ref[...]
    for c in range(4):
        p = _dot(h, w_ref[:, c * RET_WIDTH:(c + 1) * RET_WIDTH])
        if c < 2:
            cs = tab[:, (2 * c) * LANES:(2 * c + 1) * LANES]
            sn = tab[:, (2 * c + 1) * LANES:(2 * c + 2) * LANES]
            for hh in range(RET_HEADS):
                ph = p[:, hh * RET_DK:(hh + 1) * RET_DK]
                ret_ref[0, :, c * RET_WIDTH + hh * RET_DK:c * RET_WIDTH + (hh + 1) * RET_DK] = (
                    ph * cs + pltpu.roll(ph, RET_DK // 2, 1) * sn)
        else:
            ret_ref[0, :, c * RET_WIDTH:(c + 1) * RET_WIDTH] = p
    base = 4 * RET_WIDTH
    p = _dot(h, w_ref[:, base:base + SB_WIDTH]) * (LOG2E * SB_HEAD_DIM ** -0.5)
    for hh in range(SB_HEADS):
        q_ref[0, hh] = p[:, hh * SB_HEAD_DIM:(hh + 1) * SB_HEAD_DIM].astype(q_ref.dtype)
    pt = _dot_nt(wkv_ref[...], h)
    for c, ref in enumerate((k_ref, v_ref)):
        for hh in range(SB_HEADS):
            r0 = c * SB_WIDTH + hh * SB_HEAD_DIM
            ref[0, hh] = pt[r0:r0 + SB_HEAD_DIM, :]


def _rope_table(pos0, t):
    half = RET_DK // 2
    inv = ROPE_BASE ** (-np.arange(half, dtype=np.float64) / half)
    ang = (pos0 + np.arange(t, dtype=np.float64))[:, None] * inv[None, :]
    cos, sin = np.cos(ang), np.sin(ang)
    cs = np.concatenate([cos, cos], axis=1)
    sn = np.concatenate([-sin, sin], axis=1)
    ks = RET_DK ** -0.5
    return jnp.asarray(np.concatenate([cs, sn, cs * ks, sn * ks], axis=1), dtype=F32)


def _in_proj(x, mod, w_a_bf, w_kvt_bf, pos0, tm):
    b, t, _ = x.shape
    tab = _rope_table(pos0, t)
    wa = w_a_bf.shape[1]
    hs = jax.ShapeDtypeStruct((b, SB_HEADS, SB_HEAD_DIM, t), F32)
    q_spec = pl.BlockSpec((1, SB_HEADS, tm, SB_HEAD_DIM), lambda i, j: (i, 0, j, 0))
    kv_spec = pl.BlockSpec((1, SB_HEADS, SB_HEAD_DIM, tm), lambda i, j: (i, 0, 0, j))
    return pl.pallas_call(
        _in_proj_kernel,
        grid=(b, t // tm),
        in_specs=[pl.BlockSpec((1, tm, D_MODEL), lambda i, j: (i, j, 0)),
                  pl.BlockSpec((1, 1, 3 * D_MODEL), lambda i, j: (i, 0, 0)),
                  pl.BlockSpec((D_MODEL, wa), lambda i, j: (0, 0)),
                  pl.BlockSpec((2 * SB_WIDTH, D_MODEL), lambda i, j: (0, 0)),
                  pl.BlockSpec((tm, 4 * LANES), lambda i, j: (j, 0))],
        out_specs=[pl.BlockSpec((1, tm, 4 * RET_WIDTH), lambda i, j: (i, j, 0)),
                   q_spec, kv_spec, kv_spec],
        out_shape=[jax.ShapeDtypeStruct((b, t, 4 * RET_WIDTH), F32),
                   jax.ShapeDtypeStruct((b, SB_HEADS, t, SB_HEAD_DIM), BF16), hs, hs],
        compiler_params=_cparams(("parallel", "arbitrary")),
        name="in_proj",
    )(x, mod.reshape(b, 1, -1), w_a_bf, w_kvt_bf, tab)


def _ret_kernel(q_ref, k_ref, v_ref, g_ref, s0_ref, dec_ref, qd_ref, kd_ref, ng_ref,
                o_ref, so_ref, st_ref, *, chunk_decay):
    j = pl.program_id(1)

    @pl.when(j == 0)
    def _():
        st_ref[...] = s0_ref[0]

    for hh in range(RET_HEADS):
        sl = slice(hh * RET_DK, (hh + 1) * RET_DK)
        q = q_ref[0, :, sl]
        k = k_ref[0, :, sl]
        v = v_ref[0, :, sl]
        vb = v.astype(BF16)
        st = st_ref[hh]
        scores = _dot_nt(q.astype(BF16), k.astype(BF16)) * dec_ref[hh]
        o = _dot(scores.astype(BF16), vb) + _dot((q * qd_ref[hh]).astype(BF16), st.astype(BF16))
        st_ref[hh] = st * chunk_decay[hh] + _dot_tn((k * kd_ref[hh]).astype(BF16), vb)
        o = _ln_plain(o) * ng_ref[:, sl] * _silu(g_ref[0, :, sl])
        o_ref[0, :, sl] = o.astype(o_ref.dtype)

    @pl.when(j == pl.num_programs(1) - 1)
    def _():
        so_ref[0] = st_ref[...]


def _retention(ret_in, state0, norm_g, chunk):
    b, t, _ = ret_in.shape
    lg = np.log1p(-np.exp2(-5.0 - np.arange(RET_HEADS, dtype=np.float64)))
    idx = np.arange(chunk, dtype=np.float64)
    rel = idx[:, None] - idx[None, :]
    dec = np.where(rel >= 0, np.exp(lg[:, None, None] * np.maximum(rel, 0.0)), 0.0)
    qd = np.broadcast_to(np.exp(lg[:, None] * (idx + 1.0))[:, :, None], (RET_HEADS, chunk, RET_DK))
    kd = np.broadcast_to(np.exp(lg[:, None] * (chunk - 1.0 - idx))[:, :, None], (RET_HEADS, chunk, RET_DK))
    chunk_decay = tuple(float(v) for v in np.exp(lg * chunk))

    def col(c):
        return pl.BlockSpec((1, chunk, RET_WIDTH), lambda i, j, c=c: (i, j, c))

    const3 = lambda shape: pl.BlockSpec(shape, lambda i, j: (0, 0, 0))
    state_spec = pl.BlockSpec((1, RET_HEADS, RET_DK, RET_DV), lambda i, j: (i, 0, 0, 0))
    return pl.pallas_call(
        functools.partial(_ret_kernel, chunk_decay=chunk_decay),
        grid=(b, t // chunk),
        in_specs=[col(0), col(1), col(2), col(3), state_spec,
                  const3((RET_HEADS, chunk, chunk)), const3((RET_HEADS, chunk, RET_DK)),
                  const3((RET_HEADS, chunk, RET_DK)),
                  pl.BlockSpec((1, RET_WIDTH), lambda i, j: (0, 0))],
        out_specs=[pl.BlockSpec((1, chunk, RET_WIDTH), lambda i, j: (i, j, 0)), state_spec],
        out_shape=[jax.ShapeDtypeStruct((b, t, RET_WIDTH), BF16),
                   jax.ShapeDtypeStruct((b, RET_HEADS, RET_DK, RET_DV), F32)],
        scratch_shapes=[pltpu.VMEM((RET_HEADS, RET_DK, RET_DV), F32)],
        compiler_params=_cparams(("parallel", "arbitrary")),
        name="retention",
    )(ret_in, ret_in, ret_in, ret_in, state0,
      jnp.asarray(dec, F32), jnp.asarray(qd, F32), jnp.asarray(kd, F32), norm_g.reshape(1, -1))


SB_KEY_BLOCK = 256


LOG2E = 1.4426950408889634


MASKED_LOGIT = -1e30
SB_TERMS = 1


def _sb_stage1(qs, kts, z_ref, hl_ref, masked):
    tq, kb = qs[0].shape[0], kts[0].shape[1]
    if masked:
        valid = (lax.broadcasted_iota(jnp.int32, (tq, kb), 1)
                 < lax.broadcasted_iota(jnp.int32, (tq, kb), 0))
    for c, (q, kt) in enumerate(zip(qs, kts)):
        z = _dot(q, kt.astype(BF16))
        p = jnp.maximum(z, 0.0) + jnp.log2(1.0 + jnp.exp2(-jnp.abs(z)))
        if masked:
            p = jnp.where(valid, p, 0.0)
            z = jnp.where(valid, z, MASKED_LOGIT)
        hi = p.astype(BF16)
        z_ref[c] = z
        hl_ref[c, :, :kb] = hi
        if SB_TERMS == 2:
            hl_ref[c, :, kb:] = (p - hi.astype(F32)).astype(BF16)


def _sb_stage2(z_ref, hl_ref, vts, tri2, acc_ref, car_ref):
    n, tq, kb = z_ref.shape
    r = _dot(hl_ref[...].reshape(n * tq, SB_TERMS * kb), tri2)
    for c in range(n):
        incl = r[c * tq:(c + 1) * tq]
        car = car_ref[c]
        w = jnp.exp2(z_ref[c] + incl + car)
        acc_ref[c] += _dot_nt(w.astype(BF16), vts[c].astype(BF16))
        car_ref[c] = car + incl[:, 0:1]


def _sb_finish(o, g):
    return o * lax.rsqrt(jnp.mean(o * o, axis=-1, keepdims=True) + LN_EPS) * g


def _tri_matrix():
    idx = np.arange(SB_KEY_BLOCK)
    t = -(idx[:, None] >= idx[None, :]).astype(np.float32)
    return jnp.asarray(np.concatenate([t] * SB_TERMS, axis=0), dtype=BF16)


def _sb_prompt_kernel(q_ref, kt_ref, vt_ref, tri_ref, g_ref, o_ref, acc_ref, car_ref,
                      za_ref, ha_ref, zb_ref, hb_ref, *, tq, nh):
    i = pl.program_id(2)
    qs = [q_ref[0, hh] for hh in range(nh)]
    acc_ref[...] = jnp.zeros_like(acc_ref)
    car_ref[...] = jnp.zeros_like(car_ref)

    def s1(blk, z_ref, hl_ref, masked=False):
        start = pl.multiple_of(blk * tq, tq)
        _sb_stage1(qs, [kt_ref[0, hh, :, pl.ds(start, tq)] for hh in range(nh)], z_ref, hl_ref, masked)

    def s2(blk, z_ref, hl_ref):
        start = pl.multiple_of(blk * tq, tq)
        _sb_stage2(z_ref, hl_ref, [vt_ref[0, hh, :, pl.ds(start, tq)] for hh in range(nh)],
                   tri_ref[...], acc_ref, car_ref)

    s1(i, za_ref, ha_ref, masked=True)

    def pair(p, _):
        blk = i - 2 * p
        s1(blk - 1, zb_ref, hb_ref)
        s2(blk, za_ref, ha_ref)
        s1(blk - 2, za_ref, ha_ref)
        s2(blk - 1, zb_ref, hb_ref)
        return 0

    lax.fori_loop(0, i // 2, pair, 0)

    @pl.when(i % 2 == 0)
    def _():
        s2(0, za_ref, ha_ref)

    @pl.when(i % 2 == 1)
    def _():
        s1(0, zb_ref, hb_ref)
        s2(1, za_ref, ha_ref)
        s2(0, zb_ref, hb_ref)

    d = SB_HEAD_DIM
    outs = [_sb_finish(acc_ref[hh], g_ref[0, :, hh * d:(hh + 1) * d]) for hh in range(nh)]
    o_ref[0] = jnp.concatenate(outs, axis=-1).astype(o_ref.dtype)


def _sb_prompt(q, kt, vt, norm_g):
    b, _, t, d = q.shape
    tq = SB_KEY_BLOCK
    nh = 4
    kv_spec = pl.BlockSpec((1, nh, d, t), lambda i, h, j: (i, h, 0, 0))
    return pl.pallas_call(
        functools.partial(_sb_prompt_kernel, tq=tq, nh=nh),
        grid=(b, SB_HEADS // nh, t // tq),
        in_specs=[pl.BlockSpec((1, nh, tq, d), lambda i, h, j: (i, h, j, 0)), kv_spec, kv_spec,
                  pl.BlockSpec((SB_TERMS * tq, tq), lambda i, h, j: (0, 0)),
                  pl.BlockSpec((1, 1, nh * d), lambda i, h, j: (h, 0, 0))],
        out_specs=pl.BlockSpec((1, tq, nh * d), lambda i, h, j: (i, j, h)),
        out_shape=jax.ShapeDtypeStruct((b, t, SB_WIDTH), BF16),
        scratch_shapes=[pltpu.VMEM((nh, tq, d), F32), pltpu.VMEM((nh, tq, 1), F32),
                        pltpu.VMEM((nh, tq, tq), F32), pltpu.VMEM((nh, tq, SB_TERMS * tq), BF16),
                        pltpu.VMEM((nh, tq, tq), F32), pltpu.VMEM((nh, tq, SB_TERMS * tq), BF16)],
        compiler_params=_cparams(("parallel", "parallel", "arbitrary"), SB_FLAGS),
        name="sb_prompt",
    )(q, kt, vt, _tri_matrix(), norm_g.reshape(SB_HEADS // nh, 1, nh * d))


def _sb_sample_kernel(q_ref, kt_ref, vt_ref, ktp_ref, vtp_ref, tri_ref, g_ref, o_ref, acc_ref, car_ref,
                      zd_ref, hd_ref, za_ref, ha_ref, zb_ref, hb_ref, *, nh):
    kb = SB_KEY_BLOCK
    n_past = ktp_ref.shape[3] // kb
    t = q_ref.shape[2]
    qs = [q_ref[0, hh] for hh in range(nh)]
    acc_ref[...] = jnp.zeros_like(acc_ref)
    car_ref[...] = jnp.zeros_like(car_ref)

    def s1(blk, z_ref, hl_ref):
        start = pl.multiple_of(blk * kb, kb)
        _sb_stage1(qs, [ktp_ref[0, hh, :, pl.ds(start, kb)] for hh in range(nh)], z_ref, hl_ref, False)

    def s2(blk, z_ref, hl_ref):
        start = pl.multiple_of(blk * kb, kb)
        _sb_stage2(z_ref, hl_ref, [vtp_ref[0, hh, :, pl.ds(start, kb)] for hh in range(nh)],
                   tri_ref[...], acc_ref, car_ref)

    _sb_stage1(qs, [kt_ref[0, hh] for hh in range(nh)], zd_ref, hd_ref, True)
    s1(n_past - 1, za_ref, ha_ref)
    _sb_stage2(zd_ref, hd_ref, [vt_ref[0, hh] for hh in range(nh)],
               jnp.concatenate([tri_ref[:t, :t]] * SB_TERMS, axis=0), acc_ref, car_ref)

    def pair(p, _):
        blk = n_past - 1 - 2 * p
        s1(blk - 1, zb_ref, hb_ref)
        s2(blk, za_ref, ha_ref)
        s1(blk - 2, za_ref, ha_ref)
        s2(blk - 1, zb_ref, hb_ref)
        return 0

    lax.fori_loop(0, (n_past - 1) // 2, pair, 0)
    if (n_past - 1) % 2 == 0:
        s2(0, za_ref, ha_ref)
    else:
        s1(0, zb_ref, hb_ref)
        s2(1, za_ref, ha_ref)
        s2(0, zb_ref, hb_ref)
    d = SB_HEAD_DIM
    outs = [_sb_finish(acc_ref[hh], g_ref[0, :, hh * d:(hh + 1) * d]) for hh in range(nh)]
    o_ref[0] = jnp.concatenate(outs, axis=-1).astype(o_ref.dtype)


def _sb_sample(q, kt, vt, kt_past, vt_past, norm_g):
    b, _, t, d = q.shape
    p = kt_past.shape[3]
    nh = 4
    kb = SB_KEY_BLOCK
    new_spec = pl.BlockSpec((1, nh, d, t), lambda i, h: (i, h, 0, 0))
    past_spec = pl.BlockSpec((1, nh, d, p), lambda i, h: (i, h, 0, 0))
    return pl.pallas_call(
        functools.partial(_sb_sample_kernel, nh=nh),
        grid=(b, SB_HEADS // nh),
        in_specs=[pl.BlockSpec((1, nh, t, d), lambda i, h: (i, h, 0, 0)), new_spec, new_spec,
                  past_spec, past_spec,
                  pl.BlockSpec((SB_TERMS * kb, kb), lambda i, h: (0, 0)),
                  pl.BlockSpec((1, 1, nh * d), lambda i, h: (h, 0, 0))],
        out_specs=pl.BlockSpec((1, t, nh * d), lambda i, h: (i, 0, h)),
        out_shape=jax.ShapeDtypeStruct((b, t, SB_WIDTH), BF16),
        scratch_shapes=[pltpu.VMEM((nh, t, d), F32), pltpu.VMEM((nh, t, 1), F32),
                        pltpu.VMEM((nh, t, t), F32), pltpu.VMEM((nh, t, SB_TERMS * t), BF16),
                        pltpu.VMEM((nh, t, kb), F32), pltpu.VMEM((nh, t, SB_TERMS * kb), BF16),
                        pltpu.VMEM((nh, t, kb), F32), pltpu.VMEM((nh, t, SB_TERMS * kb), BF16)],
        compiler_params=_cparams(("parallel", "arbitrary"), SB_FLAGS),
        name="sb_sample",
    )(q, kt, vt, kt_past, vt_past, _tri_matrix(), norm_g.reshape(SB_HEADS // nh, 1, nh * d))


N_PAIRS = 6
N_BUCKETS = N_GROUPS * N_PAIRS
MOE_TM = 256
XW = D_MODEL + LANES
_PAIRS = [(a, b) for a in range(EXPERTS_PER_GROUP) for b in range(a + 1, EXPERTS_PER_GROUP)]
BUCKET_E0 = np.array([g * EXPERTS_PER_GROUP + a for g in range(N_GROUPS) for a, _ in _PAIRS], np.int32)
BUCKET_E1 = np.array([g * EXPERTS_PER_GROUP + b for g in range(N_GROUPS) for _, b in _PAIRS], np.int32)


def _route(logits):
    lane = lax.broadcasted_iota(jnp.int32, logits.shape, 1)
    neg = -jnp.inf
    big = jnp.int32(2 * LANES)
    gl = jnp.where(lane < N_GROUPS, logits, neg)
    gmax = jnp.max(gl, axis=-1, keepdims=True)
    g_idx = jnp.min(jnp.where(gl == gmax, lane, big), axis=-1, keepdims=True)
    g_p = 1.0 / jnp.sum(jnp.exp(gl - gmax), axis=-1, keepdims=True)
    lo = N_GROUPS + g_idx * EXPERTS_PER_GROUP
    el = jnp.where((lane >= lo) & (lane < lo + EXPERTS_PER_GROUP), logits, neg)
    v1 = jnp.max(el, axis=-1, keepdims=True)
    i1 = jnp.min(jnp.where(el == v1, lane, big), axis=-1, keepdims=True)
    el2 = jnp.where(lane == i1, neg, el)
    v2 = jnp.max(el2, axis=-1, keepdims=True)
    i2 = jnp.min(jnp.where(el2 == v2, lane, big), axis=-1, keepdims=True)
    e21 = jnp.exp(v2 - v1)
    p1 = 1.0 / (1.0 + e21)
    p2 = e21 * p1
    first_lo = i1 < i2
    w_lo = jnp.where(first_lo, p1, p2) * g_p
    w_hi = jnp.where(first_lo, p2, p1) * g_p
    a = jnp.minimum(i1, i2) - lo
    b = jnp.maximum(i1, i2) - lo
    pair = jnp.where(a == 0, b - 1, jnp.where(a == 1, b + 1, 5))
    return g_idx * N_PAIRS + pair, w_lo, w_hi


def _out_proj_kernel(x_ref, mr_ref, ms_ref, wo_ref, mm_ref, mf_ref, lg_ref, lb_ref, wr_ref, br_ref,
                     tril_ref, x1_ref, hx_ref, rt_ref, cnt_ref, run_ref):
    @pl.when((pl.program_id(0) == 0) & (pl.program_id(1) == 0))
    def _():
        run_ref[...] = jnp.zeros_like(run_ref)

    mix = _dot(mr_ref[0], wo_ref[:MIX_HALF]) + _dot(ms_ref[0], wo_ref[MIX_HALF:])
    gate = mm_ref[0][:, 2 * D_MODEL:]
    x1 = _ln_plain(DEEPNORM_ALPHA * x_ref[0] + gate * mix) * lg_ref[...] + lb_ref[...]
    x1_ref[0] = x1
    mf = mf_ref[0]
    h2 = _ln_plain(x1) * (1.0 + mf[:, D_MODEL:2 * D_MODEL]) + mf[:, :D_MODEL]
    bucket, w_lo, w_hi = _route(_dot(h2.astype(BF16), wr_ref[...]) + br_ref[...])
    tm = h2.shape[0]
    lane = lax.broadcasted_iota(jnp.int32, (tm, LANES), 1)
    hx_ref[0, :, :D_MODEL] = h2
    hx_ref[0, :, D_MODEL:] = jnp.where(lane == 0, w_lo, jnp.where(lane == 1, w_hi, 0.0))
    hit = lane == bucket
    onehot = hit.astype(BF16)
    before = _dot(tril_ref[...], onehot)
    run = run_ref[...]
    rank = jnp.sum(jnp.where(hit, before + run, 0.0), axis=-1, keepdims=True)
    run = run + before[tm - 1:tm] + onehot[tm - 1:tm].astype(F32)
    run_ref[...] = run
    cnt_ref[...] = run
    rt_ref[0] = jnp.where(lane == 0, bucket.astype(F32), jnp.where(lane == 1, rank, 0.0))


MIX_HALF = RET_WIDTH


def _out_proj(x, mr, ms, w_out_bf, mod_mix, mod_ffn, ln_g, ln_b, w_route_bf, b_route, tm):
    b, t, _ = x.shape
    tok = lambda w: pl.BlockSpec((1, tm, w), lambda i, j: (i, j, 0))
    modspec = pl.BlockSpec((1, 1, 3 * D_MODEL), lambda i, j: (i, 0, 0))
    row = lambda w: pl.BlockSpec((1, w), lambda i, j: (0, 0))
    idx = np.arange(tm)
    tril = jnp.asarray(idx[:, None] > idx[None, :], dtype=BF16)
    return pl.pallas_call(
        _out_proj_kernel,
        grid=(b, t // tm),
        in_specs=[tok(D_MODEL), tok(RET_WIDTH), tok(SB_WIDTH),
                  pl.BlockSpec((D_MODEL, D_MODEL), lambda i, j: (0, 0)),
                  modspec, modspec, row(D_MODEL), row(D_MODEL),
                  pl.BlockSpec((D_MODEL, LANES), lambda i, j: (0, 0)), row(LANES),
                  pl.BlockSpec((tm, tm), lambda i, j: (0, 0))],
        out_specs=[tok(D_MODEL), tok(XW), tok(LANES), row(LANES)],
        out_shape=[jax.ShapeDtypeStruct((b, t, D_MODEL), F32),
                   jax.ShapeDtypeStruct((b, t, XW), F32),
                   jax.ShapeDtypeStruct((b, t, LANES), F32),
                   jax.ShapeDtypeStruct((1, LANES), F32)],
        scratch_shapes=[pltpu.VMEM((1, LANES), F32)],
        compiler_params=_cparams(("arbitrary", "arbitrary")),
        name="out_proj",
    )(x, mr, ms, w_out_bf, mod_mix.reshape(b, 1, -1), mod_ffn.reshape(b, 1, -1),
      ln_g.reshape(1, -1), ln_b.reshape(1, -1), w_route_bf, b_route, tril)


def _route_plan(route, counts, n_tiles):
    n = route.shape[0] * route.shape[1]
    bucket = route[..., 0].astype(jnp.int32).reshape(n)
    rank = route[..., 1].astype(jnp.int32).reshape(n)
    cnt = counts[0, :N_BUCKETS].astype(jnp.int32)
    padded = ((cnt + MOE_TM - 1) // MOE_TM) * MOE_TM
    ends = jnp.cumsum(padded)
    starts = ends - padded
    n_used = ends[-1] // MOE_TM
    tile = jnp.arange(n_tiles, dtype=jnp.int32)
    last = jnp.maximum(n_used - 1, 0)
    tile_idx = jnp.minimum(tile, last)
    tile_bucket = jnp.sum((ends[None, :] <= (tile_idx * MOE_TM)[:, None]).astype(jnp.int32), axis=1)
    tile_bucket = jnp.minimum(tile_bucket, N_BUCKETS - 1)
    e0 = jnp.asarray(BUCKET_E0)[tile_bucket]
    e1 = jnp.asarray(BUCKET_E1)[tile_bucket]
    in_bucket = bucket[:, None] == jnp.arange(N_BUCKETS, dtype=jnp.int32)[None, :]
    dest = rank + jnp.sum(jnp.where(in_bucket, starts[None, :], 0), axis=1)
    trailing = n_used + jnp.arange(N_BUCKETS, dtype=jnp.int32)
    ztiles = jnp.concatenate([jnp.where(padded > 0, ends - MOE_TM, -1),
                              jnp.where(trailing < n_tiles, trailing * MOE_TM, -1)])
    return (dest.astype(jnp.int32), ztiles.astype(jnp.int32), e0, e1, tile_idx,
            n_used.reshape(1).astype(jnp.int32))


def _row_copy(src_ref, s, dst_ref, d, sem):
    return pltpu.make_async_copy(src_ref.at[pl.ds(s, 1)], dst_ref.at[pl.ds(d, 1)], sem)


DMA_UNROLL = 8


def _scatter_kernel(dest_ref, ztile_ref, src_ref, out_ref, zbuf, sem, zsem, *, rows):
    @pl.when(pl.program_id(0) == 0)
    def _():
        zbuf[...] = jnp.zeros_like(zbuf)

        def fill(k, _):
            @pl.when(ztile_ref[k] >= 0)
            def _():
                row0 = pl.multiple_of(ztile_ref[k], MOE_TM)
                pltpu.make_async_copy(zbuf, out_ref.at[pl.ds(row0, MOE_TM)], zsem).start()
            return 0

        def drain(k, _):
            @pl.when(ztile_ref[k] >= 0)
            def _():
                pltpu.make_async_copy(zbuf, out_ref.at[pl.ds(0, MOE_TM)], zsem).wait()
            return 0

        lax.fori_loop(0, ztile_ref.shape[0], fill, 0)
        lax.fori_loop(0, ztile_ref.shape[0], drain, 0)

    base = pl.program_id(0) * rows

    def issue(r, _):
        _row_copy(src_ref, r, out_ref, dest_ref[base + r], sem).start()
        return 0

    lax.fori_loop(0, rows, issue, 0, unroll=DMA_UNROLL)
    pltpu.make_async_copy(src_ref, out_ref.at[pl.ds(0, rows)], sem).wait()


def _scatter_rows(hx, dest, ztiles, n_rows):
    n, w = hx.shape
    rows = min(256, n)
    return pl.pallas_call(
        functools.partial(_scatter_kernel, rows=rows),
        grid_spec=pltpu.PrefetchScalarGridSpec(
            num_scalar_prefetch=2, grid=(n // rows,),
            in_specs=[pl.BlockSpec((rows, w), lambda i, d, z: (i, 0))],
            out_specs=pl.BlockSpec(memory_space=pl.ANY),
            scratch_shapes=[pltpu.VMEM((MOE_TM, w), F32), pltpu.SemaphoreType.DMA(()),
                            pltpu.SemaphoreType.DMA(())]),
        out_shape=jax.ShapeDtypeStruct((n_rows, w), F32),
        compiler_params=_cparams(("arbitrary",)),
        name="scatter_rows",
    )(dest, ztiles, hx)


def _moe_kernel(e0_ref, e1_ref, ti_ref, nu_ref, x_ref, wg0, wu0, wd0, wg1, wu1, wd1, y_ref):
    del e0_ref, e1_ref, ti_ref

    @pl.when(pl.program_id(0) < nu_ref[0])
    def _():
        x = x_ref[...]
        h = x[:, :D_MODEL].astype(BF16)
        wx = x[:, D_MODEL:]
        lane = lax.broadcasted_iota(jnp.int32, wx.shape, 1)
        w_lo = jnp.sum(jnp.where(lane == 0, wx, 0.0), axis=-1, keepdims=True)
        w_hi = jnp.sum(jnp.where(lane == 1, wx, 0.0), axis=-1, keepdims=True)

        def expert(wg, wu, wd):
            a = _silu(_dot(h, wg[0])) * _dot(h, wu[0])
            return _dot(a.astype(BF16), wd[0])

        y_ref[...] = w_lo * expert(wg0, wu0, wd0) + w_hi * expert(wg1, wu1, wd1)

    @pl.when(pl.program_id(0) >= nu_ref[0])
    def _():
        y_ref[...] = jnp.zeros_like(y_ref)


def _moe(xs, e0, e1, tile_idx, n_used, wg_bf, wu_bf, wd_bf):
    n_rows = xs.shape[0]
    up = lambda sel: pl.BlockSpec((1, D_MODEL, D_EXPERT), lambda t, e0, e1, ti, nu: ((e0, e1)[sel][t], 0, 0))
    down = lambda sel: pl.BlockSpec((1, D_EXPERT, D_MODEL), lambda t, e0, e1, ti, nu: ((e0, e1)[sel][t], 0, 0))
    return pl.pallas_call(
        _moe_kernel,
        grid_spec=pltpu.PrefetchScalarGridSpec(
            num_scalar_prefetch=4, grid=(n_rows // MOE_TM,),
            in_specs=[pl.BlockSpec((MOE_TM, XW), lambda t, e0, e1, ti, nu: (ti[t], 0)),
                      up(0), up(0), down(0), up(1), up(1), down(1)],
            out_specs=pl.BlockSpec((MOE_TM, D_MODEL), lambda t, e0, e1, ti, nu: (t, 0))),
        out_shape=jax.ShapeDtypeStruct((n_rows, D_MODEL), F32),
        compiler_params=_cparams(("arbitrary",)),
        name="moe_routed",
    )(e0, e1, tile_idx, n_used, xs, wg_bf, wu_bf, wd_bf, wg_bf, wu_bf, wd_bf)


def _final_kernel(dest_ref, x1_ref, ys_ref, mf_ref, lg_ref, lb_ref, o_ref, buf, sem, *, tm):
    nj = pl.num_programs(1)
    step = pl.program_id(0) * nj + pl.program_id(1)
    n_steps = pl.num_programs(0) * nj
    slot = step % 2

    def fetch(s, sl):
        def issue(r, _):
            _row_copy(ys_ref, dest_ref[s * tm + r], buf.at[sl], r, sem.at[sl]).start()
            return 0
        lax.fori_loop(0, tm, issue, 0, unroll=DMA_UNROLL)

    @pl.when(step == 0)
    def _():
        fetch(step, slot)

    @pl.when(step + 1 < n_steps)
    def _():
        fetch(step + 1, 1 - slot)

    pltpu.make_async_copy(ys_ref.at[pl.ds(0, tm)], buf.at[slot], sem.at[slot]).wait()
    gate = mf_ref[0][:, 2 * D_MODEL:]
    o_ref[0] = _ln_plain(DEEPNORM_ALPHA * x1_ref[0] + gate * buf[slot]) * lg_ref[...] + lb_ref[...]


def _final(x1, ys, dest, mod_ffn, ln_g, ln_b, tm):
    b, t, _ = x1.shape
    tok = pl.BlockSpec((1, tm, D_MODEL), lambda i, j, *_: (i, j, 0))
    row = pl.BlockSpec((1, D_MODEL), lambda i, j, *_: (0, 0))
    return pl.pallas_call(
        functools.partial(_final_kernel, tm=tm),
        grid_spec=pltpu.PrefetchScalarGridSpec(
            num_scalar_prefetch=1, grid=(b, t // tm),
            in_specs=[tok, pl.BlockSpec(memory_space=pl.ANY),
                      pl.BlockSpec((1, 1, 3 * D_MODEL), lambda i, j, *_: (i, 0, 0)), row, row],
            out_specs=tok,
            scratch_shapes=[pltpu.VMEM((2, tm, D_MODEL), F32), pltpu.SemaphoreType.DMA((2,))]),
        out_shape=jax.ShapeDtypeStruct((b, t, D_MODEL), F32),
        compiler_params=_cparams(("arbitrary", "arbitrary")),
        name="final_ln",
    )(dest, x1, ys, mod_ffn.reshape(b, 1, -1), ln_g.reshape(1, -1), ln_b.reshape(1, -1))


def _trunk(x, mod_mix, mod_ffn, pos0, state0, k_past, v_past, wts):
    b, t, _ = x.shape
    tm = min(256, t)
    ret_in, sq, skt, svt = _in_proj(x, mod_mix, wts["w_in_a"], wts["w_in_kvt"], pos0, tm)
    mr, state = _retention(ret_in, state0, wts["ret_norm_g"], min(256, t))
    if k_past is None:
        ms = _sb_prompt(sq, skt, svt, wts["sb_norm_g"])
    else:
        ms = _sb_sample(sq, skt, svt, jnp.swapaxes(k_past, 2, 3), jnp.swapaxes(v_past, 2, 3),
                        wts["sb_norm_g"])
    sk, sv = jnp.swapaxes(skt, 2, 3), jnp.swapaxes(svt, 2, 3)
    x1, hx, route, counts = _out_proj(x, mr, ms, wts["w_out"], mod_mix, mod_ffn, wts["ln_mix_g"],
                                      wts["ln_mix_b"], wts["w_route"], wts["b_route"], min(512, t))
    n = b * t
    n_tiles = n // MOE_TM + N_BUCKETS
    dest, ztiles, e0, e1, tile_idx, n_used = _route_plan(route, counts, n_tiles)
    xs = _scatter_rows(hx.reshape(n, XW), dest, ztiles, n_tiles * MOE_TM)
    ys = _moe(xs, e0, e1, tile_idx, n_used, wts["w_e_gate"], wts["w_e_up"], wts["w_e_down"])
    out = _final(x1, ys, dest, mod_ffn, wts["ln_ffn_g"], wts["ln_ffn_b"], tm)
    return out, sk[None], sv[None], state[None]


def kernel(x_prompt, x_sample, cache_sb_k, cache_sb_v, state_ret, c_prompt, c_sample, w_in, w_out, ret_norm_g, sb_norm_g, w_ada_mix, b_ada_mix, ln_mix_g, ln_mix_b, w_ada_ffn, b_ada_ffn, ln_ffn_g, ln_ffn_b, w_group, b_group, w_router, b_router, w_e_gate, w_e_up, w_e_down):
    bp = x_prompt.shape[0]
    c_all = jnp.concatenate([c_prompt, c_sample], axis=0)
    mod_mix = _ada(c_all, w_ada_mix[0], b_ada_mix[0])
    mod_ffn = _ada(c_all, w_ada_ffn[0], b_ada_ffn[0])
    pad = LANES - N_GROUPS - N_EXPERTS
    w_route = jnp.concatenate([w_group[0], w_router[0], jnp.zeros((D_MODEL, pad), F32)], axis=1)
    b_route = jnp.concatenate([b_group[0], b_router[0], jnp.zeros((pad,), F32)]).reshape(1, LANES)
    n_a = 4 * RET_WIDTH + SB_WIDTH
    wts = dict(w_in_a=w_in[0, :, :n_a].astype(BF16), w_in_kvt=w_in[0, :, n_a:].T.astype(BF16),
               w_out=w_out[0].astype(BF16),
               ret_norm_g=ret_norm_g[0], sb_norm_g=sb_norm_g[0],
               ln_mix_g=ln_mix_g[0], ln_mix_b=ln_mix_b[0], ln_ffn_g=ln_ffn_g[0], ln_ffn_b=ln_ffn_b[0],
               w_route=w_route.astype(BF16), b_route=b_route,
               w_e_gate=w_e_gate[0].astype(BF16), w_e_up=w_e_up[0].astype(BF16),
               w_e_down=w_e_down[0].astype(BF16))
    ret_zero = jnp.zeros((bp, RET_HEADS, RET_DK, RET_DV), F32)
    y_p, k_p, v_p, r_p = _trunk(x_prompt, mod_mix[:bp], mod_ffn[:bp], 0, ret_zero, None, None, wts)
    y_s, k_s, v_s, r_s = _trunk(x_sample, mod_mix[bp:], mod_ffn[bp:], cache_sb_k.shape[3], state_ret[0],
                                cache_sb_k[0], cache_sb_v[0], wts)
    return (y_p, y_s, k_p, v_p, r_p, k_s, v_s, r_s)
```

```python
import functools
import math

import numpy as np
import jax
import jax.numpy as jnp
from jax import lax
from jax.experimental import pallas as pl
from jax.experimental.pallas import tpu as pltpu

D_MODEL = 1024
RET_HEADS = 4
RET_DK = 128
RET_DV = 128
RET_WIDTH = RET_HEADS * RET_DV
SB_HEADS = 8
SB_HEAD_DIM = 64
SB_WIDTH = SB_HEADS * SB_HEAD_DIM
IN_WIDTH = 2 * RET_HEADS * RET_DK + 2 * RET_WIDTH + 3 * SB_WIDTH
ROPE_BASE = 10000.0
N_GROUPS = 4
EXPERTS_PER_GROUP = 4
N_EXPERTS = N_GROUPS * EXPERTS_PER_GROUP
D_EXPERT = 512
DEPTH = 1
DEEPNORM_ALPHA = (2.0 * DEPTH) ** 0.25
LN_EPS = 1e-5

LANES = 128
VMEM_LIMIT = 48 * 1024 * 1024

F32 = jnp.float32
BF16 = jnp.bfloat16


def _cparams(sem, flags=None):
    return pltpu.CompilerParams(dimension_semantics=sem, vmem_limit_bytes=VMEM_LIMIT, flags=flags)


SB_FLAGS = None


def _dot(a, b):
    return jnp.dot(a, b, preferred_element_type=F32)


def _dot_nt(a, b):
    return lax.dot_general(a, b, (((1,), (1,)), ((), ())), preferred_element_type=F32)


def _dot_tn(a, b):
    return lax.dot_general(a, b, (((0,), (0,)), ((), ())), preferred_element_type=F32)


def _split_dot(a, w_hi, w_lo):
    a_hi = a.astype(BF16)
    a_lo = (a - a_hi.astype(F32)).astype(BF16)
    return _dot(a_hi, w_hi) + (_dot(a_hi, w_lo) + _dot(a_lo, w_hi))


def _ln_plain(x):
    mu = jnp.mean(x, axis=-1, keepdims=True)
    xc = x - mu
    var = jnp.mean(xc * xc, axis=-1, keepdims=True)
    return xc * lax.rsqrt(var + LN_EPS)


def _silu(x):
    return x * (1.0 / (1.0 + jnp.exp(-x)))


def _ada_kernel(c_ref, w_ref, b_ref, o_ref):
    c = c_ref[...]
    w = w_ref[...]
    w_hi = w.astype(BF16)
    w_lo = (w - w_hi.astype(F32)).astype(BF16)
    o_ref[...] = _split_dot(_silu(c), w_hi, w_lo) + b_ref[...]


def _ada(c, w, b):
    r = c.shape[0]
    tn = 768
    return pl.pallas_call(
        _ada_kernel,
        grid=(3 * D_MODEL // tn,),
        in_specs=[pl.BlockSpec((r, D_MODEL), lambda j: (0, 0)),
                  pl.BlockSpec((D_MODEL, tn), lambda j: (0, j)),
                  pl.BlockSpec((1, tn), lambda j: (0, j))],
        out_specs=pl.BlockSpec((r, tn), lambda j: (0, j)),
        out_shape=jax.ShapeDtypeStruct((r, 3 * D_MODEL), F32),
        compiler_params=_cparams(("arbitrary",)),
        name="ada_mod",
    )(c, w, b.reshape(1, -1))


def _in_proj_kernel(x_ref, mod_ref, w_ref, wkv_ref, tab_ref, ret_ref, q_ref, k_ref, v_ref):
    m = mod_ref[0]
    h = _ln_plain(x_ref[0]) * (1.0 + m[:, D_MODEL:2 * D_MODEL]) + m[:, :D_MODEL]
    h = h.astype(BF16)
    tab = tab_ref[...]
    for c in range(4):
        p = _dot(h, w_ref[:, c * RET_WIDTH:(c + 1) * RET_WIDTH])
        if c < 2:
            cs = tab[:, (2 * c) * LANES:(2 * c + 1) * LANES]
            sn = tab[:, (2 * c + 1) * LANES:(2 * c + 2) * LANES]
            for hh in range(RET_HEADS):
                ph = p[:, hh * RET_DK:(hh + 1) * RET_DK]
                ret_ref[0, :, c * RET_WIDTH + hh * RET_DK:c * RET_WIDTH + (hh + 1) * RET_DK] = (
                    ph * cs + pltpu.roll(ph, RET_DK // 2, 1) * sn)
        else:
            ret_ref[0, :, c * RET_WIDTH:(c + 1) * RET_WIDTH] = p
    base = 4 * RET_WIDTH
    p = _dot(h, w_ref[:, base:base + SB_WIDTH]) * (LOG2E * SB_HEAD_DIM ** -0.5)
    for hh in range(SB_HEADS):
        q_ref[0, hh] = p[:, hh * SB_HEAD_DIM:(hh + 1) * SB_HEAD_DIM].astype(q_ref.dtype)
    pt = _dot_nt(wkv_ref[...], h)
    for c, ref in enumerate((k_ref, v_ref)):
        for hh in range(SB_HEADS):
            r0 = c * SB_WIDTH + hh * SB_HEAD_DIM
            ref[0, hh] = pt[r0:r0 + SB_HEAD_DIM, :]


def _rope_table(pos0, t):
    half = RET_DK // 2
    inv = ROPE_BASE ** (-np.arange(half, dtype=np.float64) / half)
    ang = (pos0 + np.arange(t, dtype=np.float64))[:, None] * inv[None, :]
    cos, sin = np.cos(ang), np.sin(ang)
    cs = np.concatenate([cos, cos], axis=1)
    sn = np.concatenate([-sin, sin], axis=1)
    ks = RET_DK ** -0.5
    return jnp.asarray(np.concatenate([cs, sn, cs * ks, sn * ks], axis=1), dtype=F32)


def _in_proj(x, mod, w_a_bf, w_kvt_bf, pos0, tm):
    b, t, _ = x.shape
    tab = _rope_table(pos0, t)
    wa = w_a_bf.shape[1]
    hs = jax.ShapeDtypeStruct((b, SB_HEADS, SB_HEAD_DIM, t), F32)
    q_spec = pl.BlockSpec((1, SB_HEADS, tm, SB_HEAD_DIM), lambda i, j: (i, 0, j, 0))
    kv_spec = pl.BlockSpec((1, SB_HEADS, SB_HEAD_DIM, tm), lambda i, j: (i, 0, 0, j))
    return pl.pallas_call(
        _in_proj_kernel,
        grid=(b, t // tm),
        in_specs=[pl.BlockSpec((1, tm, D_MODEL), lambda i, j: (i, j, 0)),
                  pl.BlockSpec((1, 1, 3 * D_MODEL), lambda i, j: (i, 0, 0)),
                  pl.BlockSpec((D_MODEL, wa), lambda i, j: (0, 0)),
                  pl.BlockSpec((2 * SB_WIDTH, D_MODEL), lambda i, j: (0, 0)),
                  pl.BlockSpec((tm, 4 * LANES), lambda i, j: (j, 0))],
        out_specs=[pl.BlockSpec((1, tm, 4 * RET_WIDTH), lambda i, j: (i, j, 0)),
                   q_spec, kv_spec, kv_spec],
        out_shape=[jax.ShapeDtypeStruct((b, t, 4 * RET_WIDTH), F32),
                   jax.ShapeDtypeStruct((b, SB_HEADS, t, SB_HEAD_DIM), BF16), hs, hs],
        compiler_params=_cparams(("parallel", "arbitrary")),
        name="in_proj",
    )(x, mod.reshape(b, 1, -1), w_a_bf, w_kvt_bf, tab)


def _ret_kernel(q_ref, k_ref, v_ref, g_ref, s0_ref, dec_ref, qd_ref, kd_ref, ng_ref,
                o_ref, so_ref, st_ref, *, chunk_decay):
    j = pl.program_id(1)

    @pl.when(j == 0)
    def _():
        st_ref[...] = s0_ref[0]

    for hh in range(RET_HEADS):
        sl = slice(hh * RET_DK, (hh + 1) * RET_DK)
        q = q_ref[0, :, sl]
        k = k_ref[0, :, sl]
        v = v_ref[0, :, sl]
        vb = v.astype(BF16)
        st = st_ref[hh]
        scores = _dot_nt(q.astype(BF16), k.astype(BF16)) * dec_ref[hh]
        o = _dot(scores.astype(BF16), vb) + _dot((q * qd_ref[hh]).astype(BF16), st.astype(BF16))
        st_ref[hh] = st * chunk_decay[hh] + _dot_tn((k * kd_ref[hh]).astype(BF16), vb)
        o = _ln_plain(o) * ng_ref[:, sl] * _silu(g_ref[0, :, sl])
        o_ref[0, :, sl] = o.astype(o_ref.dtype)

    @pl.when(j == pl.num_programs(1) - 1)
    def _():
        so_ref[0] = st_ref[...]


def _retention(ret_in, state0, norm_g, chunk):
    b, t, _ = ret_in.shape
    lg = np.log1p(-np.exp2(-5.0 - np.arange(RET_HEADS, dtype=np.float64)))
    idx = np.arange(chunk, dtype=np.float64)
    rel = idx[:, None] - idx[None, :]
    dec = np.where(rel >= 0, np.exp(lg[:, None, None] * np.maximum(rel, 0.0)), 0.0)
    qd = np.broadcast_to(np.exp(lg[:, None] * (idx + 1.0))[:, :, None], (RET_HEADS, chunk, RET_DK))
    kd = np.broadcast_to(np.exp(lg[:, None] * (chunk - 1.0 - idx))[:, :, None], (RET_HEADS, chunk, RET_DK))
    chunk_decay = tuple(float(v) for v in np.exp(lg * chunk))

    def col(c):
        return pl.BlockSpec((1, chunk, RET_WIDTH), lambda i, j, c=c: (i, j, c))

    const3 = lambda shape: pl.BlockSpec(shape, lambda i, j: (0, 0, 0))
    state_spec = pl.BlockSpec((1, RET_HEADS, RET_DK, RET_DV), lambda i, j: (i, 0, 0, 0))
    return pl.pallas_call(
        functools.partial(_ret_kernel, chunk_decay=chunk_decay),
        grid=(b, t // chunk),
        in_specs=[col(0), col(1), col(2), col(3), state_spec,
                  const3((RET_HEADS, chunk, chunk)), const3((RET_HEADS, chunk, RET_DK)),
                  const3((RET_HEADS, chunk, RET_DK)),
                  pl.BlockSpec((1, RET_WIDTH), lambda i, j: (0, 0))],
        out_specs=[pl.BlockSpec((1, chunk, RET_WIDTH), lambda i, j: (i, j, 0)), state_spec],
        out_shape=[jax.ShapeDtypeStruct((b, t, RET_WIDTH), BF16),
                   jax.ShapeDtypeStruct((b, RET_HEADS, RET_DK, RET_DV), F32)],
        scratch_shapes=[pltpu.VMEM((RET_HEADS, RET_DK, RET_DV), F32)],
        compiler_params=_cparams(("parallel", "arbitrary")),
        name="retention",
    )(ret_in, ret_in, ret_in, ret_in, state0,
      jnp.asarray(dec, F32), jnp.asarray(qd, F32), jnp.asarray(kd, F32), norm_g.reshape(1, -1))


SB_KEY_BLOCK = 256


LOG2E = 1.4426950408889634


MASKED_LOGIT = -1e30
SB_TERMS = 1


def _sb_stage1(qs, kts, z_ref, hl_ref, masked):
    tq, kb = qs[0].shape[0], kts[0].shape[1]
    if masked:
        valid = (lax.broadcasted_iota(jnp.int32, (tq, kb), 1)
                 < lax.broadcasted_iota(jnp.int32, (tq, kb), 0))
    for c, (q, kt) in enumerate(zip(qs, kts)):
        z = _dot(q, kt.astype(BF16))
        p = jnp.maximum(z, 0.0) + jnp.log2(1.0 + jnp.exp2(-jnp.abs(z)))
        if masked:
            p = jnp.where(valid, p, 0.0)
            z = jnp.where(valid, z, MASKED_LOGIT)
        hi = p.astype(BF16)
        z_ref[c] = z
        hl_ref[c, :, :kb] = hi
        if SB_TERMS == 2:
            hl_ref[c, :, kb:] = (p - hi.astype(F32)).astype(BF16)


def _sb_stage2(z_ref, hl_ref, vts, tri2, acc_ref, car_ref):
    n, tq, kb = z_ref.shape
    r = _dot(hl_ref[...].reshape(n * tq, SB_TERMS * kb), tri2)
    for c in range(n):
        incl = r[c * tq:(c + 1) * tq]
        car = car_ref[c]
        w = jnp.exp2(z_ref[c] + incl + car)
        acc_ref[c] += _dot_nt(w.astype(BF16), vts[c].astype(BF16))
        car_ref[c] = car + incl[:, 0:1]


def _sb_finish(o, g):
    return o * lax.rsqrt(jnp.mean(o * o, axis=-1, keepdims=True) + LN_EPS) * g


def _tri_matrix():
    idx = np.arange(SB_KEY_BLOCK)
    t = -(idx[:, None] >= idx[None, :]).astype(np.float32)
    return jnp.asarray(np.concatenate([t] * SB_TERMS, axis=0), dtype=BF16)


def _sb_prompt_kernel(q_ref, kt_ref, vt_ref, tri_ref, g_ref, o_ref, acc_ref, car_ref,
                      za_ref, ha_ref, zb_ref, hb_ref, *, tq, nh):
    i = pl.program_id(2)
    qs = [q_ref[0, hh] for hh in range(nh)]
    acc_ref[...] = jnp.zeros_like(acc_ref)
    car_ref[...] = jnp.zeros_like(car_ref)

    def s1(blk, z_ref, hl_ref, masked=False):
        start = pl.multiple_of(blk * tq, tq)
        _sb_stage1(qs, [kt_ref[0, hh, :, pl.ds(start, tq)] for hh in range(nh)], z_ref, hl_ref, masked)

    def s2(blk, z_ref, hl_ref):
        start = pl.multiple_of(blk * tq, tq)
        _sb_stage2(z_ref, hl_ref, [vt_ref[0, hh, :, pl.ds(start, tq)] for hh in range(nh)],
                   tri_ref[...], acc_ref, car_ref)

    s1(i, za_ref, ha_ref, masked=True)

    def pair(p, _):
        blk = i - 2 * p
        s1(blk - 1, zb_ref, hb_ref)
        s2(blk, za_ref, ha_ref)
        s1(blk - 2, za_ref, ha_ref)
        s2(blk - 1, zb_ref, hb_ref)
        return 0

    lax.fori_loop(0, i // 2, pair, 0)

    @pl.when(i % 2 == 0)
    def _():
        s2(0, za_ref, ha_ref)

    @pl.when(i % 2 == 1)
    def _():
        s1(0, zb_ref, hb_ref)
        s2(1, za_ref, ha_ref)
        s2(0, zb_ref, hb_ref)

    d = SB_HEAD_DIM
    outs = [_sb_finish(acc_ref[hh], g_ref[0, :, hh * d:(hh + 1) * d]) for hh in range(nh)]
    o_ref[0] = jnp.concatenate(outs, axis=-1).astype(o_ref.dtype)


def _sb_prompt(q, kt, vt, norm_g):
    b, _, t, d = q.shape
    tq = SB_KEY_BLOCK
    nh = SB_HEADS
    kv_spec = pl.BlockSpec((1, nh, d, t), lambda i, h, j: (i, h, 0, 0))
    return pl.pallas_call(
        functools.partial(_sb_prompt_kernel, tq=tq, nh=nh),
        grid=(b, SB_HEADS // nh, t // tq),
        in_specs=[pl.BlockSpec((1, nh, tq, d), lambda i, h, j: (i, h, j, 0)), kv_spec, kv_spec,
                  pl.BlockSpec((SB_TERMS * tq, tq), lambda i, h, j: (0, 0)),
                  pl.BlockSpec((1, 1, nh * d), lambda i, h, j: (h, 0, 0))],
        out_specs=pl.BlockSpec((1, tq, nh * d), lambda i, h, j: (i, j, h)),
        out_shape=jax.ShapeDtypeStruct((b, t, SB_WIDTH), BF16),
        scratch_shapes=[pltpu.VMEM((nh, tq, d), F32), pltpu.VMEM((nh, tq, 1), F32),
                        pltpu.VMEM((nh, tq, tq), F32), pltpu.VMEM((nh, tq, SB_TERMS * tq), BF16),
                        pltpu.VMEM((nh, tq, tq), F32), pltpu.VMEM((nh, tq, SB_TERMS * tq), BF16)],
        compiler_params=_cparams(("parallel", "parallel", "arbitrary"), SB_FLAGS),
        name="sb_prompt",
    )(q, kt, vt, _tri_matrix(), norm_g.reshape(SB_HEADS // nh, 1, nh * d))


def _sb_sample_kernel(q_ref, kt_ref, vt_ref, ktp_ref, vtp_ref, tri_ref, g_ref, o_ref, acc_ref, car_ref,
                      zd_ref, hd_ref, za_ref, ha_ref, zb_ref, hb_ref, *, nh):
    kb = SB_KEY_BLOCK
    n_past = ktp_ref.shape[3] // kb
    t = q_ref.shape[2]
    qs = [q_ref[0, hh] for hh in range(nh)]
    acc_ref[...] = jnp.zeros_like(acc_ref)
    car_ref[...] = jnp.zeros_like(car_ref)

    def s1(blk, z_ref, hl_ref):
        start = pl.multiple_of(blk * kb, kb)
        _sb_stage1(qs, [ktp_ref[0, hh, :, pl.ds(start, kb)] for hh in range(nh)], z_ref, hl_ref, False)

    def s2(blk, z_ref, hl_ref):
        start = pl.multiple_of(blk * kb, kb)
        _sb_stage2(z_ref, hl_ref, [vtp_ref[0, hh, :, pl.ds(start, kb)] for hh in range(nh)],
                   tri_ref[...], acc_ref, car_ref)

    _sb_stage1(qs, [kt_ref[0, hh] for hh in range(nh)], zd_ref, hd_ref, True)
    s1(n_past - 1, za_ref, ha_ref)
    _sb_stage2(zd_ref, hd_ref, [vt_ref[0, hh] for hh in range(nh)],
               jnp.concatenate([tri_ref[:t, :t]] * SB_TERMS, axis=0), acc_ref, car_ref)

    def pair(p, _):
        blk = n_past - 1 - 2 * p
        s1(blk - 1, zb_ref, hb_ref)
        s2(blk, za_ref, ha_ref)
        s1(blk - 2, za_ref, ha_ref)
        s2(blk - 1, zb_ref, hb_ref)
        return 0

    lax.fori_loop(0, (n_past - 1) // 2, pair, 0)
    if (n_past - 1) % 2 == 0:
        s2(0, za_ref, ha_ref)
    else:
        s1(0, zb_ref, hb_ref)
        s2(1, za_ref, ha_ref)
        s2(0, zb_ref, hb_ref)
    d = SB_HEAD_DIM
    outs = [_sb_finish(acc_ref[hh], g_ref[0, :, hh * d:(hh + 1) * d]) for hh in range(nh)]
    o_ref[0] = jnp.concatenate(outs, axis=-1).astype(o_ref.dtype)


def _sb_sample(q, kt, vt, kt_past, vt_past, norm_g):
    b, _, t, d = q.shape
    p = kt_past.shape[3]
    nh = 4
    kb = SB_KEY_BLOCK
    new_spec = pl.BlockSpec((1, nh, d, t), lambda i, h: (i, h, 0, 0))
    past_spec = pl.BlockSpec((1, nh, d, p), lambda i, h: (i, h, 0, 0))
    return pl.pallas_call(
        functools.partial(_sb_sample_kernel, nh=nh),
        grid=(b, SB_HEADS // nh),
        in_specs=[pl.BlockSpec((1, nh, t, d), lambda i, h: (i, h, 0, 0)), new_spec, new_spec,
                  past_spec, past_spec,
                  pl.BlockSpec((SB_TERMS * kb, kb), lambda i, h: (0, 0)),
                  pl.BlockSpec((1, 1, nh * d), lambda i, h: (h, 0, 0))],
        out_specs=pl.BlockSpec((1, t, nh * d), lambda i, h: (i, 0, h)),
        out_shape=jax.ShapeDtypeStruct((b, t, SB_WIDTH), BF16),
        scratch_shapes=[pltpu.VMEM((nh, t, d), F32), pltpu.VMEM((nh, t, 1), F32),
                        pltpu.VMEM((nh, t, t), F32), pltpu.VMEM((nh, t, SB_TERMS * t), BF16),
                        pltpu.VMEM((nh, t, kb), F32), pltpu.VMEM((nh, t, SB_TERMS * kb), BF16),
                        pltpu.VMEM((nh, t, kb), F32), pltpu.VMEM((nh, t, SB_TERMS * kb), BF16)],
        compiler_params=_cparams(("parallel", "arbitrary"), SB_FLAGS),
        name="sb_sample",
    )(q, kt, vt, kt_past, vt_past, _tri_matrix(), norm_g.reshape(SB_HEADS // nh, 1, nh * d))


N_PAIRS = 6
N_BUCKETS = N_GROUPS * N_PAIRS
MOE_TM = 256
XW = D_MODEL + LANES
_PAIRS = [(a, b) for a in range(EXPERTS_PER_GROUP) for b in range(a + 1, EXPERTS_PER_GROUP)]
BUCKET_E0 = np.array([g * EXPERTS_PER_GROUP + a for g in range(N_GROUPS) for a, _ in _PAIRS], np.int32)
BUCKET_E1 = np.array([g * EXPERTS_PER_GROUP + b for g in range(N_GROUPS) for _, b in _PAIRS], np.int32)


def _route(logits):
    lane = lax.broadcasted_iota(jnp.int32, logits.shape, 1)
    neg = -jnp.inf
    big = jnp.int32(2 * LANES)
    gl = jnp.where(lane < N_GROUPS, logits, neg)
    gmax = jnp.max(gl, axis=-1, keepdims=True)
    g_idx = jnp.min(jnp.where(gl == gmax, lane, big), axis=-1, keepdims=True)
    g_p = 1.0 / jnp.sum(jnp.exp(gl - gmax), axis=-1, keepdims=True)
    lo = N_GROUPS + g_idx * EXPERTS_PER_GROUP
    el = jnp.where((lane >= lo) & (lane < lo + EXPERTS_PER_GROUP), logits, neg)
    v1 = jnp.max(el, axis=-1, keepdims=True)
    i1 = jnp.min(jnp.where(el == v1, lane, big), axis=-1, keepdims=True)
    el2 = jnp.where(lane == i1, neg, el)
    v2 = jnp.max(el2, axis=-1, keepdims=True)
    i2 = jnp.min(jnp.where(el2 == v2, lane, big), axis=-1, keepdims=True)
    e21 = jnp.exp(v2 - v1)
    p1 = 1.0 / (1.0 + e21)
    p2 = e21 * p1
    first_lo = i1 < i2
    w_lo = jnp.where(first_lo, p1, p2) * g_p
    w_hi = jnp.where(first_lo, p2, p1) * g_p
    a = jnp.minimum(i1, i2) - lo
    b = jnp.maximum(i1, i2) - lo
    pair = jnp.where(a == 0, b - 1, jnp.where(a == 1, b + 1, 5))
    return g_idx * N_PAIRS + pair, w_lo, w_hi


def _out_proj_kernel(x_ref, mr_ref, ms_ref, wo_ref, mm_ref, mf_ref, lg_ref, lb_ref, wr_ref, br_ref,
                     tril_ref, cin_ref, x1_ref, hx_ref, rt_ref, cnt_ref, run_ref):
    @pl.when((pl.program_id(0) == 0) & (pl.program_id(1) == 0))
    def _():
        run_ref[...] = cin_ref[...]

    mix = _dot(mr_ref[0], wo_ref[:MIX_HALF]) + _dot(ms_ref[0], wo_ref[MIX_HALF:])
    gate = mm_ref[0][:, 2 * D_MODEL:]
    x1 = _ln_plain(DEEPNORM_ALPHA * x_ref[0] + gate * mix) * lg_ref[...] + lb_ref[...]
    x1_ref[0] = x1
    mf = mf_ref[0]
    h2 = _ln_plain(x1) * (1.0 + mf[:, D_MODEL:2 * D_MODEL]) + mf[:, :D_MODEL]
    bucket, w_lo, w_hi = _route(_dot(h2.astype(BF16), wr_ref[...]) + br_ref[...])
    tm = h2.shape[0]
    lane = lax.broadcasted_iota(jnp.int32, (tm, LANES), 1)
    hx_ref[0, :, :D_MODEL] = h2
    hx_ref[0, :, D_MODEL:] = jnp.where(lane == 0, w_lo, jnp.where(lane == 1, w_hi, 0.0))
    hit = lane == bucket
    onehot = hit.astype(BF16)
    before = _dot(tril_ref[...], onehot)
    run = run_ref[...]
    rank = jnp.sum(jnp.where(hit, before + run, 0.0), axis=-1, keepdims=True)
    run = run + before[tm - 1:tm] + onehot[tm - 1:tm].astype(F32)
    run_ref[...] = run
    cnt_ref[...] = run
    rt_ref[0] = jnp.where(lane == 0, bucket.astype(F32), jnp.where(lane == 1, rank, 0.0))


MIX_HALF = RET_WIDTH


def _out_proj(x, mr, ms, w_out_bf, mod_mix, mod_ffn, ln_g, ln_b, w_route_bf, b_route, counts_in, tm):
    b, t, _ = x.shape
    tok = lambda w: pl.BlockSpec((1, tm, w), lambda i, j: (i, j, 0))
    modspec = pl.BlockSpec((1, 1, 3 * D_MODEL), lambda i, j: (i, 0, 0))
    row = lambda w: pl.BlockSpec((1, w), lambda i, j: (0, 0))
    idx = np.arange(tm)
    tril = jnp.asarray(idx[:, None] > idx[None, :], dtype=BF16)
    return pl.pallas_call(
        _out_proj_kernel,
        grid=(b, t // tm),
        in_specs=[tok(D_MODEL), tok(RET_WIDTH), tok(SB_WIDTH),
                  pl.BlockSpec((D_MODEL, D_MODEL), lambda i, j: (0, 0)),
                  modspec, modspec, row(D_MODEL), row(D_MODEL),
                  pl.BlockSpec((D_MODEL, LANES), lambda i, j: (0, 0)), row(LANES),
                  pl.BlockSpec((tm, tm), lambda i, j: (0, 0)), row(LANES)],
        out_specs=[tok(D_MODEL), tok(XW), tok(LANES), row(LANES)],
        out_shape=[jax.ShapeDtypeStruct((b, t, D_MODEL), F32),
                   jax.ShapeDtypeStruct((b, t, XW), F32),
                   jax.ShapeDtypeStruct((b, t, LANES), F32),
                   jax.ShapeDtypeStruct((1, LANES), F32)],
        scratch_shapes=[pltpu.VMEM((1, LANES), F32)],
        compiler_params=_cparams(("arbitrary", "arbitrary")),
        name="out_proj",
    )(x, mr, ms, w_out_bf, mod_mix.reshape(b, 1, -1), mod_ffn.reshape(b, 1, -1),
      ln_g.reshape(1, -1), ln_b.reshape(1, -1), w_route_bf, b_route, tril, counts_in)


def _route_plan(routes, counts, n_tiles):
    bucket = jnp.concatenate([r[..., 0].astype(jnp.int32).reshape(-1) for r in routes])
    rank = jnp.concatenate([r[..., 1].astype(jnp.int32).reshape(-1) for r in routes])
    cnt = counts[0, :N_BUCKETS].astype(jnp.int32)
    padded = ((cnt + MOE_TM - 1) // MOE_TM) * MOE_TM
    ends = jnp.cumsum(padded)
    starts = ends - padded
    n_used = ends[-1] // MOE_TM
    tile = jnp.arange(n_tiles, dtype=jnp.int32)
    last = jnp.maximum(n_used - 1, 0)
    tile_idx = jnp.minimum(tile, last)
    tile_bucket = jnp.sum((ends[None, :] <= (tile_idx * MOE_TM)[:, None]).astype(jnp.int32), axis=1)
    tile_bucket = jnp.minimum(tile_bucket, N_BUCKETS - 1)
    e0 = jnp.asarray(BUCKET_E0)[tile_bucket]
    e1 = jnp.asarray(BUCKET_E1)[tile_bucket]
    in_bucket = bucket[:, None] == jnp.arange(N_BUCKETS, dtype=jnp.int32)[None, :]
    dest = rank + jnp.sum(jnp.where(in_bucket, starts[None, :], 0), axis=1)
    trailing = n_used + jnp.arange(N_BUCKETS, dtype=jnp.int32)
    ztiles = jnp.concatenate([jnp.where(padded > 0, ends - MOE_TM, -1),
                              jnp.where(trailing < n_tiles, trailing * MOE_TM, -1)])
    return (dest.astype(jnp.int32), ztiles.astype(jnp.int32), e0, e1, tile_idx,
            n_used.reshape(1).astype(jnp.int32))


def _row_copy(src_ref, s, dst_ref, d, sem):
    return pltpu.make_async_copy(src_ref.at[pl.ds(s, 1)], dst_ref.at[pl.ds(d, 1)], sem)


DMA_UNROLL = 8


def _scatter_kernel(dest_ref, ztile_ref, *refs, rows, steps):
    srcs = refs[:len(steps)]
    out_ref, zbuf, sem, zsem = refs[len(steps):]
    i = pl.program_id(0)

    @pl.when(i == 0)
    def _():
        zbuf[...] = jnp.zeros_like(zbuf)

        def fill(k, _):
            @pl.when(ztile_ref[k] >= 0)
            def _():
                row0 = pl.multiple_of(ztile_ref[k], MOE_TM)
                pltpu.make_async_copy(zbuf, out_ref.at[pl.ds(row0, MOE_TM)], zsem).start()
            return 0

        def drain(k, _):
            @pl.when(ztile_ref[k] >= 0)
            def _():
                pltpu.make_async_copy(zbuf, out_ref.at[pl.ds(0, MOE_TM)], zsem).wait()
            return 0

        lax.fori_loop(0, ztile_ref.shape[0], fill, 0)
        lax.fori_loop(0, ztile_ref.shape[0], drain, 0)

    base = i * rows
    first = 0
    for src_ref, n_steps in zip(srcs, steps):
        @pl.when((i >= first) & (i < first + n_steps))
        def _(src_ref=src_ref):
            def issue(r, _):
                _row_copy(src_ref, r, out_ref, dest_ref[base + r], sem).start()
                return 0

            lax.fori_loop(0, rows, issue, 0, unroll=DMA_UNROLL)
            pltpu.make_async_copy(src_ref, out_ref.at[pl.ds(0, rows)], sem).wait()
        first += n_steps


def _scatter_rows(hxs, dest, ztiles, n_rows):
    w = hxs[0].shape[1]
    rows = min([256] + [h.shape[0] for h in hxs])
    steps = tuple(h.shape[0] // rows for h in hxs)
    firsts = [sum(steps[:k]) for k in range(len(steps))]
    in_specs = [pl.BlockSpec((rows, w), lambda i, d, z, f=f, s=s: (jnp.clip(i - f, 0, s - 1), 0))
                for f, s in zip(firsts, steps)]
    return pl.pallas_call(
        functools.partial(_scatter_kernel, rows=rows, steps=steps),
        grid_spec=pltpu.PrefetchScalarGridSpec(
            num_scalar_prefetch=2, grid=(sum(steps),),
            in_specs=in_specs,
            out_specs=pl.BlockSpec(memory_space=pl.ANY),
            scratch_shapes=[pltpu.VMEM((MOE_TM, w), F32), pltpu.SemaphoreType.DMA(()),
                            pltpu.SemaphoreType.DMA(())]),
        out_shape=jax.ShapeDtypeStruct((n_rows, w), F32),
        compiler_params=_cparams(("arbitrary",)),
        name="scatter_rows",
    )(dest, ztiles, *hxs)


def _moe_kernel(e0_ref, e1_ref, ti_ref, nu_ref, x_ref, wg0, wu0, wd0, wg1, wu1, wd1, y_ref):
    del e0_ref, e1_ref, ti_ref

    @pl.when(pl.program_id(0) < nu_ref[0])
    def _():
        x = x_ref[...]
        h = x[:, :D_MODEL].astype(BF16)
        wx = x[:, D_MODEL:]
        lane = lax.broadcasted_iota(jnp.int32, wx.shape, 1)
        w_lo = jnp.sum(jnp.where(lane == 0, wx, 0.0), axis=-1, keepdims=True)
        w_hi = jnp.sum(jnp.where(lane == 1, wx, 0.0), axis=-1, keepdims=True)

        def expert(wg, wu, wd):
            a = _silu(_dot(h, wg[0])) * _dot(h, wu[0])
            return _dot(a.astype(BF16), wd[0])

        y_ref[...] = w_lo * expert(wg0, wu0, wd0) + w_hi * expert(wg1, wu1, wd1)

    @pl.when(pl.program_id(0) >= nu_ref[0])
    def _():
        y_ref[...] = jnp.zeros_like(y_ref)


def _moe(xs, e0, e1, tile_idx, n_used, wg_bf, wu_bf, wd_bf):
    n_rows = xs.shape[0]
    up = lambda sel: pl.BlockSpec((1, D_MODEL, D_EXPERT), lambda t, e0, e1, ti, nu: ((e0, e1)[sel][t], 0, 0))
    down = lambda sel: pl.BlockSpec((1, D_EXPERT, D_MODEL), lambda t, e0, e1, ti, nu: ((e0, e1)[sel][t], 0, 0))
    return pl.pallas_call(
        _moe_kernel,
        grid_spec=pltpu.PrefetchScalarGridSpec(
            num_scalar_prefetch=4, grid=(n_rows // MOE_TM,),
            in_specs=[pl.BlockSpec((MOE_TM, XW), lambda t, e0, e1, ti, nu: (ti[t], 0)),
                      up(0), up(0), down(0), up(1), up(1), down(1)],
            out_specs=pl.BlockSpec((MOE_TM, D_MODEL), lambda t, e0, e1, ti, nu: (t, 0))),
        out_shape=jax.ShapeDtypeStruct((n_rows, D_MODEL), F32),
        compiler_params=_cparams(("arbitrary",)),
        name="moe_routed",
    )(e0, e1, tile_idx, n_used, xs, wg_bf, wu_bf, wd_bf, wg_bf, wu_bf, wd_bf)


def _final_kernel(dest_ref, x1_ref, ys_ref, mf_ref, lg_ref, lb_ref, o_ref, buf, sem, *, tm):
    nj = pl.num_programs(1)
    step = pl.program_id(0) * nj + pl.program_id(1)
    n_steps = pl.num_programs(0) * nj
    slot = step % 2

    def fetch(s, sl):
        def issue(r, _):
            _row_copy(ys_ref, dest_ref[s * tm + r], buf.at[sl], r, sem.at[sl]).start()
            return 0
        lax.fori_loop(0, tm, issue, 0, unroll=DMA_UNROLL)

    @pl.when(step == 0)
    def _():
        fetch(step, slot)

    @pl.when(step + 1 < n_steps)
    def _():
        fetch(step + 1, 1 - slot)

    pltpu.make_async_copy(ys_ref.at[pl.ds(0, tm)], buf.at[slot], sem.at[slot]).wait()
    gate = mf_ref[0][:, 2 * D_MODEL:]
    o_ref[0] = _ln_plain(DEEPNORM_ALPHA * x1_ref[0] + gate * buf[slot]) * lg_ref[...] + lb_ref[...]


def _final(x1, ys, dest, mod_ffn, ln_g, ln_b, tm):
    b, t, _ = x1.shape
    tok = pl.BlockSpec((1, tm, D_MODEL), lambda i, j, *_: (i, j, 0))
    row = pl.BlockSpec((1, D_MODEL), lambda i, j, *_: (0, 0))
    return pl.pallas_call(
        functools.partial(_final_kernel, tm=tm),
        grid_spec=pltpu.PrefetchScalarGridSpec(
            num_scalar_prefetch=1, grid=(b, t // tm),
            in_specs=[tok, pl.BlockSpec(memory_space=pl.ANY),
                      pl.BlockSpec((1, 1, 3 * D_MODEL), lambda i, j, *_: (i, 0, 0)), row, row],
            out_specs=tok,
            scratch_shapes=[pltpu.VMEM((2, tm, D_MODEL), F32), pltpu.SemaphoreType.DMA((2,))]),
        out_shape=jax.ShapeDtypeStruct((b, t, D_MODEL), F32),
        compiler_params=_cparams(("arbitrary", "arbitrary")),
        name="final_ln",
    )(dest, x1, ys, mod_ffn.reshape(b, 1, -1), ln_g.reshape(1, -1), ln_b.reshape(1, -1))


def _mixer_half(x, mod_mix, mod_ffn, pos0, state0, k_past, v_past, counts_in, wts):
    b, t, _ = x.shape
    ret_in, sq, skt, svt = _in_proj(x, mod_mix, wts["w_in_a"], wts["w_in_kvt"], pos0, min(512, t))
    mr, state = _retention(ret_in, state0, wts["ret_norm_g"], min(512, t))
    if k_past is None:
        ms = _sb_prompt(sq, skt, svt, wts["sb_norm_g"])
    else:
        ms = _sb_sample(sq, skt, svt, jnp.swapaxes(k_past, 2, 3), jnp.swapaxes(v_past, 2, 3),
                        wts["sb_norm_g"])
    sk, sv = jnp.swapaxes(skt, 2, 3), jnp.swapaxes(svt, 2, 3)
    x1, hx, route, counts = _out_proj(x, mr, ms, wts["w_out"], mod_mix, mod_ffn, wts["ln_mix_g"],
                                      wts["ln_mix_b"], wts["w_route"], wts["b_route"], counts_in,
                                      min(512, t))
    return dict(x1=x1, hx=hx.reshape(b * t, XW), route=route, counts=counts, mod_ffn=mod_ffn,
                sk=sk[None], sv=sv[None], state=state[None])


def _ffn_half(groups, wts):
    n = sum(g["hx"].shape[0] for g in groups)
    n_tiles = n // MOE_TM + N_BUCKETS
    dest, ztiles, e0, e1, tile_idx, n_used = _route_plan([g["route"] for g in groups],
                                                         groups[-1]["counts"], n_tiles)
    xs = _scatter_rows([g["hx"] for g in groups], dest, ztiles, n_tiles * MOE_TM)
    ys = _moe(xs, e0, e1, tile_idx, n_used, wts["w_e_gate"], wts["w_e_up"], wts["w_e_down"])
    outs, first = [], 0
    for g in groups:
        t = g["x1"].shape[1]
        n_g = g["hx"].shape[0]
        outs.append(_final(g["x1"], ys, dest[first:first + n_g], g["mod_ffn"], wts["ln_ffn_g"],
                           wts["ln_ffn_b"], min(256, t)))
        first += n_g
    return outs


def kernel(x_prompt, x_sample, cache_sb_k, cache_sb_v, state_ret, c_prompt, c_sample, w_in, w_out, ret_norm_g, sb_norm_g, w_ada_mix, b_ada_mix, ln_mix_g, ln_mix_b, w_ada_ffn, b_ada_ffn, ln_ffn_g, ln_ffn_b, w_group, b_group, w_router, b_router, w_e_gate, w_e_up, w_e_down):
    bp = x_prompt.shape[0]
    c_all = jnp.concatenate([c_prompt, c_sample], axis=0)
    mod_mix = _ada(c_all, w_ada_mix[0], b_ada_mix[0])
    mod_ffn = _ada(c_all, w_ada_ffn[0], b_ada_ffn[0])
    pad = LANES - N_GROUPS - N_EXPERTS
    w_route = jnp.concatenate([w_group[0], w_router[0], jnp.zeros((D_MODEL, pad), F32)], axis=1)
    b_route = jnp.concatenate([b_group[0], b_router[0], jnp.zeros((pad,), F32)]).reshape(1, LANES)
    n_a = 4 * RET_WIDTH + SB_WIDTH
    wts = dict(w_in_a=w_in[0, :, :n_a].astype(BF16), w_in_kvt=w_in[0, :, n_a:].T.astype(BF16),
               w_out=w_out[0].astype(BF16),
               ret_norm_g=ret_norm_g[0], sb_norm_g=sb_norm_g[0],
               ln_mix_g=ln_mix_g[0], ln_mix_b=ln_mix_b[0], ln_ffn_g=ln_ffn_g[0], ln_ffn_b=ln_ffn_b[0],
               w_route=w_route.astype(BF16), b_route=b_route,
               w_e_gate=w_e_gate[0].astype(BF16), w_e_up=w_e_up[0].astype(BF16),
               w_e_down=w_e_down[0].astype(BF16))
    ret_zero = jnp.zeros((bp, RET_HEADS, RET_DK, RET_DV), F32)
    gp = _mixer_half(x_prompt, mod_mix[:bp], mod_ffn[:bp], 0, ret_zero, None, None,
                     jnp.zeros((1, LANES), F32), wts)
    gs = _mixer_half(x_sample, mod_mix[bp:], mod_ffn[bp:], cache_sb_k.shape[3], state_ret[0],
                     cache_sb_k[0], cache_sb_v[0], gp["counts"], wts)
    y_p, y_s = _ffn_half([gp, gs], wts)
    return (y_p, y_s, gp["sk"], gp["sv"], gp["state"], gs["sk"], gs["sv"], gs["state"])
```

```python
import functools
import math

import numpy as np
import jax
import jax.numpy as jnp
from jax import lax
from jax.experimental import pallas as pl
from jax.experimental.pallas import tpu as pltpu

D_MODEL = 1024
RET_HEADS = 4
RET_DK = 128
RET_DV = 128
RET_WIDTH = RET_HEADS * RET_DV
SB_HEADS = 8
SB_HEAD_DIM = 64
SB_WIDTH = SB_HEADS * SB_HEAD_DIM
IN_WIDTH = 2 * RET_HEADS * RET_DK + 2 * RET_WIDTH + 3 * SB_WIDTH
ROPE_BASE = 10000.0
N_GROUPS = 4
EXPERTS_PER_GROUP = 4
N_EXPERTS = N_GROUPS * EXPERTS_PER_GROUP
D_EXPERT = 512
DEPTH = 1
DEEPNORM_ALPHA = (2.0 * DEPTH) ** 0.25
LN_EPS = 1e-5

LANES = 128
VMEM_LIMIT = 48 * 1024 * 1024

F32 = jnp.float32
BF16 = jnp.bfloat16


def _cparams(sem, flags=None):
    return pltpu.CompilerParams(dimension_semantics=sem, vmem_limit_bytes=VMEM_LIMIT, flags=flags)


SB_FLAGS = None


def _dot(a, b):
    return jnp.dot(a, b, preferred_element_type=F32)


def _dot_nt(a, b):
    return lax.dot_general(a, b, (((1,), (1,)), ((), ())), preferred_element_type=F32)


def _dot_tn(a, b):
    return lax.dot_general(a, b, (((0,), (0,)), ((), ())), preferred_element_type=F32)


def _split_dot(a, w_hi, w_lo):
    a_hi = a.astype(BF16)
    a_lo = (a - a_hi.astype(F32)).astype(BF16)
    return _dot(a_hi, w_hi) + (_dot(a_hi, w_lo) + _dot(a_lo, w_hi))


def _ln_plain(x):
    mu = jnp.mean(x, axis=-1, keepdims=True)
    xc = x - mu
    var = jnp.mean(xc * xc, axis=-1, keepdims=True)
    return xc * lax.rsqrt(var + LN_EPS)


def _silu(x):
    return x * (1.0 / (1.0 + jnp.exp(-x)))


def _ada_kernel(c_ref, w_ref, b_ref, o_ref):
    c = c_ref[...]
    w = w_ref[...]
    w_hi = w.astype(BF16)
    w_lo = (w - w_hi.astype(F32)).astype(BF16)
    o_ref[...] = _split_dot(_silu(c), w_hi, w_lo) + b_ref[...]


def _ada(c, w, b):
    r = c.shape[0]
    tn = 768
    return pl.pallas_call(
        _ada_kernel,
        grid=(3 * D_MODEL // tn,),
        in_specs=[pl.BlockSpec((r, D_MODEL), lambda j: (0, 0)),
                  pl.BlockSpec((D_MODEL, tn), lambda j: (0, j)),
                  pl.BlockSpec((1, tn), lambda j: (0, j))],
        out_specs=pl.BlockSpec((r, tn), lambda j: (0, j)),
        out_shape=jax.ShapeDtypeStruct((r, 3 * D_MODEL), F32),
        compiler_params=_cparams(("arbitrary",)),
        name="ada_mod",
    )(c, w, b.reshape(1, -1))


def _in_proj_kernel(x_ref, mod_ref, w_ref, wkv_ref, tab_ref, ret_ref, q_ref, k_ref, v_ref):
    m = mod_ref[0]
    h = _ln_plain(x_ref[0]) * (1.0 + m[:, D_MODEL:2 * D_MODEL]) + m[:, :D_MODEL]
    h = h.astype(BF16)
    tab = tab_ref[...]
    for c in range(4):
        p = _dot(h, w_ref[:, c * RET_WIDTH:(c + 1) * RET_WIDTH])
        if c < 2:
            cs = tab[:, (2 * c) * LANES:(2 * c + 1) * LANES]
            sn = tab[:, (2 * c + 1) * LANES:(2 * c + 2) * LANES]
            for hh in range(RET_HEADS):
                ph = p[:, hh * RET_DK:(hh + 1) * RET_DK]
                ret_ref[0, :, c * RET_WIDTH + hh * RET_DK:c * RET_WIDTH + (hh + 1) * RET_DK] = (
                    ph * cs + pltpu.roll(ph, RET_DK // 2, 1) * sn)
        else:
            ret_ref[0, :, c * RET_WIDTH:(c + 1) * RET_WIDTH] = p
    base = 4 * RET_WIDTH
    p = _dot(h, w_ref[:, base:base + SB_WIDTH]) * (LOG2E * SB_HEAD_DIM ** -0.5)
    for hh in range(SB_HEADS):
        q_ref[0, hh] = p[:, hh * SB_HEAD_DIM:(hh + 1) * SB_HEAD_DIM].astype(q_ref.dtype)
    pt = _dot_nt(wkv_ref[...], h)
    for c, ref in enumerate((k_ref, v_ref)):
        for hh in range(SB_HEADS):
            r0 = c * SB_WIDTH + hh * SB_HEAD_DIM
            ref[0, hh] = pt[r0:r0 + SB_HEAD_DIM, :]


def _rope_table(pos0, t):
    half = RET_DK // 2
    inv = ROPE_BASE ** (-np.arange(half, dtype=np.float64) / half)
    ang = (pos0 + np.arange(t, dtype=np.float64))[:, None] * inv[None, :]
    cos, sin = np.cos(ang), np.sin(ang)
    cs = np.concatenate([cos, cos], axis=1)
    sn = np.concatenate([-sin, sin], axis=1)
    ks = RET_DK ** -0.5
    return jnp.asarray(np.concatenate([cs, sn, cs * ks, sn * ks], axis=1), dtype=F32)


def _in_proj(x, mod, w_a_bf, w_kvt_bf, pos0, tm):
    b, t, _ = x.shape
    tab = _rope_table(pos0, t)
    wa = w_a_bf.shape[1]
    hs = jax.ShapeDtypeStruct((b, SB_HEADS, SB_HEAD_DIM, t), F32)
    q_spec = pl.BlockSpec((1, SB_HEADS, tm, SB_HEAD_DIM), lambda i, j: (i, 0, j, 0))
    kv_spec = pl.BlockSpec((1, SB_HEADS, SB_HEAD_DIM, tm), lambda i, j: (i, 0, 0, j))
    return pl.pallas_call(
        _in_proj_kernel,
        grid=(b, t // tm),
        in_specs=[pl.BlockSpec((1, tm, D_MODEL), lambda i, j: (i, j, 0)),
                  pl.BlockSpec((1, 1, 3 * D_MODEL), lambda i, j: (i, 0, 0)),
                  pl.BlockSpec((D_MODEL, wa), lambda i, j: (0, 0)),
                  pl.BlockSpec((2 * SB_WIDTH, D_MODEL), lambda i, j: (0, 0)),
                  pl.BlockSpec((tm, 4 * LANES), lambda i, j: (j, 0))],
        out_specs=[pl.BlockSpec((1, tm, 4 * RET_WIDTH), lambda i, j: (i, j, 0)),
                   q_spec, kv_spec, kv_spec],
        out_shape=[jax.ShapeDtypeStruct((b, t, 4 * RET_WIDTH), F32),
                   jax.ShapeDtypeStruct((b, SB_HEADS, t, SB_HEAD_DIM), BF16), hs, hs],
        compiler_params=_cparams(("parallel", "arbitrary")),
        name="in_proj",
    )(x, mod.reshape(b, 1, -1), w_a_bf, w_kvt_bf, tab)


def _ret_kernel(q_ref, k_ref, v_ref, g_ref, s0_ref, dec_ref, qd_ref, kd_ref, ng_ref,
                o_ref, so_ref, st_ref, *, chunk_decay):
    j = pl.program_id(1)

    @pl.when(j == 0)
    def _():
        st_ref[...] = s0_ref[0]

    for hh in range(RET_HEADS):
        sl = slice(hh * RET_DK, (hh + 1) * RET_DK)
        q = q_ref[0, :, sl]
        k = k_ref[0, :, sl]
        v = v_ref[0, :, sl]
        vb = v.astype(BF16)
        st = st_ref[hh]
        scores = _dot_nt(q.astype(BF16), k.astype(BF16)) * dec_ref[hh]
        o = _dot(scores.astype(BF16), vb) + _dot((q * qd_ref[hh]).astype(BF16), st.astype(BF16))
        st_ref[hh] = st * chunk_decay[hh] + _dot_tn((k * kd_ref[hh]).astype(BF16), vb)
        o = _ln_plain(o) * ng_ref[:, sl] * _silu(g_ref[0, :, sl])
        o_ref[0, :, sl] = o.astype(o_ref.dtype)

    @pl.when(j == pl.num_programs(1) - 1)
    def _():
        so_ref[0] = st_ref[...]


def _retention(ret_in, state0, norm_g, chunk):
    b, t, _ = ret_in.shape
    lg = np.log1p(-np.exp2(-5.0 - np.arange(RET_HEADS, dtype=np.float64)))
    idx = np.arange(chunk, dtype=np.float64)
    rel = idx[:, None] - idx[None, :]
    dec = np.where(rel >= 0, np.exp(lg[:, None, None] * np.maximum(rel, 0.0)), 0.0)
    qd = np.broadcast_to(np.exp(lg[:, None] * (idx + 1.0))[:, :, None], (RET_HEADS, chunk, RET_DK))
    kd = np.broadcast_to(np.exp(lg[:, None] * (chunk - 1.0 - idx))[:, :, None], (RET_HEADS, chunk, RET_DK))
    chunk_decay = tuple(float(v) for v in np.exp(lg * chunk))

    def col(c):
        return pl.BlockSpec((1, chunk, RET_WIDTH), lambda i, j, c=c: (i, j, c))

    const3 = lambda shape: pl.BlockSpec(shape, lambda i, j: (0, 0, 0))
    state_spec = pl.BlockSpec((1, RET_HEADS, RET_DK, RET_DV), lambda i, j: (i, 0, 0, 0))
    return pl.pallas_call(
        functools.partial(_ret_kernel, chunk_decay=chunk_decay),
        grid=(b, t // chunk),
        in_specs=[col(0), col(1), col(2), col(3), state_spec,
                  const3((RET_HEADS, chunk, chunk)), const3((RET_HEADS, chunk, RET_DK)),
                  const3((RET_HEADS, chunk, RET_DK)),
                  pl.BlockSpec((1, RET_WIDTH), lambda i, j: (0, 0))],
        out_specs=[pl.BlockSpec((1, chunk, RET_WIDTH), lambda i, j: (i, j, 0)), state_spec],
        out_shape=[jax.ShapeDtypeStruct((b, t, RET_WIDTH), BF16),
                   jax.ShapeDtypeStruct((b, RET_HEADS, RET_DK, RET_DV), F32)],
        scratch_shapes=[pltpu.VMEM((RET_HEADS, RET_DK, RET_DV), F32)],
        compiler_params=_cparams(("parallel", "arbitrary")),
        name="retention",
    )(ret_in, ret_in, ret_in, ret_in, state0,
      jnp.asarray(dec, F32), jnp.asarray(qd, F32), jnp.asarray(kd, F32), norm_g.reshape(1, -1))


SB_KEY_BLOCK = 256


LOG2E = 1.4426950408889634


MASKED_LOGIT = -1e30
EXP2_CLAMP = 126.0
SB_TERMS = 1


def _sb_stage1(qs, kts, z_ref, hl_ref, masked):
    tq, kb = qs[0].shape[0], kts[0].shape[1]
    if masked:
        valid = (lax.broadcasted_iota(jnp.int32, (tq, kb), 1)
                 < lax.broadcasted_iota(jnp.int32, (tq, kb), 0))
    for c, (q, kt) in enumerate(zip(qs, kts)):
        z = _dot(q, kt.astype(BF16))
        p = jnp.maximum(z, jnp.log2(1.0 + jnp.exp2(jnp.minimum(z, EXP2_CLAMP))))
        if masked:
            p = jnp.where(valid, p, 0.0)
            z = jnp.where(valid, z, MASKED_LOGIT)
        hi = p.astype(BF16)
        z_ref[c] = z
        hl_ref[c, :, :kb] = hi
        if SB_TERMS == 2:
            hl_ref[c, :, kb:] = (p - hi.astype(F32)).astype(BF16)


def _sb_stage2(z_ref, hl_ref, vts, tri2, acc_ref, car_ref):
    n, tq, kb = z_ref.shape
    r = _dot(hl_ref[...].reshape(n * tq, SB_TERMS * kb), tri2)
    for c in range(n):
        incl = r[c * tq:(c + 1) * tq]
        car = car_ref[c]
        w = jnp.exp2(z_ref[c] + incl + car)
        acc_ref[c] += _dot_nt(w.astype(BF16), vts[c].astype(BF16))
        car_ref[c] = car + incl[:, 0:1]


def _sb_finish(o, g):
    return o * lax.rsqrt(jnp.mean(o * o, axis=-1, keepdims=True) + LN_EPS) * g


def _tri_matrix():
    idx = np.arange(SB_KEY_BLOCK)
    t = -(idx[:, None] >= idx[None, :]).astype(np.float32)
    return jnp.asarray(np.concatenate([t] * SB_TERMS, axis=0), dtype=BF16)


def _sb_prompt_kernel(q_ref, kt_ref, vt_ref, tri_ref, g_ref, o_ref, acc_ref, car_ref,
                      za_ref, ha_ref, zb_ref, hb_ref, *, tq, nh):
    i = pl.program_id(2)
    qs = [q_ref[0, hh] for hh in range(nh)]
    acc_ref[...] = jnp.zeros_like(acc_ref)
    car_ref[...] = jnp.zeros_like(car_ref)

    def s1(blk, z_ref, hl_ref, masked=False):
        start = pl.multiple_of(blk * tq, tq)
        _sb_stage1(qs, [kt_ref[0, hh, :, pl.ds(start, tq)] for hh in range(nh)], z_ref, hl_ref, masked)

    def s2(blk, z_ref, hl_ref):
        start = pl.multiple_of(blk * tq, tq)
        _sb_stage2(z_ref, hl_ref, [vt_ref[0, hh, :, pl.ds(start, tq)] for hh in range(nh)],
                   tri_ref[...], acc_ref, car_ref)

    s1(i, za_ref, ha_ref, masked=True)

    def pair(p, _):
        blk = i - 2 * p
        s1(blk - 1, zb_ref, hb_ref)
        s2(blk, za_ref, ha_ref)
        s1(blk - 2, za_ref, ha_ref)
        s2(blk - 1, zb_ref, hb_ref)
        return 0

    lax.fori_loop(0, i // 2, pair, 0)

    @pl.when(i % 2 == 0)
    def _():
        s2(0, za_ref, ha_ref)

    @pl.when(i % 2 == 1)
    def _():
        s1(0, zb_ref, hb_ref)
        s2(1, za_ref, ha_ref)
        s2(0, zb_ref, hb_ref)

    d = SB_HEAD_DIM
    outs = [_sb_finish(acc_ref[hh], g_ref[0, :, hh * d:(hh + 1) * d]) for hh in range(nh)]
    o_ref[0] = jnp.concatenate(outs, axis=-1).astype(o_ref.dtype)


def _sb_prompt(q, kt, vt, norm_g):
    b, _, t, d = q.shape
    tq = SB_KEY_BLOCK
    nh = SB_HEADS
    kv_spec = pl.BlockSpec((1, nh, d, t), lambda i, h, j: (i, h, 0, 0))
    return pl.pallas_call(
        functools.partial(_sb_prompt_kernel, tq=tq, nh=nh),
        grid=(b, SB_HEADS // nh, t // tq),
        in_specs=[pl.BlockSpec((1, nh, tq, d), lambda i, h, j: (i, h, j, 0)), kv_spec, kv_spec,
                  pl.BlockSpec((SB_TERMS * tq, tq), lambda i, h, j: (0, 0)),
                  pl.BlockSpec((1, 1, nh * d), lambda i, h, j: (h, 0, 0))],
        out_specs=pl.BlockSpec((1, tq, nh * d), lambda i, h, j: (i, j, h)),
        out_shape=jax.ShapeDtypeStruct((b, t, SB_WIDTH), BF16),
        scratch_shapes=[pltpu.VMEM((nh, tq, d), F32), pltpu.VMEM((nh, tq, 1), F32),
                        pltpu.VMEM((nh, tq, tq), F32), pltpu.VMEM((nh, tq, SB_TERMS * tq), BF16),
                        pltpu.VMEM((nh, tq, tq), F32), pltpu.VMEM((nh, tq, SB_TERMS * tq), BF16)],
        compiler_params=_cparams(("parallel", "parallel", "arbitrary"), SB_FLAGS),
        name="sb_prompt",
    )(q, kt, vt, _tri_matrix(), norm_g.reshape(SB_HEADS // nh, 1, nh * d))


def _sb_sample_kernel(q_ref, kt_ref, vt_ref, ktp_ref, vtp_ref, tri_ref, g_ref, o_ref, acc_ref, car_ref,
                      zd_ref, hd_ref, za_ref, ha_ref, zb_ref, hb_ref, *, nh):
    kb = SB_KEY_BLOCK
    n_past = ktp_ref.shape[3] // kb
    t = q_ref.shape[2]
    qs = [q_ref[0, hh] for hh in range(nh)]
    acc_ref[...] = jnp.zeros_like(acc_ref)
    car_ref[...] = jnp.zeros_like(car_ref)

    def s1(blk, z_ref, hl_ref):
        start = pl.multiple_of(blk * kb, kb)
        _sb_stage1(qs, [ktp_ref[0, hh, :, pl.ds(start, kb)] for hh in range(nh)], z_ref, hl_ref, False)

    def s2(blk, z_ref, hl_ref):
        start = pl.multiple_of(blk * kb, kb)
        _sb_stage2(z_ref, hl_ref, [vtp_ref[0, hh, :, pl.ds(start, kb)] for hh in range(nh)],
                   tri_ref[...], acc_ref, car_ref)

    _sb_stage1(qs, [kt_ref[0, hh] for hh in range(nh)], zd_ref, hd_ref, True)
    s1(n_past - 1, za_ref, ha_ref)
    _sb_stage2(zd_ref, hd_ref, [vt_ref[0, hh] for hh in range(nh)],
               jnp.concatenate([tri_ref[:t, :t]] * SB_TERMS, axis=0), acc_ref, car_ref)

    def pair(p, _):
        blk = n_past - 1 - 2 * p
        s1(blk - 1, zb_ref, hb_ref)
        s2(blk, za_ref, ha_ref)
        s1(blk - 2, za_ref, ha_ref)
        s2(blk - 1, zb_ref, hb_ref)
        return 0

    lax.fori_loop(0, (n_past - 1) // 2, pair, 0)
    if (n_past - 1) % 2 == 0:
        s2(0, za_ref, ha_ref)
    else:
        s1(0, zb_ref, hb_ref)
        s2(1, za_ref, ha_ref)
        s2(0, zb_ref, hb_ref)
    d = SB_HEAD_DIM
    outs = [_sb_finish(acc_ref[hh], g_ref[0, :, hh * d:(hh + 1) * d]) for hh in range(nh)]
    o_ref[0] = jnp.concatenate(outs, axis=-1).astype(o_ref.dtype)


def _sb_sample(q, kt, vt, kt_past, vt_past, norm_g):
    b, _, t, d = q.shape
    p = kt_past.shape[3]
    nh = 4
    kb = SB_KEY_BLOCK
    new_spec = pl.BlockSpec((1, nh, d, t), lambda i, h: (i, h, 0, 0))
    past_spec = pl.BlockSpec((1, nh, d, p), lambda i, h: (i, h, 0, 0))
    return pl.pallas_call(
        functools.partial(_sb_sample_kernel, nh=nh),
        grid=(b, SB_HEADS // nh),
        in_specs=[pl.BlockSpec((1, nh, t, d), lambda i, h: (i, h, 0, 0)), new_spec, new_spec,
                  past_spec, past_spec,
                  pl.BlockSpec((SB_TERMS * kb, kb), lambda i, h: (0, 0)),
                  pl.BlockSpec((1, 1, nh * d), lambda i, h: (h, 0, 0))],
        out_specs=pl.BlockSpec((1, t, nh * d), lambda i, h: (i, 0, h)),
        out_shape=jax.ShapeDtypeStruct((b, t, SB_WIDTH), BF16),
        scratch_shapes=[pltpu.VMEM((nh, t, d), F32), pltpu.VMEM((nh, t, 1), F32),
                        pltpu.VMEM((nh, t, t), F32), pltpu.VMEM((nh, t, SB_TERMS * t), BF16),
                        pltpu.VMEM((nh, t, kb), F32), pltpu.VMEM((nh, t, SB_TERMS * kb), BF16),
                        pltpu.VMEM((nh, t, kb), F32), pltpu.VMEM((nh, t, SB_TERMS * kb), BF16)],
        compiler_params=_cparams(("parallel", "arbitrary"), SB_FLAGS),
        name="sb_sample",
    )(q, kt, vt, kt_past, vt_past, _tri_matrix(), norm_g.reshape(SB_HEADS // nh, 1, nh * d))


N_PAIRS = 6
N_BUCKETS = N_GROUPS * N_PAIRS
MOE_TM = 256
XW = D_MODEL + LANES
_PAIRS = [(a, b) for a in range(EXPERTS_PER_GROUP) for b in range(a + 1, EXPERTS_PER_GROUP)]
BUCKET_E0 = np.array([g * EXPERTS_PER_GROUP + a for g in range(N_GROUPS) for a, _ in _PAIRS], np.int32)
BUCKET_E1 = np.array([g * EXPERTS_PER_GROUP + b for g in range(N_GROUPS) for _, b in _PAIRS], np.int32)


def _route(logits):
    lane = lax.broadcasted_iota(jnp.int32, logits.shape, 1)
    neg = -jnp.inf
    big = jnp.int32(2 * LANES)
    gl = jnp.where(lane < N_GROUPS, logits, neg)
    gmax = jnp.max(gl, axis=-1, keepdims=True)
    g_idx = jnp.min(jnp.where(gl == gmax, lane, big), axis=-1, keepdims=True)
    g_p = 1.0 / jnp.sum(jnp.exp(gl - gmax), axis=-1, keepdims=True)
    lo = N_GROUPS + g_idx * EXPERTS_PER_GROUP
    el = jnp.where((lane >= lo) & (lane < lo + EXPERTS_PER_GROUP), logits, neg)
    v1 = jnp.max(el, axis=-1, keepdims=True)
    i1 = jnp.min(jnp.where(el == v1, lane, big), axis=-1, keepdims=True)
    el2 = jnp.where(lane == i1, neg, el)
    v2 = jnp.max(el2, axis=-1, keepdims=True)
    i2 = jnp.min(jnp.where(el2 == v2, lane, big), axis=-1, keepdims=True)
    e21 = jnp.exp(v2 - v1)
    p1 = 1.0 / (1.0 + e21)
    p2 = e21 * p1
    first_lo = i1 < i2
    w_lo = jnp.where(first_lo, p1, p2) * g_p
    w_hi = jnp.where(first_lo, p2, p1) * g_p
    a = jnp.minimum(i1, i2) - lo
    b = jnp.maximum(i1, i2) - lo
    pair = jnp.where(a == 0, b - 1, jnp.where(a == 1, b + 1, 5))
    return g_idx * N_PAIRS + pair, w_lo, w_hi


def _out_proj_kernel(x_ref, mr_ref, ms_ref, wo_ref, mm_ref, mf_ref, lg_ref, lb_ref, wr_ref, br_ref,
                     tril_ref, cin_ref, x1_ref, hx_ref, rt_ref, cnt_ref, run_ref):
    @pl.when((pl.program_id(0) == 0) & (pl.program_id(1) == 0))
    def _():
        run_ref[...] = cin_ref[...]

    mix = _dot(mr_ref[0], wo_ref[:MIX_HALF]) + _dot(ms_ref[0], wo_ref[MIX_HALF:])
    gate = mm_ref[0][:, 2 * D_MODEL:]
    x1 = _ln_plain(DEEPNORM_ALPHA * x_ref[0] + gate * mix) * lg_ref[...] + lb_ref[...]
    x1_ref[0] = x1
    mf = mf_ref[0]
    h2 = _ln_plain(x1) * (1.0 + mf[:, D_MODEL:2 * D_MODEL]) + mf[:, :D_MODEL]
    bucket, w_lo, w_hi = _route(_dot(h2.astype(BF16), wr_ref[...]) + br_ref[...])
    tm = h2.shape[0]
    lane = lax.broadcasted_iota(jnp.int32, (tm, LANES), 1)
    hx_ref[0, :, :D_MODEL] = h2
    hx_ref[0, :, D_MODEL:] = jnp.where(lane == 0, w_lo, jnp.where(lane == 1, w_hi, 0.0))
    hit = lane == bucket
    onehot = hit.astype(BF16)
    before = _dot(tril_ref[...], onehot)
    run = run_ref[...]
    rank = jnp.sum(jnp.where(hit, before + run, 0.0), axis=-1, keepdims=True)
    run = run + before[tm - 1:tm] + onehot[tm - 1:tm].astype(F32)
    run_ref[...] = run
    cnt_ref[...] = run
    rt_ref[0] = jnp.where(lane == 0, bucket.astype(F32), jnp.where(lane == 1, rank, 0.0))


MIX_HALF = RET_WIDTH


def _out_proj(x, mr, ms, w_out_bf, mod_mix, mod_ffn, ln_g, ln_b, w_route_bf, b_route, counts_in, tm):
    b, t, _ = x.shape
    tok = lambda w: pl.BlockSpec((1, tm, w), lambda i, j: (i, j, 0))
    modspec = pl.BlockSpec((1, 1, 3 * D_MODEL), lambda i, j: (i, 0, 0))
    row = lambda w: pl.BlockSpec((1, w), lambda i, j: (0, 0))
    idx = np.arange(tm)
    tril = jnp.asarray(idx[:, None] > idx[None, :], dtype=BF16)
    return pl.pallas_call(
        _out_proj_kernel,
        grid=(b, t // tm),
        in_specs=[tok(D_MODEL), tok(RET_WIDTH), tok(SB_WIDTH),
                  pl.BlockSpec((D_MODEL, D_MODEL), lambda i, j: (0, 0)),
                  modspec, modspec, row(D_MODEL), row(D_MODEL),
                  pl.BlockSpec((D_MODEL, LANES), lambda i, j: (0, 0)), row(LANES),
                  pl.BlockSpec((tm, tm), lambda i, j: (0, 0)), row(LANES)],
        out_specs=[tok(D_MODEL), tok(XW), tok(LANES), row(LANES)],
        out_shape=[jax.ShapeDtypeStruct((b, t, D_MODEL), F32),
                   jax.ShapeDtypeStruct((b, t, XW), F32),
                   jax.ShapeDtypeStruct((b, t, LANES), F32),
                   jax.ShapeDtypeStruct((1, LANES), F32)],
        scratch_shapes=[pltpu.VMEM((1, LANES), F32)],
        compiler_params=_cparams(("arbitrary", "arbitrary")),
        name="out_proj",
    )(x, mr, ms, w_out_bf, mod_mix.reshape(b, 1, -1), mod_ffn.reshape(b, 1, -1),
      ln_g.reshape(1, -1), ln_b.reshape(1, -1), w_route_bf, b_route, tril, counts_in)


def _route_plan(routes, counts, n_tiles):
    bucket = jnp.concatenate([r[..., 0].astype(jnp.int32).reshape(-1) for r in routes])
    rank = jnp.concatenate([r[..., 1].astype(jnp.int32).reshape(-1) for r in routes])
    cnt = counts[0, :N_BUCKETS].astype(jnp.int32)
    padded = ((cnt + MOE_TM - 1) // MOE_TM) * MOE_TM
    ends = jnp.cumsum(padded)
    starts = ends - padded
    n_used = ends[-1] // MOE_TM
    tile = jnp.arange(n_tiles, dtype=jnp.int32)
    last = jnp.maximum(n_used - 1, 0)
    tile_idx = jnp.minimum(tile, last)
    tile_bucket = jnp.sum((ends[None, :] <= (tile_idx * MOE_TM)[:, None]).astype(jnp.int32), axis=1)
    tile_bucket = jnp.minimum(tile_bucket, N_BUCKETS - 1)
    e0 = jnp.asarray(BUCKET_E0)[tile_bucket]
    e1 = jnp.asarray(BUCKET_E1)[tile_bucket]
    in_bucket = bucket[:, None] == jnp.arange(N_BUCKETS, dtype=jnp.int32)[None, :]
    dest = rank + jnp.sum(jnp.where(in_bucket, starts[None, :], 0), axis=1)
    trailing = n_used + jnp.arange(N_BUCKETS, dtype=jnp.int32)
    ztiles = jnp.concatenate([jnp.where(padded > 0, ends - MOE_TM, -1),
                              jnp.where(trailing < n_tiles, trailing * MOE_TM, -1)])
    return (dest.astype(jnp.int32), ztiles.astype(jnp.int32), e0, e1, tile_idx,
            n_used.reshape(1).astype(jnp.int32))


def _row_copy(src_ref, s, dst_ref, d, sem):
    return pltpu.make_async_copy(src_ref.at[pl.ds(s, 1)], dst_ref.at[pl.ds(d, 1)], sem)


DMA_UNROLL = 8


SUBLANES = 8


def _scatter_kernel(dhi_ref, dlo_ref, ztile_ref, *refs, rows, steps):
    srcs = refs[:len(steps)]
    out_ref, zbuf, sem, zsem = refs[len(steps):]
    i = pl.program_id(0)

    @pl.when(i == 0)
    def _():
        zbuf[...] = jnp.zeros_like(zbuf)

        def fill(k, _):
            @pl.when(ztile_ref[k] >= 0)
            def _():
                g0 = ztile_ref[k] // SUBLANES
                pltpu.make_async_copy(zbuf, out_ref.at[pl.ds(g0, MOE_TM // SUBLANES)], zsem).start()
            return 0

        def drain(k, _):
            @pl.when(ztile_ref[k] >= 0)
            def _():
                pltpu.make_async_copy(zbuf, out_ref.at[pl.ds(0, MOE_TM // SUBLANES)], zsem).wait()
            return 0

        lax.fori_loop(0, ztile_ref.shape[0], fill, 0)
        lax.fori_loop(0, ztile_ref.shape[0], drain, 0)

    base = i * rows
    first = 0
    for src_ref, n_steps in zip(srcs, steps):
        @pl.when((i >= first) & (i < first + n_steps))
        def _(src_ref=src_ref):
            def issue(g, _):
                for u in range(SUBLANES):
                    tok = base + g * SUBLANES + u
                    pltpu.make_async_copy(src_ref.at[g, pl.ds(u, 1)],
                                          out_ref.at[dhi_ref[tok], pl.ds(dlo_ref[tok], 1)], sem).start()
                return 0

            lax.fori_loop(0, rows // SUBLANES, issue, 0)
            pltpu.make_async_copy(src_ref, out_ref.at[pl.ds(0, rows // SUBLANES)], sem).wait()
        first += n_steps


def _scatter_rows(hxs, dest, ztiles, n_rows):
    w = hxs[0].shape[1]
    rows = min([256] + [h.shape[0] for h in hxs])
    steps = tuple(h.shape[0] // rows for h in hxs)
    firsts = [sum(steps[:k]) for k in range(len(steps))]
    in_specs = [pl.BlockSpec((rows // SUBLANES, SUBLANES, w),
                             lambda i, dh, dl, z, f=f, s=s: (jnp.clip(i - f, 0, s - 1), 0, 0))
                for f, s in zip(firsts, steps)]
    out = pl.pallas_call(
        functools.partial(_scatter_kernel, rows=rows, steps=steps),
        grid_spec=pltpu.PrefetchScalarGridSpec(
            num_scalar_prefetch=3, grid=(sum(steps),),
            in_specs=in_specs,
            out_specs=pl.BlockSpec(memory_space=pl.ANY),
            scratch_shapes=[pltpu.VMEM((MOE_TM // SUBLANES, SUBLANES, w), F32),
                            pltpu.SemaphoreType.DMA(()), pltpu.SemaphoreType.DMA(())]),
        out_shape=jax.ShapeDtypeStruct((n_rows // SUBLANES, SUBLANES, w), F32),
        compiler_params=_cparams(("arbitrary",)),
        name="scatter_rows",
    )(dest // SUBLANES, dest % SUBLANES, ztiles,
      *[h.reshape(h.shape[0] // SUBLANES, SUBLANES, w) for h in hxs])
    return out.reshape(n_rows, w)


def _moe_kernel(e0_ref, e1_ref, ti_ref, nu_ref, x_ref, wg0, wu0, wd0, wg1, wu1, wd1, y_ref):
    del e0_ref, e1_ref, ti_ref

    @pl.when(pl.program_id(0) < nu_ref[0])
    def _():
        x = x_ref[...]
        h = x[:, :D_MODEL].astype(BF16)
        wx = x[:, D_MODEL:]
        lane = lax.broadcasted_iota(jnp.int32, wx.shape, 1)
        w_lo = jnp.sum(jnp.where(lane == 0, wx, 0.0), axis=-1, keepdims=True)
        w_hi = jnp.sum(jnp.where(lane == 1, wx, 0.0), axis=-1, keepdims=True)

        def expert(wg, wu, wd):
            a = _silu(_dot(h, wg[0])) * _dot(h, wu[0])
            return _dot(a.astype(BF16), wd[0])

        y_ref[...] = w_lo * expert(wg0, wu0, wd0) + w_hi * expert(wg1, wu1, wd1)

    @pl.when(pl.program_id(0) >= nu_ref[0])
    def _():
        y_ref[...] = jnp.zeros_like(y_ref)


def _moe(xs, e0, e1, tile_idx, n_used, wg_bf, wu_bf, wd_bf):
    n_rows = xs.shape[0]
    up = lambda sel: pl.BlockSpec((1, D_MODEL, D_EXPERT), lambda t, e0, e1, ti, nu: ((e0, e1)[sel][t], 0, 0))
    down = lambda sel: pl.BlockSpec((1, D_EXPERT, D_MODEL), lambda t, e0, e1, ti, nu: ((e0, e1)[sel][t], 0, 0))
    return pl.pallas_call(
        _moe_kernel,
        grid_spec=pltpu.PrefetchScalarGridSpec(
            num_scalar_prefetch=4, grid=(n_rows // MOE_TM,),
            in_specs=[pl.BlockSpec((MOE_TM, XW), lambda t, e0, e1, ti, nu: (ti[t], 0)),
                      up(0), up(0), down(0), up(1), up(1), down(1)],
            out_specs=pl.BlockSpec((MOE_TM, D_MODEL), lambda t, e0, e1, ti, nu: (t, 0))),
        out_shape=jax.ShapeDtypeStruct((n_rows, D_MODEL), F32),
        compiler_params=_cparams(("arbitrary",)),
        name="moe_routed",
    )(e0, e1, tile_idx, n_used, xs, wg_bf, wu_bf, wd_bf, wg_bf, wu_bf, wd_bf)


def _final_kernel(dest_ref, x1_ref, ys_ref, mf_ref, lg_ref, lb_ref, o_ref, buf, sem, *, tm):
    nj = pl.num_programs(1)
    step = pl.program_id(0) * nj + pl.program_id(1)
    n_steps = pl.num_programs(0) * nj
    slot = step % 2

    def fetch(s, sl):
        def issue(r, _):
            _row_copy(ys_ref, dest_ref[s * tm + r], buf.at[sl], r, sem.at[sl]).start()
            return 0
        lax.fori_loop(0, tm, issue, 0, unroll=DMA_UNROLL)

    @pl.when(step == 0)
    def _():
        fetch(step, slot)

    @pl.when(step + 1 < n_steps)
    def _():
        fetch(step + 1, 1 - slot)

    pltpu.make_async_copy(ys_ref.at[pl.ds(0, tm)], buf.at[slot], sem.at[slot]).wait()
    gate = mf_ref[0][:, 2 * D_MODEL:]
    o_ref[0] = _ln_plain(DEEPNORM_ALPHA * x1_ref[0] + gate * buf[slot]) * lg_ref[...] + lb_ref[...]


def _final(x1, ys, dest, mod_ffn, ln_g, ln_b, tm):
    b, t, _ = x1.shape
    tok = pl.BlockSpec((1, tm, D_MODEL), lambda i, j, *_: (i, j, 0))
    row = pl.BlockSpec((1, D_MODEL), lambda i, j, *_: (0, 0))
    return pl.pallas_call(
        functools.partial(_final_kernel, tm=tm),
        grid_spec=pltpu.PrefetchScalarGridSpec(
            num_scalar_prefetch=1, grid=(b, t // tm),
            in_specs=[tok, pl.BlockSpec(memory_space=pl.ANY),
                      pl.BlockSpec((1, 1, 3 * D_MODEL), lambda i, j, *_: (i, 0, 0)), row, row],
            out_specs=tok,
            scratch_shapes=[pltpu.VMEM((2, tm, D_MODEL), F32), pltpu.SemaphoreType.DMA((2,))]),
        out_shape=jax.ShapeDtypeStruct((b, t, D_MODEL), F32),
        compiler_params=_cparams(("arbitrary", "arbitrary")),
        name="final_ln",
    )(dest, x1, ys, mod_ffn.reshape(b, 1, -1), ln_g.reshape(1, -1), ln_b.reshape(1, -1))


def _mixer_half(x, mod_mix, mod_ffn, pos0, state0, k_past, v_past, counts_in, wts):
    b, t, _ = x.shape
    ret_in, sq, skt, svt = _in_proj(x, mod_mix, wts["w_in_a"], wts["w_in_kvt"], pos0, min(512, t))
    mr, state = _retention(ret_in, state0, wts["ret_norm_g"], min(512, t))
    if k_past is None:
        ms = _sb_prompt(sq, skt, svt, wts["sb_norm_g"])
    else:
        ms = _sb_sample(sq, skt, svt, jnp.swapaxes(k_past, 2, 3), jnp.swapaxes(v_past, 2, 3),
                        wts["sb_norm_g"])
    sk, sv = jnp.swapaxes(skt, 2, 3), jnp.swapaxes(svt, 2, 3)
    x1, hx, route, counts = _out_proj(x, mr, ms, wts["w_out"], mod_mix, mod_ffn, wts["ln_mix_g"],
                                      wts["ln_mix_b"], wts["w_route"], wts["b_route"], counts_in,
                                      min(512, t))
    return dict(x1=x1, hx=hx.reshape(b * t, XW), route=route, counts=counts, mod_ffn=mod_ffn,
                sk=sk[None], sv=sv[None], state=state[None])


def _ffn_half(groups, wts):
    n = sum(g["hx"].shape[0] for g in groups)
    n_tiles = n // MOE_TM + N_BUCKETS
    dest, ztiles, e0, e1, tile_idx, n_used = _route_plan([g["route"] for g in groups],
                                                         groups[-1]["counts"], n_tiles)
    xs = _scatter_rows([g["hx"] for g in groups], dest, ztiles, n_tiles * MOE_TM)
    ys = _moe(xs, e0, e1, tile_idx, n_used, wts["w_e_gate"], wts["w_e_up"], wts["w_e_down"])
    outs, first = [], 0
    for g in groups:
        t = g["x1"].shape[1]
        n_g = g["hx"].shape[0]
        outs.append(_final(g["x1"], ys, dest[first:first + n_g], g["mod_ffn"], wts["ln_ffn_g"],
                           wts["ln_ffn_b"], min(256, t)))
        first += n_g
    return outs


def kernel(x_prompt, x_sample, cache_sb_k, cache_sb_v, state_ret, c_prompt, c_sample, w_in, w_out, ret_norm_g, sb_norm_g, w_ada_mix, b_ada_mix, ln_mix_g, ln_mix_b, w_ada_ffn, b_ada_ffn, ln_ffn_g, ln_ffn_b, w_group, b_group, w_router, b_router, w_e_gate, w_e_up, w_e_down):
    bp = x_prompt.shape[0]
    c_all = jnp.concatenate([c_prompt, c_sample], axis=0)
    mod_mix = _ada(c_all, w_ada_mix[0], b_ada_mix[0])
    mod_ffn = _ada(c_all, w_ada_ffn[0], b_ada_ffn[0])
    pad = LANES - N_GROUPS - N_EXPERTS
    w_route = jnp.concatenate([w_group[0], w_router[0], jnp.zeros((D_MODEL, pad), F32)], axis=1)
    b_route = jnp.concatenate([b_group[0], b_router[0], jnp.zeros((pad,), F32)]).reshape(1, LANES)
    n_a = 4 * RET_WIDTH + SB_WIDTH
    wts = dict(w_in_a=w_in[0, :, :n_a].astype(BF16), w_in_kvt=w_in[0, :, n_a:].T.astype(BF16),
               w_out=w_out[0].astype(BF16),
               ret_norm_g=ret_norm_g[0], sb_norm_g=sb_norm_g[0],
               ln_mix_g=ln_mix_g[0], ln_mix_b=ln_mix_b[0], ln_ffn_g=ln_ffn_g[0], ln_ffn_b=ln_ffn_b[0],
               w_route=w_route.astype(BF16), b_route=b_route,
               w_e_gate=w_e_gate[0].astype(BF16), w_e_up=w_e_up[0].astype(BF16),
               w_e_down=w_e_down[0].astype(BF16))
    ret_zero = jnp.zeros((bp, RET_HEADS, RET_DK, RET_DV), F32)
    gp = _mixer_half(x_prompt, mod_mix[:bp], mod_ffn[:bp], 0, ret_zero, None, None,
                     jnp.zeros((1, LANES), F32), wts)
    gs = _mixer_half(x_sample, mod_mix[bp:], mod_ffn[bp:], cache_sb_k.shape[3], state_ret[0],
                     cache_sb_k[0], cache_sb_v[0], gp["counts"], wts)
    y_p, y_s = _ffn_half([gp, gs], wts)
    return (y_p, y_s, gp["sk"], gp["sv"], gp["state"], gs["sk"], gs["sv"], gs["state"])
```

```python
import functools
import math

import numpy as np
import jax
import jax.numpy as jnp
from jax import lax
from jax.experimental import pallas as pl
from jax.experimental.pallas import tpu as pltpu

D_MODEL = 1024
RET_HEADS = 4
RET_DK = 128
RET_DV = 128
RET_WIDTH = RET_HEADS * RET_DV
SB_HEADS = 8
SB_HEAD_DIM = 64
SB_WIDTH = SB_HEADS * SB_HEAD_DIM
IN_WIDTH = 2 * RET_HEADS * RET_DK + 2 * RET_WIDTH + 3 * SB_WIDTH
ROPE_BASE = 10000.0
N_GROUPS = 4
EXPERTS_PER_GROUP = 4
N_EXPERTS = N_GROUPS * EXPERTS_PER_GROUP
D_EXPERT = 512
DEPTH = 1
DEEPNORM_ALPHA = (2.0 * DEPTH) ** 0.25
LN_EPS = 1e-5

LANES = 128
VMEM_LIMIT = 48 * 1024 * 1024

F32 = jnp.float32
BF16 = jnp.bfloat16


def _cparams(sem, flags=None):
    return pltpu.CompilerParams(dimension_semantics=sem, vmem_limit_bytes=VMEM_LIMIT, flags=flags)


SB_FLAGS = None


def _dot(a, b):
    return jnp.dot(a, b, preferred_element_type=F32)


def _dot_nt(a, b):
    return lax.dot_general(a, b, (((1,), (1,)), ((), ())), preferred_element_type=F32)


def _dot_tn(a, b):
    return lax.dot_general(a, b, (((0,), (0,)), ((), ())), preferred_element_type=F32)


def _split_dot(a, w_hi, w_lo):
    a_hi = a.astype(BF16)
    a_lo = (a - a_hi.astype(F32)).astype(BF16)
    return _dot(a_hi, w_hi) + (_dot(a_hi, w_lo) + _dot(a_lo, w_hi))


def _ln_plain(x):
    mu = jnp.mean(x, axis=-1, keepdims=True)
    xc = x - mu
    var = jnp.mean(xc * xc, axis=-1, keepdims=True)
    return xc * lax.rsqrt(var + LN_EPS)


def _silu(x):
    return x * (1.0 / (1.0 + jnp.exp(-x)))


def _ada_kernel(c_ref, w_ref, b_ref, o_ref):
    c = c_ref[...]
    w = w_ref[...]
    w_hi = w.astype(BF16)
    w_lo = (w - w_hi.astype(F32)).astype(BF16)
    o_ref[...] = _split_dot(_silu(c), w_hi, w_lo) + b_ref[...]


def _ada(c, w, b):
    r = c.shape[0]
    tn = 768
    return pl.pallas_call(
        _ada_kernel,
        grid=(3 * D_MODEL // tn,),
        in_specs=[pl.BlockSpec((r, D_MODEL), lambda j: (0, 0)),
                  pl.BlockSpec((D_MODEL, tn), lambda j: (0, j)),
                  pl.BlockSpec((1, tn), lambda j: (0, j))],
        out_specs=pl.BlockSpec((r, tn), lambda j: (0, j)),
        out_shape=jax.ShapeDtypeStruct((r, 3 * D_MODEL), F32),
        compiler_params=_cparams(("arbitrary",)),
        name="ada_mod",
    )(c, w, b.reshape(1, -1))


def _in_proj_kernel(x_ref, mod_ref, w_ref, wkv_ref, tab_ref, ret_ref, q_ref, k_ref, v_ref):
    m = mod_ref[0]
    h = _ln_plain(x_ref[0]) * (1.0 + m[:, D_MODEL:2 * D_MODEL]) + m[:, :D_MODEL]
    h = h.astype(BF16)
    tab = tab_ref[...]
    for c in range(4):
        p = _dot(h, w_ref[:, c * RET_WIDTH:(c + 1) * RET_WIDTH])
        if c < 2:
            cs = tab[:, (2 * c) * LANES:(2 * c + 1) * LANES]
            sn = tab[:, (2 * c + 1) * LANES:(2 * c + 2) * LANES]
            for hh in range(RET_HEADS):
                ph = p[:, hh * RET_DK:(hh + 1) * RET_DK]
                ret_ref[0, :, c * RET_WIDTH + hh * RET_DK:c * RET_WIDTH + (hh + 1) * RET_DK] = (
                    ph * cs + pltpu.roll(ph, RET_DK // 2, 1) * sn)
        else:
            ret_ref[0, :, c * RET_WIDTH:(c + 1) * RET_WIDTH] = p
    base = 4 * RET_WIDTH
    p = _dot(h, w_ref[:, base:base + SB_WIDTH]) * (LOG2E * SB_HEAD_DIM ** -0.5)
    for hh in range(SB_HEADS):
        q_ref[0, hh] = p[:, hh * SB_HEAD_DIM:(hh + 1) * SB_HEAD_DIM].astype(q_ref.dtype)
    pt = _dot_nt(wkv_ref[...], h)
    for c, ref in enumerate((k_ref, v_ref)):
        for hh in range(SB_HEADS):
            r0 = c * SB_WIDTH + hh * SB_HEAD_DIM
            ref[0, hh] = pt[r0:r0 + SB_HEAD_DIM, :]


def _rope_table(pos0, t):
    half = RET_DK // 2
    inv = ROPE_BASE ** (-np.arange(half, dtype=np.float64) / half)
    ang = (pos0 + np.arange(t, dtype=np.float64))[:, None] * inv[None, :]
    cos, sin = np.cos(ang), np.sin(ang)
    cs = np.concatenate([cos, cos], axis=1)
    sn = np.concatenate([-sin, sin], axis=1)
    ks = RET_DK ** -0.5
    return jnp.asarray(np.concatenate([cs, sn, cs * ks, sn * ks], axis=1), dtype=F32)


def _in_proj(x, mod, w_a_bf, w_kvt_bf, pos0, tm):
    b, t, _ = x.shape
    tab = _rope_table(pos0, t)
    wa = w_a_bf.shape[1]
    hs = jax.ShapeDtypeStruct((b, SB_HEADS, SB_HEAD_DIM, t), F32)
    q_spec = pl.BlockSpec((1, SB_HEADS, tm, SB_HEAD_DIM), lambda i, j: (i, 0, j, 0))
    kv_spec = pl.BlockSpec((1, SB_HEADS, SB_HEAD_DIM, tm), lambda i, j: (i, 0, 0, j))
    return pl.pallas_call(
        _in_proj_kernel,
        grid=(b, t // tm),
        in_specs=[pl.BlockSpec((1, tm, D_MODEL), lambda i, j: (i, j, 0)),
                  pl.BlockSpec((1, 1, 3 * D_MODEL), lambda i, j: (i, 0, 0)),
                  pl.BlockSpec((D_MODEL, wa), lambda i, j: (0, 0)),
                  pl.BlockSpec((2 * SB_WIDTH, D_MODEL), lambda i, j: (0, 0)),
                  pl.BlockSpec((tm, 4 * LANES), lambda i, j: (j, 0))],
        out_specs=[pl.BlockSpec((1, tm, 4 * RET_WIDTH), lambda i, j: (i, j, 0)),
                   q_spec, kv_spec, kv_spec],
        out_shape=[jax.ShapeDtypeStruct((b, t, 4 * RET_WIDTH), F32),
                   jax.ShapeDtypeStruct((b, SB_HEADS, t, SB_HEAD_DIM), BF16), hs, hs],
        compiler_params=_cparams(("parallel", "arbitrary")),
        name="in_proj",
    )(x, mod.reshape(b, 1, -1), w_a_bf, w_kvt_bf, tab)


def _ret_kernel(q_ref, k_ref, v_ref, g_ref, s0_ref, dec_ref, qd_ref, kd_ref, ng_ref,
                o_ref, so_ref, st_ref, *, chunk_decay):
    j = pl.program_id(1)

    @pl.when(j == 0)
    def _():
        st_ref[...] = s0_ref[0]

    for hh in range(RET_HEADS):
        sl = slice(hh * RET_DK, (hh + 1) * RET_DK)
        q = q_ref[0, :, sl]
        k = k_ref[0, :, sl]
        v = v_ref[0, :, sl]
        vb = v.astype(BF16)
        st = st_ref[hh]
        scores = _dot_nt(q.astype(BF16), k.astype(BF16)) * dec_ref[hh]
        o = _dot(scores.astype(BF16), vb) + _dot((q * qd_ref[hh]).astype(BF16), st.astype(BF16))
        st_ref[hh] = st * chunk_decay[hh] + _dot_tn((k * kd_ref[hh]).astype(BF16), vb)
        o = _ln_plain(o) * ng_ref[:, sl] * _silu(g_ref[0, :, sl])
        o_ref[0, :, sl] = o.astype(o_ref.dtype)

    @pl.when(j == pl.num_programs(1) - 1)
    def _():
        so_ref[0] = st_ref[...]


def _retention(ret_in, state0, norm_g, chunk):
    b, t, _ = ret_in.shape
    lg = np.log1p(-np.exp2(-5.0 - np.arange(RET_HEADS, dtype=np.float64)))
    idx = np.arange(chunk, dtype=np.float64)
    rel = idx[:, None] - idx[None, :]
    dec = np.where(rel >= 0, np.exp(lg[:, None, None] * np.maximum(rel, 0.0)), 0.0)
    qd = np.broadcast_to(np.exp(lg[:, None] * (idx + 1.0))[:, :, None], (RET_HEADS, chunk, RET_DK))
    kd = np.broadcast_to(np.exp(lg[:, None] * (chunk - 1.0 - idx))[:, :, None], (RET_HEADS, chunk, RET_DK))
    chunk_decay = tuple(float(v) for v in np.exp(lg * chunk))

    def col(c):
        return pl.BlockSpec((1, chunk, RET_WIDTH), lambda i, j, c=c: (i, j, c))

    const3 = lambda shape: pl.BlockSpec(shape, lambda i, j: (0, 0, 0))
    state_spec = pl.BlockSpec((1, RET_HEADS, RET_DK, RET_DV), lambda i, j: (i, 0, 0, 0))
    return pl.pallas_call(
        functools.partial(_ret_kernel, chunk_decay=chunk_decay),
        grid=(b, t // chunk),
        in_specs=[col(0), col(1), col(2), col(3), state_spec,
                  const3((RET_HEADS, chunk, chunk)), const3((RET_HEADS, chunk, RET_DK)),
                  const3((RET_HEADS, chunk, RET_DK)),
                  pl.BlockSpec((1, RET_WIDTH), lambda i, j: (0, 0))],
        out_specs=[pl.BlockSpec((1, chunk, RET_WIDTH), lambda i, j: (i, j, 0)), state_spec],
        out_shape=[jax.ShapeDtypeStruct((b, t, RET_WIDTH), BF16),
                   jax.ShapeDtypeStruct((b, RET_HEADS, RET_DK, RET_DV), F32)],
        scratch_shapes=[pltpu.VMEM((RET_HEADS, RET_DK, RET_DV), F32)],
        compiler_params=_cparams(("parallel", "arbitrary")),
        name="retention",
    )(ret_in, ret_in, ret_in, ret_in, state0,
      jnp.asarray(dec, F32), jnp.asarray(qd, F32), jnp.asarray(kd, F32), norm_g.reshape(1, -1))


SB_KEY_BLOCK = 256


LOG2E = 1.4426950408889634


MASKED_LOGIT = -1e30
EXP2_CLAMP = 126.0
SB_TERMS = 1


def _sb_stage1(qs, kts, z_ref, hl_ref, masked):
    tq, kb = qs[0].shape[0], kts[0].shape[1]
    if masked:
        valid = (lax.broadcasted_iota(jnp.int32, (tq, kb), 1)
                 < lax.broadcasted_iota(jnp.int32, (tq, kb), 0))
    for c, (q, kt) in enumerate(zip(qs, kts)):
        z = _dot(q, kt.astype(BF16))
        p = jnp.maximum(z, jnp.log2(1.0 + jnp.exp2(jnp.minimum(z, EXP2_CLAMP))))
        if masked:
            p = jnp.where(valid, p, 0.0)
            z = jnp.where(valid, z, MASKED_LOGIT)
        hi = p.astype(BF16)
        z_ref[c] = z
        hl_ref[c, :, :kb] = hi
        if SB_TERMS == 2:
            hl_ref[c, :, kb:] = (p - hi.astype(F32)).astype(BF16)


def _sb_stage2(z_ref, hl_ref, vts, tri2, acc_ref, car_ref):
    n, tq, kb = z_ref.shape
    r = _dot(hl_ref[...].reshape(n * tq, SB_TERMS * kb), tri2)
    for c in range(n):
        incl = r[c * tq:(c + 1) * tq]
        car = car_ref[c]
        w = jnp.exp2(z_ref[c] + incl + car)
        acc_ref[c] += _dot_nt(w.astype(BF16), vts[c].astype(BF16))
        car_ref[c] = car + incl[:, 0:1]


def _sb_finish(o, g):
    return o * lax.rsqrt(jnp.mean(o * o, axis=-1, keepdims=True) + LN_EPS) * g


def _tri_matrix():
    idx = np.arange(SB_KEY_BLOCK)
    t = -(idx[:, None] >= idx[None, :]).astype(np.float32)
    return jnp.asarray(np.concatenate([t] * SB_TERMS, axis=0), dtype=BF16)


def _sb_prompt_kernel(q_ref, kt_ref, vt_ref, tri_ref, g_ref, o_ref, acc_ref, car_ref,
                      za_ref, ha_ref, zb_ref, hb_ref, *, tq, nh):
    i = pl.program_id(2)
    qs = [q_ref[0, hh] for hh in range(nh)]
    acc_ref[...] = jnp.zeros_like(acc_ref)
    car_ref[...] = jnp.zeros_like(car_ref)

    def s1(blk, z_ref, hl_ref, masked=False):
        start = pl.multiple_of(blk * tq, tq)
        _sb_stage1(qs, [kt_ref[0, hh, :, pl.ds(start, tq)] for hh in range(nh)], z_ref, hl_ref, masked)

    def s2(blk, z_ref, hl_ref):
        start = pl.multiple_of(blk * tq, tq)
        _sb_stage2(z_ref, hl_ref, [vt_ref[0, hh, :, pl.ds(start, tq)] for hh in range(nh)],
                   tri_ref[...], acc_ref, car_ref)

    s1(i, za_ref, ha_ref, masked=True)

    def pair(p, _):
        blk = i - 2 * p
        s1(blk - 1, zb_ref, hb_ref)
        s2(blk, za_ref, ha_ref)
        s1(blk - 2, za_ref, ha_ref)
        s2(blk - 1, zb_ref, hb_ref)
        return 0

    lax.fori_loop(0, i // 2, pair, 0)

    @pl.when(i % 2 == 0)
    def _():
        s2(0, za_ref, ha_ref)

    @pl.when(i % 2 == 1)
    def _():
        s1(0, zb_ref, hb_ref)
        s2(1, za_ref, ha_ref)
        s2(0, zb_ref, hb_ref)

    d = SB_HEAD_DIM
    outs = [_sb_finish(acc_ref[hh], g_ref[0, :, hh * d:(hh + 1) * d]) for hh in range(nh)]
    o_ref[0] = jnp.concatenate(outs, axis=-1).astype(o_ref.dtype)


def _sb_prompt(q, kt, vt, norm_g):
    b, _, t, d = q.shape
    tq = SB_KEY_BLOCK
    nh = SB_HEADS
    kv_spec = pl.BlockSpec((1, nh, d, t), lambda i, h, j: (i, h, 0, 0))
    return pl.pallas_call(
        functools.partial(_sb_prompt_kernel, tq=tq, nh=nh),
        grid=(b, SB_HEADS // nh, t // tq),
        in_specs=[pl.BlockSpec((1, nh, tq, d), lambda i, h, j: (i, h, j, 0)), kv_spec, kv_spec,
                  pl.BlockSpec((SB_TERMS * tq, tq), lambda i, h, j: (0, 0)),
                  pl.BlockSpec((1, 1, nh * d), lambda i, h, j: (h, 0, 0))],
        out_specs=pl.BlockSpec((1, tq, nh * d), lambda i, h, j: (i, j, h)),
        out_shape=jax.ShapeDtypeStruct((b, t, SB_WIDTH), BF16),
        scratch_shapes=[pltpu.VMEM((nh, tq, d), F32), pltpu.VMEM((nh, tq, 1), F32),
                        pltpu.VMEM((nh, tq, tq), F32), pltpu.VMEM((nh, tq, SB_TERMS * tq), BF16),
                        pltpu.VMEM((nh, tq, tq), F32), pltpu.VMEM((nh, tq, SB_TERMS * tq), BF16)],
        compiler_params=_cparams(("parallel", "parallel", "arbitrary"), SB_FLAGS),
        name="sb_prompt",
    )(q, kt, vt, _tri_matrix(), norm_g.reshape(SB_HEADS // nh, 1, nh * d))


def _sb_sample_kernel(q_ref, kt_ref, vt_ref, ktp_ref, vtp_ref, tri_ref, g_ref, o_ref, acc_ref, car_ref,
                      zd_ref, hd_ref, za_ref, ha_ref, zb_ref, hb_ref, *, nh):
    kb = SB_KEY_BLOCK
    n_past = ktp_ref.shape[3] // kb
    t = q_ref.shape[2]
    qs = [q_ref[0, hh] for hh in range(nh)]
    acc_ref[...] = jnp.zeros_like(acc_ref)
    car_ref[...] = jnp.zeros_like(car_ref)

    def s1(blk, z_ref, hl_ref):
        start = pl.multiple_of(blk * kb, kb)
        _sb_stage1(qs, [ktp_ref[0, hh, :, pl.ds(start, kb)] for hh in range(nh)], z_ref, hl_ref, False)

    def s2(blk, z_ref, hl_ref):
        start = pl.multiple_of(blk * kb, kb)
        _sb_stage2(z_ref, hl_ref, [vtp_ref[0, hh, :, pl.ds(start, kb)] for hh in range(nh)],
                   tri_ref[...], acc_ref, car_ref)

    _sb_stage1(qs, [kt_ref[0, hh] for hh in range(nh)], zd_ref, hd_ref, True)
    s1(n_past - 1, za_ref, ha_ref)
    _sb_stage2(zd_ref, hd_ref, [vt_ref[0, hh] for hh in range(nh)],
               jnp.concatenate([tri_ref[:t, :t]] * SB_TERMS, axis=0), acc_ref, car_ref)

    def pair(p, _):
        blk = n_past - 1 - 2 * p
        s1(blk - 1, zb_ref, hb_ref)
        s2(blk, za_ref, ha_ref)
        s1(blk - 2, za_ref, ha_ref)
        s2(blk - 1, zb_ref, hb_ref)
        return 0

    lax.fori_loop(0, (n_past - 1) // 2, pair, 0)
    if (n_past - 1) % 2 == 0:
        s2(0, za_ref, ha_ref)
    else:
        s1(0, zb_ref, hb_ref)
        s2(1, za_ref, ha_ref)
        s2(0, zb_ref, hb_ref)
    d = SB_HEAD_DIM
    outs = [_sb_finish(acc_ref[hh], g_ref[0, :, hh * d:(hh + 1) * d]) for hh in range(nh)]
    o_ref[0] = jnp.concatenate(outs, axis=-1).astype(o_ref.dtype)


def _sb_sample(q, kt, vt, kt_past, vt_past, norm_g):
    b, _, t, d = q.shape
    p = kt_past.shape[3]
    nh = SB_HEADS
    kb = SB_KEY_BLOCK
    new_spec = pl.BlockSpec((1, nh, d, t), lambda i, h: (i, h, 0, 0))
    past_spec = pl.BlockSpec((1, nh, d, p), lambda i, h: (i, h, 0, 0))
    return pl.pallas_call(
        functools.partial(_sb_sample_kernel, nh=nh),
        grid=(b, SB_HEADS // nh),
        in_specs=[pl.BlockSpec((1, nh, t, d), lambda i, h: (i, h, 0, 0)), new_spec, new_spec,
                  past_spec, past_spec,
                  pl.BlockSpec((SB_TERMS * kb, kb), lambda i, h: (0, 0)),
                  pl.BlockSpec((1, 1, nh * d), lambda i, h: (h, 0, 0))],
        out_specs=pl.BlockSpec((1, t, nh * d), lambda i, h: (i, 0, h)),
        out_shape=jax.ShapeDtypeStruct((b, t, SB_WIDTH), BF16),
        scratch_shapes=[pltpu.VMEM((nh, t, d), F32), pltpu.VMEM((nh, t, 1), F32),
                        pltpu.VMEM((nh, t, t), F32), pltpu.VMEM((nh, t, SB_TERMS * t), BF16),
                        pltpu.VMEM((nh, t, kb), F32), pltpu.VMEM((nh, t, SB_TERMS * kb), BF16),
                        pltpu.VMEM((nh, t, kb), F32), pltpu.VMEM((nh, t, SB_TERMS * kb), BF16)],
        compiler_params=_cparams(("parallel", "arbitrary"), SB_FLAGS),
        name="sb_sample",
    )(q, kt, vt, kt_past, vt_past, _tri_matrix(), norm_g.reshape(SB_HEADS // nh, 1, nh * d))


N_PAIRS = 6
N_BUCKETS = N_GROUPS * N_PAIRS
MOE_TM = 256
XW = D_MODEL + LANES
_PAIRS = [(a, b) for a in range(EXPERTS_PER_GROUP) for b in range(a + 1, EXPERTS_PER_GROUP)]
BUCKET_E0 = np.array([g * EXPERTS_PER_GROUP + a for g in range(N_GROUPS) for a, _ in _PAIRS], np.int32)
BUCKET_E1 = np.array([g * EXPERTS_PER_GROUP + b for g in range(N_GROUPS) for _, b in _PAIRS], np.int32)


def _route(logits):
    lane = lax.broadcasted_iota(jnp.int32, logits.shape, 1)
    neg = -jnp.inf
    big = jnp.int32(2 * LANES)
    gl = jnp.where(lane < N_GROUPS, logits, neg)
    gmax = jnp.max(gl, axis=-1, keepdims=True)
    g_idx = jnp.min(jnp.where(gl == gmax, lane, big), axis=-1, keepdims=True)
    g_p = 1.0 / jnp.sum(jnp.exp(gl - gmax), axis=-1, keepdims=True)
    lo = N_GROUPS + g_idx * EXPERTS_PER_GROUP
    el = jnp.where((lane >= lo) & (lane < lo + EXPERTS_PER_GROUP), logits, neg)
    v1 = jnp.max(el, axis=-1, keepdims=True)
    i1 = jnp.min(jnp.where(el == v1, lane, big), axis=-1, keepdims=True)
    el2 = jnp.where(lane == i1, neg, el)
    v2 = jnp.max(el2, axis=-1, keepdims=True)
    i2 = jnp.min(jnp.where(el2 == v2, lane, big), axis=-1, keepdims=True)
    e21 = jnp.exp(v2 - v1)
    p1 = 1.0 / (1.0 + e21)
    p2 = e21 * p1
    first_lo = i1 < i2
    w_lo = jnp.where(first_lo, p1, p2) * g_p
    w_hi = jnp.where(first_lo, p2, p1) * g_p
    a = jnp.minimum(i1, i2) - lo
    b = jnp.maximum(i1, i2) - lo
    pair = jnp.where(a == 0, b - 1, jnp.where(a == 1, b + 1, 5))
    return g_idx * N_PAIRS + pair, w_lo, w_hi


def _out_proj_kernel(x_ref, mr_ref, ms_ref, wo_ref, mm_ref, mf_ref, lg_ref, lb_ref, wr_ref, br_ref,
                     tril_ref, cin_ref, x1_ref, hx_ref, rt_ref, cnt_ref, run_ref):
    @pl.when((pl.program_id(0) == 0) & (pl.program_id(1) == 0))
    def _():
        run_ref[...] = cin_ref[...]

    mix = _dot(mr_ref[0], wo_ref[:MIX_HALF]) + _dot(ms_ref[0], wo_ref[MIX_HALF:])
    gate = mm_ref[0][:, 2 * D_MODEL:]
    x1 = _ln_plain(DEEPNORM_ALPHA * x_ref[0] + gate * mix) * lg_ref[...] + lb_ref[...]
    x1_ref[0] = x1
    mf = mf_ref[0]
    h2 = _ln_plain(x1) * (1.0 + mf[:, D_MODEL:2 * D_MODEL]) + mf[:, :D_MODEL]
    bucket, w_lo, w_hi = _route(_dot(h2.astype(BF16), wr_ref[...]) + br_ref[...])
    tm = h2.shape[0]
    lane = lax.broadcasted_iota(jnp.int32, (tm, LANES), 1)
    hx_ref[0, :, :D_MODEL] = h2
    hx_ref[0, :, D_MODEL:] = jnp.where(lane == 0, w_lo, jnp.where(lane == 1, w_hi, 0.0))
    hit = lane == bucket
    onehot = hit.astype(BF16)
    before = _dot(tril_ref[...], onehot)
    run = run_ref[...]
    rank = jnp.sum(jnp.where(hit, before + run, 0.0), axis=-1, keepdims=True)
    run = run + before[tm - 1:tm] + onehot[tm - 1:tm].astype(F32)
    run_ref[...] = run
    cnt_ref[...] = run
    rt_ref[0] = jnp.where(lane == 0, bucket.astype(F32), jnp.where(lane == 1, rank, 0.0))


MIX_HALF = RET_WIDTH


def _out_proj(x, mr, ms, w_out_bf, mod_mix, mod_ffn, ln_g, ln_b, w_route_bf, b_route, counts_in, tm):
    b, t, _ = x.shape
    tok = lambda w: pl.BlockSpec((1, tm, w), lambda i, j: (i, j, 0))
    modspec = pl.BlockSpec((1, 1, 3 * D_MODEL), lambda i, j: (i, 0, 0))
    row = lambda w: pl.BlockSpec((1, w), lambda i, j: (0, 0))
    idx = np.arange(tm)
    tril = jnp.asarray(idx[:, None] > idx[None, :], dtype=BF16)
    return pl.pallas_call(
        _out_proj_kernel,
        grid=(b, t // tm),
        in_specs=[tok(D_MODEL), tok(RET_WIDTH), tok(SB_WIDTH),
                  pl.BlockSpec((D_MODEL, D_MODEL), lambda i, j: (0, 0)),
                  modspec, modspec, row(D_MODEL), row(D_MODEL),
                  pl.BlockSpec((D_MODEL, LANES), lambda i, j: (0, 0)), row(LANES),
                  pl.BlockSpec((tm, tm), lambda i, j: (0, 0)), row(LANES)],
        out_specs=[tok(D_MODEL), tok(XW), tok(LANES), row(LANES)],
        out_shape=[jax.ShapeDtypeStruct((b, t, D_MODEL), F32),
                   jax.ShapeDtypeStruct((b, t, XW), F32),
                   jax.ShapeDtypeStruct((b, t, LANES), F32),
                   jax.ShapeDtypeStruct((1, LANES), F32)],
        scratch_shapes=[pltpu.VMEM((1, LANES), F32)],
        compiler_params=_cparams(("arbitrary", "arbitrary")),
        name="out_proj",
    )(x, mr, ms, w_out_bf, mod_mix.reshape(b, 1, -1), mod_ffn.reshape(b, 1, -1),
      ln_g.reshape(1, -1), ln_b.reshape(1, -1), w_route_bf, b_route, tril, counts_in)


def _route_plan(routes, counts, n_tiles):
    bucket = jnp.concatenate([r[..., 0].astype(jnp.int32).reshape(-1) for r in routes])
    rank = jnp.concatenate([r[..., 1].astype(jnp.int32).reshape(-1) for r in routes])
    cnt = counts[0, :N_BUCKETS].astype(jnp.int32)
    padded = ((cnt + MOE_TM - 1) // MOE_TM) * MOE_TM
    ends = jnp.cumsum(padded)
    starts = ends - padded
    n_used = ends[-1] // MOE_TM
    tile = jnp.arange(n_tiles, dtype=jnp.int32)
    last = jnp.maximum(n_used - 1, 0)
    tile_idx = jnp.minimum(tile, last)
    tile_bucket = jnp.sum((ends[None, :] <= (tile_idx * MOE_TM)[:, None]).astype(jnp.int32), axis=1)
    tile_bucket = jnp.minimum(tile_bucket, N_BUCKETS - 1)
    e0 = jnp.asarray(BUCKET_E0)[tile_bucket]
    e1 = jnp.asarray(BUCKET_E1)[tile_bucket]
    in_bucket = bucket[:, None] == jnp.arange(N_BUCKETS, dtype=jnp.int32)[None, :]
    dest = rank + jnp.sum(jnp.where(in_bucket, starts[None, :], 0), axis=1)
    trailing = n_used + jnp.arange(N_BUCKETS, dtype=jnp.int32)
    ztiles = jnp.concatenate([jnp.where(padded > 0, ends - MOE_TM, -1),
                              jnp.where(trailing < n_tiles, trailing * MOE_TM, -1)])
    return (dest.astype(jnp.int32), ztiles.astype(jnp.int32), e0, e1, tile_idx,
            n_used.reshape(1).astype(jnp.int32))


def _row_copy(src_ref, s, dst_ref, d, sem):
    return pltpu.make_async_copy(src_ref.at[pl.ds(s, 1)], dst_ref.at[pl.ds(d, 1)], sem)


DMA_UNROLL = 8


SUBLANES = 8


def _scatter_kernel(dhi_ref, dlo_ref, ztile_ref, *refs, rows, steps):
    srcs = refs[:len(steps)]
    out_ref, zbuf, sem, zsem = refs[len(steps):]
    i = pl.program_id(0)

    @pl.when(i == 0)
    def _():
        zbuf[...] = jnp.zeros_like(zbuf)

        def fill(k, _):
            @pl.when(ztile_ref[k] >= 0)
            def _():
                g0 = ztile_ref[k] // SUBLANES
                pltpu.make_async_copy(zbuf, out_ref.at[pl.ds(g0, MOE_TM // SUBLANES)], zsem).start()
            return 0

        def drain(k, _):
            @pl.when(ztile_ref[k] >= 0)
            def _():
                pltpu.make_async_copy(zbuf, out_ref.at[pl.ds(0, MOE_TM // SUBLANES)], zsem).wait()
            return 0

        lax.fori_loop(0, ztile_ref.shape[0], fill, 0)
        lax.fori_loop(0, ztile_ref.shape[0], drain, 0)

    base = i * rows
    first = 0
    for src_ref, n_steps in zip(srcs, steps):
        @pl.when((i >= first) & (i < first + n_steps))
        def _(src_ref=src_ref):
            def issue(g, _):
                for u in range(SUBLANES):
                    tok = base + g * SUBLANES + u
                    pltpu.make_async_copy(src_ref.at[g, pl.ds(u, 1)],
                                          out_ref.at[dhi_ref[tok], pl.ds(dlo_ref[tok], 1)], sem).start()
                return 0

            lax.fori_loop(0, rows // SUBLANES, issue, 0)
            pltpu.make_async_copy(src_ref, out_ref.at[pl.ds(0, rows // SUBLANES)], sem).wait()
        first += n_steps


def _scatter_rows(hxs, dest, ztiles, n_rows):
    w = hxs[0].shape[1]
    rows = min([256] + [h.shape[0] for h in hxs])
    steps = tuple(h.shape[0] // rows for h in hxs)
    firsts = [sum(steps[:k]) for k in range(len(steps))]
    in_specs = [pl.BlockSpec((rows // SUBLANES, SUBLANES, w),
                             lambda i, dh, dl, z, f=f, s=s: (jnp.clip(i - f, 0, s - 1), 0, 0))
                for f, s in zip(firsts, steps)]
    out = pl.pallas_call(
        functools.partial(_scatter_kernel, rows=rows, steps=steps),
        grid_spec=pltpu.PrefetchScalarGridSpec(
            num_scalar_prefetch=3, grid=(sum(steps),),
            in_specs=in_specs,
            out_specs=pl.BlockSpec(memory_space=pl.ANY),
            scratch_shapes=[pltpu.VMEM((MOE_TM // SUBLANES, SUBLANES, w), F32),
                            pltpu.SemaphoreType.DMA(()), pltpu.SemaphoreType.DMA(())]),
        out_shape=jax.ShapeDtypeStruct((n_rows // SUBLANES, SUBLANES, w), F32),
        compiler_params=_cparams(("arbitrary",)),
        name="scatter_rows",
    )(dest // SUBLANES, dest % SUBLANES, ztiles,
      *[h.reshape(h.shape[0] // SUBLANES, SUBLANES, w) for h in hxs])
    return out.reshape(n_rows, w)


def _moe_kernel(e0_ref, e1_ref, ti_ref, nu_ref, x_ref, wg0, wu0, wd0, wg1, wu1, wd1, y_ref):
    del e0_ref, e1_ref, ti_ref

    @pl.when(pl.program_id(0) < nu_ref[0])
    def _():
        x = x_ref[...]
        h = x[:, :D_MODEL].astype(BF16)
        wx = x[:, D_MODEL:]
        lane = lax.broadcasted_iota(jnp.int32, wx.shape, 1)
        w_lo = jnp.sum(jnp.where(lane == 0, wx, 0.0), axis=-1, keepdims=True)
        w_hi = jnp.sum(jnp.where(lane == 1, wx, 0.0), axis=-1, keepdims=True)

        def expert(wg, wu, wd):
            a = _silu(_dot(h, wg[0])) * _dot(h, wu[0])
            return _dot(a.astype(BF16), wd[0])

        y_ref[...] = w_lo * expert(wg0, wu0, wd0) + w_hi * expert(wg1, wu1, wd1)

    @pl.when(pl.program_id(0) >= nu_ref[0])
    def _():
        y_ref[...] = jnp.zeros_like(y_ref)


def _moe(xs, e0, e1, tile_idx, n_used, wg_bf, wu_bf, wd_bf):
    n_rows = xs.shape[0]
    up = lambda sel: pl.BlockSpec((1, D_MODEL, D_EXPERT), lambda t, e0, e1, ti, nu: ((e0, e1)[sel][t], 0, 0))
    down = lambda sel: pl.BlockSpec((1, D_EXPERT, D_MODEL), lambda t, e0, e1, ti, nu: ((e0, e1)[sel][t], 0, 0))
    return pl.pallas_call(
        _moe_kernel,
        grid_spec=pltpu.PrefetchScalarGridSpec(
            num_scalar_prefetch=4, grid=(n_rows // MOE_TM,),
            in_specs=[pl.BlockSpec((MOE_TM, XW), lambda t, e0, e1, ti, nu: (ti[t], 0)),
                      up(0), up(0), down(0), up(1), up(1), down(1)],
            out_specs=pl.BlockSpec((MOE_TM, D_MODEL), lambda t, e0, e1, ti, nu: (t, 0))),
        out_shape=jax.ShapeDtypeStruct((n_rows, D_MODEL), F32),
        compiler_params=_cparams(("arbitrary",)),
        name="moe_routed",
    )(e0, e1, tile_idx, n_used, xs, wg_bf, wu_bf, wd_bf, wg_bf, wu_bf, wd_bf)


def _final_kernel(dest_ref, x1_ref, ys_ref, mf_ref, lg_ref, lb_ref, o_ref, buf, sem, *, tm):
    nj = pl.num_programs(1)
    step = pl.program_id(0) * nj + pl.program_id(1)
    n_steps = pl.num_programs(0) * nj
    slot = step % 2

    def fetch(s, sl):
        def issue(r, _):
            _row_copy(ys_ref, dest_ref[s * tm + r], buf.at[sl], r, sem.at[sl]).start()
            return 0
        lax.fori_loop(0, tm, issue, 0, unroll=DMA_UNROLL)

    @pl.when(step == 0)
    def _():
        fetch(step, slot)

    @pl.when(step + 1 < n_steps)
    def _():
        fetch(step + 1, 1 - slot)

    pltpu.make_async_copy(ys_ref.at[pl.ds(0, tm)], buf.at[slot], sem.at[slot]).wait()
    gate = mf_ref[0][:, 2 * D_MODEL:]
    o_ref[0] = _ln_plain(DEEPNORM_ALPHA * x1_ref[0] + gate * buf[slot]) * lg_ref[...] + lb_ref[...]


def _final(x1, ys, dest, mod_ffn, ln_g, ln_b, tm):
    b, t, _ = x1.shape
    tok = pl.BlockSpec((1, tm, D_MODEL), lambda i, j, *_: (i, j, 0))
    row = pl.BlockSpec((1, D_MODEL), lambda i, j, *_: (0, 0))
    return pl.pallas_call(
        functools.partial(_final_kernel, tm=tm),
        grid_spec=pltpu.PrefetchScalarGridSpec(
            num_scalar_prefetch=1, grid=(b, t // tm),
            in_specs=[tok, pl.BlockSpec(memory_space=pl.ANY),
                      pl.BlockSpec((1, 1, 3 * D_MODEL), lambda i, j, *_: (i, 0, 0)), row, row],
            out_specs=tok,
            scratch_shapes=[pltpu.VMEM((2, tm, D_MODEL), F32), pltpu.SemaphoreType.DMA((2,))]),
        out_shape=jax.ShapeDtypeStruct((b, t, D_MODEL), F32),
        compiler_params=_cparams(("arbitrary", "arbitrary")),
        name="final_ln",
    )(dest, x1, ys, mod_ffn.reshape(b, 1, -1), ln_g.reshape(1, -1), ln_b.reshape(1, -1))


def _mixer_half(x, mod_mix, mod_ffn, pos0, state0, k_past, v_past, counts_in, wts):
    b, t, _ = x.shape
    ret_in, sq, skt, svt = _in_proj(x, mod_mix, wts["w_in_a"], wts["w_in_kvt"], pos0, min(512, t))
    mr, state = _retention(ret_in, state0, wts["ret_norm_g"], min(512, t))
    if k_past is None:
        ms = _sb_prompt(sq, skt, svt, wts["sb_norm_g"])
    else:
        ms = _sb_sample(sq, skt, svt, jnp.swapaxes(k_past, 2, 3), jnp.swapaxes(v_past, 2, 3),
                        wts["sb_norm_g"])
    sk, sv = jnp.swapaxes(skt, 2, 3), jnp.swapaxes(svt, 2, 3)
    x1, hx, route, counts = _out_proj(x, mr, ms, wts["w_out"], mod_mix, mod_ffn, wts["ln_mix_g"],
                                      wts["ln_mix_b"], wts["w_route"], wts["b_route"], counts_in,
                                      min(1024, t))
    return dict(x1=x1, hx=hx.reshape(b * t, XW), route=route, counts=counts, mod_ffn=mod_ffn,
                sk=sk[None], sv=sv[None], state=state[None])


def _ffn_half(groups, wts):
    n = sum(g["hx"].shape[0] for g in groups)
    n_tiles = n // MOE_TM + N_BUCKETS
    dest, ztiles, e0, e1, tile_idx, n_used = _route_plan([g["route"] for g in groups],
                                                         groups[-1]["counts"], n_tiles)
    xs = _scatter_rows([g["hx"] for g in groups], dest, ztiles, n_tiles * MOE_TM)
    ys = _moe(xs, e0, e1, tile_idx, n_used, wts["w_e_gate"], wts["w_e_up"], wts["w_e_down"])
    outs, first = [], 0
    for g in groups:
        t = g["x1"].shape[1]
        n_g = g["hx"].shape[0]
        outs.append(_final(g["x1"], ys, dest[first:first + n_g], g["mod_ffn"], wts["ln_ffn_g"],
                           wts["ln_ffn_b"], min(256, t)))
        first += n_g
    return outs


def kernel(x_prompt, x_sample, cache_sb_k, cache_sb_v, state_ret, c_prompt, c_sample, w_in, w_out, ret_norm_g, sb_norm_g, w_ada_mix, b_ada_mix, ln_mix_g, ln_mix_b, w_ada_ffn, b_ada_ffn, ln_ffn_g, ln_ffn_b, w_group, b_group, w_router, b_router, w_e_gate, w_e_up, w_e_down):
    bp = x_prompt.shape[0]
    c_all = jnp.concatenate([c_prompt, c_sample], axis=0)
    mod_mix = _ada(c_all, w_ada_mix[0], b_ada_mix[0])
    mod_ffn = _ada(c_all, w_ada_ffn[0], b_ada_ffn[0])
    pad = LANES - N_GROUPS - N_EXPERTS
    w_route = jnp.concatenate([w_group[0], w_router[0], jnp.zeros((D_MODEL, pad), F32)], axis=1)
    b_route = jnp.concatenate([b_group[0], b_router[0], jnp.zeros((pad,), F32)]).reshape(1, LANES)
    n_a = 4 * RET_WIDTH + SB_WIDTH
    wts = dict(w_in_a=w_in[0, :, :n_a].astype(BF16), w_in_kvt=w_in[0, :, n_a:].T.astype(BF16),
               w_out=w_out[0].astype(BF16),
               ret_norm_g=ret_norm_g[0], sb_norm_g=sb_norm_g[0],
               ln_mix_g=ln_mix_g[0], ln_mix_b=ln_mix_b[0], ln_ffn_g=ln_ffn_g[0], ln_ffn_b=ln_ffn_b[0],
               w_route=w_route.astype(BF16), b_route=b_route,
               w_e_gate=w_e_gate[0].astype(BF16), w_e_up=w_e_up[0].astype(BF16),
               w_e_down=w_e_down[0].astype(BF16))
    ret_zero = jnp.zeros((bp, RET_HEADS, RET_DK, RET_DV), F32)
    gp = _mixer_half(x_prompt, mod_mix[:bp], mod_ffn[:bp], 0, ret_zero, None, None,
                     jnp.zeros((1, LANES), F32), wts)
    gs = _mixer_half(x_sample, mod_mix[bp:], mod_ffn[bp:], cache_sb_k.shape[3], state_ret[0],
                     cache_sb_k[0], cache_sb_v[0], gp["counts"], wts)
    y_p, y_s = _ffn_half([gp, gs], wts)
    return (y_p, y_s, gp["sk"], gp["sv"], gp["state"], gs["sk"], gs["sv"], gs["state"])
```

```python
import functools
import math

import numpy as np
import jax
import jax.numpy as jnp
from jax import lax
from jax.experimental import pallas as pl
from jax.experimental.pallas import tpu as pltpu

D_MODEL = 1024
RET_HEADS = 4
RET_DK = 128
RET_DV = 128
RET_WIDTH = RET_HEADS * RET_DV
SB_HEADS = 8
SB_HEAD_DIM = 64
SB_WIDTH = SB_HEADS * SB_HEAD_DIM
IN_WIDTH = 2 * RET_HEADS * RET_DK + 2 * RET_WIDTH + 3 * SB_WIDTH
ROPE_BASE = 10000.0
N_GROUPS = 4
EXPERTS_PER_GROUP = 4
N_EXPERTS = N_GROUPS * EXPERTS_PER_GROUP
D_EXPERT = 512
DEPTH = 1
DEEPNORM_ALPHA = (2.0 * DEPTH) ** 0.25
LN_EPS = 1e-5

LANES = 128
VMEM_LIMIT = 48 * 1024 * 1024

F32 = jnp.float32
BF16 = jnp.bfloat16


def _cparams(sem, flags=None):
    return pltpu.CompilerParams(dimension_semantics=sem, vmem_limit_bytes=VMEM_LIMIT, flags=flags)


SB_FLAGS = None


def _dot(a, b):
    return jnp.dot(a, b, preferred_element_type=F32)


def _dot_nt(a, b):
    return lax.dot_general(a, b, (((1,), (1,)), ((), ())), preferred_element_type=F32)


def _dot_tn(a, b):
    return lax.dot_general(a, b, (((0,), (0,)), ((), ())), preferred_element_type=F32)


def _split_dot(a, w_hi, w_lo):
    a_hi = a.astype(BF16)
    a_lo = (a - a_hi.astype(F32)).astype(BF16)
    return _dot(a_hi, w_hi) + (_dot(a_hi, w_lo) + _dot(a_lo, w_hi))


def _ln_plain(x):
    mu = jnp.mean(x, axis=-1, keepdims=True)
    xc = x - mu
    var = jnp.mean(xc * xc, axis=-1, keepdims=True)
    return xc * lax.rsqrt(var + LN_EPS)


def _silu(x):
    return x * (1.0 / (1.0 + jnp.exp(-x)))


def _ada_kernel(c_ref, w_ref, b_ref, o_ref):
    c = c_ref[...]
    w = w_ref[...]
    w_hi = w.astype(BF16)
    w_lo = (w - w_hi.astype(F32)).astype(BF16)
    o_ref[...] = _split_dot(_silu(c), w_hi, w_lo) + b_ref[...]


def _ada(c, w, b):
    r = c.shape[0]
    tn = 768
    return pl.pallas_call(
        _ada_kernel,
        grid=(3 * D_MODEL // tn,),
        in_specs=[pl.BlockSpec((r, D_MODEL), lambda j: (0, 0)),
                  pl.BlockSpec((D_MODEL, tn), lambda j: (0, j)),
                  pl.BlockSpec((1, tn), lambda j: (0, j))],
        out_specs=pl.BlockSpec((r, tn), lambda j: (0, j)),
        out_shape=jax.ShapeDtypeStruct((r, 3 * D_MODEL), F32),
        compiler_params=_cparams(("arbitrary",)),
        name="ada_mod",
    )(c, w, b.reshape(1, -1))


def _in_proj_kernel(x_ref, mod_ref, w_ref, wkv_ref, tab_ref, s0_ref, dec_ref, qd_ref, kd_ref, ng_ref,
                    mr_ref, q_ref, k_ref, v_ref, so_ref, st_ref, *, chunk, chunk_decay):
    j = pl.program_id(1)

    @pl.when(j == 0)
    def _():
        st_ref[...] = s0_ref[0]

    m = mod_ref[0]
    h = _ln_plain(x_ref[0]) * (1.0 + m[:, D_MODEL:2 * D_MODEL]) + m[:, :D_MODEL]
    h = h.astype(BF16)
    tm = h.shape[0]
    tab = tab_ref[...]
    grp = []
    for c in range(4):
        p = _dot(h, w_ref[:, c * RET_WIDTH:(c + 1) * RET_WIDTH])
        if c < 2:
            cs = tab[:, (2 * c) * LANES:(2 * c + 1) * LANES]
            sn = tab[:, (2 * c + 1) * LANES:(2 * c + 2) * LANES]
            heads = []
            for hh in range(RET_HEADS):
                ph = p[:, hh * RET_DK:(hh + 1) * RET_DK]
                heads.append(ph * cs + pltpu.roll(ph, RET_DK // 2, 1) * sn)
            grp.append(heads)
        else:
            grp.append([p[:, hh * RET_DV:(hh + 1) * RET_DV] for hh in range(RET_HEADS)])
    for hh in range(RET_HEADS):
        sl = slice(hh * RET_DV, (hh + 1) * RET_DV)
        for ch in range(tm // chunk):
            rows = slice(ch * chunk, (ch + 1) * chunk)
            q, k, v, g = (grp[c][hh][rows] for c in range(4))
            vb = v.astype(BF16)
            st = st_ref[hh]
            scores = _dot_nt(q.astype(BF16), k.astype(BF16)) * dec_ref[hh]
            o = _dot(scores.astype(BF16), vb) + _dot((q * qd_ref[hh]).astype(BF16), st.astype(BF16))
            st_ref[hh] = st * chunk_decay[hh] + _dot_tn((k * kd_ref[hh]).astype(BF16), vb)
            o = _ln_plain(o) * ng_ref[:, sl] * _silu(g)
            mr_ref[0, rows, sl] = o.astype(mr_ref.dtype)

    @pl.when(j == pl.num_programs(1) - 1)
    def _():
        so_ref[0] = st_ref[...]

    base = 4 * RET_WIDTH
    p = _dot(h, w_ref[:, base:base + SB_WIDTH]) * (LOG2E * SB_HEAD_DIM ** -0.5)
    for hh in range(SB_HEADS):
        q_ref[0, hh] = p[:, hh * SB_HEAD_DIM:(hh + 1) * SB_HEAD_DIM].astype(q_ref.dtype)
    pt = _dot_nt(wkv_ref[...], h)
    for c, ref in enumerate((k_ref, v_ref)):
        for hh in range(SB_HEADS):
            r0 = c * SB_WIDTH + hh * SB_HEAD_DIM
            ref[0, hh] = pt[r0:r0 + SB_HEAD_DIM, :]


def _rope_table(pos0, t):
    half = RET_DK // 2
    inv = ROPE_BASE ** (-np.arange(half, dtype=np.float64) / half)
    ang = (pos0 + np.arange(t, dtype=np.float64))[:, None] * inv[None, :]
    cos, sin = np.cos(ang), np.sin(ang)
    cs = np.concatenate([cos, cos], axis=1)
    sn = np.concatenate([-sin, sin], axis=1)
    ks = RET_DK ** -0.5
    return jnp.asarray(np.concatenate([cs, sn, cs * ks, sn * ks], axis=1), dtype=F32)


def _in_proj(x, mod, w_a_bf, w_kvt_bf, pos0, state0, norm_g, tm):
    b, t, _ = x.shape
    tab = _rope_table(pos0, t)
    wa = w_a_bf.shape[1]
    chunk = min(256, tm)
    lg = np.log1p(-np.exp2(-5.0 - np.arange(RET_HEADS, dtype=np.float64)))
    idx = np.arange(chunk, dtype=np.float64)
    rel = idx[:, None] - idx[None, :]
    dec = np.where(rel >= 0, np.exp(lg[:, None, None] * np.maximum(rel, 0.0)), 0.0)
    qd = np.broadcast_to(np.exp(lg[:, None] * (idx + 1.0))[:, :, None], (RET_HEADS, chunk, RET_DK))
    kd = np.broadcast_to(np.exp(lg[:, None] * (chunk - 1.0 - idx))[:, :, None], (RET_HEADS, chunk, RET_DK))
    chunk_decay = tuple(float(v) for v in np.exp(lg * chunk))
    hs = jax.ShapeDtypeStruct((b, SB_HEADS, SB_HEAD_DIM, t), F32)
    q_spec = pl.BlockSpec((1, SB_HEADS, tm, SB_HEAD_DIM), lambda i, j: (i, 0, j, 0))
    kv_spec = pl.BlockSpec((1, SB_HEADS, SB_HEAD_DIM, tm), lambda i, j: (i, 0, 0, j))
    const3 = lambda shape: pl.BlockSpec(shape, lambda i, j: (0, 0, 0))
    state_spec = pl.BlockSpec((1, RET_HEADS, RET_DK, RET_DV), lambda i, j: (i, 0, 0, 0))
    return pl.pallas_call(
        functools.partial(_in_proj_kernel, chunk=chunk, chunk_decay=chunk_decay),
        grid=(b, t // tm),
        in_specs=[pl.BlockSpec((1, tm, D_MODEL), lambda i, j: (i, j, 0)),
                  pl.BlockSpec((1, 1, 3 * D_MODEL), lambda i, j: (i, 0, 0)),
                  pl.BlockSpec((D_MODEL, wa), lambda i, j: (0, 0)),
                  pl.BlockSpec((2 * SB_WIDTH, D_MODEL), lambda i, j: (0, 0)),
                  pl.BlockSpec((tm, 4 * LANES), lambda i, j: (j, 0)),
                  state_spec, const3((RET_HEADS, chunk, chunk)), const3((RET_HEADS, chunk, RET_DK)),
                  const3((RET_HEADS, chunk, RET_DK)),
                  pl.BlockSpec((1, RET_WIDTH), lambda i, j: (0, 0))],
        out_specs=[pl.BlockSpec((1, tm, RET_WIDTH), lambda i, j: (i, j, 0)),
                   q_spec, kv_spec, kv_spec, state_spec],
        out_shape=[jax.ShapeDtypeStruct((b, t, RET_WIDTH), BF16),
                   jax.ShapeDtypeStruct((b, SB_HEADS, t, SB_HEAD_DIM), BF16), hs, hs,
                   jax.ShapeDtypeStruct((b, RET_HEADS, RET_DK, RET_DV), F32)],
        scratch_shapes=[pltpu.VMEM((RET_HEADS, RET_DK, RET_DV), F32)],
        compiler_params=_cparams(("parallel", "arbitrary")),
        name="in_proj",
    )(x, mod.reshape(b, 1, -1), w_a_bf, w_kvt_bf, tab, state0,
      jnp.asarray(dec, F32), jnp.asarray(qd, F32), jnp.asarray(kd, F32), norm_g.reshape(1, -1))


SB_KEY_BLOCK = 256


LOG2E = 1.4426950408889634


MASKED_LOGIT = -1e30
EXP2_CLAMP = 126.0
SB_TERMS = 1


def _sb_stage1(qs, kts, z_ref, hl_ref, masked):
    tq, kb = qs[0].shape[0], kts[0].shape[1]
    if masked:
        valid = (lax.broadcasted_iota(jnp.int32, (tq, kb), 1)
                 < lax.broadcasted_iota(jnp.int32, (tq, kb), 0))
    for c, (q, kt) in enumerate(zip(qs, kts)):
        z = _dot(q, kt.astype(BF16))
        p = jnp.maximum(z, jnp.log2(1.0 + jnp.exp2(jnp.minimum(z, EXP2_CLAMP))))
        if masked:
            p = jnp.where(valid, p, 0.0)
            z = jnp.where(valid, z, MASKED_LOGIT)
        hi = p.astype(BF16)
        z_ref[c] = z
        hl_ref[c, :, :kb] = hi
        if SB_TERMS == 2:
            hl_ref[c, :, kb:] = (p - hi.astype(F32)).astype(BF16)


def _sb_stage2(z_ref, hl_ref, vts, tri2, acc_ref, car_ref):
    n, tq, kb = z_ref.shape
    r = _dot(hl_ref[...].reshape(n * tq, SB_TERMS * kb), tri2)
    for c in range(n):
        incl = r[c * tq:(c + 1) * tq]
        car = car_ref[c]
        w = jnp.exp2(z_ref[c] + incl + car)
        acc_ref[c] += _dot_nt(w.astype(BF16), vts[c].astype(BF16))
        car_ref[c] = car + incl[:, 0:1]


def _sb_finish(o, g):
    return o * lax.rsqrt(jnp.mean(o * o, axis=-1, keepdims=True) + LN_EPS) * g


def _tri_matrix():
    idx = np.arange(SB_KEY_BLOCK)
    t = -(idx[:, None] >= idx[None, :]).astype(np.float32)
    return jnp.asarray(np.concatenate([t] * SB_TERMS, axis=0), dtype=BF16)


def _sb_prompt_kernel(q_ref, kt_ref, vt_ref, tri_ref, g_ref, o_ref, acc_ref, car_ref,
                      za_ref, ha_ref, zb_ref, hb_ref, *, tq, nh):
    i = pl.program_id(2)
    qs = [q_ref[0, hh] for hh in range(nh)]
    acc_ref[...] = jnp.zeros_like(acc_ref)
    car_ref[...] = jnp.zeros_like(car_ref)

    def s1(blk, z_ref, hl_ref, masked=False):
        start = pl.multiple_of(blk * tq, tq)
        _sb_stage1(qs, [kt_ref[0, hh, :, pl.ds(start, tq)] for hh in range(nh)], z_ref, hl_ref, masked)

    def s2(blk, z_ref, hl_ref):
        start = pl.multiple_of(blk * tq, tq)
        _sb_stage2(z_ref, hl_ref, [vt_ref[0, hh, :, pl.ds(start, tq)] for hh in range(nh)],
                   tri_ref[...], acc_ref, car_ref)

    s1(i, za_ref, ha_ref, masked=True)

    def pair(p, _):
        blk = i - 2 * p
        s1(blk - 1, zb_ref, hb_ref)
        s2(blk, za_ref, ha_ref)
        s1(blk - 2, za_ref, ha_ref)
        s2(blk - 1, zb_ref, hb_ref)
        return 0

    lax.fori_loop(0, i // 2, pair, 0)

    @pl.when(i % 2 == 0)
    def _():
        s2(0, za_ref, ha_ref)

    @pl.when(i % 2 == 1)
    def _():
        s1(0, zb_ref, hb_ref)
        s2(1, za_ref, ha_ref)
        s2(0, zb_ref, hb_ref)

    d = SB_HEAD_DIM
    outs = [_sb_finish(acc_ref[hh], g_ref[0, :, hh * d:(hh + 1) * d]) for hh in range(nh)]
    o_ref[0] = jnp.concatenate(outs, axis=-1).astype(o_ref.dtype)


def _sb_prompt(q, kt, vt, norm_g):
    b, _, t, d = q.shape
    tq = SB_KEY_BLOCK
    nh = SB_HEADS
    kv_spec = pl.BlockSpec((1, nh, d, t), lambda i, h, j: (i, h, 0, 0))
    return pl.pallas_call(
        functools.partial(_sb_prompt_kernel, tq=tq, nh=nh),
        grid=(b, SB_HEADS // nh, t // tq),
        in_specs=[pl.BlockSpec((1, nh, tq, d), lambda i, h, j: (i, h, j, 0)), kv_spec, kv_spec,
                  pl.BlockSpec((SB_TERMS * tq, tq), lambda i, h, j: (0, 0)),
                  pl.BlockSpec((1, 1, nh * d), lambda i, h, j: (h, 0, 0))],
        out_specs=pl.BlockSpec((1, tq, nh * d), lambda i, h, j: (i, j, h)),
        out_shape=jax.ShapeDtypeStruct((b, t, SB_WIDTH), BF16),
        scratch_shapes=[pltpu.VMEM((nh, tq, d), F32), pltpu.VMEM((nh, tq, 1), F32),
                        pltpu.VMEM((nh, tq, tq), F32), pltpu.VMEM((nh, tq, SB_TERMS * tq), BF16),
                        pltpu.VMEM((nh, tq, tq), F32), pltpu.VMEM((nh, tq, SB_TERMS * tq), BF16)],
        compiler_params=_cparams(("parallel", "parallel", "arbitrary"), SB_FLAGS),
        name="sb_prompt",
    )(q, kt, vt, _tri_matrix(), norm_g.reshape(SB_HEADS // nh, 1, nh * d))


def _sb_sample_kernel(q_ref, kt_ref, vt_ref, ktp_ref, vtp_ref, tri_ref, g_ref, o_ref, acc_ref, car_ref,
                      zd_ref, hd_ref, za_ref, ha_ref, zb_ref, hb_ref, *, nh):
    kb = SB_KEY_BLOCK
    n_past = ktp_ref.shape[3] // kb
    t = q_ref.shape[2]
    qs = [q_ref[0, hh] for hh in range(nh)]
    acc_ref[...] = jnp.zeros_like(acc_ref)
    car_ref[...] = jnp.zeros_like(car_ref)

    def s1(blk, z_ref, hl_ref):
        start = pl.multiple_of(blk * kb, kb)
        _sb_stage1(qs, [ktp_ref[0, hh, :, pl.ds(start, kb)] for hh in range(nh)], z_ref, hl_ref, False)

    def s2(blk, z_ref, hl_ref):
        start = pl.multiple_of(blk * kb, kb)
        _sb_stage2(z_ref, hl_ref, [vtp_ref[0, hh, :, pl.ds(start, kb)] for hh in range(nh)],
                   tri_ref[...], acc_ref, car_ref)

    _sb_stage1(qs, [kt_ref[0, hh] for hh in range(nh)], zd_ref, hd_ref, True)
    s1(n_past - 1, za_ref, ha_ref)
    _sb_stage2(zd_ref, hd_ref, [vt_ref[0, hh] for hh in range(nh)],
               jnp.concatenate([tri_ref[:t, :t]] * SB_TERMS, axis=0), acc_ref, car_ref)

    def pair(p, _):
        blk = n_past - 1 - 2 * p
        s1(blk - 1, zb_ref, hb_ref)
        s2(blk, za_ref, ha_ref)
        s1(blk - 2, za_ref, ha_ref)
        s2(blk - 1, zb_ref, hb_ref)
        return 0

    lax.fori_loop(0, (n_past - 1) // 2, pair, 0)
    if (n_past - 1) % 2 == 0:
        s2(0, za_ref, ha_ref)
    else:
        s1(0, zb_ref, hb_ref)
        s2(1, za_ref, ha_ref)
        s2(0, zb_ref, hb_ref)
    d = SB_HEAD_DIM
    outs = [_sb_finish(acc_ref[hh], g_ref[0, :, hh * d:(hh + 1) * d]) for hh in range(nh)]
    o_ref[0] = jnp.concatenate(outs, axis=-1).astype(o_ref.dtype)


def _sb_sample(q, kt, vt, kt_past, vt_past, norm_g):
    b, _, t, d = q.shape
    p = kt_past.shape[3]
    nh = SB_HEADS
    kb = SB_KEY_BLOCK
    new_spec = pl.BlockSpec((1, nh, d, t), lambda i, h: (i, h, 0, 0))
    past_spec = pl.BlockSpec((1, nh, d, p), lambda i, h: (i, h, 0, 0))
    return pl.pallas_call(
        functools.partial(_sb_sample_kernel, nh=nh),
        grid=(b, SB_HEADS // nh),
        in_specs=[pl.BlockSpec((1, nh, t, d), lambda i, h: (i, h, 0, 0)), new_spec, new_spec,
                  past_spec, past_spec,
                  pl.BlockSpec((SB_TERMS * kb, kb), lambda i, h: (0, 0)),
                  pl.BlockSpec((1, 1, nh * d), lambda i, h: (h, 0, 0))],
        out_specs=pl.BlockSpec((1, t, nh * d), lambda i, h: (i, 0, h)),
        out_shape=jax.ShapeDtypeStruct((b, t, SB_WIDTH), BF16),
        scratch_shapes=[pltpu.VMEM((nh, t, d), F32), pltpu.VMEM((nh, t, 1), F32),
                        pltpu.VMEM((nh, t, t), F32), pltpu.VMEM((nh, t, SB_TERMS * t), BF16),
                        pltpu.VMEM((nh, t, kb), F32), pltpu.VMEM((nh, t, SB_TERMS * kb), BF16),
                        pltpu.VMEM((nh, t, kb), F32), pltpu.VMEM((nh, t, SB_TERMS * kb), BF16)],
        compiler_params=_cparams(("parallel", "arbitrary"), SB_FLAGS),
        name="sb_sample",
    )(q, kt, vt, kt_past, vt_past, _tri_matrix(), norm_g.reshape(SB_HEADS // nh, 1, nh * d))


N_PAIRS = 6
N_BUCKETS = N_GROUPS * N_PAIRS
MOE_TM = 256
XW = D_MODEL + LANES
_PAIRS = [(a, b) for a in range(EXPERTS_PER_GROUP) for b in range(a + 1, EXPERTS_PER_GROUP)]
BUCKET_E0 = np.array([g * EXPERTS_PER_GROUP + a for g in range(N_GROUPS) for a, _ in _PAIRS], np.int32)
BUCKET_E1 = np.array([g * EXPERTS_PER_GROUP + b for g in range(N_GROUPS) for _, b in _PAIRS], np.int32)


def _route(logits):
    lane = lax.broadcasted_iota(jnp.int32, logits.shape, 1)
    neg = -jnp.inf
    big = jnp.int32(2 * LANES)
    gl = jnp.where(lane < N_GROUPS, logits, neg)
    gmax = jnp.max(gl, axis=-1, keepdims=True)
    g_idx = jnp.min(jnp.where(gl == gmax, lane, big), axis=-1, keepdims=True)
    g_p = 1.0 / jnp.sum(jnp.exp(gl - gmax), axis=-1, keepdims=True)
    lo = N_GROUPS + g_idx * EXPERTS_PER_GROUP
    el = jnp.where((lane >= lo) & (lane < lo + EXPERTS_PER_GROUP), logits, neg)
    v1 = jnp.max(el, axis=-1, keepdims=True)
    i1 = jnp.min(jnp.where(el == v1, lane, big), axis=-1, keepdims=True)
    el2 = jnp.where(lane == i1, neg, el)
    v2 = jnp.max(el2, axis=-1, keepdims=True)
    i2 = jnp.min(jnp.where(el2 == v2, lane, big), axis=-1, keepdims=True)
    e21 = jnp.exp(v2 - v1)
    p1 = 1.0 / (1.0 + e21)
    p2 = e21 * p1
    first_lo = i1 < i2
    w_lo = jnp.where(first_lo, p1, p2) * g_p
    w_hi = jnp.where(first_lo, p2, p1) * g_p
    a = jnp.minimum(i1, i2) - lo
    b = jnp.maximum(i1, i2) - lo
    pair = jnp.where(a == 0, b - 1, jnp.where(a == 1, b + 1, 5))
    return g_idx * N_PAIRS + pair, w_lo, w_hi


def _out_proj_kernel(x_ref, mr_ref, ms_ref, wo_ref, mm_ref, mf_ref, lg_ref, lb_ref, wr_ref, br_ref,
                     tril_ref, cin_ref, x1_ref, hx_ref, rt_ref, cnt_ref, run_ref):
    @pl.when((pl.program_id(0) == 0) & (pl.program_id(1) == 0))
    def _():
        run_ref[...] = cin_ref[...]

    mix = _dot(mr_ref[0], wo_ref[:MIX_HALF]) + _dot(ms_ref[0], wo_ref[MIX_HALF:])
    gate = mm_ref[0][:, 2 * D_MODEL:]
    x1 = _ln_plain(DEEPNORM_ALPHA * x_ref[0] + gate * mix) * lg_ref[...] + lb_ref[...]
    x1_ref[0] = x1
    mf = mf_ref[0]
    h2 = _ln_plain(x1) * (1.0 + mf[:, D_MODEL:2 * D_MODEL]) + mf[:, :D_MODEL]
    bucket, w_lo, w_hi = _route(_dot(h2.astype(BF16), wr_ref[...]) + br_ref[...])
    tm = h2.shape[0]
    lane = lax.broadcasted_iota(jnp.int32, (tm, LANES), 1)
    hx_ref[0, :, :D_MODEL] = h2
    hx_ref[0, :, D_MODEL:] = jnp.where(lane == 0, w_lo, jnp.where(lane == 1, w_hi, 0.0))
    hit = lane == bucket
    onehot = hit.astype(BF16)
    before = _dot(tril_ref[...], onehot)
    run = run_ref[...]
    rank = jnp.sum(jnp.where(hit, before + run, 0.0), axis=-1, keepdims=True)
    run = run + before[tm - 1:tm] + onehot[tm - 1:tm].astype(F32)
    run_ref[...] = run
    cnt_ref[...] = run
    rt_ref[0] = jnp.where(lane == 0, bucket.astype(F32), jnp.where(lane == 1, rank, 0.0))


MIX_HALF = RET_WIDTH


def _out_proj(x, mr, ms, w_out_bf, mod_mix, mod_ffn, ln_g, ln_b, w_route_bf, b_route, counts_in, tm):
    b, t, _ = x.shape
    tok = lambda w: pl.BlockSpec((1, tm, w), lambda i, j: (i, j, 0))
    modspec = pl.BlockSpec((1, 1, 3 * D_MODEL), lambda i, j: (i, 0, 0))
    row = lambda w: pl.BlockSpec((1, w), lambda i, j: (0, 0))
    idx = np.arange(tm)
    tril = jnp.asarray(idx[:, None] > idx[None, :], dtype=BF16)
    return pl.pallas_call(
        _out_proj_kernel,
        grid=(b, t // tm),
        in_specs=[tok(D_MODEL), tok(RET_WIDTH), tok(SB_WIDTH),
                  pl.BlockSpec((D_MODEL, D_MODEL), lambda i, j: (0, 0)),
                  modspec, modspec, row(D_MODEL), row(D_MODEL),
                  pl.BlockSpec((D_MODEL, LANES), lambda i, j: (0, 0)), row(LANES),
                  pl.BlockSpec((tm, tm), lambda i, j: (0, 0)), row(LANES)],
        out_specs=[tok(D_MODEL), tok(XW), tok(LANES), row(LANES)],
        out_shape=[jax.ShapeDtypeStruct((b, t, D_MODEL), F32),
                   jax.ShapeDtypeStruct((b, t, XW), F32),
                   jax.ShapeDtypeStruct((b, t, LANES), F32),
                   jax.ShapeDtypeStruct((1, LANES), F32)],
        scratch_shapes=[pltpu.VMEM((1, LANES), F32)],
        compiler_params=_cparams(("arbitrary", "arbitrary")),
        name="out_proj",
    )(x, mr, ms, w_out_bf, mod_mix.reshape(b, 1, -1), mod_ffn.reshape(b, 1, -1),
      ln_g.reshape(1, -1), ln_b.reshape(1, -1), w_route_bf, b_route, tril, counts_in)


def _route_plan(routes, counts, n_tiles):
    bucket = jnp.concatenate([r[..., 0].astype(jnp.int32).reshape(-1) for r in routes])
    rank = jnp.concatenate([r[..., 1].astype(jnp.int32).reshape(-1) for r in routes])
    cnt = counts[0, :N_BUCKETS].astype(jnp.int32)
    padded = ((cnt + MOE_TM - 1) // MOE_TM) * MOE_TM
    ends = jnp.cumsum(padded)
    starts = ends - padded
    n_used = ends[-1] // MOE_TM
    tile = jnp.arange(n_tiles, dtype=jnp.int32)
    last = jnp.maximum(n_used - 1, 0)
    tile_idx = jnp.minimum(tile, last)
    tile_bucket = jnp.sum((ends[None, :] <= (tile_idx * MOE_TM)[:, None]).astype(jnp.int32), axis=1)
    tile_bucket = jnp.minimum(tile_bucket, N_BUCKETS - 1)
    e0 = jnp.asarray(BUCKET_E0)[tile_bucket]
    e1 = jnp.asarray(BUCKET_E1)[tile_bucket]
    in_bucket = bucket[:, None] == jnp.arange(N_BUCKETS, dtype=jnp.int32)[None, :]
    dest = rank + jnp.sum(jnp.where(in_bucket, starts[None, :], 0), axis=1)
    trailing = n_used + jnp.arange(N_BUCKETS, dtype=jnp.int32)
    ztiles = jnp.concatenate([jnp.where(padded > 0, ends - MOE_TM, -1),
                              jnp.where(trailing < n_tiles, trailing * MOE_TM, -1)])
    return (dest.astype(jnp.int32), ztiles.astype(jnp.int32), e0, e1, tile_idx,
            n_used.reshape(1).astype(jnp.int32))


def _row_copy(src_ref, s, dst_ref, d, sem):
    return pltpu.make_async_copy(src_ref.at[pl.ds(s, 1)], dst_ref.at[pl.ds(d, 1)], sem)


DMA_UNROLL = 8


SUBLANES = 8


def _scatter_kernel(dhi_ref, dlo_ref, ztile_ref, *refs, rows, steps):
    srcs = refs[:len(steps)]
    out_ref, zbuf, sem, zsem = refs[len(steps):]
    i = pl.program_id(0)

    @pl.when(i == 0)
    def _():
        zbuf[...] = jnp.zeros_like(zbuf)

        def fill(k, _):
            @pl.when(ztile_ref[k] >= 0)
            def _():
                g0 = ztile_ref[k] // SUBLANES
                pltpu.make_async_copy(zbuf, out_ref.at[pl.ds(g0, MOE_TM // SUBLANES)], zsem).start()
            return 0

        def drain(k, _):
            @pl.when(ztile_ref[k] >= 0)
            def _():
                pltpu.make_async_copy(zbuf, out_ref.at[pl.ds(0, MOE_TM // SUBLANES)], zsem).wait()
            return 0

        lax.fori_loop(0, ztile_ref.shape[0], fill, 0)
        lax.fori_loop(0, ztile_ref.shape[0], drain, 0)

    base = i * rows
    first = 0
    for src_ref, n_steps in zip(srcs, steps):
        @pl.when((i >= first) & (i < first + n_steps))
        def _(src_ref=src_ref):
            def issue(g, _):
                for u in range(SUBLANES):
                    tok = base + g * SUBLANES + u
                    pltpu.make_async_copy(src_ref.at[g, pl.ds(u, 1)],
                                          out_ref.at[dhi_ref[tok], pl.ds(dlo_ref[tok], 1)], sem).start()
                return 0

            lax.fori_loop(0, rows // SUBLANES, issue, 0)
            pltpu.make_async_copy(src_ref, out_ref.at[pl.ds(0, rows // SUBLANES)], sem).wait()
        first += n_steps


def _scatter_rows(hxs, dest, ztiles, n_rows):
    w = hxs[0].shape[1]
    rows = min([256] + [h.shape[0] for h in hxs])
    steps = tuple(h.shape[0] // rows for h in hxs)
    firsts = [sum(steps[:k]) for k in range(len(steps))]
    in_specs = [pl.BlockSpec((rows // SUBLANES, SUBLANES, w),
                             lambda i, dh, dl, z, f=f, s=s: (jnp.clip(i - f, 0, s - 1), 0, 0))
                for f, s in zip(firsts, steps)]
    out = pl.pallas_call(
        functools.partial(_scatter_kernel, rows=rows, steps=steps),
        grid_spec=pltpu.PrefetchScalarGridSpec(
            num_scalar_prefetch=3, grid=(sum(steps),),
            in_specs=in_specs,
            out_specs=pl.BlockSpec(memory_space=pl.ANY),
            scratch_shapes=[pltpu.VMEM((MOE_TM // SUBLANES, SUBLANES, w), F32),
                            pltpu.SemaphoreType.DMA(()), pltpu.SemaphoreType.DMA(())]),
        out_shape=jax.ShapeDtypeStruct((n_rows // SUBLANES, SUBLANES, w), F32),
        compiler_params=_cparams(("arbitrary",)),
        name="scatter_rows",
    )(dest // SUBLANES, dest % SUBLANES, ztiles,
      *[h.reshape(h.shape[0] // SUBLANES, SUBLANES, w) for h in hxs])
    return out.reshape(n_rows, w)


def _moe_kernel(e0_ref, e1_ref, ti_ref, nu_ref, x_ref, wg0, wu0, wd0, wg1, wu1, wd1, y_ref):
    del e0_ref, e1_ref, ti_ref

    @pl.when(pl.program_id(0) < nu_ref[0])
    def _():
        x = x_ref[...]
        h = x[:, :D_MODEL].astype(BF16)
        wx = x[:, D_MODEL:]
        lane = lax.broadcasted_iota(jnp.int32, wx.shape, 1)
        w_lo = jnp.sum(jnp.where(lane == 0, wx, 0.0), axis=-1, keepdims=True)
        w_hi = jnp.sum(jnp.where(lane == 1, wx, 0.0), axis=-1, keepdims=True)

        def expert(wg, wu, wd):
            a = _silu(_dot(h, wg[0])) * _dot(h, wu[0])
            return _dot(a.astype(BF16), wd[0])

        y_ref[...] = w_lo * expert(wg0, wu0, wd0) + w_hi * expert(wg1, wu1, wd1)

    @pl.when(pl.program_id(0) >= nu_ref[0])
    def _():
        y_ref[...] = jnp.zeros_like(y_ref)


def _moe(xs, e0, e1, tile_idx, n_used, wg_bf, wu_bf, wd_bf):
    n_rows = xs.shape[0]
    up = lambda sel: pl.BlockSpec((1, D_MODEL, D_EXPERT), lambda t, e0, e1, ti, nu: ((e0, e1)[sel][t], 0, 0))
    down = lambda sel: pl.BlockSpec((1, D_EXPERT, D_MODEL), lambda t, e0, e1, ti, nu: ((e0, e1)[sel][t], 0, 0))
    return pl.pallas_call(
        _moe_kernel,
        grid_spec=pltpu.PrefetchScalarGridSpec(
            num_scalar_prefetch=4, grid=(n_rows // MOE_TM,),
            in_specs=[pl.BlockSpec((MOE_TM, XW), lambda t, e0, e1, ti, nu: (ti[t], 0)),
                      up(0), up(0), down(0), up(1), up(1), down(1)],
            out_specs=pl.BlockSpec((MOE_TM, D_MODEL), lambda t, e0, e1, ti, nu: (t, 0))),
        out_shape=jax.ShapeDtypeStruct((n_rows, D_MODEL), F32),
        compiler_params=_cparams(("arbitrary",)),
        name="moe_routed",
    )(e0, e1, tile_idx, n_used, xs, wg_bf, wu_bf, wd_bf, wg_bf, wu_bf, wd_bf)


def _final_kernel(dest_ref, x1_ref, ys_ref, mf_ref, lg_ref, lb_ref, o_ref, buf, sem, *, tm):
    nj = pl.num_programs(1)
    step = pl.program_id(0) * nj + pl.program_id(1)
    n_steps = pl.num_programs(0) * nj
    slot = step % 2

    def fetch(s, sl):
        def issue(r, _):
            _row_copy(ys_ref, dest_ref[s * tm + r], buf.at[sl], r, sem.at[sl]).start()
            return 0
        lax.fori_loop(0, tm, issue, 0, unroll=DMA_UNROLL)

    @pl.when(step == 0)
    def _():
        fetch(step, slot)

    @pl.when(step + 1 < n_steps)
    def _():
        fetch(step + 1, 1 - slot)

    pltpu.make_async_copy(ys_ref.at[pl.ds(0, tm)], buf.at[slot], sem.at[slot]).wait()
    gate = mf_ref[0][:, 2 * D_MODEL:]
    o_ref[0] = _ln_plain(DEEPNORM_ALPHA * x1_ref[0] + gate * buf[slot]) * lg_ref[...] + lb_ref[...]


def _final(x1, ys, dest, mod_ffn, ln_g, ln_b, tm):
    b, t, _ = x1.shape
    tok = pl.BlockSpec((1, tm, D_MODEL), lambda i, j, *_: (i, j, 0))
    row = pl.BlockSpec((1, D_MODEL), lambda i, j, *_: (0, 0))
    return pl.pallas_call(
        functools.partial(_final_kernel, tm=tm),
        grid_spec=pltpu.PrefetchScalarGridSpec(
            num_scalar_prefetch=1, grid=(b, t // tm),
            in_specs=[tok, pl.BlockSpec(memory_space=pl.ANY),
                      pl.BlockSpec((1, 1, 3 * D_MODEL), lambda i, j, *_: (i, 0, 0)), row, row],
            out_specs=tok,
            scratch_shapes=[pltpu.VMEM((2, tm, D_MODEL), F32), pltpu.SemaphoreType.DMA((2,))]),
        out_shape=jax.ShapeDtypeStruct((b, t, D_MODEL), F32),
        compiler_params=_cparams(("arbitrary", "arbitrary")),
        name="final_ln",
    )(dest, x1, ys, mod_ffn.reshape(b, 1, -1), ln_g.reshape(1, -1), ln_b.reshape(1, -1))


def _mixer_half(x, mod_mix, mod_ffn, pos0, state0, k_past, v_past, counts_in, wts):
    b, t, _ = x.shape
    mr, sq, skt, svt, state = _in_proj(x, mod_mix, wts["w_in_a"], wts["w_in_kvt"], pos0, state0,
                                       wts["ret_norm_g"], min(512, t))
    if k_past is None:
        ms = _sb_prompt(sq, skt, svt, wts["sb_norm_g"])
    else:
        ms = _sb_sample(sq, skt, svt, jnp.swapaxes(k_past, 2, 3), jnp.swapaxes(v_past, 2, 3),
                        wts["sb_norm_g"])
    sk, sv = jnp.swapaxes(skt, 2, 3), jnp.swapaxes(svt, 2, 3)
    x1, hx, route, counts = _out_proj(x, mr, ms, wts["w_out"], mod_mix, mod_ffn, wts["ln_mix_g"],
                                      wts["ln_mix_b"], wts["w_route"], wts["b_route"], counts_in,
                                      min(1024, t))
    return dict(x1=x1, hx=hx.reshape(b * t, XW), route=route, counts=counts, mod_ffn=mod_ffn,
                sk=sk[None], sv=sv[None], state=state[None])


def _ffn_half(groups, wts):
    n = sum(g["hx"].shape[0] for g in groups)
    n_tiles = n // MOE_TM + N_BUCKETS
    dest, ztiles, e0, e1, tile_idx, n_used = _route_plan([g["route"] for g in groups],
                                                         groups[-1]["counts"], n_tiles)
    xs = _scatter_rows([g["hx"] for g in groups], dest, ztiles, n_tiles * MOE_TM)
    ys = _moe(xs, e0, e1, tile_idx, n_used, wts["w_e_gate"], wts["w_e_up"], wts["w_e_down"])
    outs, first = [], 0
    for g in groups:
        t = g["x1"].shape[1]
        n_g = g["hx"].shape[0]
        outs.append(_final(g["x1"], ys, dest[first:first + n_g], g["mod_ffn"], wts["ln_ffn_g"],
                           wts["ln_ffn_b"], min(256, t)))
        first += n_g
    return outs


def kernel(x_prompt, x_sample, cache_sb_k, cache_sb_v, state_ret, c_prompt, c_sample, w_in, w_out, ret_norm_g, sb_norm_g, w_ada_mix, b_ada_mix, ln_mix_g, ln_mix_b, w_ada_ffn, b_ada_ffn, ln_ffn_g, ln_ffn_b, w_group, b_group, w_router, b_router, w_e_gate, w_e_up, w_e_down):
    bp = x_prompt.shape[0]
    c_all = jnp.concatenate([c_prompt, c_sample], axis=0)
    mod_mix = _ada(c_all, w_ada_mix[0], b_ada_mix[0])
    mod_ffn = _ada(c_all, w_ada_ffn[0], b_ada_ffn[0])
    pad = LANES - N_GROUPS - N_EXPERTS
    w_route = jnp.concatenate([w_group[0], w_router[0], jnp.zeros((D_MODEL, pad), F32)], axis=1)
    b_route = jnp.concatenate([b_group[0], b_router[0], jnp.zeros((pad,), F32)]).reshape(1, LANES)
    n_a = 4 * RET_WIDTH + SB_WIDTH
    wts = dict(w_in_a=w_in[0, :, :n_a].astype(BF16), w_in_kvt=w_in[0, :, n_a:].T.astype(BF16),
               w_out=w_out[0].astype(BF16),
               ret_norm_g=ret_norm_g[0], sb_norm_g=sb_norm_g[0],
               ln_mix_g=ln_mix_g[0], ln_mix_b=ln_mix_b[0], ln_ffn_g=ln_ffn_g[0], ln_ffn_b=ln_ffn_b[0],
               w_route=w_route.astype(BF16), b_route=b_route,
               w_e_gate=w_e_gate[0].astype(BF16), w_e_up=w_e_up[0].astype(BF16),
               w_e_down=w_e_down[0].astype(BF16))
    ret_zero = jnp.zeros((bp, RET_HEADS, RET_DK, RET_DV), F32)
    gp = _mixer_half(x_prompt, mod_mix[:bp], mod_ffn[:bp], 0, ret_zero, None, None,
                     jnp.zeros((1, LANES), F32), wts)
    gs = _mixer_half(x_sample, mod_mix[bp:], mod_ffn[bp:], cache_sb_k.shape[3], state_ret[0],
                     cache_sb_k[0], cache_sb_v[0], gp["counts"], wts)
    y_p, y_s = _ffn_half([gp, gs], wts)
    return (y_p, y_s, gp["sk"], gp["sv"], gp["state"], gs["sk"], gs["sv"], gs["state"])
```

```python
import functools
import math

import numpy as np
import jax
import jax.numpy as jnp
from jax import lax
from jax.experimental import pallas as pl
from jax.experimental.pallas import tpu as pltpu

D_MODEL = 1024
RET_HEADS = 4
RET_DK = 128
RET_DV = 128
RET_WIDTH = RET_HEADS * RET_DV
SB_HEADS = 8
SB_HEAD_DIM = 64
SB_WIDTH = SB_HEADS * SB_HEAD_DIM
IN_WIDTH = 2 * RET_HEADS * RET_DK + 2 * RET_WIDTH + 3 * SB_WIDTH
ROPE_BASE = 10000.0
N_GROUPS = 4
EXPERTS_PER_GROUP = 4
N_EXPERTS = N_GROUPS * EXPERTS_PER_GROUP
D_EXPERT = 512
DEPTH = 1
DEEPNORM_ALPHA = (2.0 * DEPTH) ** 0.25
LN_EPS = 1e-5

LANES = 128
VMEM_LIMIT = 48 * 1024 * 1024

F32 = jnp.float32
BF16 = jnp.bfloat16


def _cparams(sem, flags=None):
    return pltpu.CompilerParams(dimension_semantics=sem, vmem_limit_bytes=VMEM_LIMIT, flags=flags)


SB_FLAGS = None


def _dot(a, b):
    return jnp.dot(a, b, preferred_element_type=F32)


def _dot_nt(a, b):
    return lax.dot_general(a, b, (((1,), (1,)), ((), ())), preferred_element_type=F32)


def _dot_tn(a, b):
    return lax.dot_general(a, b, (((0,), (0,)), ((), ())), preferred_element_type=F32)


def _split_dot(a, w_hi, w_lo):
    a_hi = a.astype(BF16)
    a_lo = (a - a_hi.astype(F32)).astype(BF16)
    return _dot(a_hi, w_hi) + (_dot(a_hi, w_lo) + _dot(a_lo, w_hi))


def _ln_plain(x):
    mu = jnp.mean(x, axis=-1, keepdims=True)
    xc = x - mu
    var = jnp.mean(xc * xc, axis=-1, keepdims=True)
    return xc * lax.rsqrt(var + LN_EPS)


def _silu(x):
    return x * (1.0 / (1.0 + jnp.exp(-x)))


def _ada_kernel(c_ref, w_ref, b_ref, o_ref):
    c = c_ref[...]
    w = w_ref[...]
    w_hi = w.astype(BF16)
    w_lo = (w - w_hi.astype(F32)).astype(BF16)
    o_ref[...] = _split_dot(_silu(c), w_hi, w_lo) + b_ref[...]


def _ada(c, w, b):
    r = c.shape[0]
    tn = 768
    return pl.pallas_call(
        _ada_kernel,
        grid=(3 * D_MODEL // tn,),
        in_specs=[pl.BlockSpec((r, D_MODEL), lambda j: (0, 0)),
                  pl.BlockSpec((D_MODEL, tn), lambda j: (0, j)),
                  pl.BlockSpec((1, tn), lambda j: (0, j))],
        out_specs=pl.BlockSpec((r, tn), lambda j: (0, j)),
        out_shape=jax.ShapeDtypeStruct((r, 3 * D_MODEL), F32),
        compiler_params=_cparams(("arbitrary",)),
        name="ada_mod",
    )(c, w, b.reshape(1, -1))


def _in_proj_kernel(x_ref, mod_ref, w_ref, wkv_ref, tab_ref, s0_ref, dec_ref, qd_ref, kd_ref, ng_ref,
                    mr_ref, q_ref, k_ref, v_ref, so_ref, st_ref, *, chunk, chunk_decay):
    j = pl.program_id(1)

    @pl.when(j == 0)
    def _():
        st_ref[...] = s0_ref[0]

    m = mod_ref[0]
    h = _ln_plain(x_ref[0]) * (1.0 + m[:, D_MODEL:2 * D_MODEL]) + m[:, :D_MODEL]
    h = h.astype(BF16)
    tm = h.shape[0]
    tab = tab_ref[...]
    grp = []
    for c in range(4):
        p = _dot(h, w_ref[:, c * RET_WIDTH:(c + 1) * RET_WIDTH])
        if c < 2:
            cs = tab[:, (2 * c) * LANES:(2 * c + 1) * LANES]
            sn = tab[:, (2 * c + 1) * LANES:(2 * c + 2) * LANES]
            heads = []
            for hh in range(RET_HEADS):
                ph = p[:, hh * RET_DK:(hh + 1) * RET_DK]
                heads.append(ph * cs + pltpu.roll(ph, RET_DK // 2, 1) * sn)
            grp.append(heads)
        else:
            grp.append([p[:, hh * RET_DV:(hh + 1) * RET_DV] for hh in range(RET_HEADS)])
    for hh in range(RET_HEADS):
        sl = slice(hh * RET_DV, (hh + 1) * RET_DV)
        for ch in range(tm // chunk):
            rows = slice(ch * chunk, (ch + 1) * chunk)
            q, k, v, g = (grp[c][hh][rows] for c in range(4))
            vb = v.astype(BF16)
            st = st_ref[hh]
            scores = _dot_nt(q.astype(BF16), k.astype(BF16)) * dec_ref[hh]
            o = _dot(scores.astype(BF16), vb) + _dot((q * qd_ref[hh]).astype(BF16), st.astype(BF16))
            st_ref[hh] = st * chunk_decay[hh] + _dot_tn((k * kd_ref[hh]).astype(BF16), vb)
            o = _ln_plain(o) * ng_ref[:, sl] * _silu(g)
            mr_ref[0, rows, sl] = o.astype(mr_ref.dtype)

    @pl.when(j == pl.num_programs(1) - 1)
    def _():
        so_ref[0] = st_ref[...]

    base = 4 * RET_WIDTH
    p = _dot(h, w_ref[:, base:base + SB_WIDTH]) * (LOG2E * SB_HEAD_DIM ** -0.5)
    for hh in range(SB_HEADS):
        q_ref[0, hh] = p[:, hh * SB_HEAD_DIM:(hh + 1) * SB_HEAD_DIM].astype(q_ref.dtype)
    pt = _dot_nt(wkv_ref[...], h)
    for c, ref in enumerate((k_ref, v_ref)):
        for hh in range(SB_HEADS):
            r0 = c * SB_WIDTH + hh * SB_HEAD_DIM
            ref[0, hh] = pt[r0:r0 + SB_HEAD_DIM, :]


def _rope_table(pos0, t):
    half = RET_DK // 2
    inv = ROPE_BASE ** (-np.arange(half, dtype=np.float64) / half)
    ang = (pos0 + np.arange(t, dtype=np.float64))[:, None] * inv[None, :]
    cos, sin = np.cos(ang), np.sin(ang)
    cs = np.concatenate([cos, cos], axis=1)
    sn = np.concatenate([-sin, sin], axis=1)
    ks = RET_DK ** -0.5
    return jnp.asarray(np.concatenate([cs, sn, cs * ks, sn * ks], axis=1), dtype=F32)


def _in_proj(x, mod, w_a_bf, w_kvt_bf, pos0, state0, norm_g, tm):
    b, t, _ = x.shape
    tab = _rope_table(pos0, t)
    wa = w_a_bf.shape[1]
    chunk = min(256, tm)
    lg = np.log1p(-np.exp2(-5.0 - np.arange(RET_HEADS, dtype=np.float64)))
    idx = np.arange(chunk, dtype=np.float64)
    rel = idx[:, None] - idx[None, :]
    dec = np.where(rel >= 0, np.exp(lg[:, None, None] * np.maximum(rel, 0.0)), 0.0)
    qd = np.broadcast_to(np.exp(lg[:, None] * (idx + 1.0))[:, :, None], (RET_HEADS, chunk, RET_DK))
    kd = np.broadcast_to(np.exp(lg[:, None] * (chunk - 1.0 - idx))[:, :, None], (RET_HEADS, chunk, RET_DK))
    chunk_decay = tuple(float(v) for v in np.exp(lg * chunk))
    hs = jax.ShapeDtypeStruct((b, SB_HEADS, SB_HEAD_DIM, t), F32)
    q_spec = pl.BlockSpec((1, SB_HEADS, tm, SB_HEAD_DIM), lambda i, j: (i, 0, j, 0))
    kv_spec = pl.BlockSpec((1, SB_HEADS, SB_HEAD_DIM, tm), lambda i, j: (i, 0, 0, j))
    const3 = lambda shape: pl.BlockSpec(shape, lambda i, j: (0, 0, 0))
    state_spec = pl.BlockSpec((1, RET_HEADS, RET_DK, RET_DV), lambda i, j: (i, 0, 0, 0))
    return pl.pallas_call(
        functools.partial(_in_proj_kernel, chunk=chunk, chunk_decay=chunk_decay),
        grid=(b, t // tm),
        in_specs=[pl.BlockSpec((1, tm, D_MODEL), lambda i, j: (i, j, 0)),
                  pl.BlockSpec((1, 1, 3 * D_MODEL), lambda i, j: (i, 0, 0)),
                  pl.BlockSpec((D_MODEL, wa), lambda i, j: (0, 0)),
                  pl.BlockSpec((2 * SB_WIDTH, D_MODEL), lambda i, j: (0, 0)),
                  pl.BlockSpec((tm, 4 * LANES), lambda i, j: (j, 0)),
                  state_spec, const3((RET_HEADS, chunk, chunk)), const3((RET_HEADS, chunk, RET_DK)),
                  const3((RET_HEADS, chunk, RET_DK)),
                  pl.BlockSpec((1, RET_WIDTH), lambda i, j: (0, 0))],
        out_specs=[pl.BlockSpec((1, tm, RET_WIDTH), lambda i, j: (i, j, 0)),
                   q_spec, kv_spec, kv_spec, state_spec],
        out_shape=[jax.ShapeDtypeStruct((b, t, RET_WIDTH), BF16),
                   jax.ShapeDtypeStruct((b, SB_HEADS, t, SB_HEAD_DIM), BF16), hs, hs,
                   jax.ShapeDtypeStruct((b, RET_HEADS, RET_DK, RET_DV), F32)],
        scratch_shapes=[pltpu.VMEM((RET_HEADS, RET_DK, RET_DV), F32)],
        compiler_params=_cparams(("parallel", "arbitrary")),
        name="in_proj",
    )(x, mod.reshape(b, 1, -1), w_a_bf, w_kvt_bf, tab, state0,
      jnp.asarray(dec, F32), jnp.asarray(qd, F32), jnp.asarray(kd, F32), norm_g.reshape(1, -1))


SB_KEY_BLOCK = 256


LOG2E = 1.4426950408889634


MASKED_LOGIT = -1e30
EXP2_CLAMP = 126.0
SB_TERMS = 1


def _sb_stage1(qs, kts, z_ref, hl_ref, masked):
    tq, kb = qs[0].shape[0], kts[0].shape[1]
    if masked:
        valid = (lax.broadcasted_iota(jnp.int32, (tq, kb), 1)
                 < lax.broadcasted_iota(jnp.int32, (tq, kb), 0))
    for c, (q, kt) in enumerate(zip(qs, kts)):
        z = _dot(q, kt.astype(BF16))
        p = jnp.maximum(z, jnp.log2(1.0 + jnp.exp2(jnp.minimum(z, EXP2_CLAMP))))
        if masked:
            p = jnp.where(valid, p, 0.0)
            z = jnp.where(valid, z, MASKED_LOGIT)
        hi = p.astype(BF16)
        z_ref[c] = z
        hl_ref[c, :, :kb] = hi
        if SB_TERMS == 2:
            hl_ref[c, :, kb:] = (p - hi.astype(F32)).astype(BF16)


def _sb_stage2(z_ref, hl_ref, vts, tri2, acc_ref, car_ref):
    n, tq, kb = z_ref.shape
    r = _dot(hl_ref[...].reshape(n * tq, SB_TERMS * kb), tri2)
    for c in range(n):
        incl = r[c * tq:(c + 1) * tq]
        car = car_ref[c]
        w = jnp.exp2(z_ref[c] + incl + car)
        acc_ref[c] += _dot_nt(w.astype(BF16), vts[c].astype(BF16))
        car_ref[c] = car + incl[:, 0:1]


def _sb_finish(o, g):
    return o * lax.rsqrt(jnp.mean(o * o, axis=-1, keepdims=True) + LN_EPS) * g


def _tri_matrix():
    idx = np.arange(SB_KEY_BLOCK)
    t = -(idx[:, None] >= idx[None, :]).astype(np.float32)
    return jnp.asarray(np.concatenate([t] * SB_TERMS, axis=0), dtype=BF16)


def _sb_prompt_kernel(q_ref, kt_ref, vt_ref, tri_ref, g_ref, o_ref, acc_ref, car_ref,
                      za_ref, ha_ref, zb_ref, hb_ref, *, tq, nh):
    i = pl.program_id(2)
    qs = [q_ref[0, hh] for hh in range(nh)]
    acc_ref[...] = jnp.zeros_like(acc_ref)
    car_ref[...] = jnp.zeros_like(car_ref)

    def s1(blk, z_ref, hl_ref, masked=False):
        start = pl.multiple_of(blk * tq, tq)
        _sb_stage1(qs, [kt_ref[0, hh, :, pl.ds(start, tq)] for hh in range(nh)], z_ref, hl_ref, masked)

    def s2(blk, z_ref, hl_ref):
        start = pl.multiple_of(blk * tq, tq)
        _sb_stage2(z_ref, hl_ref, [vt_ref[0, hh, :, pl.ds(start, tq)] for hh in range(nh)],
                   tri_ref[...], acc_ref, car_ref)

    s1(i, za_ref, ha_ref, masked=True)

    def pair(p, _):
        blk = i - 2 * p
        s1(blk - 1, zb_ref, hb_ref)
        s2(blk, za_ref, ha_ref)
        s1(blk - 2, za_ref, ha_ref)
        s2(blk - 1, zb_ref, hb_ref)
        return 0

    lax.fori_loop(0, i // 2, pair, 0)

    @pl.when(i % 2 == 0)
    def _():
        s2(0, za_ref, ha_ref)

    @pl.when(i % 2 == 1)
    def _():
        s1(0, zb_ref, hb_ref)
        s2(1, za_ref, ha_ref)
        s2(0, zb_ref, hb_ref)

    d = SB_HEAD_DIM
    outs = [_sb_finish(acc_ref[hh], g_ref[0, :, hh * d:(hh + 1) * d]) for hh in range(nh)]
    o_ref[0] = jnp.concatenate(outs, axis=-1).astype(o_ref.dtype)


def _sb_prompt(q, kt, vt, norm_g):
    b, _, t, d = q.shape
    tq = SB_KEY_BLOCK
    nh = SB_HEADS
    kv_spec = pl.BlockSpec((1, nh, d, t), lambda i, h, j: (i, h, 0, 0))
    return pl.pallas_call(
        functools.partial(_sb_prompt_kernel, tq=tq, nh=nh),
        grid=(b, SB_HEADS // nh, t // tq),
        in_specs=[pl.BlockSpec((1, nh, tq, d), lambda i, h, j: (i, h, j, 0)), kv_spec, kv_spec,
                  pl.BlockSpec((SB_TERMS * tq, tq), lambda i, h, j: (0, 0)),
                  pl.BlockSpec((1, 1, nh * d), lambda i, h, j: (h, 0, 0))],
        out_specs=pl.BlockSpec((1, tq, nh * d), lambda i, h, j: (i, j, h)),
        out_shape=jax.ShapeDtypeStruct((b, t, SB_WIDTH), BF16),
        scratch_shapes=[pltpu.VMEM((nh, tq, d), F32), pltpu.VMEM((nh, tq, 1), F32),
                        pltpu.VMEM((nh, tq, tq), F32), pltpu.VMEM((nh, tq, SB_TERMS * tq), BF16),
                        pltpu.VMEM((nh, tq, tq), F32), pltpu.VMEM((nh, tq, SB_TERMS * tq), BF16)],
        compiler_params=_cparams(("parallel", "parallel", "arbitrary"), SB_FLAGS),
        name="sb_prompt",
    )(q, kt, vt, _tri_matrix(), norm_g.reshape(SB_HEADS // nh, 1, nh * d))


def _sb_sample_kernel(q_ref, kt_ref, vt_ref, ktp_ref, vtp_ref, tri_ref, g_ref, o_ref, acc_ref, car_ref,
                      zd_ref, hd_ref, za_ref, ha_ref, zb_ref, hb_ref, *, nh):
    kb = SB_KEY_BLOCK
    n_past = ktp_ref.shape[3] // kb
    t = q_ref.shape[2]
    qs = [q_ref[0, hh] for hh in range(nh)]
    acc_ref[...] = jnp.zeros_like(acc_ref)
    car_ref[...] = jnp.zeros_like(car_ref)

    def s1(blk, z_ref, hl_ref):
        start = pl.multiple_of(blk * kb, kb)
        _sb_stage1(qs, [ktp_ref[0, hh, :, pl.ds(start, kb)] for hh in range(nh)], z_ref, hl_ref, False)

    def s2(blk, z_ref, hl_ref):
        start = pl.multiple_of(blk * kb, kb)
        _sb_stage2(z_ref, hl_ref, [vtp_ref[0, hh, :, pl.ds(start, kb)] for hh in range(nh)],
                   tri_ref[...], acc_ref, car_ref)

    _sb_stage1(qs, [kt_ref[0, hh] for hh in range(nh)], zd_ref, hd_ref, True)
    s1(n_past - 1, za_ref, ha_ref)
    _sb_stage2(zd_ref, hd_ref, [vt_ref[0, hh] for hh in range(nh)],
               jnp.concatenate([tri_ref[:t, :t]] * SB_TERMS, axis=0), acc_ref, car_ref)

    def pair(p, _):
        blk = n_past - 1 - 2 * p
        s1(blk - 1, zb_ref, hb_ref)
        s2(blk, za_ref, ha_ref)
        s1(blk - 2, za_ref, ha_ref)
        s2(blk - 1, zb_ref, hb_ref)
        return 0

    lax.fori_loop(0, (n_past - 1) // 2, pair, 0)
    if (n_past - 1) % 2 == 0:
        s2(0, za_ref, ha_ref)
    else:
        s1(0, zb_ref, hb_ref)
        s2(1, za_ref, ha_ref)
        s2(0, zb_ref, hb_ref)
    d = SB_HEAD_DIM
    outs = [_sb_finish(acc_ref[hh], g_ref[0, :, hh * d:(hh + 1) * d]) for hh in range(nh)]
    o_ref[0] = jnp.concatenate(outs, axis=-1).astype(o_ref.dtype)


def _sb_sample(q, kt, vt, kt_past, vt_past, norm_g):
    b, _, t, d = q.shape
    p = kt_past.shape[3]
    nh = SB_HEADS
    kb = SB_KEY_BLOCK
    new_spec = pl.BlockSpec((1, nh, d, t), lambda i, h: (i, h, 0, 0))
    past_spec = pl.BlockSpec((1, nh, d, p), lambda i, h: (i, h, 0, 0))
    return pl.pallas_call(
        functools.partial(_sb_sample_kernel, nh=nh),
        grid=(b, SB_HEADS // nh),
        in_specs=[pl.BlockSpec((1, nh, t, d), lambda i, h: (i, h, 0, 0)), new_spec, new_spec,
                  past_spec, past_spec,
                  pl.BlockSpec((SB_TERMS * kb, kb), lambda i, h: (0, 0)),
                  pl.BlockSpec((1, 1, nh * d), lambda i, h: (h, 0, 0))],
        out_specs=pl.BlockSpec((1, t, nh * d), lambda i, h: (i, 0, h)),
        out_shape=jax.ShapeDtypeStruct((b, t, SB_WIDTH), BF16),
        scratch_shapes=[pltpu.VMEM((nh, t, d), F32), pltpu.VMEM((nh, t, 1), F32),
                        pltpu.VMEM((nh, t, t), F32), pltpu.VMEM((nh, t, SB_TERMS * t), BF16),
                        pltpu.VMEM((nh, t, kb), F32), pltpu.VMEM((nh, t, SB_TERMS * kb), BF16),
                        pltpu.VMEM((nh, t, kb), F32), pltpu.VMEM((nh, t, SB_TERMS * kb), BF16)],
        compiler_params=_cparams(("parallel", "arbitrary"), SB_FLAGS),
        name="sb_sample",
    )(q, kt, vt, kt_past, vt_past, _tri_matrix(), norm_g.reshape(SB_HEADS // nh, 1, nh * d))


N_PAIRS = 6
N_BUCKETS = N_GROUPS * N_PAIRS
MOE_TM = 256
XW = D_MODEL + LANES
_PAIRS = [(a, b) for a in range(EXPERTS_PER_GROUP) for b in range(a + 1, EXPERTS_PER_GROUP)]
BUCKET_E0 = np.array([g * EXPERTS_PER_GROUP + a for g in range(N_GROUPS) for a, _ in _PAIRS], np.int32)
BUCKET_E1 = np.array([g * EXPERTS_PER_GROUP + b for g in range(N_GROUPS) for _, b in _PAIRS], np.int32)


def _route(logits):
    lane = lax.broadcasted_iota(jnp.int32, logits.shape, 1)
    neg = -jnp.inf
    big = jnp.int32(2 * LANES)
    gl = jnp.where(lane < N_GROUPS, logits, neg)
    gmax = jnp.max(gl, axis=-1, keepdims=True)
    g_idx = jnp.min(jnp.where(gl == gmax, lane, big), axis=-1, keepdims=True)
    g_p = 1.0 / jnp.sum(jnp.exp(gl - gmax), axis=-1, keepdims=True)
    lo = N_GROUPS + g_idx * EXPERTS_PER_GROUP
    el = jnp.where((lane >= lo) & (lane < lo + EXPERTS_PER_GROUP), logits, neg)
    v1 = jnp.max(el, axis=-1, keepdims=True)
    i1 = jnp.min(jnp.where(el == v1, lane, big), axis=-1, keepdims=True)
    el2 = jnp.where(lane == i1, neg, el)
    v2 = jnp.max(el2, axis=-1, keepdims=True)
    i2 = jnp.min(jnp.where(el2 == v2, lane, big), axis=-1, keepdims=True)
    e21 = jnp.exp(v2 - v1)
    p1 = 1.0 / (1.0 + e21)
    p2 = e21 * p1
    first_lo = i1 < i2
    w_lo = jnp.where(first_lo, p1, p2) * g_p
    w_hi = jnp.where(first_lo, p2, p1) * g_p
    a = jnp.minimum(i1, i2) - lo
    b = jnp.maximum(i1, i2) - lo
    pair = jnp.where(a == 0, b - 1, jnp.where(a == 1, b + 1, 5))
    return g_idx * N_PAIRS + pair, w_lo, w_hi


def _out_proj_kernel(x_ref, mr_ref, ms_ref, wo_ref, mm_ref, mf_ref, lg_ref, lb_ref, wr_ref, br_ref,
                     tril_ref, cin_ref, x1_ref, hx_ref, rt_ref, cnt_ref, run_ref):
    @pl.when((pl.program_id(0) == 0) & (pl.program_id(1) == 0))
    def _():
        run_ref[...] = cin_ref[...]

    mix = _dot(mr_ref[0], wo_ref[:MIX_HALF]) + _dot(ms_ref[0], wo_ref[MIX_HALF:])
    gate = mm_ref[0][:, 2 * D_MODEL:]
    x1 = _ln_plain(DEEPNORM_ALPHA * x_ref[0] + gate * mix) * lg_ref[...] + lb_ref[...]
    x1_ref[0] = x1
    mf = mf_ref[0]
    h2 = _ln_plain(x1) * (1.0 + mf[:, D_MODEL:2 * D_MODEL]) + mf[:, :D_MODEL]
    bucket, w_lo, w_hi = _route(_dot(h2.astype(BF16), wr_ref[...]) + br_ref[...])
    tm = h2.shape[0]
    lane = lax.broadcasted_iota(jnp.int32, (tm, LANES), 1)
    hx_ref[0, :, :D_MODEL] = h2
    hx_ref[0, :, D_MODEL:] = jnp.where(lane == 0, w_lo, jnp.where(lane == 1, w_hi, 0.0))
    hit = lane == bucket
    onehot = hit.astype(BF16)
    before = _dot(tril_ref[...], onehot)
    run = run_ref[...]
    rank = jnp.sum(jnp.where(hit, before + run, 0.0), axis=-1, keepdims=True)
    run = run + before[tm - 1:tm] + onehot[tm - 1:tm].astype(F32)
    run_ref[...] = run
    cnt_ref[...] = run
    rt_ref[0] = jnp.where(lane == 0, bucket.astype(F32), jnp.where(lane == 1, rank, 0.0))


MIX_HALF = RET_WIDTH


def _out_proj(x, mr, ms, w_out_bf, mod_mix, mod_ffn, ln_g, ln_b, w_route_bf, b_route, counts_in, tm):
    b, t, _ = x.shape
    tok = lambda w: pl.BlockSpec((1, tm, w), lambda i, j: (i, j, 0))
    modspec = pl.BlockSpec((1, 1, 3 * D_MODEL), lambda i, j: (i, 0, 0))
    row = lambda w: pl.BlockSpec((1, w), lambda i, j: (0, 0))
    idx = np.arange(tm)
    tril = jnp.asarray(idx[:, None] > idx[None, :], dtype=BF16)
    return pl.pallas_call(
        _out_proj_kernel,
        grid=(b, t // tm),
        in_specs=[tok(D_MODEL), tok(RET_WIDTH), tok(SB_WIDTH),
                  pl.BlockSpec((D_MODEL, D_MODEL), lambda i, j: (0, 0)),
                  modspec, modspec, row(D_MODEL), row(D_MODEL),
                  pl.BlockSpec((D_MODEL, LANES), lambda i, j: (0, 0)), row(LANES),
                  pl.BlockSpec((tm, tm), lambda i, j: (0, 0)), row(LANES)],
        out_specs=[tok(D_MODEL), tok(XW), tok(LANES), row(LANES)],
        out_shape=[jax.ShapeDtypeStruct((b, t, D_MODEL), F32),
                   jax.ShapeDtypeStruct((b, t, XW), F32),
                   jax.ShapeDtypeStruct((b, t, LANES), F32),
                   jax.ShapeDtypeStruct((1, LANES), F32)],
        scratch_shapes=[pltpu.VMEM((1, LANES), F32)],
        compiler_params=_cparams(("arbitrary", "arbitrary")),
        name="out_proj",
    )(x, mr, ms, w_out_bf, mod_mix.reshape(b, 1, -1), mod_ffn.reshape(b, 1, -1),
      ln_g.reshape(1, -1), ln_b.reshape(1, -1), w_route_bf, b_route, tril, counts_in)


def _route_plan(routes, counts, n_tiles):
    bucket = jnp.concatenate([r[..., 0].astype(jnp.int32).reshape(-1) for r in routes])
    rank = jnp.concatenate([r[..., 1].astype(jnp.int32).reshape(-1) for r in routes])
    cnt = counts[0, :N_BUCKETS].astype(jnp.int32)
    padded = ((cnt + MOE_TM - 1) // MOE_TM) * MOE_TM
    ends = jnp.cumsum(padded)
    starts = ends - padded
    n_used = ends[-1] // MOE_TM
    tile = jnp.arange(n_tiles, dtype=jnp.int32)
    last = jnp.maximum(n_used - 1, 0)
    tile_idx = jnp.minimum(tile, last)
    tile_bucket = jnp.sum((ends[None, :] <= (tile_idx * MOE_TM)[:, None]).astype(jnp.int32), axis=1)
    tile_bucket = jnp.minimum(tile_bucket, N_BUCKETS - 1)
    e0 = jnp.asarray(BUCKET_E0)[tile_bucket]
    e1 = jnp.asarray(BUCKET_E1)[tile_bucket]
    in_bucket = bucket[:, None] == jnp.arange(N_BUCKETS, dtype=jnp.int32)[None, :]
    dest = rank + jnp.sum(jnp.where(in_bucket, starts[None, :], 0), axis=1)
    trailing = n_used + jnp.arange(N_BUCKETS, dtype=jnp.int32)
    ztiles = jnp.concatenate([jnp.where(padded > 0, ends - MOE_TM, -1),
                              jnp.where(trailing < n_tiles, trailing * MOE_TM, -1)])
    return (dest.astype(jnp.int32), ztiles.astype(jnp.int32), e0, e1, tile_idx,
            n_used.reshape(1).astype(jnp.int32))


def _row_copy(src_ref, s, dst_ref, d, sem):
    return pltpu.make_async_copy(src_ref.at[pl.ds(s, 1)], dst_ref.at[pl.ds(d, 1)], sem)


DMA_UNROLL = 8
FINAL_GROUPS = 8


SUBLANES = 8


def _scatter_kernel(dhi_ref, dlo_ref, ztile_ref, *refs, rows, steps):
    srcs = refs[:len(steps)]
    out_ref, zbuf, sem, zsem = refs[len(steps):]
    i = pl.program_id(0)

    @pl.when(i == 0)
    def _():
        zbuf[...] = jnp.zeros_like(zbuf)

        def fill(k, _):
            @pl.when(ztile_ref[k] >= 0)
            def _():
                g0 = ztile_ref[k] // SUBLANES
                pltpu.make_async_copy(zbuf, out_ref.at[pl.ds(g0, MOE_TM // SUBLANES)], zsem).start()
            return 0

        def drain(k, _):
            @pl.when(ztile_ref[k] >= 0)
            def _():
                pltpu.make_async_copy(zbuf, out_ref.at[pl.ds(0, MOE_TM // SUBLANES)], zsem).wait()
            return 0

        lax.fori_loop(0, ztile_ref.shape[0], fill, 0)
        lax.fori_loop(0, ztile_ref.shape[0], drain, 0)

    base = i * rows
    first = 0
    for src_ref, n_steps in zip(srcs, steps):
        @pl.when((i >= first) & (i < first + n_steps))
        def _(src_ref=src_ref):
            for r in range(rows):
                tok = base + r
                pltpu.make_async_copy(src_ref.at[r // SUBLANES, pl.ds(r % SUBLANES, 1)],
                                      out_ref.at[dhi_ref[tok], pl.ds(dlo_ref[tok], 1)], sem).start()
            pltpu.make_async_copy(src_ref, out_ref.at[pl.ds(0, rows // SUBLANES)], sem).wait()
        first += n_steps


def _scatter_rows(hxs, dest, ztiles, n_rows):
    w = hxs[0].shape[1]
    rows = min([256] + [h.shape[0] for h in hxs])
    steps = tuple(h.shape[0] // rows for h in hxs)
    firsts = [sum(steps[:k]) for k in range(len(steps))]
    in_specs = [pl.BlockSpec((rows // SUBLANES, SUBLANES, w),
                             lambda i, dh, dl, z, f=f, s=s: (jnp.clip(i - f, 0, s - 1), 0, 0))
                for f, s in zip(firsts, steps)]
    out = pl.pallas_call(
        functools.partial(_scatter_kernel, rows=rows, steps=steps),
        grid_spec=pltpu.PrefetchScalarGridSpec(
            num_scalar_prefetch=3, grid=(sum(steps),),
            in_specs=in_specs,
            out_specs=pl.BlockSpec(memory_space=pl.ANY),
            scratch_shapes=[pltpu.VMEM((MOE_TM // SUBLANES, SUBLANES, w), F32),
                            pltpu.SemaphoreType.DMA(()), pltpu.SemaphoreType.DMA(())]),
        out_shape=jax.ShapeDtypeStruct((n_rows // SUBLANES, SUBLANES, w), F32),
        compiler_params=_cparams(("arbitrary",)),
        name="scatter_rows",
    )(dest // SUBLANES, dest % SUBLANES, ztiles,
      *[h.reshape(h.shape[0] // SUBLANES, SUBLANES, w) for h in hxs])
    return out.reshape(n_rows, w)


def _moe_kernel(e0_ref, e1_ref, ti_ref, nu_ref, x_ref, wg0, wu0, wd0, wg1, wu1, wd1, y_ref):
    del e0_ref, e1_ref, ti_ref

    @pl.when(pl.program_id(0) < nu_ref[0])
    def _():
        x = x_ref[...]
        h = x[:, :D_MODEL].astype(BF16)
        wx = x[:, D_MODEL:]
        lane = lax.broadcasted_iota(jnp.int32, wx.shape, 1)
        w_lo = jnp.sum(jnp.where(lane == 0, wx, 0.0), axis=-1, keepdims=True)
        w_hi = jnp.sum(jnp.where(lane == 1, wx, 0.0), axis=-1, keepdims=True)

        def expert(wg, wu, wd):
            a = _silu(_dot(h, wg[0])) * _dot(h, wu[0])
            return _dot(a.astype(BF16), wd[0])

        y_ref[...] = w_lo * expert(wg0, wu0, wd0) + w_hi * expert(wg1, wu1, wd1)

    @pl.when(pl.program_id(0) >= nu_ref[0])
    def _():
        y_ref[...] = jnp.zeros_like(y_ref)


def _moe(xs, e0, e1, tile_idx, n_used, wg_bf, wu_bf, wd_bf):
    n_rows = xs.shape[0]
    up = lambda sel: pl.BlockSpec((1, D_MODEL, D_EXPERT), lambda t, e0, e1, ti, nu: ((e0, e1)[sel][t], 0, 0))
    down = lambda sel: pl.BlockSpec((1, D_EXPERT, D_MODEL), lambda t, e0, e1, ti, nu: ((e0, e1)[sel][t], 0, 0))
    return pl.pallas_call(
        _moe_kernel,
        grid_spec=pltpu.PrefetchScalarGridSpec(
            num_scalar_prefetch=4, grid=(n_rows // MOE_TM,),
            in_specs=[pl.BlockSpec((MOE_TM, XW), lambda t, e0, e1, ti, nu: (ti[t], 0)),
                      up(0), up(0), down(0), up(1), up(1), down(1)],
            out_specs=pl.BlockSpec((MOE_TM, D_MODEL), lambda t, e0, e1, ti, nu: (t, 0))),
        out_shape=jax.ShapeDtypeStruct((n_rows, D_MODEL), F32),
        compiler_params=_cparams(("arbitrary",)),
        name="moe_routed",
    )(e0, e1, tile_idx, n_used, xs, wg_bf, wu_bf, wd_bf, wg_bf, wu_bf, wd_bf)


def _final_kernel(dest_ref, x1_ref, ys_ref, mf_ref, lg_ref, lb_ref, o_ref, buf, sem, *, tm):
    nj = pl.num_programs(1)
    step = pl.program_id(0) * nj + pl.program_id(1)
    n_steps = pl.num_programs(0) * nj
    slot = step % 2

    def slot_wait(sl):
        pltpu.make_async_copy(ys_ref.at[pl.ds(0, tm)], buf.at[sl], sem.at[sl]).wait()

    @pl.when(step == 0)
    def _():
        def issue(r, _):
            _row_copy(ys_ref, dest_ref[r], buf.at[slot], r, sem.at[slot]).start()
            return 0
        lax.fori_loop(0, tm, issue, 0, unroll=DMA_UNROLL)

    slot_wait(slot)
    nxt = jnp.minimum(step + 1, n_steps - 1) * tm
    gate = mf_ref[0][:, 2 * D_MODEL:]
    rg = tm // FINAL_GROUPS
    for c in range(FINAL_GROUPS):
        for r in range(c * rg, (c + 1) * rg):
            _row_copy(ys_ref, dest_ref[nxt + r], buf.at[1 - slot], r, sem.at[1 - slot]).start()
        rows = pl.ds(c * rg, rg)
        y = buf[slot, rows]
        o_ref[0, rows] = (_ln_plain(DEEPNORM_ALPHA * x1_ref[0, rows] + gate * y) * lg_ref[...]
                          + lb_ref[...])

    @pl.when(step == n_steps - 1)
    def _():
        slot_wait(1 - slot)


def _final(x1, ys, dest, mod_ffn, ln_g, ln_b, tm):
    b, t, _ = x1.shape
    tok = pl.BlockSpec((1, tm, D_MODEL), lambda i, j, *_: (i, j, 0))
    row = pl.BlockSpec((1, D_MODEL), lambda i, j, *_: (0, 0))
    return pl.pallas_call(
        functools.partial(_final_kernel, tm=tm),
        grid_spec=pltpu.PrefetchScalarGridSpec(
            num_scalar_prefetch=1, grid=(b, t // tm),
            in_specs=[tok, pl.BlockSpec(memory_space=pl.ANY),
                      pl.BlockSpec((1, 1, 3 * D_MODEL), lambda i, j, *_: (i, 0, 0)), row, row],
            out_specs=tok,
            scratch_shapes=[pltpu.VMEM((2, tm, D_MODEL), F32), pltpu.SemaphoreType.DMA((2,))]),
        out_shape=jax.ShapeDtypeStruct((b, t, D_MODEL), F32),
        compiler_params=_cparams(("arbitrary", "arbitrary")),
        name="final_ln",
    )(dest, x1, ys, mod_ffn.reshape(b, 1, -1), ln_g.reshape(1, -1), ln_b.reshape(1, -1))


def _mixer_half(x, mod_mix, mod_ffn, pos0, state0, k_past, v_past, counts_in, wts):
    b, t, _ = x.shape
    mr, sq, skt, svt, state = _in_proj(x, mod_mix, wts["w_in_a"], wts["w_in_kvt"], pos0, state0,
                                       wts["ret_norm_g"], min(512, t))
    if k_past is None:
        ms = _sb_prompt(sq, skt, svt, wts["sb_norm_g"])
    else:
        ms = _sb_sample(sq, skt, svt, jnp.swapaxes(k_past, 2, 3), jnp.swapaxes(v_past, 2, 3),
                        wts["sb_norm_g"])
    sk, sv = jnp.swapaxes(skt, 2, 3), jnp.swapaxes(svt, 2, 3)
    x1, hx, route, counts = _out_proj(x, mr, ms, wts["w_out"], mod_mix, mod_ffn, wts["ln_mix_g"],
                                      wts["ln_mix_b"], wts["w_route"], wts["b_route"], counts_in,
                                      min(1024, t))
    return dict(x1=x1, hx=hx.reshape(b * t, XW), route=route, counts=counts, mod_ffn=mod_ffn,
                sk=sk[None], sv=sv[None], state=state[None])


def _ffn_half(groups, wts):
    n = sum(g["hx"].shape[0] for g in groups)
    n_tiles = n // MOE_TM + N_BUCKETS
    dest, ztiles, e0, e1, tile_idx, n_used = _route_plan([g["route"] for g in groups],
                                                         groups[-1]["counts"], n_tiles)
    xs = _scatter_rows([g["hx"] for g in groups], dest, ztiles, n_tiles * MOE_TM)
    ys = _moe(xs, e0, e1, tile_idx, n_used, wts["w_e_gate"], wts["w_e_up"], wts["w_e_down"])
    outs, first = [], 0
    for g in groups:
        t = g["x1"].shape[1]
        n_g = g["hx"].shape[0]
        outs.append(_final(g["x1"], ys, dest[first:first + n_g], g["mod_ffn"], wts["ln_ffn_g"],
                           wts["ln_ffn_b"], min(256, t)))
        first += n_g
    return outs


def kernel(x_prompt, x_sample, cache_sb_k, cache_sb_v, state_ret, c_prompt, c_sample, w_in, w_out, ret_norm_g, sb_norm_g, w_ada_mix, b_ada_mix, ln_mix_g, ln_mix_b, w_ada_ffn, b_ada_ffn, ln_ffn_g, ln_ffn_b, w_group, b_group, w_router, b_router, w_e_gate, w_e_up, w_e_down):
    bp = x_prompt.shape[0]
    c_all = jnp.concatenate([c_prompt, c_sample], axis=0)
    mod_mix = _ada(c_all, w_ada_mix[0], b_ada_mix[0])
    mod_ffn = _ada(c_all, w_ada_ffn[0], b_ada_ffn[0])
    pad = LANES - N_GROUPS - N_EXPERTS
    w_route = jnp.concatenate([w_group[0], w_router[0], jnp.zeros((D_MODEL, pad), F32)], axis=1)
    b_route = jnp.concatenate([b_group[0], b_router[0], jnp.zeros((pad,), F32)]).reshape(1, LANES)
    n_a = 4 * RET_WIDTH + SB_WIDTH
    wts = dict(w_in_a=w_in[0, :, :n_a].astype(BF16), w_in_kvt=w_in[0, :, n_a:].T.astype(BF16),
               w_out=w_out[0].astype(BF16),
               ret_norm_g=ret_norm_g[0], sb_norm_g=sb_norm_g[0],
               ln_mix_g=ln_mix_g[0], ln_mix_b=ln_mix_b[0], ln_ffn_g=ln_ffn_g[0], ln_ffn_b=ln_ffn_b[0],
               w_route=w_route.astype(BF16), b_route=b_route,
               w_e_gate=w_e_gate[0].astype(BF16), w_e_up=w_e_up[0].astype(BF16),
               w_e_down=w_e_down[0].astype(BF16))
    ret_zero = jnp.zeros((bp, RET_HEADS, RET_DK, RET_DV), F32)
    gp = _mixer_half(x_prompt, mod_mix[:bp], mod_ffn[:bp], 0, ret_zero, None, None,
                     jnp.zeros((1, LANES), F32), wts)
    gs = _mixer_half(x_sample, mod_mix[bp:], mod_ffn[bp:], cache_sb_k.shape[3], state_ret[0],
                     cache_sb_k[0], cache_sb_v[0], gp["counts"], wts)
    y_p, y_s = _ffn_half([gp, gs], wts)
    return (y_p, y_s, gp["sk"], gp["sv"], gp["state"], gs["sk"], gs["sv"], gs["state"])
```

```python
import functools
import math

import numpy as np
import jax
import jax.numpy as jnp
from jax import lax
from jax.experimental import pallas as pl
from jax.experimental.pallas import tpu as pltpu

D_MODEL = 1024
RET_HEADS = 4
RET_DK = 128
RET_DV = 128
RET_WIDTH = RET_HEADS * RET_DV
SB_HEADS = 8
SB_HEAD_DIM = 64
SB_WIDTH = SB_HEADS * SB_HEAD_DIM
IN_WIDTH = 2 * RET_HEADS * RET_DK + 2 * RET_WIDTH + 3 * SB_WIDTH
ROPE_BASE = 10000.0
N_GROUPS = 4
EXPERTS_PER_GROUP = 4
N_EXPERTS = N_GROUPS * EXPERTS_PER_GROUP
D_EXPERT = 512
DEPTH = 1
DEEPNORM_ALPHA = (2.0 * DEPTH) ** 0.25
LN_EPS = 1e-5

LANES = 128
VMEM_LIMIT = 48 * 1024 * 1024

F32 = jnp.float32
BF16 = jnp.bfloat16


def _cparams(sem, flags=None):
    return pltpu.CompilerParams(dimension_semantics=sem, vmem_limit_bytes=VMEM_LIMIT, flags=flags)


SB_FLAGS = None


def _dot(a, b):
    return jnp.dot(a, b, preferred_element_type=F32)


def _dot_nt(a, b):
    return lax.dot_general(a, b, (((1,), (1,)), ((), ())), preferred_element_type=F32)


def _dot_tn(a, b):
    return lax.dot_general(a, b, (((0,), (0,)), ((), ())), preferred_element_type=F32)


def _split_dot(a, w_hi, w_lo):
    a_hi = a.astype(BF16)
    a_lo = (a - a_hi.astype(F32)).astype(BF16)
    return _dot(a_hi, w_hi) + (_dot(a_hi, w_lo) + _dot(a_lo, w_hi))


def _ln_plain(x):
    mu = jnp.mean(x, axis=-1, keepdims=True)
    xc = x - mu
    var = jnp.mean(xc * xc, axis=-1, keepdims=True)
    return xc * lax.rsqrt(var + LN_EPS)


def _silu(x):
    return x * (1.0 / (1.0 + jnp.exp(-x)))


def _ada_kernel(c_ref, w_ref, b_ref, o_ref):
    c = c_ref[...]
    w = w_ref[...]
    w_hi = w.astype(BF16)
    w_lo = (w - w_hi.astype(F32)).astype(BF16)
    o_ref[...] = _split_dot(_silu(c), w_hi, w_lo) + b_ref[...]


def _ada(c, w, b):
    r = c.shape[0]
    tn = 768
    return pl.pallas_call(
        _ada_kernel,
        grid=(3 * D_MODEL // tn,),
        in_specs=[pl.BlockSpec((r, D_MODEL), lambda j: (0, 0)),
                  pl.BlockSpec((D_MODEL, tn), lambda j: (0, j)),
                  pl.BlockSpec((1, tn), lambda j: (0, j))],
        out_specs=pl.BlockSpec((r, tn), lambda j: (0, j)),
        out_shape=jax.ShapeDtypeStruct((r, 3 * D_MODEL), F32),
        compiler_params=_cparams(("arbitrary",)),
        name="ada_mod",
    )(c, w, b.reshape(1, -1))


def _in_proj_kernel(x_ref, mod_ref, w_ref, wkv_ref, tab_ref, s0_ref, dec_ref, qd_ref, kd_ref, ng_ref,
                    mr_ref, q_ref, k_ref, v_ref, so_ref, st_ref, *, chunk, chunk_decay):
    j = pl.program_id(1)

    @pl.when(j == 0)
    def _():
        st_ref[...] = s0_ref[0]

    m = mod_ref[0]
    h = _ln_plain(x_ref[0]) * (1.0 + m[:, D_MODEL:2 * D_MODEL]) + m[:, :D_MODEL]
    h = h.astype(BF16)
    tm = h.shape[0]
    tab = tab_ref[...]
    grp = []
    for c in range(4):
        p = _dot(h, w_ref[:, c * RET_WIDTH:(c + 1) * RET_WIDTH])
        if c < 2:
            cs = tab[:, (2 * c) * LANES:(2 * c + 1) * LANES]
            sn = tab[:, (2 * c + 1) * LANES:(2 * c + 2) * LANES]
            heads = []
            for hh in range(RET_HEADS):
                ph = p[:, hh * RET_DK:(hh + 1) * RET_DK]
                heads.append(ph * cs + pltpu.roll(ph, RET_DK // 2, 1) * sn)
            grp.append(heads)
        else:
            grp.append([p[:, hh * RET_DV:(hh + 1) * RET_DV] for hh in range(RET_HEADS)])
    for hh in range(RET_HEADS):
        sl = slice(hh * RET_DV, (hh + 1) * RET_DV)
        for ch in range(tm // chunk):
            rows = slice(ch * chunk, (ch + 1) * chunk)
            q, k, v, g = (grp[c][hh][rows] for c in range(4))
            vb = v.astype(BF16)
            st = st_ref[hh]
            scores = _dot_nt(q.astype(BF16), k.astype(BF16)) * dec_ref[hh]
            o = _dot(scores.astype(BF16), vb) + _dot((q * qd_ref[hh]).astype(BF16), st.astype(BF16))
            st_ref[hh] = st * chunk_decay[hh] + _dot_tn((k * kd_ref[hh]).astype(BF16), vb)
            o = _ln_plain(o) * ng_ref[:, sl] * _silu(g)
            mr_ref[0, rows, sl] = o.astype(mr_ref.dtype)

    @pl.when(j == pl.num_programs(1) - 1)
    def _():
        so_ref[0] = st_ref[...]

    base = 4 * RET_WIDTH
    p = _dot(h, w_ref[:, base:base + SB_WIDTH]) * (LOG2E * SB_HEAD_DIM ** -0.5)
    for hh in range(SB_HEADS):
        q_ref[0, hh] = p[:, hh * SB_HEAD_DIM:(hh + 1) * SB_HEAD_DIM].astype(q_ref.dtype)
    pt = _dot_nt(wkv_ref[...], h)
    for c, ref in enumerate((k_ref, v_ref)):
        for hh in range(SB_HEADS):
            r0 = c * SB_WIDTH + hh * SB_HEAD_DIM
            ref[0, hh] = pt[r0:r0 + SB_HEAD_DIM, :]


def _rope_table(pos0, t):
    half = RET_DK // 2
    inv = ROPE_BASE ** (-np.arange(half, dtype=np.float64) / half)
    ang = (pos0 + np.arange(t, dtype=np.float64))[:, None] * inv[None, :]
    cos, sin = np.cos(ang), np.sin(ang)
    cs = np.concatenate([cos, cos], axis=1)
    sn = np.concatenate([-sin, sin], axis=1)
    ks = RET_DK ** -0.5
    return jnp.asarray(np.concatenate([cs, sn, cs * ks, sn * ks], axis=1), dtype=F32)


def _in_proj(x, mod, w_a_bf, w_kvt_bf, pos0, state0, norm_g, tm):
    b, t, _ = x.shape
    tab = _rope_table(pos0, t)
    wa = w_a_bf.shape[1]
    chunk = min(256, tm)
    lg = np.log1p(-np.exp2(-5.0 - np.arange(RET_HEADS, dtype=np.float64)))
    idx = np.arange(chunk, dtype=np.float64)
    rel = idx[:, None] - idx[None, :]
    dec = np.where(rel >= 0, np.exp(lg[:, None, None] * np.maximum(rel, 0.0)), 0.0)
    qd = np.broadcast_to(np.exp(lg[:, None] * (idx + 1.0))[:, :, None], (RET_HEADS, chunk, RET_DK))
    kd = np.broadcast_to(np.exp(lg[:, None] * (chunk - 1.0 - idx))[:, :, None], (RET_HEADS, chunk, RET_DK))
    chunk_decay = tuple(float(v) for v in np.exp(lg * chunk))
    hs = jax.ShapeDtypeStruct((b, SB_HEADS, SB_HEAD_DIM, t), F32)
    q_spec = pl.BlockSpec((1, SB_HEADS, tm, SB_HEAD_DIM), lambda i, j: (i, 0, j, 0))
    kv_spec = pl.BlockSpec((1, SB_HEADS, SB_HEAD_DIM, tm), lambda i, j: (i, 0, 0, j))
    const3 = lambda shape: pl.BlockSpec(shape, lambda i, j: (0, 0, 0))
    state_spec = pl.BlockSpec((1, RET_HEADS, RET_DK, RET_DV), lambda i, j: (i, 0, 0, 0))
    return pl.pallas_call(
        functools.partial(_in_proj_kernel, chunk=chunk, chunk_decay=chunk_decay),
        grid=(b, t // tm),
        in_specs=[pl.BlockSpec((1, tm, D_MODEL), lambda i, j: (i, j, 0)),
                  pl.BlockSpec((1, 1, 3 * D_MODEL), lambda i, j: (i, 0, 0)),
                  pl.BlockSpec((D_MODEL, wa), lambda i, j: (0, 0)),
                  pl.BlockSpec((2 * SB_WIDTH, D_MODEL), lambda i, j: (0, 0)),
                  pl.BlockSpec((tm, 4 * LANES), lambda i, j: (j, 0)),
                  state_spec, const3((RET_HEADS, chunk, chunk)), const3((RET_HEADS, chunk, RET_DK)),
                  const3((RET_HEADS, chunk, RET_DK)),
                  pl.BlockSpec((1, RET_WIDTH), lambda i, j: (0, 0))],
        out_specs=[pl.BlockSpec((1, tm, RET_WIDTH), lambda i, j: (i, j, 0)),
                   q_spec, kv_spec, kv_spec, state_spec],
        out_shape=[jax.ShapeDtypeStruct((b, t, RET_WIDTH), BF16),
                   jax.ShapeDtypeStruct((b, SB_HEADS, t, SB_HEAD_DIM), BF16), hs, hs,
                   jax.ShapeDtypeStruct((b, RET_HEADS, RET_DK, RET_DV), F32)],
        scratch_shapes=[pltpu.VMEM((RET_HEADS, RET_DK, RET_DV), F32)],
        compiler_params=_cparams(("parallel", "arbitrary")),
        name="in_proj",
    )(x, mod.reshape(b, 1, -1), w_a_bf, w_kvt_bf, tab, state0,
      jnp.asarray(dec, F32), jnp.asarray(qd, F32), jnp.asarray(kd, F32), norm_g.reshape(1, -1))


SB_KEY_BLOCK = 256


LOG2E = 1.4426950408889634


MASKED_LOGIT = -1e30
EXP2_CLAMP = 126.0
SB_TERMS = 1


def _sb_stage1(qs, kts, z_ref, hl_ref, masked):
    tq, kb = qs[0].shape[0], kts[0].shape[1]
    if masked:
        valid = (lax.broadcasted_iota(jnp.int32, (tq, kb), 1)
                 < lax.broadcasted_iota(jnp.int32, (tq, kb), 0))
    for c, (q, kt) in enumerate(zip(qs, kts)):
        z = _dot(q, kt.astype(BF16))
        p = jnp.maximum(z, jnp.log2(1.0 + jnp.exp2(jnp.minimum(z, EXP2_CLAMP))))
        if masked:
            p = jnp.where(valid, p, 0.0)
            z = jnp.where(valid, z, MASKED_LOGIT)
        hi = p.astype(BF16)
        z_ref[c] = z
        hl_ref[c, :, :kb] = hi
        if SB_TERMS == 2:
            hl_ref[c, :, kb:] = (p - hi.astype(F32)).astype(BF16)


def _sb_stage2(z_ref, hl_ref, vts, tri2, acc_ref, car_ref):
    n, tq, kb = z_ref.shape
    r = _dot(hl_ref[...].reshape(n * tq, SB_TERMS * kb), tri2)
    for c in range(n):
        incl = r[c * tq:(c + 1) * tq]
        car = car_ref[c]
        w = jnp.exp2(z_ref[c] + incl + car)
        acc_ref[c] += _dot_nt(w.astype(BF16), vts[c].astype(BF16))
        car_ref[c] = car + incl[:, 0:1]


def _sb_finish(o, g):
    return o * lax.rsqrt(jnp.mean(o * o, axis=-1, keepdims=True) + LN_EPS) * g


def _tri_matrix():
    idx = np.arange(SB_KEY_BLOCK)
    t = -(idx[:, None] >= idx[None, :]).astype(np.float32)
    return jnp.asarray(np.concatenate([t] * SB_TERMS, axis=0), dtype=BF16)


def _sb_prompt_kernel(q_ref, kt_ref, vt_ref, tri_ref, g_ref, o_ref, acc_ref, car_ref,
                      za_ref, ha_ref, zb_ref, hb_ref, *, tq, nh):
    i = pl.program_id(2)
    qs = [q_ref[0, hh] for hh in range(nh)]
    acc_ref[...] = jnp.zeros_like(acc_ref)
    car_ref[...] = jnp.zeros_like(car_ref)

    def s1(blk, z_ref, hl_ref, masked=False):
        start = pl.multiple_of(blk * tq, tq)
        _sb_stage1(qs, [kt_ref[0, hh, :, pl.ds(start, tq)] for hh in range(nh)], z_ref, hl_ref, masked)

    def s2(blk, z_ref, hl_ref):
        start = pl.multiple_of(blk * tq, tq)
        _sb_stage2(z_ref, hl_ref, [vt_ref[0, hh, :, pl.ds(start, tq)] for hh in range(nh)],
                   tri_ref[...], acc_ref, car_ref)

    s1(i, za_ref, ha_ref, masked=True)

    def pair(p, _):
        blk = i - 2 * p
        s1(blk - 1, zb_ref, hb_ref)
        s2(blk, za_ref, ha_ref)
        s1(blk - 2, za_ref, ha_ref)
        s2(blk - 1, zb_ref, hb_ref)
        return 0

    lax.fori_loop(0, i // 2, pair, 0)

    @pl.when(i % 2 == 0)
    def _():
        s2(0, za_ref, ha_ref)

    @pl.when(i % 2 == 1)
    def _():
        s1(0, zb_ref, hb_ref)
        s2(1, za_ref, ha_ref)
        s2(0, zb_ref, hb_ref)

    d = SB_HEAD_DIM
    outs = [_sb_finish(acc_ref[hh], g_ref[0, :, hh * d:(hh + 1) * d]) for hh in range(nh)]
    o_ref[0] = jnp.concatenate(outs, axis=-1).astype(o_ref.dtype)


def _sb_prompt(q, kt, vt, norm_g):
    b, _, t, d = q.shape
    tq = SB_KEY_BLOCK
    nh = SB_HEADS
    kv_spec = pl.BlockSpec((1, nh, d, t), lambda i, h, j: (i, h, 0, 0))
    return pl.pallas_call(
        functools.partial(_sb_prompt_kernel, tq=tq, nh=nh),
        grid=(b, SB_HEADS // nh, t // tq),
        in_specs=[pl.BlockSpec((1, nh, tq, d), lambda i, h, j: (i, h, j, 0)), kv_spec, kv_spec,
                  pl.BlockSpec((SB_TERMS * tq, tq), lambda i, h, j: (0, 0)),
                  pl.BlockSpec((1, 1, nh * d), lambda i, h, j: (h, 0, 0))],
        out_specs=pl.BlockSpec((1, tq, nh * d), lambda i, h, j: (i, j, h)),
        out_shape=jax.ShapeDtypeStruct((b, t, SB_WIDTH), BF16),
        scratch_shapes=[pltpu.VMEM((nh, tq, d), F32), pltpu.VMEM((nh, tq, 1), F32),
                        pltpu.VMEM((nh, tq, tq), F32), pltpu.VMEM((nh, tq, SB_TERMS * tq), BF16),
                        pltpu.VMEM((nh, tq, tq), F32), pltpu.VMEM((nh, tq, SB_TERMS * tq), BF16)],
        compiler_params=_cparams(("parallel", "parallel", "arbitrary"), SB_FLAGS),
        name="sb_prompt",
    )(q, kt, vt, _tri_matrix(), norm_g.reshape(SB_HEADS // nh, 1, nh * d))


def _sb_sample_kernel(q_ref, kt_ref, vt_ref, ktp_ref, vtp_ref, tri_ref, g_ref, o_ref, acc_ref, car_ref,
                      zd_ref, hd_ref, za_ref, ha_ref, zb_ref, hb_ref, *, nh):
    kb = SB_KEY_BLOCK
    n_past = ktp_ref.shape[3] // kb
    t = q_ref.shape[2]
    qs = [q_ref[0, hh] for hh in range(nh)]
    acc_ref[...] = jnp.zeros_like(acc_ref)
    car_ref[...] = jnp.zeros_like(car_ref)

    def s1(blk, z_ref, hl_ref):
        start = pl.multiple_of(blk * kb, kb)
        _sb_stage1(qs, [ktp_ref[0, hh, :, pl.ds(start, kb)] for hh in range(nh)], z_ref, hl_ref, False)

    def s2(blk, z_ref, hl_ref):
        start = pl.multiple_of(blk * kb, kb)
        _sb_stage2(z_ref, hl_ref, [vtp_ref[0, hh, :, pl.ds(start, kb)] for hh in range(nh)],
                   tri_ref[...], acc_ref, car_ref)

    _sb_stage1(qs, [kt_ref[0, hh] for hh in range(nh)], zd_ref, hd_ref, True)
    s1(n_past - 1, za_ref, ha_ref)
    _sb_stage2(zd_ref, hd_ref, [vt_ref[0, hh] for hh in range(nh)],
               jnp.concatenate([tri_ref[:t, :t]] * SB_TERMS, axis=0), acc_ref, car_ref)

    def pair(p, _):
        blk = n_past - 1 - 2 * p
        s1(blk - 1, zb_ref, hb_ref)
        s2(blk, za_ref, ha_ref)
        s1(blk - 2, za_ref, ha_ref)
        s2(blk - 1, zb_ref, hb_ref)
        return 0

    lax.fori_loop(0, (n_past - 1) // 2, pair, 0)
    if (n_past - 1) % 2 == 0:
        s2(0, za_ref, ha_ref)
    else:
        s1(0, zb_ref, hb_ref)
        s2(1, za_ref, ha_ref)
        s2(0, zb_ref, hb_ref)
    d = SB_HEAD_DIM
    outs = [_sb_finish(acc_ref[hh], g_ref[0, :, hh * d:(hh + 1) * d]) for hh in range(nh)]
    o_ref[0] = jnp.concatenate(outs, axis=-1).astype(o_ref.dtype)


def _sb_sample(q, kt, vt, kt_past, vt_past, norm_g):
    b, _, t, d = q.shape
    p = kt_past.shape[3]
    nh = SB_HEADS
    kb = SB_KEY_BLOCK
    new_spec = pl.BlockSpec((1, nh, d, t), lambda i, h: (i, h, 0, 0))
    past_spec = pl.BlockSpec((1, nh, d, p), lambda i, h: (i, h, 0, 0))
    return pl.pallas_call(
        functools.partial(_sb_sample_kernel, nh=nh),
        grid=(b, SB_HEADS // nh),
        in_specs=[pl.BlockSpec((1, nh, t, d), lambda i, h: (i, h, 0, 0)), new_spec, new_spec,
                  past_spec, past_spec,
                  pl.BlockSpec((SB_TERMS * kb, kb), lambda i, h: (0, 0)),
                  pl.BlockSpec((1, 1, nh * d), lambda i, h: (h, 0, 0))],
        out_specs=pl.BlockSpec((1, t, nh * d), lambda i, h: (i, 0, h)),
        out_shape=jax.ShapeDtypeStruct((b, t, SB_WIDTH), BF16),
        scratch_shapes=[pltpu.VMEM((nh, t, d), F32), pltpu.VMEM((nh, t, 1), F32),
                        pltpu.VMEM((nh, t, t), F32), pltpu.VMEM((nh, t, SB_TERMS * t), BF16),
                        pltpu.VMEM((nh, t, kb), F32), pltpu.VMEM((nh, t, SB_TERMS * kb), BF16),
                        pltpu.VMEM((nh, t, kb), F32), pltpu.VMEM((nh, t, SB_TERMS * kb), BF16)],
        compiler_params=_cparams(("parallel", "arbitrary"), SB_FLAGS),
        name="sb_sample",
    )(q, kt, vt, kt_past, vt_past, _tri_matrix(), norm_g.reshape(SB_HEADS // nh, 1, nh * d))


N_PAIRS = 6
N_BUCKETS = N_GROUPS * N_PAIRS
MOE_TM = 256
XW = D_MODEL + LANES
_PAIRS = [(a, b) for a in range(EXPERTS_PER_GROUP) for b in range(a + 1, EXPERTS_PER_GROUP)]
BUCKET_E0 = np.array([g * EXPERTS_PER_GROUP + a for g in range(N_GROUPS) for a, _ in _PAIRS], np.int32)
BUCKET_E1 = np.array([g * EXPERTS_PER_GROUP + b for g in range(N_GROUPS) for _, b in _PAIRS], np.int32)


def _route(logits):
    lane = lax.broadcasted_iota(jnp.int32, logits.shape, 1)
    neg = -jnp.inf
    big = jnp.int32(2 * LANES)
    gl = jnp.where(lane < N_GROUPS, logits, neg)
    gmax = jnp.max(gl, axis=-1, keepdims=True)
    g_idx = jnp.min(jnp.where(gl == gmax, lane, big), axis=-1, keepdims=True)
    g_p = 1.0 / jnp.sum(jnp.exp(gl - gmax), axis=-1, keepdims=True)
    lo = N_GROUPS + g_idx * EXPERTS_PER_GROUP
    el = jnp.where((lane >= lo) & (lane < lo + EXPERTS_PER_GROUP), logits, neg)
    v1 = jnp.max(el, axis=-1, keepdims=True)
    i1 = jnp.min(jnp.where(el == v1, lane, big), axis=-1, keepdims=True)
    el2 = jnp.where(lane == i1, neg, el)
    v2 = jnp.max(el2, axis=-1, keepdims=True)
    i2 = jnp.min(jnp.where(el2 == v2, lane, big), axis=-1, keepdims=True)
    e21 = jnp.exp(v2 - v1)
    p1 = 1.0 / (1.0 + e21)
    p2 = e21 * p1
    first_lo = i1 < i2
    w_lo = jnp.where(first_lo, p1, p2) * g_p
    w_hi = jnp.where(first_lo, p2, p1) * g_p
    a = jnp.minimum(i1, i2) - lo
    b = jnp.maximum(i1, i2) - lo
    pair = jnp.where(a == 0, b - 1, jnp.where(a == 1, b + 1, 5))
    return g_idx * N_PAIRS + pair, w_lo, w_hi


def _out_proj_kernel(x_ref, mr_ref, ms_ref, wo_ref, mm_ref, mf_ref, lg_ref, lb_ref, wr_ref, br_ref,
                     tril_ref, cin_ref, x1_ref, hx_ref, rt_ref, cnt_ref, run_ref):
    @pl.when((pl.program_id(0) == 0) & (pl.program_id(1) == 0))
    def _():
        run_ref[...] = cin_ref[...]

    mix = _dot(mr_ref[0], wo_ref[:MIX_HALF]) + _dot(ms_ref[0], wo_ref[MIX_HALF:])
    gate = mm_ref[0][:, 2 * D_MODEL:]
    x1 = _ln_plain(DEEPNORM_ALPHA * x_ref[0] + gate * mix) * lg_ref[...] + lb_ref[...]
    x1_ref[0] = x1
    mf = mf_ref[0]
    h2 = _ln_plain(x1) * (1.0 + mf[:, D_MODEL:2 * D_MODEL]) + mf[:, :D_MODEL]
    bucket, w_lo, w_hi = _route(_dot(h2.astype(BF16), wr_ref[...]) + br_ref[...])
    tm = h2.shape[0]
    lane = lax.broadcasted_iota(jnp.int32, (tm, LANES), 1)
    hx_ref[0, :, :D_MODEL] = h2
    hx_ref[0, :, D_MODEL:] = jnp.where(lane == 0, w_lo, jnp.where(lane == 1, w_hi, 0.0))
    hit = lane == bucket
    onehot = hit.astype(BF16)
    before = _dot(tril_ref[...], onehot)
    run = run_ref[...]
    rank = jnp.sum(jnp.where(hit, before + run, 0.0), axis=-1, keepdims=True)
    run = run + before[tm - 1:tm] + onehot[tm - 1:tm].astype(F32)
    run_ref[...] = run
    cnt_ref[...] = run
    rt_ref[0] = jnp.where(lane == 0, bucket.astype(F32), jnp.where(lane == 1, rank, 0.0))


MIX_HALF = RET_WIDTH


def _out_proj(x, mr, ms, w_out_bf, mod_mix, mod_ffn, ln_g, ln_b, w_route_bf, b_route, counts_in, tm):
    b, t, _ = x.shape
    tok = lambda w: pl.BlockSpec((1, tm, w), lambda i, j: (i, j, 0))
    modspec = pl.BlockSpec((1, 1, 3 * D_MODEL), lambda i, j: (i, 0, 0))
    row = lambda w: pl.BlockSpec((1, w), lambda i, j: (0, 0))
    idx = np.arange(tm)
    tril = jnp.asarray(idx[:, None] > idx[None, :], dtype=BF16)
    return pl.pallas_call(
        _out_proj_kernel,
        grid=(b, t // tm),
        in_specs=[tok(D_MODEL), tok(RET_WIDTH), tok(SB_WIDTH),
                  pl.BlockSpec((D_MODEL, D_MODEL), lambda i, j: (0, 0)),
                  modspec, modspec, row(D_MODEL), row(D_MODEL),
                  pl.BlockSpec((D_MODEL, LANES), lambda i, j: (0, 0)), row(LANES),
                  pl.BlockSpec((tm, tm), lambda i, j: (0, 0)), row(LANES)],
        out_specs=[tok(D_MODEL), tok(XW), tok(LANES), row(LANES)],
        out_shape=[jax.ShapeDtypeStruct((b, t, D_MODEL), F32),
                   jax.ShapeDtypeStruct((b, t, XW), F32),
                   jax.ShapeDtypeStruct((b, t, LANES), F32),
                   jax.ShapeDtypeStruct((1, LANES), F32)],
        scratch_shapes=[pltpu.VMEM((1, LANES), F32)],
        compiler_params=_cparams(("arbitrary", "arbitrary")),
        name="out_proj",
    )(x, mr, ms, w_out_bf, mod_mix.reshape(b, 1, -1), mod_ffn.reshape(b, 1, -1),
      ln_g.reshape(1, -1), ln_b.reshape(1, -1), w_route_bf, b_route, tril, counts_in)


def _route_plan(routes, counts, n_tiles):
    bucket = jnp.concatenate([r[..., 0].astype(jnp.int32).reshape(-1) for r in routes])
    rank = jnp.concatenate([r[..., 1].astype(jnp.int32).reshape(-1) for r in routes])
    cnt = counts[0, :N_BUCKETS].astype(jnp.int32)
    padded = ((cnt + MOE_TM - 1) // MOE_TM) * MOE_TM
    ends = jnp.cumsum(padded)
    starts = ends - padded
    n_used = ends[-1] // MOE_TM
    tile = jnp.arange(n_tiles, dtype=jnp.int32)
    last = jnp.maximum(n_used - 1, 0)
    tile_idx = jnp.minimum(tile, last)
    tile_bucket = jnp.sum((ends[None, :] <= (tile_idx * MOE_TM)[:, None]).astype(jnp.int32), axis=1)
    tile_bucket = jnp.minimum(tile_bucket, N_BUCKETS - 1)
    e0 = jnp.asarray(BUCKET_E0)[tile_bucket]
    e1 = jnp.asarray(BUCKET_E1)[tile_bucket]
    in_bucket = bucket[:, None] == jnp.arange(N_BUCKETS, dtype=jnp.int32)[None, :]
    dest = rank + jnp.sum(jnp.where(in_bucket, starts[None, :], 0), axis=1)
    trailing = n_used + jnp.arange(N_BUCKETS, dtype=jnp.int32)
    ztiles = jnp.concatenate([jnp.where(padded > 0, ends - MOE_TM, -1),
                              jnp.where(trailing < n_tiles, trailing * MOE_TM, -1)])
    return (dest.astype(jnp.int32), ztiles.astype(jnp.int32), e0, e1, tile_idx,
            n_used.reshape(1).astype(jnp.int32))


def _row_copy(src_ref, s, dst_ref, d, sem):
    return pltpu.make_async_copy(src_ref.at[pl.ds(s, 1)], dst_ref.at[pl.ds(d, 1)], sem)


DMA_UNROLL = 8
FINAL_GROUPS = 8


SUBLANES = 8


def _scatter_kernel(dhi_ref, dlo_ref, ztile_ref, *refs, rows, steps):
    srcs = refs[:len(steps)]
    out_ref, zbuf, sem, zsem = refs[len(steps):]
    i = pl.program_id(0)

    @pl.when(i == 0)
    def _():
        zbuf[...] = jnp.zeros_like(zbuf)

        def fill(k, _):
            @pl.when(ztile_ref[k] >= 0)
            def _():
                g0 = ztile_ref[k] // SUBLANES
                pltpu.make_async_copy(zbuf, out_ref.at[pl.ds(g0, MOE_TM // SUBLANES)], zsem).start()
            return 0

        def drain(k, _):
            @pl.when(ztile_ref[k] >= 0)
            def _():
                pltpu.make_async_copy(zbuf, out_ref.at[pl.ds(0, MOE_TM // SUBLANES)], zsem).wait()
            return 0

        lax.fori_loop(0, ztile_ref.shape[0], fill, 0)
        lax.fori_loop(0, ztile_ref.shape[0], drain, 0)

    base = i * rows
    first = 0
    for src_ref, n_steps in zip(srcs, steps):
        @pl.when((i >= first) & (i < first + n_steps))
        def _(src_ref=src_ref):
            for r in range(rows):
                tok = base + r
                pltpu.make_async_copy(src_ref.at[r // SUBLANES, pl.ds(r % SUBLANES, 1)],
                                      out_ref.at[dhi_ref[tok], pl.ds(dlo_ref[tok], 1)], sem).start()
            pltpu.make_async_copy(src_ref, out_ref.at[pl.ds(0, rows // SUBLANES)], sem).wait()
        first += n_steps


def _scatter_rows(hxs, dest, ztiles, n_rows):
    w = hxs[0].shape[1]
    rows = min([256] + [h.shape[0] for h in hxs])
    steps = tuple(h.shape[0] // rows for h in hxs)
    firsts = [sum(steps[:k]) for k in range(len(steps))]
    in_specs = [pl.BlockSpec((rows // SUBLANES, SUBLANES, w),
                             lambda i, dh, dl, z, f=f, s=s: (jnp.clip(i - f, 0, s - 1), 0, 0))
                for f, s in zip(firsts, steps)]
    out = pl.pallas_call(
        functools.partial(_scatter_kernel, rows=rows, steps=steps),
        grid_spec=pltpu.PrefetchScalarGridSpec(
            num_scalar_prefetch=3, grid=(sum(steps),),
            in_specs=in_specs,
            out_specs=pl.BlockSpec(memory_space=pl.ANY),
            scratch_shapes=[pltpu.VMEM((MOE_TM // SUBLANES, SUBLANES, w), F32),
                            pltpu.SemaphoreType.DMA(()), pltpu.SemaphoreType.DMA(())]),
        out_shape=jax.ShapeDtypeStruct((n_rows // SUBLANES, SUBLANES, w), F32),
        compiler_params=_cparams(("arbitrary",)),
        name="scatter_rows",
    )(dest // SUBLANES, dest % SUBLANES, ztiles,
      *[h.reshape(h.shape[0] // SUBLANES, SUBLANES, w) for h in hxs])
    return out.reshape(n_rows, w)


def _moe_kernel(e0_ref, e1_ref, ti_ref, nu_ref, x_ref, wg0, wu0, wd0, wg1, wu1, wd1, y_ref):
    del e0_ref, e1_ref, ti_ref

    @pl.when(pl.program_id(0) < nu_ref[0])
    def _():
        x = x_ref[...]
        h = x[:, :D_MODEL].astype(BF16)
        wx = x[:, D_MODEL:]
        lane = lax.broadcasted_iota(jnp.int32, wx.shape, 1)
        w_lo = jnp.sum(jnp.where(lane == 0, wx, 0.0), axis=-1, keepdims=True)
        w_hi = jnp.sum(jnp.where(lane == 1, wx, 0.0), axis=-1, keepdims=True)

        def expert(wg, wu, wd):
            a = _silu(_dot(h, wg[0])) * _dot(h, wu[0])
            return _dot(a.astype(BF16), wd[0])

        y = w_lo * expert(wg0, wu0, wd0) + w_hi * expert(wg1, wu1, wd1)
        y_ref[...] = y.reshape(MOE_TM, SUBLANES, LANES)

    @pl.when(pl.program_id(0) >= nu_ref[0])
    def _():
        y_ref[...] = jnp.zeros_like(y_ref)


def _moe(xs, e0, e1, tile_idx, n_used, wg_bf, wu_bf, wd_bf):
    n_rows = xs.shape[0]
    up = lambda sel: pl.BlockSpec((1, D_MODEL, D_EXPERT), lambda t, e0, e1, ti, nu: ((e0, e1)[sel][t], 0, 0))
    down = lambda sel: pl.BlockSpec((1, D_EXPERT, D_MODEL), lambda t, e0, e1, ti, nu: ((e0, e1)[sel][t], 0, 0))
    return pl.pallas_call(
        _moe_kernel,
        grid_spec=pltpu.PrefetchScalarGridSpec(
            num_scalar_prefetch=4, grid=(n_rows // MOE_TM,),
            in_specs=[pl.BlockSpec((MOE_TM, XW), lambda t, e0, e1, ti, nu: (ti[t], 0)),
                      up(0), up(0), down(0), up(1), up(1), down(1)],
            out_specs=pl.BlockSpec((MOE_TM, SUBLANES, LANES), lambda t, e0, e1, ti, nu: (t, 0, 0))),
        out_shape=jax.ShapeDtypeStruct((n_rows, SUBLANES, LANES), F32),
        compiler_params=_cparams(("arbitrary",)),
        name="moe_routed",
    )(e0, e1, tile_idx, n_used, xs, wg_bf, wu_bf, wd_bf, wg_bf, wu_bf, wd_bf)


def _final_kernel(dest_ref, x1_ref, ys_ref, mf_ref, lg_ref, lb_ref, o_ref, buf, sem, *, tm):
    nj = pl.num_programs(1)
    step = pl.program_id(0) * nj + pl.program_id(1)
    n_steps = pl.num_programs(0) * nj
    slot = step % 2

    def slot_wait(sl):
        pltpu.make_async_copy(ys_ref.at[pl.ds(0, tm)], buf.at[sl], sem.at[sl]).wait()

    @pl.when(step == 0)
    def _():
        def issue(r, _):
            _row_copy(ys_ref, dest_ref[r], buf.at[slot], r, sem.at[slot]).start()
            return 0
        lax.fori_loop(0, tm, issue, 0, unroll=DMA_UNROLL)

    slot_wait(slot)
    nxt = jnp.minimum(step + 1, n_steps - 1) * tm
    gate = mf_ref[0][:, 2 * D_MODEL:]
    rg = tm // FINAL_GROUPS
    for c in range(FINAL_GROUPS):
        for r in range(c * rg, (c + 1) * rg):
            _row_copy(ys_ref, dest_ref[nxt + r], buf.at[1 - slot], r, sem.at[1 - slot]).start()
        rows = pl.ds(c * rg, rg)
        y = buf[slot, rows].reshape(rg, D_MODEL)
        o_ref[0, rows] = (_ln_plain(DEEPNORM_ALPHA * x1_ref[0, rows] + gate * y) * lg_ref[...]
                          + lb_ref[...])

    @pl.when(step == n_steps - 1)
    def _():
        slot_wait(1 - slot)


def _final(x1, ys, dest, mod_ffn, ln_g, ln_b, tm):
    b, t, _ = x1.shape
    tok = pl.BlockSpec((1, tm, D_MODEL), lambda i, j, *_: (i, j, 0))
    row = pl.BlockSpec((1, D_MODEL), lambda i, j, *_: (0, 0))
    return pl.pallas_call(
        functools.partial(_final_kernel, tm=tm),
        grid_spec=pltpu.PrefetchScalarGridSpec(
            num_scalar_prefetch=1, grid=(b, t // tm),
            in_specs=[tok, pl.BlockSpec(memory_space=pl.ANY),
                      pl.BlockSpec((1, 1, 3 * D_MODEL), lambda i, j, *_: (i, 0, 0)), row, row],
            out_specs=tok,
            scratch_shapes=[pltpu.VMEM((2, tm, SUBLANES, LANES), F32), pltpu.SemaphoreType.DMA((2,))]),
        out_shape=jax.ShapeDtypeStruct((b, t, D_MODEL), F32),
        compiler_params=_cparams(("arbitrary", "arbitrary")),
        name="final_ln",
    )(dest, x1, ys, mod_ffn.reshape(b, 1, -1), ln_g.reshape(1, -1), ln_b.reshape(1, -1))


def _mixer_half(x, mod_mix, mod_ffn, pos0, state0, k_past, v_past, counts_in, wts):
    b, t, _ = x.shape
    mr, sq, skt, svt, state = _in_proj(x, mod_mix, wts["w_in_a"], wts["w_in_kvt"], pos0, state0,
                                       wts["ret_norm_g"], min(512, t))
    if k_past is None:
        ms = _sb_prompt(sq, skt, svt, wts["sb_norm_g"])
    else:
        ms = _sb_sample(sq, skt, svt, jnp.swapaxes(k_past, 2, 3), jnp.swapaxes(v_past, 2, 3),
                        wts["sb_norm_g"])
    sk, sv = jnp.swapaxes(skt, 2, 3), jnp.swapaxes(svt, 2, 3)
    x1, hx, route, counts = _out_proj(x, mr, ms, wts["w_out"], mod_mix, mod_ffn, wts["ln_mix_g"],
                                      wts["ln_mix_b"], wts["w_route"], wts["b_route"], counts_in,
                                      min(1024, t))
    return dict(x1=x1, hx=hx.reshape(b * t, XW), route=route, counts=counts, mod_ffn=mod_ffn,
                sk=sk[None], sv=sv[None], state=state[None])


def _ffn_half(groups, wts):
    n = sum(g["hx"].shape[0] for g in groups)
    n_tiles = n // MOE_TM + N_BUCKETS
    dest, ztiles, e0, e1, tile_idx, n_used = _route_plan([g["route"] for g in groups],
                                                         groups[-1]["counts"], n_tiles)
    xs = _scatter_rows([g["hx"] for g in groups], dest, ztiles, n_tiles * MOE_TM)
    ys = _moe(xs, e0, e1, tile_idx, n_used, wts["w_e_gate"], wts["w_e_up"], wts["w_e_down"])
    outs, first = [], 0
    for g in groups:
        t = g["x1"].shape[1]
        n_g = g["hx"].shape[0]
        outs.append(_final(g["x1"], ys, dest[first:first + n_g], g["mod_ffn"], wts["ln_ffn_g"],
                           wts["ln_ffn_b"], min(256, t)))
        first += n_g
    return outs


def kernel(x_prompt, x_sample, cache_sb_k, cache_sb_v, state_ret, c_prompt, c_sample, w_in, w_out, ret_norm_g, sb_norm_g, w_ada_mix, b_ada_mix, ln_mix_g, ln_mix_b, w_ada_ffn, b_ada_ffn, ln_ffn_g, ln_ffn_b, w_group, b_group, w_router, b_router, w_e_gate, w_e_up, w_e_down):
    bp = x_prompt.shape[0]
    c_all = jnp.concatenate([c_prompt, c_sample], axis=0)
    mod_mix = _ada(c_all, w_ada_mix[0], b_ada_mix[0])
    mod_ffn = _ada(c_all, w_ada_ffn[0], b_ada_ffn[0])
    pad = LANES - N_GROUPS - N_EXPERTS
    w_route = jnp.concatenate([w_group[0], w_router[0], jnp.zeros((D_MODEL, pad), F32)], axis=1)
    b_route = jnp.concatenate([b_group[0], b_router[0], jnp.zeros((pad,), F32)]).reshape(1, LANES)
    n_a = 4 * RET_WIDTH + SB_WIDTH
    wts = dict(w_in_a=w_in[0, :, :n_a].astype(BF16), w_in_kvt=w_in[0, :, n_a:].T.astype(BF16),
               w_out=w_out[0].astype(BF16),
               ret_norm_g=ret_norm_g[0], sb_norm_g=sb_norm_g[0],
               ln_mix_g=ln_mix_g[0], ln_mix_b=ln_mix_b[0], ln_ffn_g=ln_ffn_g[0], ln_ffn_b=ln_ffn_b[0],
               w_route=w_route.astype(BF16), b_route=b_route,
               w_e_gate=w_e_gate[0].astype(BF16), w_e_up=w_e_up[0].astype(BF16),
               w_e_down=w_e_down[0].astype(BF16))
    ret_zero = jnp.zeros((bp, RET_HEADS, RET_DK, RET_DV), F32)
    gp = _mixer_half(x_prompt, mod_mix[:bp], mod_ffn[:bp], 0, ret_zero, None, None,
                     jnp.zeros((1, LANES), F32), wts)
    gs = _mixer_half(x_sample, mod_mix[bp:], mod_ffn[bp:], cache_sb_k.shape[3], state_ret[0],
                     cache_sb_k[0], cache_sb_v[0], gp["counts"], wts)
    y_p, y_s = _ffn_half([gp, gs], wts)
    return (y_p, y_s, gp["sk"], gp["sv"], gp["state"], gs["sk"], gs["sv"], gs["state"])
```

```python
import functools
import math

import numpy as np
import jax
import jax.numpy as jnp
from jax import lax
from jax.experimental import pallas as pl
from jax.experimental.pallas import tpu as pltpu

D_MODEL = 1024
RET_HEADS = 4
RET_DK = 128
RET_DV = 128
RET_WIDTH = RET_HEADS * RET_DV
SB_HEADS = 8
SB_HEAD_DIM = 64
SB_WIDTH = SB_HEADS * SB_HEAD_DIM
IN_WIDTH = 2 * RET_HEADS * RET_DK + 2 * RET_WIDTH + 3 * SB_WIDTH
ROPE_BASE = 10000.0
N_GROUPS = 4
EXPERTS_PER_GROUP = 4
N_EXPERTS = N_GROUPS * EXPERTS_PER_GROUP
D_EXPERT = 512
DEPTH = 1
DEEPNORM_ALPHA = (2.0 * DEPTH) ** 0.25
LN_EPS = 1e-5

LANES = 128
VMEM_LIMIT = 48 * 1024 * 1024

F32 = jnp.float32
BF16 = jnp.bfloat16


def _cparams(sem, flags=None):
    return pltpu.CompilerParams(dimension_semantics=sem, vmem_limit_bytes=VMEM_LIMIT, flags=flags)


SB_FLAGS = None


def _dot(a, b):
    return jnp.dot(a, b, preferred_element_type=F32)


def _dot_nt(a, b):
    return lax.dot_general(a, b, (((1,), (1,)), ((), ())), preferred_element_type=F32)


def _dot_tn(a, b):
    return lax.dot_general(a, b, (((0,), (0,)), ((), ())), preferred_element_type=F32)


def _split_dot(a, w_hi, w_lo):
    a_hi = a.astype(BF16)
    a_lo = (a - a_hi.astype(F32)).astype(BF16)
    return _dot(a_hi, w_hi) + (_dot(a_hi, w_lo) + _dot(a_lo, w_hi))


def _ln_plain(x):
    mu = jnp.mean(x, axis=-1, keepdims=True)
    xc = x - mu
    var = jnp.mean(xc * xc, axis=-1, keepdims=True)
    return xc * lax.rsqrt(var + LN_EPS)


def _silu(x):
    return x * (1.0 / (1.0 + jnp.exp(-x)))


def _ada_kernel(c_ref, w_ref, b_ref, o_ref):
    c = c_ref[...]
    w = w_ref[...]
    w_hi = w.astype(BF16)
    w_lo = (w - w_hi.astype(F32)).astype(BF16)
    o_ref[...] = _split_dot(_silu(c), w_hi, w_lo) + b_ref[...]


def _ada(c, w, b):
    r = c.shape[0]
    tn = 768
    return pl.pallas_call(
        _ada_kernel,
        grid=(3 * D_MODEL // tn,),
        in_specs=[pl.BlockSpec((r, D_MODEL), lambda j: (0, 0)),
                  pl.BlockSpec((D_MODEL, tn), lambda j: (0, j)),
                  pl.BlockSpec((1, tn), lambda j: (0, j))],
        out_specs=pl.BlockSpec((r, tn), lambda j: (0, j)),
        out_shape=jax.ShapeDtypeStruct((r, 3 * D_MODEL), F32),
        compiler_params=_cparams(("arbitrary",)),
        name="ada_mod",
    )(c, w, b.reshape(1, -1))


def _in_proj_kernel(x_ref, mod_ref, w_ref, wkv_ref, tab_ref, s0_ref, dec_ref, qd_ref, kd_ref, ng_ref,
                    mr_ref, q_ref, k_ref, v_ref, so_ref, st_ref, *, chunk, chunk_decay):
    j = pl.program_id(1)

    @pl.when(j == 0)
    def _():
        st_ref[...] = s0_ref[0]

    m = mod_ref[0]
    h = _ln_plain(x_ref[0]) * (1.0 + m[:, D_MODEL:2 * D_MODEL]) + m[:, :D_MODEL]
    h = h.astype(BF16)
    tm = h.shape[0]
    tab = tab_ref[...]
    grp = []
    for c in range(4):
        p = _dot(h, w_ref[:, c * RET_WIDTH:(c + 1) * RET_WIDTH])
        if c < 2:
            cs = tab[:, (2 * c) * LANES:(2 * c + 1) * LANES]
            sn = tab[:, (2 * c + 1) * LANES:(2 * c + 2) * LANES]
            heads = []
            for hh in range(RET_HEADS):
                ph = p[:, hh * RET_DK:(hh + 1) * RET_DK]
                heads.append(ph * cs + pltpu.roll(ph, RET_DK // 2, 1) * sn)
            grp.append(heads)
        else:
            grp.append([p[:, hh * RET_DV:(hh + 1) * RET_DV] for hh in range(RET_HEADS)])
    for hh in range(RET_HEADS):
        sl = slice(hh * RET_DV, (hh + 1) * RET_DV)
        for ch in range(tm // chunk):
            rows = slice(ch * chunk, (ch + 1) * chunk)
            q, k, v, g = (grp[c][hh][rows] for c in range(4))
            vb = v.astype(BF16)
            st = st_ref[hh]
            scores = _dot_nt(q.astype(BF16), k.astype(BF16)) * dec_ref[hh]
            o = _dot(scores.astype(BF16), vb) + _dot((q * qd_ref[hh]).astype(BF16), st.astype(BF16))
            st_ref[hh] = st * chunk_decay[hh] + _dot_tn((k * kd_ref[hh]).astype(BF16), vb)
            o = _ln_plain(o) * ng_ref[:, sl] * _silu(g)
            mr_ref[0, rows, sl] = o.astype(mr_ref.dtype)

    @pl.when(j == pl.num_programs(1) - 1)
    def _():
        so_ref[0] = st_ref[...]

    base = 4 * RET_WIDTH
    p = _dot(h, w_ref[:, base:base + SB_WIDTH]) * (LOG2E * SB_HEAD_DIM ** -0.5)
    for hh in range(SB_HEADS):
        q_ref[0, hh] = p[:, hh * SB_HEAD_DIM:(hh + 1) * SB_HEAD_DIM].astype(q_ref.dtype)
    pt = _dot_nt(wkv_ref[...], h)
    for c, ref in enumerate((k_ref, v_ref)):
        for hh in range(SB_HEADS):
            r0 = c * SB_WIDTH + hh * SB_HEAD_DIM
            ref[0, hh] = pt[r0:r0 + SB_HEAD_DIM, :]


def _rope_table(pos0, t):
    half = RET_DK // 2
    inv = ROPE_BASE ** (-np.arange(half, dtype=np.float64) / half)
    ang = (pos0 + np.arange(t, dtype=np.float64))[:, None] * inv[None, :]
    cos, sin = np.cos(ang), np.sin(ang)
    cs = np.concatenate([cos, cos], axis=1)
    sn = np.concatenate([-sin, sin], axis=1)
    ks = RET_DK ** -0.5
    return jnp.asarray(np.concatenate([cs, sn, cs * ks, sn * ks], axis=1), dtype=F32)


def _in_proj(x, mod, w_a_bf, w_kvt_bf, pos0, state0, norm_g, tm):
    b, t, _ = x.shape
    tab = _rope_table(pos0, t)
    wa = w_a_bf.shape[1]
    chunk = min(256, tm)
    lg = np.log1p(-np.exp2(-5.0 - np.arange(RET_HEADS, dtype=np.float64)))
    idx = np.arange(chunk, dtype=np.float64)
    rel = idx[:, None] - idx[None, :]
    dec = np.where(rel >= 0, np.exp(lg[:, None, None] * np.maximum(rel, 0.0)), 0.0)
    qd = np.broadcast_to(np.exp(lg[:, None] * (idx + 1.0))[:, :, None], (RET_HEADS, chunk, RET_DK))
    kd = np.broadcast_to(np.exp(lg[:, None] * (chunk - 1.0 - idx))[:, :, None], (RET_HEADS, chunk, RET_DK))
    chunk_decay = tuple(float(v) for v in np.exp(lg * chunk))
    hs = jax.ShapeDtypeStruct((b, SB_HEADS, SB_HEAD_DIM, t), F32)
    q_spec = pl.BlockSpec((1, SB_HEADS, tm, SB_HEAD_DIM), lambda i, j: (i, 0, j, 0))
    kv_spec = pl.BlockSpec((1, SB_HEADS, SB_HEAD_DIM, tm), lambda i, j: (i, 0, 0, j))
    const3 = lambda shape: pl.BlockSpec(shape, lambda i, j: (0, 0, 0))
    state_spec = pl.BlockSpec((1, RET_HEADS, RET_DK, RET_DV), lambda i, j: (i, 0, 0, 0))
    return pl.pallas_call(
        functools.partial(_in_proj_kernel, chunk=chunk, chunk_decay=chunk_decay),
        grid=(b, t // tm),
        in_specs=[pl.BlockSpec((1, tm, D_MODEL), lambda i, j: (i, j, 0)),
                  pl.BlockSpec((1, 1, 3 * D_MODEL), lambda i, j: (i, 0, 0)),
                  pl.BlockSpec((D_MODEL, wa), lambda i, j: (0, 0)),
                  pl.BlockSpec((2 * SB_WIDTH, D_MODEL), lambda i, j: (0, 0)),
                  pl.BlockSpec((tm, 4 * LANES), lambda i, j: (j, 0)),
                  state_spec, const3((RET_HEADS, chunk, chunk)), const3((RET_HEADS, chunk, RET_DK)),
                  const3((RET_HEADS, chunk, RET_DK)),
                  pl.BlockSpec((1, RET_WIDTH), lambda i, j: (0, 0))],
        out_specs=[pl.BlockSpec((1, tm, RET_WIDTH), lambda i, j: (i, j, 0)),
                   q_spec, kv_spec, kv_spec, state_spec],
        out_shape=[jax.ShapeDtypeStruct((b, t, RET_WIDTH), BF16),
                   jax.ShapeDtypeStruct((b, SB_HEADS, t, SB_HEAD_DIM), BF16), hs, hs,
                   jax.ShapeDtypeStruct((b, RET_HEADS, RET_DK, RET_DV), F32)],
        scratch_shapes=[pltpu.VMEM((RET_HEADS, RET_DK, RET_DV), F32)],
        compiler_params=_cparams(("parallel", "arbitrary")),
        name="in_proj",
    )(x, mod.reshape(b, 1, -1), w_a_bf, w_kvt_bf, tab, state0,
      jnp.asarray(dec, F32), jnp.asarray(qd, F32), jnp.asarray(kd, F32), norm_g.reshape(1, -1))


SB_KEY_BLOCK = 256


LOG2E = 1.4426950408889634


MASKED_LOGIT = -1e30
EXP2_CLAMP = 126.0
SB_TERMS = 1


def _sb_stage1(qs, kts, z_ref, hl_ref, masked):
    tq, kb = qs[0].shape[0], kts[0].shape[1]
    if masked:
        valid = (lax.broadcasted_iota(jnp.int32, (tq, kb), 1)
                 < lax.broadcasted_iota(jnp.int32, (tq, kb), 0))
    for c, (q, kt) in enumerate(zip(qs, kts)):
        z = _dot(q, kt.astype(BF16))
        p = jnp.maximum(z, jnp.log2(1.0 + jnp.exp2(jnp.minimum(z, EXP2_CLAMP))))
        if masked:
            p = jnp.where(valid, p, 0.0)
            z = jnp.where(valid, z, MASKED_LOGIT)
        hi = p.astype(BF16)
        z_ref[c] = z
        hl_ref[c, :, :kb] = hi
        if SB_TERMS == 2:
            hl_ref[c, :, kb:] = (p - hi.astype(F32)).astype(BF16)


def _sb_stage2(z_ref, hl_ref, vts, tri2, acc_ref, car_ref):
    n, tq, kb = z_ref.shape
    r = _dot(hl_ref[...].reshape(n * tq, SB_TERMS * kb), tri2)
    for c in range(n):
        incl = r[c * tq:(c + 1) * tq]
        car = car_ref[c]
        w = jnp.exp2(z_ref[c] + incl + car)
        acc_ref[c] += _dot_nt(w.astype(BF16), vts[c].astype(BF16))
        car_ref[c] = car + incl[:, 0:1]


def _sb_finish(o, g):
    return o * lax.rsqrt(jnp.mean(o * o, axis=-1, keepdims=True) + LN_EPS) * g


def _tri_matrix():
    idx = np.arange(SB_KEY_BLOCK)
    t = -(idx[:, None] >= idx[None, :]).astype(np.float32)
    return jnp.asarray(np.concatenate([t] * SB_TERMS, axis=0), dtype=BF16)


def _sb_prompt_kernel(q_ref, kt_ref, vt_ref, tri_ref, g_ref, o_ref, acc_ref, car_ref,
                      za_ref, ha_ref, zb_ref, hb_ref, *, tq, nh):
    i = pl.program_id(2)
    qs = [q_ref[0, hh] for hh in range(nh)]
    acc_ref[...] = jnp.zeros_like(acc_ref)
    car_ref[...] = jnp.zeros_like(car_ref)

    def s1(blk, z_ref, hl_ref, masked=False):
        start = pl.multiple_of(blk * tq, tq)
        _sb_stage1(qs, [kt_ref[0, hh, :, pl.ds(start, tq)] for hh in range(nh)], z_ref, hl_ref, masked)

    def s2(blk, z_ref, hl_ref):
        start = pl.multiple_of(blk * tq, tq)
        _sb_stage2(z_ref, hl_ref, [vt_ref[0, hh, :, pl.ds(start, tq)] for hh in range(nh)],
                   tri_ref[...], acc_ref, car_ref)

    s1(i, za_ref, ha_ref, masked=True)

    def pair(p, _):
        blk = i - 2 * p
        s1(blk - 1, zb_ref, hb_ref)
        s2(blk, za_ref, ha_ref)
        s1(blk - 2, za_ref, ha_ref)
        s2(blk - 1, zb_ref, hb_ref)
        return 0

    lax.fori_loop(0, i // 2, pair, 0)

    @pl.when(i % 2 == 0)
    def _():
        s2(0, za_ref, ha_ref)

    @pl.when(i % 2 == 1)
    def _():
        s1(0, zb_ref, hb_ref)
        s2(1, za_ref, ha_ref)
        s2(0, zb_ref, hb_ref)

    d = SB_HEAD_DIM
    outs = [_sb_finish(acc_ref[hh], g_ref[0, :, hh * d:(hh + 1) * d]) for hh in range(nh)]
    o_ref[0] = jnp.concatenate(outs, axis=-1).astype(o_ref.dtype)


def _sb_prompt(q, kt, vt, norm_g):
    b, _, t, d = q.shape
    tq = SB_KEY_BLOCK
    nh = SB_HEADS
    kv_spec = pl.BlockSpec((1, nh, d, t), lambda i, h, j: (i, h, 0, 0))
    return pl.pallas_call(
        functools.partial(_sb_prompt_kernel, tq=tq, nh=nh),
        grid=(b, SB_HEADS // nh, t // tq),
        in_specs=[pl.BlockSpec((1, nh, tq, d), lambda i, h, j: (i, h, j, 0)), kv_spec, kv_spec,
                  pl.BlockSpec((SB_TERMS * tq, tq), lambda i, h, j: (0, 0)),
                  pl.BlockSpec((1, 1, nh * d), lambda i, h, j: (h, 0, 0))],
        out_specs=pl.BlockSpec((1, tq, nh * d), lambda i, h, j: (i, j, h)),
        out_shape=jax.ShapeDtypeStruct((b, t, SB_WIDTH), BF16),
        scratch_shapes=[pltpu.VMEM((nh, tq, d), F32), pltpu.VMEM((nh, tq, 1), F32),
                        pltpu.VMEM((nh, tq, tq), F32), pltpu.VMEM((nh, tq, SB_TERMS * tq), BF16),
                        pltpu.VMEM((nh, tq, tq), F32), pltpu.VMEM((nh, tq, SB_TERMS * tq), BF16)],
        compiler_params=_cparams(("parallel", "parallel", "arbitrary"), SB_FLAGS),
        name="sb_prompt",
    )(q, kt, vt, _tri_matrix(), norm_g.reshape(SB_HEADS // nh, 1, nh * d))


def _sb_sample_kernel(q_ref, kt_ref, vt_ref, ktp_ref, vtp_ref, tri_ref, g_ref, o_ref, acc_ref, car_ref,
                      zd_ref, hd_ref, za_ref, ha_ref, zb_ref, hb_ref, *, nh):
    kb = SB_KEY_BLOCK
    n_past = ktp_ref.shape[3] // kb
    t = q_ref.shape[2]
    qs = [q_ref[0, hh] for hh in range(nh)]
    acc_ref[...] = jnp.zeros_like(acc_ref)
    car_ref[...] = jnp.zeros_like(car_ref)

    def s1(blk, z_ref, hl_ref):
        start = pl.multiple_of(blk * kb, kb)
        _sb_stage1(qs, [ktp_ref[0, hh, :, pl.ds(start, kb)] for hh in range(nh)], z_ref, hl_ref, False)

    def s2(blk, z_ref, hl_ref):
        start = pl.multiple_of(blk * kb, kb)
        _sb_stage2(z_ref, hl_ref, [vtp_ref[0, hh, :, pl.ds(start, kb)] for hh in range(nh)],
                   tri_ref[...], acc_ref, car_ref)

    _sb_stage1(qs, [kt_ref[0, hh] for hh in range(nh)], zd_ref, hd_ref, True)
    s1(n_past - 1, za_ref, ha_ref)
    _sb_stage2(zd_ref, hd_ref, [vt_ref[0, hh] for hh in range(nh)],
               jnp.concatenate([tri_ref[:t, :t]] * SB_TERMS, axis=0), acc_ref, car_ref)

    def pair(p, _):
        blk = n_past - 1 - 2 * p
        s1(blk - 1, zb_ref, hb_ref)
        s2(blk, za_ref, ha_ref)
        s1(blk - 2, za_ref, ha_ref)
        s2(blk - 1, zb_ref, hb_ref)
        return 0

    lax.fori_loop(0, (n_past - 1) // 2, pair, 0)
    if (n_past - 1) % 2 == 0:
        s2(0, za_ref, ha_ref)
    else:
        s1(0, zb_ref, hb_ref)
        s2(1, za_ref, ha_ref)
        s2(0, zb_ref, hb_ref)
    d = SB_HEAD_DIM
    outs = [_sb_finish(acc_ref[hh], g_ref[0, :, hh * d:(hh + 1) * d]) for hh in range(nh)]
    o_ref[0] = jnp.concatenate(outs, axis=-1).astype(o_ref.dtype)


def _sb_sample(q, kt, vt, kt_past, vt_past, norm_g):
    b, _, t, d = q.shape
    p = kt_past.shape[3]
    nh = SB_HEADS
    kb = SB_KEY_BLOCK
    new_spec = pl.BlockSpec((1, nh, d, t), lambda i, h: (i, h, 0, 0))
    past_spec = pl.BlockSpec((1, nh, d, p), lambda i, h: (i, h, 0, 0))
    return pl.pallas_call(
        functools.partial(_sb_sample_kernel, nh=nh),
        grid=(b, SB_HEADS // nh),
        in_specs=[pl.BlockSpec((1, nh, t, d), lambda i, h: (i, h, 0, 0)), new_spec, new_spec,
                  past_spec, past_spec,
                  pl.BlockSpec((SB_TERMS * kb, kb), lambda i, h: (0, 0)),
                  pl.BlockSpec((1, 1, nh * d), lambda i, h: (h, 0, 0))],
        out_specs=pl.BlockSpec((1, t, nh * d), lambda i, h: (i, 0, h)),
        out_shape=jax.ShapeDtypeStruct((b, t, SB_WIDTH), BF16),
        scratch_shapes=[pltpu.VMEM((nh, t, d), F32), pltpu.VMEM((nh, t, 1), F32),
                        pltpu.VMEM((nh, t, t), F32), pltpu.VMEM((nh, t, SB_TERMS * t), BF16),
                        pltpu.VMEM((nh, t, kb), F32), pltpu.VMEM((nh, t, SB_TERMS * kb), BF16),
                        pltpu.VMEM((nh, t, kb), F32), pltpu.VMEM((nh, t, SB_TERMS * kb), BF16)],
        compiler_params=_cparams(("parallel", "arbitrary"), SB_FLAGS),
        name="sb_sample",
    )(q, kt, vt, kt_past, vt_past, _tri_matrix(), norm_g.reshape(SB_HEADS // nh, 1, nh * d))


N_PAIRS = 6
N_BUCKETS = N_GROUPS * N_PAIRS
MOE_TM = 256
XW = D_MODEL + LANES
_PAIRS = [(a, b) for a in range(EXPERTS_PER_GROUP) for b in range(a + 1, EXPERTS_PER_GROUP)]
BUCKET_E0 = np.array([g * EXPERTS_PER_GROUP + a for g in range(N_GROUPS) for a, _ in _PAIRS], np.int32)
BUCKET_E1 = np.array([g * EXPERTS_PER_GROUP + b for g in range(N_GROUPS) for _, b in _PAIRS], np.int32)


def _route(logits):
    lane = lax.broadcasted_iota(jnp.int32, logits.shape, 1)
    neg = -jnp.inf
    big = jnp.int32(2 * LANES)
    gl = jnp.where(lane < N_GROUPS, logits, neg)
    gmax = jnp.max(gl, axis=-1, keepdims=True)
    g_idx = jnp.min(jnp.where(gl == gmax, lane, big), axis=-1, keepdims=True)
    g_p = 1.0 / jnp.sum(jnp.exp(gl - gmax), axis=-1, keepdims=True)
    lo = N_GROUPS + g_idx * EXPERTS_PER_GROUP
    el = jnp.where((lane >= lo) & (lane < lo + EXPERTS_PER_GROUP), logits, neg)
    v1 = jnp.max(el, axis=-1, keepdims=True)
    i1 = jnp.min(jnp.where(el == v1, lane, big), axis=-1, keepdims=True)
    el2 = jnp.where(lane == i1, neg, el)
    v2 = jnp.max(el2, axis=-1, keepdims=True)
    i2 = jnp.min(jnp.where(el2 == v2, lane, big), axis=-1, keepdims=True)
    e21 = jnp.exp(v2 - v1)
    p1 = 1.0 / (1.0 + e21)
    p2 = e21 * p1
    first_lo = i1 < i2
    w_lo = jnp.where(first_lo, p1, p2) * g_p
    w_hi = jnp.where(first_lo, p2, p1) * g_p
    a = jnp.minimum(i1, i2) - lo
    b = jnp.maximum(i1, i2) - lo
    pair = jnp.where(a == 0, b - 1, jnp.where(a == 1, b + 1, 5))
    return g_idx * N_PAIRS + pair, w_lo, w_hi


def _out_proj_kernel(x_ref, mr_ref, ms_ref, wo_ref, mm_ref, mf_ref, lg_ref, lb_ref, wr_ref, br_ref,
                     tril_ref, cin_ref, x1_ref, hx_ref, rt_ref, cnt_ref, run_ref):
    @pl.when((pl.program_id(0) == 0) & (pl.program_id(1) == 0))
    def _():
        run_ref[...] = cin_ref[...]

    mix = _dot(mr_ref[0], wo_ref[:MIX_HALF]) + _dot(ms_ref[0], wo_ref[MIX_HALF:])
    gate = mm_ref[0][:, 2 * D_MODEL:]
    x1 = _ln_plain(DEEPNORM_ALPHA * x_ref[0] + gate * mix) * lg_ref[...] + lb_ref[...]
    x1_ref[0] = x1
    mf = mf_ref[0]
    h2 = _ln_plain(x1) * (1.0 + mf[:, D_MODEL:2 * D_MODEL]) + mf[:, :D_MODEL]
    bucket, w_lo, w_hi = _route(_dot(h2.astype(BF16), wr_ref[...]) + br_ref[...])
    tm = h2.shape[0]
    lane = lax.broadcasted_iota(jnp.int32, (tm, LANES), 1)
    hx_ref[0, :, :D_MODEL] = h2
    hx_ref[0, :, D_MODEL:] = jnp.where(lane == 0, w_lo, jnp.where(lane == 1, w_hi, 0.0))
    hit = lane == bucket
    onehot = hit.astype(BF16)
    before = _dot(tril_ref[...], onehot)
    run = run_ref[...]
    rank = jnp.sum(jnp.where(hit, before + run, 0.0), axis=-1, keepdims=True)
    run = run + before[tm - 1:tm] + onehot[tm - 1:tm].astype(F32)
    run_ref[...] = run
    cnt_ref[...] = run
    rt_ref[0] = jnp.where(lane == 0, bucket.astype(F32), jnp.where(lane == 1, rank, 0.0))


MIX_HALF = RET_WIDTH


def _out_proj(x, mr, ms, w_out_bf, mod_mix, mod_ffn, ln_g, ln_b, w_route_bf, b_route, counts_in, tm):
    b, t, _ = x.shape
    tok = lambda w: pl.BlockSpec((1, tm, w), lambda i, j: (i, j, 0))
    modspec = pl.BlockSpec((1, 1, 3 * D_MODEL), lambda i, j: (i, 0, 0))
    row = lambda w: pl.BlockSpec((1, w), lambda i, j: (0, 0))
    idx = np.arange(tm)
    tril = jnp.asarray(idx[:, None] > idx[None, :], dtype=BF16)
    return pl.pallas_call(
        _out_proj_kernel,
        grid=(b, t // tm),
        in_specs=[tok(D_MODEL), tok(RET_WIDTH), tok(SB_WIDTH),
                  pl.BlockSpec((D_MODEL, D_MODEL), lambda i, j: (0, 0)),
                  modspec, modspec, row(D_MODEL), row(D_MODEL),
                  pl.BlockSpec((D_MODEL, LANES), lambda i, j: (0, 0)), row(LANES),
                  pl.BlockSpec((tm, tm), lambda i, j: (0, 0)), row(LANES)],
        out_specs=[tok(D_MODEL), tok(XW), tok(LANES), row(LANES)],
        out_shape=[jax.ShapeDtypeStruct((b, t, D_MODEL), F32),
                   jax.ShapeDtypeStruct((b, t, XW), F32),
                   jax.ShapeDtypeStruct((b, t, LANES), F32),
                   jax.ShapeDtypeStruct((1, LANES), F32)],
        scratch_shapes=[pltpu.VMEM((1, LANES), F32)],
        compiler_params=_cparams(("arbitrary", "arbitrary")),
        name="out_proj",
    )(x, mr, ms, w_out_bf, mod_mix.reshape(b, 1, -1), mod_ffn.reshape(b, 1, -1),
      ln_g.reshape(1, -1), ln_b.reshape(1, -1), w_route_bf, b_route, tril, counts_in)


def _route_plan(routes, counts, n_tiles):
    bucket = jnp.concatenate([r[..., 0].astype(jnp.int32).reshape(-1) for r in routes])
    rank = jnp.concatenate([r[..., 1].astype(jnp.int32).reshape(-1) for r in routes])
    cnt = counts[0, :N_BUCKETS].astype(jnp.int32)
    padded = ((cnt + MOE_TM - 1) // MOE_TM) * MOE_TM
    ends = jnp.cumsum(padded)
    starts = ends - padded
    n_used = ends[-1] // MOE_TM
    tile = jnp.arange(n_tiles, dtype=jnp.int32)
    last = jnp.maximum(n_used - 1, 0)
    tile_idx = jnp.minimum(tile, last)
    tile_bucket = jnp.sum((ends[None, :] <= (tile_idx * MOE_TM)[:, None]).astype(jnp.int32), axis=1)
    tile_bucket = jnp.minimum(tile_bucket, N_BUCKETS - 1)
    e0 = jnp.asarray(BUCKET_E0)[tile_bucket]
    e1 = jnp.asarray(BUCKET_E1)[tile_bucket]
    in_bucket = bucket[:, None] == jnp.arange(N_BUCKETS, dtype=jnp.int32)[None, :]
    dest = rank + jnp.sum(jnp.where(in_bucket, starts[None, :], 0), axis=1)
    trailing = n_used + jnp.arange(N_BUCKETS, dtype=jnp.int32)
    ztiles = jnp.concatenate([jnp.where(padded > 0, ends - MOE_TM, -1),
                              jnp.where(trailing < n_tiles, trailing * MOE_TM, -1)])
    return (dest.astype(jnp.int32), ztiles.astype(jnp.int32), e0, e1, tile_idx,
            n_used.reshape(1).astype(jnp.int32))


def _row_copy(src_ref, s, dst_ref, d, sem):
    return pltpu.make_async_copy(src_ref.at[pl.ds(s, 1)], dst_ref.at[pl.ds(d, 1)], sem)


DMA_UNROLL = 8
DMA_THREADS = 2
FINAL_GROUPS = 8


SUBLANES = 8


def _scatter_kernel(dhi_ref, dlo_ref, ztile_ref, *refs, rows, steps):
    srcs = refs[:len(steps)]
    out_ref, zbuf, sem, zsem = refs[len(steps):]
    i = pl.program_id(0)

    @pl.when(i == 0)
    def _():
        zbuf[...] = jnp.zeros_like(zbuf)

        def fill(k, _):
            @pl.when(ztile_ref[k] >= 0)
            def _():
                g0 = ztile_ref[k] // SUBLANES
                pltpu.make_async_copy(zbuf, out_ref.at[pl.ds(g0, MOE_TM // SUBLANES)], zsem).start()
            return 0

        def drain(k, _):
            @pl.when(ztile_ref[k] >= 0)
            def _():
                pltpu.make_async_copy(zbuf, out_ref.at[pl.ds(0, MOE_TM // SUBLANES)], zsem).wait()
            return 0

        lax.fori_loop(0, ztile_ref.shape[0], fill, 0)
        lax.fori_loop(0, ztile_ref.shape[0], drain, 0)

    base = i * rows
    first = 0
    for src_ref, n_steps in zip(srcs, steps):
        @pl.when((i >= first) & (i < first + n_steps))
        def _(src_ref=src_ref):
            for r in range(rows):
                tok = base + r
                pltpu.make_async_copy(src_ref.at[r // SUBLANES, pl.ds(r % SUBLANES, 1)],
                                      out_ref.at[dhi_ref[tok], pl.ds(dlo_ref[tok], 1)], sem
                                      ).start(priority=r % DMA_THREADS)
            pltpu.make_async_copy(src_ref, out_ref.at[pl.ds(0, rows // SUBLANES)], sem).wait()
        first += n_steps


def _scatter_rows(hxs, dest, ztiles, n_rows):
    w = hxs[0].shape[1]
    rows = min([256] + [h.shape[0] for h in hxs])
    steps = tuple(h.shape[0] // rows for h in hxs)
    firsts = [sum(steps[:k]) for k in range(len(steps))]
    in_specs = [pl.BlockSpec((rows // SUBLANES, SUBLANES, w),
                             lambda i, dh, dl, z, f=f, s=s: (jnp.clip(i - f, 0, s - 1), 0, 0))
                for f, s in zip(firsts, steps)]
    out = pl.pallas_call(
        functools.partial(_scatter_kernel, rows=rows, steps=steps),
        grid_spec=pltpu.PrefetchScalarGridSpec(
            num_scalar_prefetch=3, grid=(sum(steps),),
            in_specs=in_specs,
            out_specs=pl.BlockSpec(memory_space=pl.ANY),
            scratch_shapes=[pltpu.VMEM((MOE_TM // SUBLANES, SUBLANES, w), F32),
                            pltpu.SemaphoreType.DMA(()), pltpu.SemaphoreType.DMA(())]),
        out_shape=jax.ShapeDtypeStruct((n_rows // SUBLANES, SUBLANES, w), F32),
        compiler_params=_cparams(("arbitrary",)),
        name="scatter_rows",
    )(dest // SUBLANES, dest % SUBLANES, ztiles,
      *[h.reshape(h.shape[0] // SUBLANES, SUBLANES, w) for h in hxs])
    return out.reshape(n_rows, w)


def _moe_kernel(e0_ref, e1_ref, ti_ref, nu_ref, x_ref, wg0, wu0, wd0, wg1, wu1, wd1, y_ref):
    del e0_ref, e1_ref, ti_ref

    @pl.when(pl.program_id(0) < nu_ref[0])
    def _():
        x = x_ref[...]
        h = x[:, :D_MODEL].astype(BF16)
        wx = x[:, D_MODEL:]
        lane = lax.broadcasted_iota(jnp.int32, wx.shape, 1)
        w_lo = jnp.sum(jnp.where(lane == 0, wx, 0.0), axis=-1, keepdims=True)
        w_hi = jnp.sum(jnp.where(lane == 1, wx, 0.0), axis=-1, keepdims=True)

        def expert(wg, wu, wd):
            a = _silu(_dot(h, wg[0])) * _dot(h, wu[0])
            return _dot(a.astype(BF16), wd[0])

        y_ref[...] = w_lo * expert(wg0, wu0, wd0) + w_hi * expert(wg1, wu1, wd1)

    @pl.when(pl.program_id(0) >= nu_ref[0])
    def _():
        y_ref[...] = jnp.zeros_like(y_ref)


def _moe(xs, e0, e1, tile_idx, n_used, wg_bf, wu_bf, wd_bf):
    n_rows = xs.shape[0]
    up = lambda sel: pl.BlockSpec((1, D_MODEL, D_EXPERT), lambda t, e0, e1, ti, nu: ((e0, e1)[sel][t], 0, 0))
    down = lambda sel: pl.BlockSpec((1, D_EXPERT, D_MODEL), lambda t, e0, e1, ti, nu: ((e0, e1)[sel][t], 0, 0))
    return pl.pallas_call(
        _moe_kernel,
        grid_spec=pltpu.PrefetchScalarGridSpec(
            num_scalar_prefetch=4, grid=(n_rows // MOE_TM,),
            in_specs=[pl.BlockSpec((MOE_TM, XW), lambda t, e0, e1, ti, nu: (ti[t], 0)),
                      up(0), up(0), down(0), up(1), up(1), down(1)],
            out_specs=pl.BlockSpec((MOE_TM, D_MODEL), lambda t, e0, e1, ti, nu: (t, 0))),
        out_shape=jax.ShapeDtypeStruct((n_rows, D_MODEL), F32),
        compiler_params=_cparams(("arbitrary",)),
        name="moe_routed",
    )(e0, e1, tile_idx, n_used, xs, wg_bf, wu_bf, wd_bf, wg_bf, wu_bf, wd_bf)


def _final_kernel(dest_ref, x1_ref, ys_ref, mf_ref, lg_ref, lb_ref, o_ref, buf, sem, *, tm):
    nj = pl.num_programs(1)
    step = pl.program_id(0) * nj + pl.program_id(1)
    n_steps = pl.num_programs(0) * nj
    slot = step % 2

    def slot_wait(sl):
        pltpu.make_async_copy(ys_ref.at[pl.ds(0, tm)], buf.at[sl], sem.at[sl]).wait()

    @pl.when(step == 0)
    def _():
        def issue(r, _):
            _row_copy(ys_ref, dest_ref[r], buf.at[slot], r, sem.at[slot]).start()
            return 0
        lax.fori_loop(0, tm, issue, 0, unroll=DMA_UNROLL)

    slot_wait(slot)
    nxt = jnp.minimum(step + 1, n_steps - 1) * tm
    gate = mf_ref[0][:, 2 * D_MODEL:]
    rg = tm // FINAL_GROUPS
    for c in range(FINAL_GROUPS):
        for r in range(c * rg, (c + 1) * rg):
            _row_copy(ys_ref, dest_ref[nxt + r], buf.at[1 - slot], r, sem.at[1 - slot]
                      ).start(priority=r % DMA_THREADS)
        rows = pl.ds(c * rg, rg)
        y = buf[slot, rows]
        o_ref[0, rows] = (_ln_plain(DEEPNORM_ALPHA * x1_ref[0, rows] + gate * y) * lg_ref[...]
                          + lb_ref[...])

    @pl.when(step == n_steps - 1)
    def _():
        slot_wait(1 - slot)


def _final(x1, ys, dest, mod_ffn, ln_g, ln_b, tm):
    b, t, _ = x1.shape
    tok = pl.BlockSpec((1, tm, D_MODEL), lambda i, j, *_: (i, j, 0))
    row = pl.BlockSpec((1, D_MODEL), lambda i, j, *_: (0, 0))
    return pl.pallas_call(
        functools.partial(_final_kernel, tm=tm),
        grid_spec=pltpu.PrefetchScalarGridSpec(
            num_scalar_prefetch=1, grid=(b, t // tm),
            in_specs=[tok, pl.BlockSpec(memory_space=pl.ANY),
                      pl.BlockSpec((1, 1, 3 * D_MODEL), lambda i, j, *_: (i, 0, 0)), row, row],
            out_specs=tok,
            scratch_shapes=[pltpu.VMEM((2, tm, D_MODEL), F32), pltpu.SemaphoreType.DMA((2,))]),
        out_shape=jax.ShapeDtypeStruct((b, t, D_MODEL), F32),
        compiler_params=_cparams(("arbitrary", "arbitrary")),
        name="final_ln",
    )(dest, x1, ys, mod_ffn.reshape(b, 1, -1), ln_g.reshape(1, -1), ln_b.reshape(1, -1))


def _mixer_half(x, mod_mix, mod_ffn, pos0, state0, k_past, v_past, counts_in, wts):
    b, t, _ = x.shape
    mr, sq, skt, svt, state = _in_proj(x, mod_mix, wts["w_in_a"], wts["w_in_kvt"], pos0, state0,
                                       wts["ret_norm_g"], min(512, t))
    if k_past is None:
        ms = _sb_prompt(sq, skt, svt, wts["sb_norm_g"])
    else:
        ms = _sb_sample(sq, skt, svt, jnp.swapaxes(k_past, 2, 3), jnp.swapaxes(v_past, 2, 3),
                        wts["sb_norm_g"])
    sk, sv = jnp.swapaxes(skt, 2, 3), jnp.swapaxes(svt, 2, 3)
    x1, hx, route, counts = _out_proj(x, mr, ms, wts["w_out"], mod_mix, mod_ffn, wts["ln_mix_g"],
                                      wts["ln_mix_b"], wts["w_route"], wts["b_route"], counts_in,
                                      min(1024, t))
    return dict(x1=x1, hx=hx.reshape(b * t, XW), route=route, counts=counts, mod_ffn=mod_ffn,
                sk=sk[None], sv=sv[None], state=state[None])


def _ffn_half(groups, wts):
    n = sum(g["hx"].shape[0] for g in groups)
    n_tiles = n // MOE_TM + N_BUCKETS
    dest, ztiles, e0, e1, tile_idx, n_used = _route_plan([g["route"] for g in groups],
                                                         groups[-1]["counts"], n_tiles)
    xs = _scatter_rows([g["hx"] for g in groups], dest, ztiles, n_tiles * MOE_TM)
    ys = _moe(xs, e0, e1, tile_idx, n_used, wts["w_e_gate"], wts["w_e_up"], wts["w_e_down"])
    outs, first = [], 0
    for g in groups:
        t = g["x1"].shape[1]
        n_g = g["hx"].shape[0]
        outs.append(_final(g["x1"], ys, dest[first:first + n_g], g["mod_ffn"], wts["ln_ffn_g"],
                           wts["ln_ffn_b"], min(256, t)))
        first += n_g
    return outs


def kernel(x_prompt, x_sample, cache_sb_k, cache_sb_v, state_ret, c_prompt, c_sample, w_in, w_out, ret_norm_g, sb_norm_g, w_ada_mix, b_ada_mix, ln_mix_g, ln_mix_b, w_ada_ffn, b_ada_ffn, ln_ffn_g, ln_ffn_b, w_group, b_group, w_router, b_router, w_e_gate, w_e_up, w_e_down):
    bp = x_prompt.shape[0]
    c_all = jnp.concatenate([c_prompt, c_sample], axis=0)
    mod_mix = _ada(c_all, w_ada_mix[0], b_ada_mix[0])
    mod_ffn = _ada(c_all, w_ada_ffn[0], b_ada_ffn[0])
    pad = LANES - N_GROUPS - N_EXPERTS
    w_route = jnp.concatenate([w_group[0], w_router[0], jnp.zeros((D_MODEL, pad), F32)], axis=1)
    b_route = jnp.concatenate([b_group[0], b_router[0], jnp.zeros((pad,), F32)]).reshape(1, LANES)
    n_a = 4 * RET_WIDTH + SB_WIDTH
    wts = dict(w_in_a=w_in[0, :, :n_a].astype(BF16), w_in_kvt=w_in[0, :, n_a:].T.astype(BF16),
               w_out=w_out[0].astype(BF16),
               ret_norm_g=ret_norm_g[0], sb_norm_g=sb_norm_g[0],
               ln_mix_g=ln_mix_g[0], ln_mix_b=ln_mix_b[0], ln_ffn_g=ln_ffn_g[0], ln_ffn_b=ln_ffn_b[0],
               w_route=w_route.astype(BF16), b_route=b_route,
               w_e_gate=w_e_gate[0].astype(BF16), w_e_up=w_e_up[0].astype(BF16),
               w_e_down=w_e_down[0].astype(BF16))
    ret_zero = jnp.zeros((bp, RET_HEADS, RET_DK, RET_DV), F32)
    gp = _mixer_half(x_prompt, mod_mix[:bp], mod_ffn[:bp], 0, ret_zero, None, None,
                     jnp.zeros((1, LANES), F32), wts)
    gs = _mixer_half(x_sample, mod_mix[bp:], mod_ffn[bp:], cache_sb_k.shape[3], state_ret[0],
                     cache_sb_k[0], cache_sb_v[0], gp["counts"], wts)
    y_p, y_s = _ffn_half([gp, gs], wts)
    return (y_p, y_s, gp["sk"], gp["sv"], gp["state"], gs["sk"], gs["sv"], gs["state"])
```

```python
import functools

import numpy as np
import jax
import jax.numpy as jnp
from jax import lax
from jax.experimental import pallas as pl
from jax.experimental.pallas import tpu as pltpu

D_MODEL = 1024
RET_HEADS = 4
RET_DK = 128
RET_DV = 128
RET_WIDTH = RET_HEADS * RET_DV
SB_HEADS = 8
SB_HEAD_DIM = 64
SB_WIDTH = SB_HEADS * SB_HEAD_DIM
IN_WIDTH = 2 * RET_HEADS * RET_DK + 2 * RET_WIDTH + 3 * SB_WIDTH
ROPE_BASE = 10000.0
N_GROUPS = 4
EXPERTS_PER_GROUP = 4
N_EXPERTS = N_GROUPS * EXPERTS_PER_GROUP
D_EXPERT = 512
DEPTH = 1
DEEPNORM_ALPHA = (2.0 * DEPTH) ** 0.25
LN_EPS = 1e-5

LANES = 128
VMEM_LIMIT = 48 * 1024 * 1024

F32 = jnp.float32
BF16 = jnp.bfloat16


def _cparams(sem):
    return pltpu.CompilerParams(dimension_semantics=sem, vmem_limit_bytes=VMEM_LIMIT)


def _dot(a, b):
    return jnp.dot(a, b, preferred_element_type=F32)


def _dot_nt(a, b):
    return lax.dot_general(a, b, (((1,), (1,)), ((), ())), preferred_element_type=F32)


def _dot_tn(a, b):
    return lax.dot_general(a, b, (((0,), (0,)), ((), ())), preferred_element_type=F32)


def _split_dot(a, w_hi, w_lo):
    a_hi = a.astype(BF16)
    a_lo = (a - a_hi.astype(F32)).astype(BF16)
    return _dot(a_hi, w_hi) + (_dot(a_hi, w_lo) + _dot(a_lo, w_hi))


def _ln_plain(x):
    mu = jnp.mean(x, axis=-1, keepdims=True)
    xc = x - mu
    var = jnp.mean(xc * xc, axis=-1, keepdims=True)
    return xc * lax.rsqrt(var + LN_EPS)


def _silu(x):
    return x * (1.0 / (1.0 + jnp.exp(-x)))


def _ada_kernel(c_ref, w_ref, b_ref, o_ref):
    c = c_ref[...]
    w = w_ref[...]
    w_hi = w.astype(BF16)
    w_lo = (w - w_hi.astype(F32)).astype(BF16)
    o_ref[...] = _split_dot(_silu(c), w_hi, w_lo) + b_ref[...]


def _ada(c, w, b):
    r = c.shape[0]
    tn = 768
    return pl.pallas_call(
        _ada_kernel,
        grid=(3 * D_MODEL // tn,),
        in_specs=[pl.BlockSpec((r, D_MODEL), lambda j: (0, 0)),
                  pl.BlockSpec((D_MODEL, tn), lambda j: (0, j)),
                  pl.BlockSpec((1, tn), lambda j: (0, j))],
        out_specs=pl.BlockSpec((r, tn), lambda j: (0, j)),
        out_shape=jax.ShapeDtypeStruct((r, 3 * D_MODEL), F32),
        compiler_params=_cparams(("arbitrary",)),
        name="ada_mod",
    )(c, w, b.reshape(1, -1))


def _in_proj_kernel(x_ref, mod_ref, w_ref, wkv_ref, tab_ref, s0_ref, dec_ref, qd_ref, kd_ref, ng_ref,
                    mr_ref, q_ref, k_ref, v_ref, so_ref, st_ref, *, chunk, chunk_decay):
    j = pl.program_id(1)

    @pl.when(j == 0)
    def _():
        st_ref[...] = s0_ref[0]

    m = mod_ref[0]
    h = _ln_plain(x_ref[0]) * (1.0 + m[:, D_MODEL:2 * D_MODEL]) + m[:, :D_MODEL]
    h = h.astype(BF16)
    tm = h.shape[0]
    tab = tab_ref[...]
    grp = []
    for c in range(4):
        p = _dot(h, w_ref[:, c * RET_WIDTH:(c + 1) * RET_WIDTH])
        if c < 2:
            cs = tab[:, (2 * c) * LANES:(2 * c + 1) * LANES]
            sn = tab[:, (2 * c + 1) * LANES:(2 * c + 2) * LANES]
            heads = []
            for hh in range(RET_HEADS):
                ph = p[:, hh * RET_DK:(hh + 1) * RET_DK]
                heads.append(ph * cs + pltpu.roll(ph, RET_DK // 2, 1) * sn)
            grp.append(heads)
        else:
            grp.append([p[:, hh * RET_DV:(hh + 1) * RET_DV] for hh in range(RET_HEADS)])
    for hh in range(RET_HEADS):
        sl = slice(hh * RET_DV, (hh + 1) * RET_DV)
        for ch in range(tm // chunk):
            rows = slice(ch * chunk, (ch + 1) * chunk)
            q, k, v, g = (grp[c][hh][rows] for c in range(4))
            vb = v.astype(BF16)
            st = st_ref[hh]
            scores = _dot_nt(q.astype(BF16), k.astype(BF16)) * dec_ref[hh]
            o = _dot(scores.astype(BF16), vb) + _dot((q * qd_ref[hh]).astype(BF16), st.astype(BF16))
            st_ref[hh] = st * chunk_decay[hh] + _dot_tn((k * kd_ref[hh]).astype(BF16), vb)
            o = _ln_plain(o) * ng_ref[:, sl] * _silu(g)
            mr_ref[0, rows, sl] = o.astype(mr_ref.dtype)

    @pl.when(j == pl.num_programs(1) - 1)
    def _():
        so_ref[0] = st_ref[...]

    base = 4 * RET_WIDTH
    p = _dot(h, w_ref[:, base:base + SB_WIDTH]) * (LOG2E * SB_HEAD_DIM ** -0.5)
    for hh in range(SB_HEADS):
        q_ref[0, hh] = p[:, hh * SB_HEAD_DIM:(hh + 1) * SB_HEAD_DIM].astype(q_ref.dtype)
    pt = _dot_nt(wkv_ref[...], h)
    for c, ref in enumerate((k_ref, v_ref)):
        for hh in range(SB_HEADS):
            r0 = c * SB_WIDTH + hh * SB_HEAD_DIM
            ref[0, hh] = pt[r0:r0 + SB_HEAD_DIM, :]


def _rope_table(pos0, t):
    half = RET_DK // 2
    inv = ROPE_BASE ** (-np.arange(half, dtype=np.float64) / half)
    ang = (pos0 + np.arange(t, dtype=np.float64))[:, None] * inv[None, :]
    cos, sin = np.cos(ang), np.sin(ang)
    cs = np.concatenate([cos, cos], axis=1)
    sn = np.concatenate([-sin, sin], axis=1)
    ks = RET_DK ** -0.5
    return jnp.asarray(np.concatenate([cs, sn, cs * ks, sn * ks], axis=1), dtype=F32)


def _in_proj(x, mod, w_a_bf, w_kvt_bf, pos0, state0, norm_g, tm):
    b, t, _ = x.shape
    tab = _rope_table(pos0, t)
    wa = w_a_bf.shape[1]
    chunk = min(256, tm)
    lg = np.log1p(-np.exp2(-5.0 - np.arange(RET_HEADS, dtype=np.float64)))
    idx = np.arange(chunk, dtype=np.float64)
    rel = idx[:, None] - idx[None, :]
    dec = np.where(rel >= 0, np.exp(lg[:, None, None] * np.maximum(rel, 0.0)), 0.0)
    qd = np.broadcast_to(np.exp(lg[:, None] * (idx + 1.0))[:, :, None], (RET_HEADS, chunk, RET_DK))
    kd = np.broadcast_to(np.exp(lg[:, None] * (chunk - 1.0 - idx))[:, :, None], (RET_HEADS, chunk, RET_DK))
    chunk_decay = tuple(float(v) for v in np.exp(lg * chunk))
    hs = jax.ShapeDtypeStruct((b, SB_HEADS, SB_HEAD_DIM, t), F32)
    q_spec = pl.BlockSpec((1, SB_HEADS, tm, SB_HEAD_DIM), lambda i, j: (i, 0, j, 0))
    kv_spec = pl.BlockSpec((1, SB_HEADS, SB_HEAD_DIM, tm), lambda i, j: (i, 0, 0, j))
    const3 = lambda shape: pl.BlockSpec(shape, lambda i, j: (0, 0, 0))
    state_spec = pl.BlockSpec((1, RET_HEADS, RET_DK, RET_DV), lambda i, j: (i, 0, 0, 0))
    return pl.pallas_call(
        functools.partial(_in_proj_kernel, chunk=chunk, chunk_decay=chunk_decay),
        grid=(b, t // tm),
        in_specs=[pl.BlockSpec((1, tm, D_MODEL), lambda i, j: (i, j, 0)),
                  pl.BlockSpec((1, 1, 3 * D_MODEL), lambda i, j: (i, 0, 0)),
                  pl.BlockSpec((D_MODEL, wa), lambda i, j: (0, 0)),
                  pl.BlockSpec((2 * SB_WIDTH, D_MODEL), lambda i, j: (0, 0)),
                  pl.BlockSpec((tm, 4 * LANES), lambda i, j: (j, 0)),
                  state_spec, const3((RET_HEADS, chunk, chunk)), const3((RET_HEADS, chunk, RET_DK)),
                  const3((RET_HEADS, chunk, RET_DK)),
                  pl.BlockSpec((1, RET_WIDTH), lambda i, j: (0, 0))],
        out_specs=[pl.BlockSpec((1, tm, RET_WIDTH), lambda i, j: (i, j, 0)),
                   q_spec, kv_spec, kv_spec, state_spec],
        out_shape=[jax.ShapeDtypeStruct((b, t, RET_WIDTH), BF16),
                   jax.ShapeDtypeStruct((b, SB_HEADS, t, SB_HEAD_DIM), BF16), hs, hs,
                   jax.ShapeDtypeStruct((b, RET_HEADS, RET_DK, RET_DV), F32)],
        scratch_shapes=[pltpu.VMEM((RET_HEADS, RET_DK, RET_DV), F32)],
        compiler_params=_cparams(("parallel", "arbitrary")),
        name="in_proj",
    )(x, mod.reshape(b, 1, -1), w_a_bf, w_kvt_bf, tab, state0,
      jnp.asarray(dec, F32), jnp.asarray(qd, F32), jnp.asarray(kd, F32), norm_g.reshape(1, -1))


SB_KEY_BLOCK = 256


LOG2E = 1.4426950408889634


MASKED_LOGIT = -1e30
EXP2_CLAMP = 126.0
SB_TERMS = 1


def _sb_stage1(qs, kts, z_ref, hl_ref, masked):
    tq, kb = qs[0].shape[0], kts[0].shape[1]
    if masked:
        valid = (lax.broadcasted_iota(jnp.int32, (tq, kb), 1)
                 < lax.broadcasted_iota(jnp.int32, (tq, kb), 0))
    for c, (q, kt) in enumerate(zip(qs, kts)):
        z = _dot(q, kt.astype(BF16))
        p = jnp.maximum(z, jnp.log2(1.0 + jnp.exp2(jnp.minimum(z, EXP2_CLAMP))))
        if masked:
            p = jnp.where(valid, p, 0.0)
            z = jnp.where(valid, z, MASKED_LOGIT)
        hi = p.astype(BF16)
        z_ref[c] = z
        hl_ref[c, :, :kb] = hi
        if SB_TERMS == 2:
            hl_ref[c, :, kb:] = (p - hi.astype(F32)).astype(BF16)


def _sb_stage2(z_ref, hl_ref, vts, tri2, acc_ref, car_ref):
    n, tq, kb = z_ref.shape
    r = _dot(hl_ref[...].reshape(n * tq, SB_TERMS * kb), tri2)
    for c in range(n):
        incl = r[c * tq:(c + 1) * tq]
        car = car_ref[c]
        w = jnp.exp2(z_ref[c] + incl + car)
        acc_ref[c] += _dot_nt(w.astype(BF16), vts[c].astype(BF16))
        car_ref[c] = car + incl[:, 0:1]


def _sb_finish(o, g):
    return o * lax.rsqrt(jnp.mean(o * o, axis=-1, keepdims=True) + LN_EPS) * g


def _tri_matrix():
    idx = np.arange(SB_KEY_BLOCK)
    t = -(idx[:, None] >= idx[None, :]).astype(np.float32)
    return jnp.asarray(np.concatenate([t] * SB_TERMS, axis=0), dtype=BF16)


def _sb_prompt_kernel(q_ref, kt_ref, vt_ref, tri_ref, g_ref, o_ref, acc_ref, car_ref,
                      za_ref, ha_ref, zb_ref, hb_ref, *, tq, nh):
    i = pl.program_id(2)
    qs = [q_ref[0, hh] for hh in range(nh)]
    acc_ref[...] = jnp.zeros_like(acc_ref)
    car_ref[...] = jnp.zeros_like(car_ref)

    def s1(blk, z_ref, hl_ref, masked=False):
        start = pl.multiple_of(blk * tq, tq)
        _sb_stage1(qs, [kt_ref[0, hh, :, pl.ds(start, tq)] for hh in range(nh)], z_ref, hl_ref, masked)

    def s2(blk, z_ref, hl_ref):
        start = pl.multiple_of(blk * tq, tq)
        _sb_stage2(z_ref, hl_ref, [vt_ref[0, hh, :, pl.ds(start, tq)] for hh in range(nh)],
                   tri_ref[...], acc_ref, car_ref)

    s1(i, za_ref, ha_ref, masked=True)

    def pair(p, _):
        blk = i - 2 * p
        s1(blk - 1, zb_ref, hb_ref)
        s2(blk, za_ref, ha_ref)
        s1(blk - 2, za_ref, ha_ref)
        s2(blk - 1, zb_ref, hb_ref)
        return 0

    lax.fori_loop(0, i // 2, pair, 0)

    @pl.when(i % 2 == 0)
    def _():
        s2(0, za_ref, ha_ref)

    @pl.when(i % 2 == 1)
    def _():
        s1(0, zb_ref, hb_ref)
        s2(1, za_ref, ha_ref)
        s2(0, zb_ref, hb_ref)

    d = SB_HEAD_DIM
    outs = [_sb_finish(acc_ref[hh], g_ref[0, :, hh * d:(hh + 1) * d]) for hh in range(nh)]
    o_ref[0] = jnp.concatenate(outs, axis=-1).astype(o_ref.dtype)


def _sb_prompt(q, kt, vt, norm_g):
    b, _, t, d = q.shape
    tq = SB_KEY_BLOCK
    nh = SB_HEADS
    kv_spec = pl.BlockSpec((1, nh, d, t), lambda i, h, j: (i, h, 0, 0))
    return pl.pallas_call(
        functools.partial(_sb_prompt_kernel, tq=tq, nh=nh),
        grid=(b, SB_HEADS // nh, t // tq),
        in_specs=[pl.BlockSpec((1, nh, tq, d), lambda i, h, j: (i, h, j, 0)), kv_spec, kv_spec,
                  pl.BlockSpec((SB_TERMS * tq, tq), lambda i, h, j: (0, 0)),
                  pl.BlockSpec((1, 1, nh * d), lambda i, h, j: (h, 0, 0))],
        out_specs=pl.BlockSpec((1, tq, nh * d), lambda i, h, j: (i, j, h)),
        out_shape=jax.ShapeDtypeStruct((b, t, SB_WIDTH), BF16),
        scratch_shapes=[pltpu.VMEM((nh, tq, d), F32), pltpu.VMEM((nh, tq, 1), F32),
                        pltpu.VMEM((nh, tq, tq), F32), pltpu.VMEM((nh, tq, SB_TERMS * tq), BF16),
                        pltpu.VMEM((nh, tq, tq), F32), pltpu.VMEM((nh, tq, SB_TERMS * tq), BF16)],
        compiler_params=_cparams(("parallel", "parallel", "arbitrary")),
        name="sb_prompt",
    )(q, kt, vt, _tri_matrix(), norm_g.reshape(SB_HEADS // nh, 1, nh * d))


def _sb_sample_kernel(q_ref, kt_ref, vt_ref, ktp_ref, vtp_ref, tri_ref, g_ref, o_ref, acc_ref, car_ref,
                      zd_ref, hd_ref, za_ref, ha_ref, zb_ref, hb_ref, *, nh):
    kb = SB_KEY_BLOCK
    n_past = ktp_ref.shape[3] // kb
    t = q_ref.shape[2]
    qs = [q_ref[0, hh] for hh in range(nh)]
    acc_ref[...] = jnp.zeros_like(acc_ref)
    car_ref[...] = jnp.zeros_like(car_ref)

    def s1(blk, z_ref, hl_ref):
        start = pl.multiple_of(blk * kb, kb)
        _sb_stage1(qs, [ktp_ref[0, hh, :, pl.ds(start, kb)] for hh in range(nh)], z_ref, hl_ref, False)

    def s2(blk, z_ref, hl_ref):
        start = pl.multiple_of(blk * kb, kb)
        _sb_stage2(z_ref, hl_ref, [vtp_ref[0, hh, :, pl.ds(start, kb)] for hh in range(nh)],
                   tri_ref[...], acc_ref, car_ref)

    _sb_stage1(qs, [kt_ref[0, hh] for hh in range(nh)], zd_ref, hd_ref, True)
    s1(n_past - 1, za_ref, ha_ref)
    _sb_stage2(zd_ref, hd_ref, [vt_ref[0, hh] for hh in range(nh)],
               jnp.concatenate([tri_ref[:t, :t]] * SB_TERMS, axis=0), acc_ref, car_ref)

    def pair(p, _):
        blk = n_past - 1 - 2 * p
        s1(blk - 1, zb_ref, hb_ref)
        s2(blk, za_ref, ha_ref)
        s1(blk - 2, za_ref, ha_ref)
        s2(blk - 1, zb_ref, hb_ref)
        return 0

    lax.fori_loop(0, (n_past - 1) // 2, pair, 0)
    if (n_past - 1) % 2 == 0:
        s2(0, za_ref, ha_ref)
    else:
        s1(0, zb_ref, hb_ref)
        s2(1, za_ref, ha_ref)
        s2(0, zb_ref, hb_ref)
    d = SB_HEAD_DIM
    outs = [_sb_finish(acc_ref[hh], g_ref[0, :, hh * d:(hh + 1) * d]) for hh in range(nh)]
    o_ref[0] = jnp.concatenate(outs, axis=-1).astype(o_ref.dtype)


def _sb_sample(q, kt, vt, kt_past, vt_past, norm_g):
    b, _, t, d = q.shape
    p = kt_past.shape[3]
    nh = SB_HEADS
    kb = SB_KEY_BLOCK
    new_spec = pl.BlockSpec((1, nh, d, t), lambda i, h: (i, h, 0, 0))
    past_spec = pl.BlockSpec((1, nh, d, p), lambda i, h: (i, h, 0, 0))
    return pl.pallas_call(
        functools.partial(_sb_sample_kernel, nh=nh),
        grid=(b, SB_HEADS // nh),
        in_specs=[pl.BlockSpec((1, nh, t, d), lambda i, h: (i, h, 0, 0)), new_spec, new_spec,
                  past_spec, past_spec,
                  pl.BlockSpec((SB_TERMS * kb, kb), lambda i, h: (0, 0)),
                  pl.BlockSpec((1, 1, nh * d), lambda i, h: (h, 0, 0))],
        out_specs=pl.BlockSpec((1, t, nh * d), lambda i, h: (i, 0, h)),
        out_shape=jax.ShapeDtypeStruct((b, t, SB_WIDTH), BF16),
        scratch_shapes=[pltpu.VMEM((nh, t, d), F32), pltpu.VMEM((nh, t, 1), F32),
                        pltpu.VMEM((nh, t, t), F32), pltpu.VMEM((nh, t, SB_TERMS * t), BF16),
                        pltpu.VMEM((nh, t, kb), F32), pltpu.VMEM((nh, t, SB_TERMS * kb), BF16),
                        pltpu.VMEM((nh, t, kb), F32), pltpu.VMEM((nh, t, SB_TERMS * kb), BF16)],
        compiler_params=_cparams(("parallel", "arbitrary")),
        name="sb_sample",
    )(q, kt, vt, kt_past, vt_past, _tri_matrix(), norm_g.reshape(SB_HEADS // nh, 1, nh * d))


N_PAIRS = 6
N_BUCKETS = N_GROUPS * N_PAIRS
MOE_TM = 256
XW = D_MODEL + LANES
_PAIRS = [(a, b) for a in range(EXPERTS_PER_GROUP) for b in range(a + 1, EXPERTS_PER_GROUP)]
BUCKET_E0 = np.array([g * EXPERTS_PER_GROUP + a for g in range(N_GROUPS) for a, _ in _PAIRS], np.int32)
BUCKET_E1 = np.array([g * EXPERTS_PER_GROUP + b for g in range(N_GROUPS) for _, b in _PAIRS], np.int32)


def _route(logits):
    lane = lax.broadcasted_iota(jnp.int32, logits.shape, 1)
    neg = -jnp.inf
    big = jnp.int32(2 * LANES)
    gl = jnp.where(lane < N_GROUPS, logits, neg)
    gmax = jnp.max(gl, axis=-1, keepdims=True)
    g_idx = jnp.min(jnp.where(gl == gmax, lane, big), axis=-1, keepdims=True)
    g_p = 1.0 / jnp.sum(jnp.exp(gl - gmax), axis=-1, keepdims=True)
    lo = N_GROUPS + g_idx * EXPERTS_PER_GROUP
    el = jnp.where((lane >= lo) & (lane < lo + EXPERTS_PER_GROUP), logits, neg)
    v1 = jnp.max(el, axis=-1, keepdims=True)
    i1 = jnp.min(jnp.where(el == v1, lane, big), axis=-1, keepdims=True)
    el2 = jnp.where(lane == i1, neg, el)
    v2 = jnp.max(el2, axis=-1, keepdims=True)
    i2 = jnp.min(jnp.where(el2 == v2, lane, big), axis=-1, keepdims=True)
    e21 = jnp.exp(v2 - v1)
    p1 = 1.0 / (1.0 + e21)
    p2 = e21 * p1
    first_lo = i1 < i2
    w_lo = jnp.where(first_lo, p1, p2) * g_p
    w_hi = jnp.where(first_lo, p2, p1) * g_p
    a = jnp.minimum(i1, i2) - lo
    b = jnp.maximum(i1, i2) - lo
    pair = jnp.where(a == 0, b - 1, jnp.where(a == 1, b + 1, 5))
    return g_idx * N_PAIRS + pair, w_lo, w_hi


def _out_proj_kernel(x_ref, mr_ref, ms_ref, wo_ref, mm_ref, mf_ref, lg_ref, lb_ref, wr_ref, br_ref,
                     tril_ref, cin_ref, x1_ref, hx_ref, rt_ref, cnt_ref, run_ref):
    @pl.when((pl.program_id(0) == 0) & (pl.program_id(1) == 0))
    def _():
        run_ref[...] = cin_ref[...]

    mix = _dot(mr_ref[0], wo_ref[:MIX_HALF]) + _dot(ms_ref[0], wo_ref[MIX_HALF:])
    gate = mm_ref[0][:, 2 * D_MODEL:]
    x1 = _ln_plain(DEEPNORM_ALPHA * x_ref[0] + gate * mix) * lg_ref[...] + lb_ref[...]
    x1_ref[0] = x1
    mf = mf_ref[0]
    h2 = _ln_plain(x1) * (1.0 + mf[:, D_MODEL:2 * D_MODEL]) + mf[:, :D_MODEL]
    bucket, w_lo, w_hi = _route(_dot(h2.astype(BF16), wr_ref[...]) + br_ref[...])
    tm = h2.shape[0]
    lane = lax.broadcasted_iota(jnp.int32, (tm, LANES), 1)
    hx_ref[0, :, :D_MODEL] = h2
    hx_ref[0, :, D_MODEL:] = jnp.where(lane == 0, w_lo, jnp.where(lane == 1, w_hi, 0.0))
    hit = lane == bucket
    onehot = hit.astype(BF16)
    before = _dot(tril_ref[...], onehot)
    run = run_ref[...]
    rank = jnp.sum(jnp.where(hit, before + run, 0.0), axis=-1, keepdims=True)
    run = run + before[tm - 1:tm] + onehot[tm - 1:tm].astype(F32)
    run_ref[...] = run
    cnt_ref[...] = run
    rt_ref[0] = jnp.where(lane == 0, bucket.astype(F32), jnp.where(lane == 1, rank, 0.0))


MIX_HALF = RET_WIDTH


def _out_proj(x, mr, ms, w_out_bf, mod_mix, mod_ffn, ln_g, ln_b, w_route_bf, b_route, counts_in, tm):
    b, t, _ = x.shape
    tok = lambda w: pl.BlockSpec((1, tm, w), lambda i, j: (i, j, 0))
    modspec = pl.BlockSpec((1, 1, 3 * D_MODEL), lambda i, j: (i, 0, 0))
    row = lambda w: pl.BlockSpec((1, w), lambda i, j: (0, 0))
    idx = np.arange(tm)
    tril = jnp.asarray(idx[:, None] > idx[None, :], dtype=BF16)
    return pl.pallas_call(
        _out_proj_kernel,
        grid=(b, t // tm),
        in_specs=[tok(D_MODEL), tok(RET_WIDTH), tok(SB_WIDTH),
                  pl.BlockSpec((D_MODEL, D_MODEL), lambda i, j: (0, 0)),
                  modspec, modspec, row(D_MODEL), row(D_MODEL),
                  pl.BlockSpec((D_MODEL, LANES), lambda i, j: (0, 0)), row(LANES),
                  pl.BlockSpec((tm, tm), lambda i, j: (0, 0)), row(LANES)],
        out_specs=[tok(D_MODEL), tok(XW), tok(LANES), row(LANES)],
        out_shape=[jax.ShapeDtypeStruct((b, t, D_MODEL), F32),
                   jax.ShapeDtypeStruct((b, t, XW), F32),
                   jax.ShapeDtypeStruct((b, t, LANES), F32),
                   jax.ShapeDtypeStruct((1, LANES), F32)],
        scratch_shapes=[pltpu.VMEM((1, LANES), F32)],
        compiler_params=_cparams(("arbitrary", "arbitrary")),
        name="out_proj",
    )(x, mr, ms, w_out_bf, mod_mix.reshape(b, 1, -1), mod_ffn.reshape(b, 1, -1),
      ln_g.reshape(1, -1), ln_b.reshape(1, -1), w_route_bf, b_route, tril, counts_in)


def _route_plan(routes, counts, n_tiles):
    bucket = jnp.concatenate([r[..., 0].astype(jnp.int32).reshape(-1) for r in routes])
    rank = jnp.concatenate([r[..., 1].astype(jnp.int32).reshape(-1) for r in routes])
    cnt = counts[0, :N_BUCKETS].astype(jnp.int32)
    padded = ((cnt + MOE_TM - 1) // MOE_TM) * MOE_TM
    ends = jnp.cumsum(padded)
    starts = ends - padded
    n_used = ends[-1] // MOE_TM
    tile = jnp.arange(n_tiles, dtype=jnp.int32)
    last = jnp.maximum(n_used - 1, 0)
    tile_idx = jnp.minimum(tile, last)
    tile_bucket = jnp.sum((ends[None, :] <= (tile_idx * MOE_TM)[:, None]).astype(jnp.int32), axis=1)
    tile_bucket = jnp.minimum(tile_bucket, N_BUCKETS - 1)
    e0 = jnp.asarray(BUCKET_E0)[tile_bucket]
    e1 = jnp.asarray(BUCKET_E1)[tile_bucket]
    in_bucket = bucket[:, None] == jnp.arange(N_BUCKETS, dtype=jnp.int32)[None, :]
    dest = rank + jnp.sum(jnp.where(in_bucket, starts[None, :], 0), axis=1)
    trailing = n_used + jnp.arange(N_BUCKETS, dtype=jnp.int32)
    ztiles = jnp.concatenate([jnp.where(padded > 0, ends - MOE_TM, -1),
                              jnp.where(trailing < n_tiles, trailing * MOE_TM, -1)])
    return (dest.astype(jnp.int32), ztiles.astype(jnp.int32), e0, e1, tile_idx,
            n_used.reshape(1).astype(jnp.int32))


def _row_copy(src_ref, s, dst_ref, d, sem):
    return pltpu.make_async_copy(src_ref.at[pl.ds(s, 1)], dst_ref.at[pl.ds(d, 1)], sem)


DMA_UNROLL = 8
DMA_THREADS = 2
FINAL_GROUPS = 8


SUBLANES = 8


def _scatter_kernel(dhi_ref, dlo_ref, ztile_ref, *refs, rows, steps):
    srcs = refs[:len(steps)]
    out_ref, zbuf, sem, zsem = refs[len(steps):]
    i = pl.program_id(0)

    @pl.when(i == 0)
    def _():
        zbuf[...] = jnp.zeros_like(zbuf)

        def fill(k, _):
            @pl.when(ztile_ref[k] >= 0)
            def _():
                g0 = ztile_ref[k] // SUBLANES
                pltpu.make_async_copy(zbuf, out_ref.at[pl.ds(g0, MOE_TM // SUBLANES)], zsem).start()
            return 0

        def drain(k, _):
            @pl.when(ztile_ref[k] >= 0)
            def _():
                pltpu.make_async_copy(zbuf, out_ref.at[pl.ds(0, MOE_TM // SUBLANES)], zsem).wait()
            return 0

        lax.fori_loop(0, ztile_ref.shape[0], fill, 0)
        lax.fori_loop(0, ztile_ref.shape[0], drain, 0)

    base = i * rows
    first = 0
    for src_ref, n_steps in zip(srcs, steps):
        @pl.when((i >= first) & (i < first + n_steps))
        def _(src_ref=src_ref):
            for r in range(rows):
                tok = base + r
                pltpu.make_async_copy(src_ref.at[r // SUBLANES, pl.ds(r % SUBLANES, 1)],
                                      out_ref.at[dhi_ref[tok], pl.ds(dlo_ref[tok], 1)], sem
                                      ).start(priority=r % DMA_THREADS)
            pltpu.make_async_copy(src_ref, out_ref.at[pl.ds(0, rows // SUBLANES)], sem).wait()
        first += n_steps


def _scatter_rows(hxs, dest, ztiles, n_rows):
    w = hxs[0].shape[1]
    rows = min([256] + [h.shape[0] for h in hxs])
    steps = tuple(h.shape[0] // rows for h in hxs)
    firsts = [sum(steps[:k]) for k in range(len(steps))]
    in_specs = [pl.BlockSpec((rows // SUBLANES, SUBLANES, w),
                             lambda i, dh, dl, z, f=f, s=s: (jnp.clip(i - f, 0, s - 1), 0, 0))
                for f, s in zip(firsts, steps)]
    out = pl.pallas_call(
        functools.partial(_scatter_kernel, rows=rows, steps=steps),
        grid_spec=pltpu.PrefetchScalarGridSpec(
            num_scalar_prefetch=3, grid=(sum(steps),),
            in_specs=in_specs,
            out_specs=pl.BlockSpec(memory_space=pl.ANY),
            scratch_shapes=[pltpu.VMEM((MOE_TM // SUBLANES, SUBLANES, w), F32),
                            pltpu.SemaphoreType.DMA(()), pltpu.SemaphoreType.DMA(())]),
        out_shape=jax.ShapeDtypeStruct((n_rows // SUBLANES, SUBLANES, w), F32),
        compiler_params=_cparams(("arbitrary",)),
        name="scatter_rows",
    )(dest // SUBLANES, dest % SUBLANES, ztiles,
      *[h.reshape(h.shape[0] // SUBLANES, SUBLANES, w) for h in hxs])
    return out.reshape(n_rows, w)


def _moe_kernel(e0_ref, e1_ref, ti_ref, nu_ref, x_ref, wg0, wu0, wd0, wg1, wu1, wd1, y_ref):
    del e0_ref, e1_ref, ti_ref

    @pl.when(pl.program_id(0) < nu_ref[0])
    def _():
        x = x_ref[...]
        h = x[:, :D_MODEL].astype(BF16)
        wx = x[:, D_MODEL:]
        lane = lax.broadcasted_iota(jnp.int32, wx.shape, 1)
        w_lo = jnp.sum(jnp.where(lane == 0, wx, 0.0), axis=-1, keepdims=True)
        w_hi = jnp.sum(jnp.where(lane == 1, wx, 0.0), axis=-1, keepdims=True)

        def expert(wg, wu, wd):
            a = _silu(_dot(h, wg[0])) * _dot(h, wu[0])
            return _dot(a.astype(BF16), wd[0])

        y_ref[...] = w_lo * expert(wg0, wu0, wd0) + w_hi * expert(wg1, wu1, wd1)

    @pl.when(pl.program_id(0) >= nu_ref[0])
    def _():
        y_ref[...] = jnp.zeros_like(y_ref)


def _moe(xs, e0, e1, tile_idx, n_used, wg_bf, wu_bf, wd_bf):
    n_rows = xs.shape[0]
    up = lambda sel: pl.BlockSpec((1, D_MODEL, D_EXPERT), lambda t, e0, e1, ti, nu: ((e0, e1)[sel][t], 0, 0))
    down = lambda sel: pl.BlockSpec((1, D_EXPERT, D_MODEL), lambda t, e0, e1, ti, nu: ((e0, e1)[sel][t], 0, 0))
    return pl.pallas_call(
        _moe_kernel,
        grid_spec=pltpu.PrefetchScalarGridSpec(
            num_scalar_prefetch=4, grid=(n_rows // MOE_TM,),
            in_specs=[pl.BlockSpec((MOE_TM, XW), lambda t, e0, e1, ti, nu: (ti[t], 0)),
                      up(0), up(0), down(0), up(1), up(1), down(1)],
            out_specs=pl.BlockSpec((MOE_TM, D_MODEL), lambda t, e0, e1, ti, nu: (t, 0))),
        out_shape=jax.ShapeDtypeStruct((n_rows, D_MODEL), F32),
        compiler_params=_cparams(("arbitrary",)),
        name="moe_routed",
    )(e0, e1, tile_idx, n_used, xs, wg_bf, wu_bf, wd_bf, wg_bf, wu_bf, wd_bf)


def _final_kernel(dest_ref, x1_ref, ys_ref, mf_ref, lg_ref, lb_ref, o_ref, buf, sem, *, tm):
    nj = pl.num_programs(1)
    step = pl.program_id(0) * nj + pl.program_id(1)
    n_steps = pl.num_programs(0) * nj
    slot = step % 2

    def slot_wait(sl):
        pltpu.make_async_copy(ys_ref.at[pl.ds(0, tm)], buf.at[sl], sem.at[sl]).wait()

    @pl.when(step == 0)
    def _():
        def issue(r, _):
            _row_copy(ys_ref, dest_ref[r], buf.at[slot], r, sem.at[slot]).start()
            return 0
        lax.fori_loop(0, tm, issue, 0, unroll=DMA_UNROLL)

    slot_wait(slot)
    nxt = jnp.minimum(step + 1, n_steps - 1) * tm
    gate = mf_ref[0][:, 2 * D_MODEL:]
    rg = tm // FINAL_GROUPS
    for c in range(FINAL_GROUPS):
        for r in range(c * rg, (c + 1) * rg):
            _row_copy(ys_ref, dest_ref[nxt + r], buf.at[1 - slot], r, sem.at[1 - slot]
                      ).start(priority=r % DMA_THREADS)
        rows = pl.ds(c * rg, rg)
        y = buf[slot, rows]
        o_ref[0, rows] = (_ln_plain(DEEPNORM_ALPHA * x1_ref[0, rows] + gate * y) * lg_ref[...]
                          + lb_ref[...])

    @pl.when(step == n_steps - 1)
    def _():
        slot_wait(1 - slot)


def _final(x1, ys, dest, mod_ffn, ln_g, ln_b, tm):
    b, t, _ = x1.shape
    tok = pl.BlockSpec((1, tm, D_MODEL), lambda i, j, *_: (i, j, 0))
    row = pl.BlockSpec((1, D_MODEL), lambda i, j, *_: (0, 0))
    return pl.pallas_call(
        functools.partial(_final_kernel, tm=tm),
        grid_spec=pltpu.PrefetchScalarGridSpec(
            num_scalar_prefetch=1, grid=(b, t // tm),
            in_specs=[tok, pl.BlockSpec(memory_space=pl.ANY),
                      pl.BlockSpec((1, 1, 3 * D_MODEL), lambda i, j, *_: (i, 0, 0)), row, row],
            out_specs=tok,
            scratch_shapes=[pltpu.VMEM((2, tm, D_MODEL), F32), pltpu.SemaphoreType.DMA((2,))]),
        out_shape=jax.ShapeDtypeStruct((b, t, D_MODEL), F32),
        compiler_params=_cparams(("arbitrary", "arbitrary")),
        name="final_ln",
    )(dest, x1, ys, mod_ffn.reshape(b, 1, -1), ln_g.reshape(1, -1), ln_b.reshape(1, -1))


def _mixer_half(x, mod_mix, mod_ffn, pos0, state0, k_past, v_past, counts_in, wts):
    b, t, _ = x.shape
    mr, sq, skt, svt, state = _in_proj(x, mod_mix, wts["w_in_a"], wts["w_in_kvt"], pos0, state0,
                                       wts["ret_norm_g"], min(512, t))
    if k_past is None:
        ms = _sb_prompt(sq, skt, svt, wts["sb_norm_g"])
    else:
        ms = _sb_sample(sq, skt, svt, jnp.swapaxes(k_past, 2, 3), jnp.swapaxes(v_past, 2, 3),
                        wts["sb_norm_g"])
    sk, sv = jnp.swapaxes(skt, 2, 3), jnp.swapaxes(svt, 2, 3)
    x1, hx, route, counts = _out_proj(x, mr, ms, wts["w_out"], mod_mix, mod_ffn, wts["ln_mix_g"],
                                      wts["ln_mix_b"], wts["w_route"], wts["b_route"], counts_in,
                                      min(1024, t))
    return dict(x1=x1, hx=hx.reshape(b * t, XW), route=route, counts=counts, mod_ffn=mod_ffn,
                sk=sk[None], sv=sv[None], state=state[None])


def _ffn_half(groups, wts):
    n = sum(g["hx"].shape[0] for g in groups)
    n_tiles = n // MOE_TM + N_BUCKETS
    dest, ztiles, e0, e1, tile_idx, n_used = _route_plan([g["route"] for g in groups],
                                                         groups[-1]["counts"], n_tiles)
    xs = _scatter_rows([g["hx"] for g in groups], dest, ztiles, n_tiles * MOE_TM)
    ys = _moe(xs, e0, e1, tile_idx, n_used, wts["w_e_gate"], wts["w_e_up"], wts["w_e_down"])
    outs, first = [], 0
    for g in groups:
        t = g["x1"].shape[1]
        n_g = g["hx"].shape[0]
        outs.append(_final(g["x1"], ys, dest[first:first + n_g], g["mod_ffn"], wts["ln_ffn_g"],
                           wts["ln_ffn_b"], min(256, t)))
        first += n_g
    return outs


def kernel(x_prompt, x_sample, cache_sb_k, cache_sb_v, state_ret, c_prompt, c_sample, w_in, w_out, ret_norm_g, sb_norm_g, w_ada_mix, b_ada_mix, ln_mix_g, ln_mix_b, w_ada_ffn, b_ada_ffn, ln_ffn_g, ln_ffn_b, w_group, b_group, w_router, b_router, w_e_gate, w_e_up, w_e_down):
    bp = x_prompt.shape[0]
    c_all = jnp.concatenate([c_prompt, c_sample], axis=0)
    mod_mix = _ada(c_all, w_ada_mix[0], b_ada_mix[0])
    mod_ffn = _ada(c_all, w_ada_ffn[0], b_ada_ffn[0])
    pad = LANES - N_GROUPS - N_EXPERTS
    w_route = jnp.concatenate([w_group[0], w_router[0], jnp.zeros((D_MODEL, pad), F32)], axis=1)
    b_route = jnp.concatenate([b_group[0], b_router[0], jnp.zeros((pad,), F32)]).reshape(1, LANES)
    n_a = 4 * RET_WIDTH + SB_WIDTH
    wts = dict(w_in_a=w_in[0, :, :n_a].astype(BF16), w_in_kvt=w_in[0, :, n_a:].T.astype(BF16),
               w_out=w_out[0].astype(BF16),
               ret_norm_g=ret_norm_g[0], sb_norm_g=sb_norm_g[0],
               ln_mix_g=ln_mix_g[0], ln_mix_b=ln_mix_b[0], ln_ffn_g=ln_ffn_g[0], ln_ffn_b=ln_ffn_b[0],
               w_route=w_route.astype(BF16), b_route=b_route,
               w_e_gate=w_e_gate[0].astype(BF16), w_e_up=w_e_up[0].astype(BF16),
               w_e_down=w_e_down[0].astype(BF16))
    ret_zero = jnp.zeros((bp, RET_HEADS, RET_DK, RET_DV), F32)
    gp = _mixer_half(x_prompt, mod_mix[:bp], mod_ffn[:bp], 0, ret_zero, None, None,
                     jnp.zeros((1, LANES), F32), wts)
    gs = _mixer_half(x_sample, mod_mix[bp:], mod_ffn[bp:], cache_sb_k.shape[3], state_ret[0],
                     cache_sb_k[0], cache_sb_v[0], gp["counts"], wts)
    y_p, y_s = _ffn_half([gp, gs], wts)
    return (y_p, y_s, gp["sk"], gp["sv"], gp["state"], gs["sk"], gs["sv"], gs["state"])
```

```python
import functools

import numpy as np
import jax
import jax.numpy as jnp
from jax import lax
from jax.experimental import pallas as pl
from jax.experimental.pallas import tpu as pltpu

D_MODEL = 1024
RET_HEADS = 4
RET_DK = 128
RET_DV = 128
RET_WIDTH = RET_HEADS * RET_DV
SB_HEADS = 8
SB_HEAD_DIM = 64
SB_WIDTH = SB_HEADS * SB_HEAD_DIM
IN_WIDTH = 2 * RET_HEADS * RET_DK + 2 * RET_WIDTH + 3 * SB_WIDTH
ROPE_BASE = 10000.0
N_GROUPS = 4
EXPERTS_PER_GROUP = 4
N_EXPERTS = N_GROUPS * EXPERTS_PER_GROUP
D_EXPERT = 512
DEPTH = 1
DEEPNORM_ALPHA = (2.0 * DEPTH) ** 0.25
LN_EPS = 1e-5

LANES = 128
VMEM_LIMIT = 48 * 1024 * 1024

F32 = jnp.float32
BF16 = jnp.bfloat16


def _cparams(sem):
    return pltpu.CompilerParams(dimension_semantics=sem, vmem_limit_bytes=VMEM_LIMIT)


def _dot(a, b):
    return jnp.dot(a, b, preferred_element_type=F32)


def _dot_nt(a, b):
    return lax.dot_general(a, b, (((1,), (1,)), ((), ())), preferred_element_type=F32)


def _dot_tn(a, b):
    return lax.dot_general(a, b, (((0,), (0,)), ((), ())), preferred_element_type=F32)


def _split_dot(a, w_hi, w_lo):
    a_hi = a.astype(BF16)
    a_lo = (a - a_hi.astype(F32)).astype(BF16)
    return _dot(a_hi, w_hi) + (_dot(a_hi, w_lo) + _dot(a_lo, w_hi))


def _ln_plain(x):
    mu = jnp.mean(x, axis=-1, keepdims=True)
    xc = x - mu
    var = jnp.mean(xc * xc, axis=-1, keepdims=True)
    return xc * lax.rsqrt(var + LN_EPS)


def _silu(x):
    return x * (1.0 / (1.0 + jnp.exp(-x)))


def _ada_kernel(c_ref, w_ref, b_ref, o_ref):
    c = c_ref[...]
    w = w_ref[...]
    w_hi = w.astype(BF16)
    w_lo = (w - w_hi.astype(F32)).astype(BF16)
    o_ref[...] = _split_dot(_silu(c), w_hi, w_lo) + b_ref[...]


def _ada(c, w, b):
    r = c.shape[0]
    tn = 768
    return pl.pallas_call(
        _ada_kernel,
        grid=(3 * D_MODEL // tn,),
        in_specs=[pl.BlockSpec((r, D_MODEL), lambda j: (0, 0)),
                  pl.BlockSpec((D_MODEL, tn), lambda j: (0, j)),
                  pl.BlockSpec((1, tn), lambda j: (0, j))],
        out_specs=pl.BlockSpec((r, tn), lambda j: (0, j)),
        out_shape=jax.ShapeDtypeStruct((r, 3 * D_MODEL), F32),
        compiler_params=_cparams(("arbitrary",)),
        name="ada_mod",
    )(c, w, b.reshape(1, -1))


def _in_proj_kernel(x_ref, mod_ref, w_ref, wkv_ref, tab_ref, s0_ref, dec_ref, qd_ref, kd_ref, ng_ref,
                    mr_ref, q_ref, k_ref, v_ref, so_ref, st_ref, *, chunk, chunk_decay):
    j = pl.program_id(1)

    @pl.when(j == 0)
    def _():
        st_ref[...] = s0_ref[0]

    m = mod_ref[0]
    h = _ln_plain(x_ref[0]) * (1.0 + m[:, D_MODEL:2 * D_MODEL]) + m[:, :D_MODEL]
    h = h.astype(BF16)
    tm = h.shape[0]
    tab = tab_ref[...]
    grp = []
    for c in range(4):
        p = _dot(h, w_ref[:, c * RET_WIDTH:(c + 1) * RET_WIDTH])
        if c < 2:
            cs = tab[:, (2 * c) * LANES:(2 * c + 1) * LANES]
            sn = tab[:, (2 * c + 1) * LANES:(2 * c + 2) * LANES]
            heads = []
            for hh in range(RET_HEADS):
                ph = p[:, hh * RET_DK:(hh + 1) * RET_DK]
                heads.append(ph * cs + pltpu.roll(ph, RET_DK // 2, 1) * sn)
            grp.append(heads)
        else:
            grp.append([p[:, hh * RET_DV:(hh + 1) * RET_DV] for hh in range(RET_HEADS)])
    for hh in range(RET_HEADS):
        sl = slice(hh * RET_DV, (hh + 1) * RET_DV)
        for ch in range(tm // chunk):
            rows = slice(ch * chunk, (ch + 1) * chunk)
            q, k, v, g = (grp[c][hh][rows] for c in range(4))
            vb = v.astype(BF16)
            st = st_ref[hh]
            scores = _dot_nt(q.astype(BF16), k.astype(BF16)) * dec_ref[hh]
            o = _dot(scores.astype(BF16), vb) + _dot((q * qd_ref[hh]).astype(BF16), st.astype(BF16))
            st_ref[hh] = st * chunk_decay[hh] + _dot_tn((k * kd_ref[hh]).astype(BF16), vb)
            o = _ln_plain(o) * ng_ref[:, sl] * _silu(g)
            mr_ref[0, rows, sl] = o.astype(mr_ref.dtype)

    @pl.when(j == pl.num_programs(1) - 1)
    def _():
        so_ref[0] = st_ref[...]

    base = 4 * RET_WIDTH
    p = _dot(h, w_ref[:, base:base + SB_WIDTH]) * (LOG2E * SB_HEAD_DIM ** -0.5)
    for hh in range(SB_HEADS):
        q_ref[0, hh] = p[:, hh * SB_HEAD_DIM:(hh + 1) * SB_HEAD_DIM].astype(q_ref.dtype)
    pt = _dot_nt(wkv_ref[...], h)
    for c, ref in enumerate((k_ref, v_ref)):
        for hh in range(SB_HEADS):
            r0 = c * SB_WIDTH + hh * SB_HEAD_DIM
            ref[0, hh] = pt[r0:r0 + SB_HEAD_DIM, :]


def _rope_table(pos0, t):
    half = RET_DK // 2
    inv = ROPE_BASE ** (-np.arange(half, dtype=np.float64) / half)
    ang = (pos0 + np.arange(t, dtype=np.float64))[:, None] * inv[None, :]
    cos, sin = np.cos(ang), np.sin(ang)
    cs = np.concatenate([cos, cos], axis=1)
    sn = np.concatenate([-sin, sin], axis=1)
    ks = RET_DK ** -0.5
    return jnp.asarray(np.concatenate([cs, sn, cs * ks, sn * ks], axis=1), dtype=F32)


def _in_proj(x, mod, w_a_bf, w_kvt_bf, pos0, state0, norm_g, tm):
    b, t, _ = x.shape
    tab = _rope_table(pos0, t)
    wa = w_a_bf.shape[1]
    chunk = min(256, tm)
    lg = np.log1p(-np.exp2(-5.0 - np.arange(RET_HEADS, dtype=np.float64)))
    idx = np.arange(chunk, dtype=np.float64)
    rel = idx[:, None] - idx[None, :]
    dec = np.where(rel >= 0, np.exp(lg[:, None, None] * np.maximum(rel, 0.0)), 0.0)
    qd = np.broadcast_to(np.exp(lg[:, None] * (idx + 1.0))[:, :, None], (RET_HEADS, chunk, RET_DK))
    kd = np.broadcast_to(np.exp(lg[:, None] * (chunk - 1.0 - idx))[:, :, None], (RET_HEADS, chunk, RET_DK))
    chunk_decay = tuple(float(v) for v in np.exp(lg * chunk))
    hs = jax.ShapeDtypeStruct((b, SB_HEADS, SB_HEAD_DIM, t), F32)
    q_spec = pl.BlockSpec((1, SB_HEADS, tm, SB_HEAD_DIM), lambda i, j: (i, 0, j, 0))
    kv_spec = pl.BlockSpec((1, SB_HEADS, SB_HEAD_DIM, tm), lambda i, j: (i, 0, 0, j))
    const3 = lambda shape: pl.BlockSpec(shape, lambda i, j: (0, 0, 0))
    state_spec = pl.BlockSpec((1, RET_HEADS, RET_DK, RET_DV), lambda i, j: (i, 0, 0, 0))
    return pl.pallas_call(
        functools.partial(_in_proj_kernel, chunk=chunk, chunk_decay=chunk_decay),
        grid=(b, t // tm),
        in_specs=[pl.BlockSpec((1, tm, D_MODEL), lambda i, j: (i, j, 0)),
                  pl.BlockSpec((1, 1, 3 * D_MODEL), lambda i, j: (i, 0, 0)),
                  pl.BlockSpec((D_MODEL, wa), lambda i, j: (0, 0)),
                  pl.BlockSpec((2 * SB_WIDTH, D_MODEL), lambda i, j: (0, 0)),
                  pl.BlockSpec((tm, 4 * LANES), lambda i, j: (j, 0)),
                  state_spec, const3((RET_HEADS, chunk, chunk)), const3((RET_HEADS, chunk, RET_DK)),
                  const3((RET_HEADS, chunk, RET_DK)),
                  pl.BlockSpec((1, RET_WIDTH), lambda i, j: (0, 0))],
        out_specs=[pl.BlockSpec((1, tm, RET_WIDTH), lambda i, j: (i, j, 0)),
                   q_spec, kv_spec, kv_spec, state_spec],
        out_shape=[jax.ShapeDtypeStruct((b, t, RET_WIDTH), BF16),
                   jax.ShapeDtypeStruct((b, SB_HEADS, t, SB_HEAD_DIM), BF16), hs, hs,
                   jax.ShapeDtypeStruct((b, RET_HEADS, RET_DK, RET_DV), F32)],
        scratch_shapes=[pltpu.VMEM((RET_HEADS, RET_DK, RET_DV), F32)],
        compiler_params=_cparams(("parallel", "arbitrary")),
        name="in_proj",
    )(x, mod.reshape(b, 1, -1), w_a_bf, w_kvt_bf, tab, state0,
      jnp.asarray(dec, F32), jnp.asarray(qd, F32), jnp.asarray(kd, F32), norm_g.reshape(1, -1))


SB_KEY_BLOCK = 256


LOG2E = 1.4426950408889634


MASKED_LOGIT = -1e30
EXP2_CLAMP = 126.0
SB_TERMS = 1


def _sb_stage1(qs, kts, z_ref, hl_ref, masked):
    tq, kb = qs[0].shape[0], kts[0].shape[1]
    if masked:
        valid = (lax.broadcasted_iota(jnp.int32, (tq, kb), 1)
                 < lax.broadcasted_iota(jnp.int32, (tq, kb), 0))
    for c, (q, kt) in enumerate(zip(qs, kts)):
        z = _dot(q, kt.astype(BF16))
        p = jnp.maximum(z, jnp.log2(1.0 + jnp.exp2(jnp.minimum(z, EXP2_CLAMP))))
        if masked:
            p = jnp.where(valid, p, 0.0)
            z = jnp.where(valid, z, MASKED_LOGIT)
        hi = p.astype(BF16)
        z_ref[c] = z
        hl_ref[c, :, :kb] = hi
        if SB_TERMS == 2:
            hl_ref[c, :, kb:] = (p - hi.astype(F32)).astype(BF16)


def _sb_stage2(z_ref, hl_ref, vts, tri2, acc_ref, car_ref):
    n, tq, kb = z_ref.shape
    r = _dot(hl_ref[...].reshape(n * tq, SB_TERMS * kb), tri2)
    for c in range(n):
        incl = r[c * tq:(c + 1) * tq]
        car = car_ref[c]
        w = jnp.exp2(z_ref[c] + incl + car)
        acc_ref[c] += _dot_nt(w.astype(BF16), vts[c].astype(BF16))
        car_ref[c] = car + incl[:, 0:1]


def _sb_finish(o, g):
    return o * lax.rsqrt(jnp.mean(o * o, axis=-1, keepdims=True) + LN_EPS) * g


def _tri_matrix():
    idx = np.arange(SB_KEY_BLOCK)
    t = -(idx[:, None] >= idx[None, :]).astype(np.float32)
    return jnp.asarray(np.concatenate([t] * SB_TERMS, axis=0), dtype=BF16)


def _sb_prompt_kernel(q_ref, kt_ref, vt_ref, tri_ref, g_ref, o_ref, acc_ref, car_ref,
                      za_ref, ha_ref, zb_ref, hb_ref, *, tq, nh):
    i = pl.program_id(2)
    qs = [q_ref[0, hh] for hh in range(nh)]
    acc_ref[...] = jnp.zeros_like(acc_ref)
    car_ref[...] = jnp.zeros_like(car_ref)

    def s1(blk, z_ref, hl_ref, masked=False):
        start = pl.multiple_of(blk * tq, tq)
        _sb_stage1(qs, [kt_ref[0, hh, :, pl.ds(start, tq)] for hh in range(nh)], z_ref, hl_ref, masked)

    def s2(blk, z_ref, hl_ref):
        start = pl.multiple_of(blk * tq, tq)
        _sb_stage2(z_ref, hl_ref, [vt_ref[0, hh, :, pl.ds(start, tq)] for hh in range(nh)],
                   tri_ref[...], acc_ref, car_ref)

    s1(i, za_ref, ha_ref, masked=True)

    def pair(p, _):
        blk = i - 2 * p
        s1(blk - 1, zb_ref, hb_ref)
        s2(blk, za_ref, ha_ref)
        s1(blk - 2, za_ref, ha_ref)
        s2(blk - 1, zb_ref, hb_ref)
        return 0

    lax.fori_loop(0, i // 2, pair, 0)

    @pl.when(i % 2 == 0)
    def _():
        s2(0, za_ref, ha_ref)

    @pl.when(i % 2 == 1)
    def _():
        s1(0, zb_ref, hb_ref)
        s2(1, za_ref, ha_ref)
        s2(0, zb_ref, hb_ref)

    d = SB_HEAD_DIM
    outs = [_sb_finish(acc_ref[hh], g_ref[0, :, hh * d:(hh + 1) * d]) for hh in range(nh)]
    o_ref[0] = jnp.concatenate(outs, axis=-1).astype(o_ref.dtype)


def _sb_prompt(q, kt, vt, norm_g):
    b, _, t, d = q.shape
    tq = SB_KEY_BLOCK
    nh = SB_HEADS
    kv_spec = pl.BlockSpec((1, nh, d, t), lambda i, h, j: (i, h, 0, 0))
    return pl.pallas_call(
        functools.partial(_sb_prompt_kernel, tq=tq, nh=nh),
        grid=(b, SB_HEADS // nh, t // tq),
        in_specs=[pl.BlockSpec((1, nh, tq, d), lambda i, h, j: (i, h, j, 0)), kv_spec, kv_spec,
                  pl.BlockSpec((SB_TERMS * tq, tq), lambda i, h, j: (0, 0)),
                  pl.BlockSpec((1, 1, nh * d), lambda i, h, j: (h, 0, 0))],
        out_specs=pl.BlockSpec((1, tq, nh * d), lambda i, h, j: (i, j, h)),
        out_shape=jax.ShapeDtypeStruct((b, t, SB_WIDTH), BF16),
        scratch_shapes=[pltpu.VMEM((nh, tq, d), F32), pltpu.VMEM((nh, tq, 1), F32),
                        pltpu.VMEM((nh, tq, tq), F32), pltpu.VMEM((nh, tq, SB_TERMS * tq), BF16),
                        pltpu.VMEM((nh, tq, tq), F32), pltpu.VMEM((nh, tq, SB_TERMS * tq), BF16)],
        compiler_params=_cparams(("parallel", "parallel", "arbitrary")),
        name="sb_prompt",
    )(q, kt, vt, _tri_matrix(), norm_g.reshape(SB_HEADS // nh, 1, nh * d))


def _sb_sample_kernel(q_ref, kt_ref, vt_ref, ktp_ref, vtp_ref, tri_ref, g_ref, o_ref, acc_ref, car_ref,
                      zd_ref, hd_ref, za_ref, ha_ref, zb_ref, hb_ref, *, nh):
    kb = SB_KEY_BLOCK
    n_past = ktp_ref.shape[3] // kb
    t = q_ref.shape[2]
    qs = [q_ref[0, hh] for hh in range(nh)]
    acc_ref[...] = jnp.zeros_like(acc_ref)
    car_ref[...] = jnp.zeros_like(car_ref)

    def s1(blk, z_ref, hl_ref):
        start = pl.multiple_of(blk * kb, kb)
        _sb_stage1(qs, [ktp_ref[0, hh, :, pl.ds(start, kb)] for hh in range(nh)], z_ref, hl_ref, False)

    def s2(blk, z_ref, hl_ref):
        start = pl.multiple_of(blk * kb, kb)
        _sb_stage2(z_ref, hl_ref, [vtp_ref[0, hh, :, pl.ds(start, kb)] for hh in range(nh)],
                   tri_ref[...], acc_ref, car_ref)

    _sb_stage1(qs, [kt_ref[0, hh] for hh in range(nh)], zd_ref, hd_ref, True)
    s1(n_past - 1, za_ref, ha_ref)
    _sb_stage2(zd_ref, hd_ref, [vt_ref[0, hh] for hh in range(nh)],
               jnp.concatenate([tri_ref[:t, :t]] * SB_TERMS, axis=0), acc_ref, car_ref)

    def pair(p, _):
        blk = n_past - 1 - 2 * p
        s1(blk - 1, zb_ref, hb_ref)
        s2(blk, za_ref, ha_ref)
        s1(blk - 2, za_ref, ha_ref)
        s2(blk - 1, zb_ref, hb_ref)
        return 0

    lax.fori_loop(0, (n_past - 1) // 2, pair, 0)
    if (n_past - 1) % 2 == 0:
        s2(0, za_ref, ha_ref)
    else:
        s1(0, zb_ref, hb_ref)
        s2(1, za_ref, ha_ref)
        s2(0, zb_ref, hb_ref)
    d = SB_HEAD_DIM
    outs = [_sb_finish(acc_ref[hh], g_ref[0, :, hh * d:(hh + 1) * d]) for hh in range(nh)]
    o_ref[0] = jnp.concatenate(outs, axis=-1).astype(o_ref.dtype)


def _sb_sample(q, kt, vt, kt_past, vt_past, norm_g):
    b, _, t, d = q.shape
    p = kt_past.shape[3]
    nh = SB_HEADS
    kb = SB_KEY_BLOCK
    new_spec = pl.BlockSpec((1, nh, d, t), lambda i, h: (i, h, 0, 0))
    past_spec = pl.BlockSpec((1, nh, d, p), lambda i, h: (i, h, 0, 0))
    return pl.pallas_call(
        functools.partial(_sb_sample_kernel, nh=nh),
        grid=(b, SB_HEADS // nh),
        in_specs=[pl.BlockSpec((1, nh, t, d), lambda i, h: (i, h, 0, 0)), new_spec, new_spec,
                  past_spec, past_spec,
                  pl.BlockSpec((SB_TERMS * kb, kb), lambda i, h: (0, 0)),
                  pl.BlockSpec((1, 1, nh * d), lambda i, h: (h, 0, 0))],
        out_specs=pl.BlockSpec((1, t, nh * d), lambda i, h: (i, 0, h)),
        out_shape=jax.ShapeDtypeStruct((b, t, SB_WIDTH), BF16),
        scratch_shapes=[pltpu.VMEM((nh, t, d), F32), pltpu.VMEM((nh, t, 1), F32),
                        pltpu.VMEM((nh, t, t), F32), pltpu.VMEM((nh, t, SB_TERMS * t), BF16),
                        pltpu.VMEM((nh, t, kb), F32), pltpu.VMEM((nh, t, SB_TERMS * kb), BF16),
                        pltpu.VMEM((nh, t, kb), F32), pltpu.VMEM((nh, t, SB_TERMS * kb), BF16)],
        compiler_params=_cparams(("parallel", "arbitrary")),
        name="sb_sample",
    )(q, kt, vt, kt_past, vt_past, _tri_matrix(), norm_g.reshape(SB_HEADS // nh, 1, nh * d))


N_PAIRS = 6
N_BUCKETS = N_GROUPS * N_PAIRS
MOE_TM = 512
XW = D_MODEL + LANES
_PAIRS = [(a, b) for a in range(EXPERTS_PER_GROUP) for b in range(a + 1, EXPERTS_PER_GROUP)]
BUCKET_E0 = np.array([g * EXPERTS_PER_GROUP + a for g in range(N_GROUPS) for a, _ in _PAIRS], np.int32)
BUCKET_E1 = np.array([g * EXPERTS_PER_GROUP + b for g in range(N_GROUPS) for _, b in _PAIRS], np.int32)


def _route(logits):
    lane = lax.broadcasted_iota(jnp.int32, logits.shape, 1)
    neg = -jnp.inf
    big = jnp.int32(2 * LANES)
    gl = jnp.where(lane < N_GROUPS, logits, neg)
    gmax = jnp.max(gl, axis=-1, keepdims=True)
    g_idx = jnp.min(jnp.where(gl == gmax, lane, big), axis=-1, keepdims=True)
    g_p = 1.0 / jnp.sum(jnp.exp(gl - gmax), axis=-1, keepdims=True)
    lo = N_GROUPS + g_idx * EXPERTS_PER_GROUP
    el = jnp.where((lane >= lo) & (lane < lo + EXPERTS_PER_GROUP), logits, neg)
    v1 = jnp.max(el, axis=-1, keepdims=True)
    i1 = jnp.min(jnp.where(el == v1, lane, big), axis=-1, keepdims=True)
    el2 = jnp.where(lane == i1, neg, el)
    v2 = jnp.max(el2, axis=-1, keepdims=True)
    i2 = jnp.min(jnp.where(el2 == v2, lane, big), axis=-1, keepdims=True)
    e21 = jnp.exp(v2 - v1)
    p1 = 1.0 / (1.0 + e21)
    p2 = e21 * p1
    first_lo = i1 < i2
    w_lo = jnp.where(first_lo, p1, p2) * g_p
    w_hi = jnp.where(first_lo, p2, p1) * g_p
    a = jnp.minimum(i1, i2) - lo
    b = jnp.maximum(i1, i2) - lo
    pair = jnp.where(a == 0, b - 1, jnp.where(a == 1, b + 1, 5))
    return g_idx * N_PAIRS + pair, w_lo, w_hi


def _out_proj_kernel(x_ref, mr_ref, ms_ref, wo_ref, mm_ref, mf_ref, lg_ref, lb_ref, wr_ref, br_ref,
                     tril_ref, cin_ref, x1_ref, hx_ref, rt_ref, cnt_ref, run_ref):
    @pl.when((pl.program_id(0) == 0) & (pl.program_id(1) == 0))
    def _():
        run_ref[...] = cin_ref[...]

    mix = _dot(mr_ref[0], wo_ref[:MIX_HALF]) + _dot(ms_ref[0], wo_ref[MIX_HALF:])
    gate = mm_ref[0][:, 2 * D_MODEL:]
    x1 = _ln_plain(DEEPNORM_ALPHA * x_ref[0] + gate * mix) * lg_ref[...] + lb_ref[...]
    x1_ref[0] = x1
    mf = mf_ref[0]
    h2 = _ln_plain(x1) * (1.0 + mf[:, D_MODEL:2 * D_MODEL]) + mf[:, :D_MODEL]
    bucket, w_lo, w_hi = _route(_dot(h2.astype(BF16), wr_ref[...]) + br_ref[...])
    tm = h2.shape[0]
    lane = lax.broadcasted_iota(jnp.int32, (tm, LANES), 1)
    hx_ref[0, :, :D_MODEL] = h2
    hx_ref[0, :, D_MODEL:] = jnp.where(lane == 0, w_lo, jnp.where(lane == 1, w_hi, 0.0))
    hit = lane == bucket
    onehot = hit.astype(BF16)
    before = _dot(tril_ref[...], onehot)
    run = run_ref[...]
    rank = jnp.sum(jnp.where(hit, before + run, 0.0), axis=-1, keepdims=True)
    run = run + before[tm - 1:tm] + onehot[tm - 1:tm].astype(F32)
    run_ref[...] = run
    cnt_ref[...] = run
    rt_ref[0] = jnp.where(lane == 0, bucket.astype(F32), jnp.where(lane == 1, rank, 0.0))


MIX_HALF = RET_WIDTH


def _out_proj(x, mr, ms, w_out_bf, mod_mix, mod_ffn, ln_g, ln_b, w_route_bf, b_route, counts_in, tm):
    b, t, _ = x.shape
    tok = lambda w: pl.BlockSpec((1, tm, w), lambda i, j: (i, j, 0))
    modspec = pl.BlockSpec((1, 1, 3 * D_MODEL), lambda i, j: (i, 0, 0))
    row = lambda w: pl.BlockSpec((1, w), lambda i, j: (0, 0))
    idx = np.arange(tm)
    tril = jnp.asarray(idx[:, None] > idx[None, :], dtype=BF16)
    return pl.pallas_call(
        _out_proj_kernel,
        grid=(b, t // tm),
        in_specs=[tok(D_MODEL), tok(RET_WIDTH), tok(SB_WIDTH),
                  pl.BlockSpec((D_MODEL, D_MODEL), lambda i, j: (0, 0)),
                  modspec, modspec, row(D_MODEL), row(D_MODEL),
                  pl.BlockSpec((D_MODEL, LANES), lambda i, j: (0, 0)), row(LANES),
                  pl.BlockSpec((tm, tm), lambda i, j: (0, 0)), row(LANES)],
        out_specs=[tok(D_MODEL), tok(XW), tok(LANES), row(LANES)],
        out_shape=[jax.ShapeDtypeStruct((b, t, D_MODEL), F32),
                   jax.ShapeDtypeStruct((b, t, XW), F32),
                   jax.ShapeDtypeStruct((b, t, LANES), F32),
                   jax.ShapeDtypeStruct((1, LANES), F32)],
        scratch_shapes=[pltpu.VMEM((1, LANES), F32)],
        compiler_params=_cparams(("arbitrary", "arbitrary")),
        name="out_proj",
    )(x, mr, ms, w_out_bf, mod_mix.reshape(b, 1, -1), mod_ffn.reshape(b, 1, -1),
      ln_g.reshape(1, -1), ln_b.reshape(1, -1), w_route_bf, b_route, tril, counts_in)


def _route_plan(routes, counts, n_tiles):
    bucket = jnp.concatenate([r[..., 0].astype(jnp.int32).reshape(-1) for r in routes])
    rank = jnp.concatenate([r[..., 1].astype(jnp.int32).reshape(-1) for r in routes])
    cnt = counts[0, :N_BUCKETS].astype(jnp.int32)
    padded = ((cnt + MOE_TM - 1) // MOE_TM) * MOE_TM
    ends = jnp.cumsum(padded)
    starts = ends - padded
    n_used = ends[-1] // MOE_TM
    tile = jnp.arange(n_tiles, dtype=jnp.int32)
    last = jnp.maximum(n_used - 1, 0)
    tile_idx = jnp.minimum(tile, last)
    tile_bucket = jnp.sum((ends[None, :] <= (tile_idx * MOE_TM)[:, None]).astype(jnp.int32), axis=1)
    tile_bucket = jnp.minimum(tile_bucket, N_BUCKETS - 1)
    e0 = jnp.asarray(BUCKET_E0)[tile_bucket]
    e1 = jnp.asarray(BUCKET_E1)[tile_bucket]
    in_bucket = bucket[:, None] == jnp.arange(N_BUCKETS, dtype=jnp.int32)[None, :]
    dest = rank + jnp.sum(jnp.where(in_bucket, starts[None, :], 0), axis=1)
    trailing = n_used + jnp.arange(N_BUCKETS, dtype=jnp.int32)
    ztiles = jnp.concatenate([jnp.where(padded > 0, ends - MOE_TM, -1),
                              jnp.where(trailing < n_tiles, trailing * MOE_TM, -1)])
    return (dest.astype(jnp.int32), ztiles.astype(jnp.int32), e0, e1, tile_idx,
            n_used.reshape(1).astype(jnp.int32))


def _row_copy(src_ref, s, dst_ref, d, sem):
    return pltpu.make_async_copy(src_ref.at[pl.ds(s, 1)], dst_ref.at[pl.ds(d, 1)], sem)


DMA_UNROLL = 8
DMA_THREADS = 2
FINAL_GROUPS = 8


SUBLANES = 8


def _scatter_kernel(dhi_ref, dlo_ref, ztile_ref, *refs, rows, steps):
    srcs = refs[:len(steps)]
    out_ref, zbuf, sem, zsem = refs[len(steps):]
    i = pl.program_id(0)

    @pl.when(i == 0)
    def _():
        zbuf[...] = jnp.zeros_like(zbuf)

        def fill(k, _):
            @pl.when(ztile_ref[k] >= 0)
            def _():
                g0 = ztile_ref[k] // SUBLANES
                pltpu.make_async_copy(zbuf, out_ref.at[pl.ds(g0, MOE_TM // SUBLANES)], zsem).start()
            return 0

        def drain(k, _):
            @pl.when(ztile_ref[k] >= 0)
            def _():
                pltpu.make_async_copy(zbuf, out_ref.at[pl.ds(0, MOE_TM // SUBLANES)], zsem).wait()
            return 0

        lax.fori_loop(0, ztile_ref.shape[0], fill, 0)
        lax.fori_loop(0, ztile_ref.shape[0], drain, 0)

    base = i * rows
    first = 0
    for src_ref, n_steps in zip(srcs, steps):
        @pl.when((i >= first) & (i < first + n_steps))
        def _(src_ref=src_ref):
            for r in range(rows):
                tok = base + r
                pltpu.make_async_copy(src_ref.at[r // SUBLANES, pl.ds(r % SUBLANES, 1)],
                                      out_ref.at[dhi_ref[tok], pl.ds(dlo_ref[tok], 1)], sem
                                      ).start(priority=r % DMA_THREADS)
            pltpu.make_async_copy(src_ref, out_ref.at[pl.ds(0, rows // SUBLANES)], sem).wait()
        first += n_steps


def _scatter_rows(hxs, dest, ztiles, n_rows):
    w = hxs[0].shape[1]
    rows = min([256] + [h.shape[0] for h in hxs])
    steps = tuple(h.shape[0] // rows for h in hxs)
    firsts = [sum(steps[:k]) for k in range(len(steps))]
    in_specs = [pl.BlockSpec((rows // SUBLANES, SUBLANES, w),
                             lambda i, dh, dl, z, f=f, s=s: (jnp.clip(i - f, 0, s - 1), 0, 0))
                for f, s in zip(firsts, steps)]
    out = pl.pallas_call(
        functools.partial(_scatter_kernel, rows=rows, steps=steps),
        grid_spec=pltpu.PrefetchScalarGridSpec(
            num_scalar_prefetch=3, grid=(sum(steps),),
            in_specs=in_specs,
            out_specs=pl.BlockSpec(memory_space=pl.ANY),
            scratch_shapes=[pltpu.VMEM((MOE_TM // SUBLANES, SUBLANES, w), F32),
                            pltpu.SemaphoreType.DMA(()), pltpu.SemaphoreType.DMA(())]),
        out_shape=jax.ShapeDtypeStruct((n_rows // SUBLANES, SUBLANES, w), F32),
        compiler_params=_cparams(("arbitrary",)),
        name="scatter_rows",
    )(dest // SUBLANES, dest % SUBLANES, ztiles,
      *[h.reshape(h.shape[0] // SUBLANES, SUBLANES, w) for h in hxs])
    return out.reshape(n_rows, w)


def _moe_kernel(e0_ref, e1_ref, ti_ref, nu_ref, x_ref, wg0, wu0, wd0, wg1, wu1, wd1, y_ref):
    del e0_ref, e1_ref, ti_ref

    @pl.when(pl.program_id(0) < nu_ref[0])
    def _():
        x = x_ref[...]
        h = x[:, :D_MODEL].astype(BF16)
        wx = x[:, D_MODEL:]
        lane = lax.broadcasted_iota(jnp.int32, wx.shape, 1)
        w_lo = jnp.sum(jnp.where(lane == 0, wx, 0.0), axis=-1, keepdims=True)
        w_hi = jnp.sum(jnp.where(lane == 1, wx, 0.0), axis=-1, keepdims=True)

        def expert(wg, wu, wd):
            a = _silu(_dot(h, wg[0])) * _dot(h, wu[0])
            return _dot(a.astype(BF16), wd[0])

        y_ref[...] = w_lo * expert(wg0, wu0, wd0) + w_hi * expert(wg1, wu1, wd1)

    @pl.when(pl.program_id(0) >= nu_ref[0])
    def _():
        y_ref[...] = jnp.zeros_like(y_ref)


def _moe(xs, e0, e1, tile_idx, n_used, wg_bf, wu_bf, wd_bf):
    n_rows = xs.shape[0]
    up = lambda sel: pl.BlockSpec((1, D_MODEL, D_EXPERT), lambda t, e0, e1, ti, nu: ((e0, e1)[sel][t], 0, 0))
    down = lambda sel: pl.BlockSpec((1, D_EXPERT, D_MODEL), lambda t, e0, e1, ti, nu: ((e0, e1)[sel][t], 0, 0))
    return pl.pallas_call(
        _moe_kernel,
        grid_spec=pltpu.PrefetchScalarGridSpec(
            num_scalar_prefetch=4, grid=(n_rows // MOE_TM,),
            in_specs=[pl.BlockSpec((MOE_TM, XW), lambda t, e0, e1, ti, nu: (ti[t], 0)),
                      up(0), up(0), down(0), up(1), up(1), down(1)],
            out_specs=pl.BlockSpec((MOE_TM, D_MODEL), lambda t, e0, e1, ti, nu: (t, 0))),
        out_shape=jax.ShapeDtypeStruct((n_rows, D_MODEL), F32),
        compiler_params=_cparams(("arbitrary",)),
        name="moe_routed",
    )(e0, e1, tile_idx, n_used, xs, wg_bf, wu_bf, wd_bf, wg_bf, wu_bf, wd_bf)


def _final_kernel(dest_ref, x1_ref, ys_ref, mf_ref, lg_ref, lb_ref, o_ref, buf, sem, *, tm):
    nj = pl.num_programs(1)
    step = pl.program_id(0) * nj + pl.program_id(1)
    n_steps = pl.num_programs(0) * nj
    slot = step % 2

    def slot_wait(sl):
        pltpu.make_async_copy(ys_ref.at[pl.ds(0, tm)], buf.at[sl], sem.at[sl]).wait()

    @pl.when(step == 0)
    def _():
        def issue(r, _):
            _row_copy(ys_ref, dest_ref[r], buf.at[slot], r, sem.at[slot]).start()
            return 0
        lax.fori_loop(0, tm, issue, 0, unroll=DMA_UNROLL)

    slot_wait(slot)
    nxt = jnp.minimum(step + 1, n_steps - 1) * tm
    gate = mf_ref[0][:, 2 * D_MODEL:]
    rg = tm // FINAL_GROUPS
    for c in range(FINAL_GROUPS):
        for r in range(c * rg, (c + 1) * rg):
            _row_copy(ys_ref, dest_ref[nxt + r], buf.at[1 - slot], r, sem.at[1 - slot]
                      ).start(priority=r % DMA_THREADS)
        rows = pl.ds(c * rg, rg)
        y = buf[slot, rows]
        o_ref[0, rows] = (_ln_plain(DEEPNORM_ALPHA * x1_ref[0, rows] + gate * y) * lg_ref[...]
                          + lb_ref[...])

    @pl.when(step == n_steps - 1)
    def _():
        slot_wait(1 - slot)


def _final(x1, ys, dest, mod_ffn, ln_g, ln_b, tm):
    b, t, _ = x1.shape
    tok = pl.BlockSpec((1, tm, D_MODEL), lambda i, j, *_: (i, j, 0))
    row = pl.BlockSpec((1, D_MODEL), lambda i, j, *_: (0, 0))
    return pl.pallas_call(
        functools.partial(_final_kernel, tm=tm),
        grid_spec=pltpu.PrefetchScalarGridSpec(
            num_scalar_prefetch=1, grid=(b, t // tm),
            in_specs=[tok, pl.BlockSpec(memory_space=pl.ANY),
                      pl.BlockSpec((1, 1, 3 * D_MODEL), lambda i, j, *_: (i, 0, 0)), row, row],
            out_specs=tok,
            scratch_shapes=[pltpu.VMEM((2, tm, D_MODEL), F32), pltpu.SemaphoreType.DMA((2,))]),
        out_shape=jax.ShapeDtypeStruct((b, t, D_MODEL), F32),
        compiler_params=_cparams(("arbitrary", "arbitrary")),
        name="final_ln",
    )(dest, x1, ys, mod_ffn.reshape(b, 1, -1), ln_g.reshape(1, -1), ln_b.reshape(1, -1))


def _mixer_half(x, mod_mix, mod_ffn, pos0, state0, k_past, v_past, counts_in, wts):
    b, t, _ = x.shape
    mr, sq, skt, svt, state = _in_proj(x, mod_mix, wts["w_in_a"], wts["w_in_kvt"], pos0, state0,
                                       wts["ret_norm_g"], min(512, t))
    if k_past is None:
        ms = _sb_prompt(sq, skt, svt, wts["sb_norm_g"])
    else:
        ms = _sb_sample(sq, skt, svt, jnp.swapaxes(k_past, 2, 3), jnp.swapaxes(v_past, 2, 3),
                        wts["sb_norm_g"])
    sk, sv = jnp.swapaxes(skt, 2, 3), jnp.swapaxes(svt, 2, 3)
    x1, hx, route, counts = _out_proj(x, mr, ms, wts["w_out"], mod_mix, mod_ffn, wts["ln_mix_g"],
                                      wts["ln_mix_b"], wts["w_route"], wts["b_route"], counts_in,
                                      min(1024, t))
    return dict(x1=x1, hx=hx.reshape(b * t, XW), route=route, counts=counts, mod_ffn=mod_ffn,
                sk=sk[None], sv=sv[None], state=state[None])


def _ffn_half(groups, wts):
    n = sum(g["hx"].shape[0] for g in groups)
    n_tiles = n // MOE_TM + N_BUCKETS
    dest, ztiles, e0, e1, tile_idx, n_used = _route_plan([g["route"] for g in groups],
                                                         groups[-1]["counts"], n_tiles)
    xs = _scatter_rows([g["hx"] for g in groups], dest, ztiles, n_tiles * MOE_TM)
    ys = _moe(xs, e0, e1, tile_idx, n_used, wts["w_e_gate"], wts["w_e_up"], wts["w_e_down"])
    outs, first = [], 0
    for g in groups:
        t = g["x1"].shape[1]
        n_g = g["hx"].shape[0]
        outs.append(_final(g["x1"], ys, dest[first:first + n_g], g["mod_ffn"], wts["ln_ffn_g"],
                           wts["ln_ffn_b"], min(256, t)))
        first += n_g
    return outs


def kernel(x_prompt, x_sample, cache_sb_k, cache_sb_v, state_ret, c_prompt, c_sample, w_in, w_out, ret_norm_g, sb_norm_g, w_ada_mix, b_ada_mix, ln_mix_g, ln_mix_b, w_ada_ffn, b_ada_ffn, ln_ffn_g, ln_ffn_b, w_group, b_group, w_router, b_router, w_e_gate, w_e_up, w_e_down):
    bp = x_prompt.shape[0]
    c_all = jnp.concatenate([c_prompt, c_sample], axis=0)
    mod_mix = _ada(c_all, w_ada_mix[0], b_ada_mix[0])
    mod_ffn = _ada(c_all, w_ada_ffn[0], b_ada_ffn[0])
    pad = LANES - N_GROUPS - N_EXPERTS
    w_route = jnp.concatenate([w_group[0], w_router[0], jnp.zeros((D_MODEL, pad), F32)], axis=1)
    b_route = jnp.concatenate([b_group[0], b_router[0], jnp.zeros((pad,), F32)]).reshape(1, LANES)
    n_a = 4 * RET_WIDTH + SB_WIDTH
    wts = dict(w_in_a=w_in[0, :, :n_a].astype(BF16), w_in_kvt=w_in[0, :, n_a:].T.astype(BF16),
               w_out=w_out[0].astype(BF16),
               ret_norm_g=ret_norm_g[0], sb_norm_g=sb_norm_g[0],
               ln_mix_g=ln_mix_g[0], ln_mix_b=ln_mix_b[0], ln_ffn_g=ln_ffn_g[0], ln_ffn_b=ln_ffn_b[0],
               w_route=w_route.astype(BF16), b_route=b_route,
               w_e_gate=w_e_gate[0].astype(BF16), w_e_up=w_e_up[0].astype(BF16),
               w_e_down=w_e_down[0].astype(BF16))
    ret_zero = jnp.zeros((bp, RET_HEADS, RET_DK, RET_DV), F32)
    gp = _mixer_half(x_prompt, mod_mix[:bp], mod_ffn[:bp], 0, ret_zero, None, None,
                     jnp.zeros((1, LANES), F32), wts)
    gs = _mixer_half(x_sample, mod_mix[bp:], mod_ffn[bp:], cache_sb_k.shape[3], state_ret[0],
                     cache_sb_k[0], cache_sb_v[0], gp["counts"], wts)
    y_p, y_s = _ffn_half([gp, gs], wts)
    return (y_p, y_s, gp["sk"], gp["sv"], gp["state"], gs["sk"], gs["sv"], gs["state"])
```

```python
import functools

import numpy as np
import jax
import jax.numpy as jnp
from jax import lax
from jax.experimental import pallas as pl
from jax.experimental.pallas import tpu as pltpu

D_MODEL = 1024
RET_HEADS = 4
RET_DK = 128
RET_DV = 128
RET_WIDTH = RET_HEADS * RET_DV
SB_HEADS = 8
SB_HEAD_DIM = 64
SB_WIDTH = SB_HEADS * SB_HEAD_DIM
IN_WIDTH = 2 * RET_HEADS * RET_DK + 2 * RET_WIDTH + 3 * SB_WIDTH
ROPE_BASE = 10000.0
N_GROUPS = 4
EXPERTS_PER_GROUP = 4
N_EXPERTS = N_GROUPS * EXPERTS_PER_GROUP
D_EXPERT = 512
DEPTH = 1
DEEPNORM_ALPHA = (2.0 * DEPTH) ** 0.25
LN_EPS = 1e-5

LANES = 128
VMEM_LIMIT = 48 * 1024 * 1024

F32 = jnp.float32
BF16 = jnp.bfloat16


def _cparams(sem):
    return pltpu.CompilerParams(dimension_semantics=sem, vmem_limit_bytes=VMEM_LIMIT)


def _dot(a, b):
    return jnp.dot(a, b, preferred_element_type=F32)


def _dot_nt(a, b):
    return lax.dot_general(a, b, (((1,), (1,)), ((), ())), preferred_element_type=F32)


def _dot_tn(a, b):
    return lax.dot_general(a, b, (((0,), (0,)), ((), ())), preferred_element_type=F32)


def _split_dot(a, w_hi, w_lo):
    a_hi = a.astype(BF16)
    a_lo = (a - a_hi.astype(F32)).astype(BF16)
    return _dot(a_hi, w_hi) + (_dot(a_hi, w_lo) + _dot(a_lo, w_hi))


def _ln_plain(x):
    mu = jnp.mean(x, axis=-1, keepdims=True)
    xc = x - mu
    var = jnp.mean(xc * xc, axis=-1, keepdims=True)
    return xc * lax.rsqrt(var + LN_EPS)


def _silu(x):
    return x * (1.0 / (1.0 + jnp.exp(-x)))


def _ada_kernel(c_ref, w_ref, b_ref, o_ref):
    c = c_ref[...]
    w = w_ref[...]
    w_hi = w.astype(BF16)
    w_lo = (w - w_hi.astype(F32)).astype(BF16)
    o_ref[...] = _split_dot(_silu(c), w_hi, w_lo) + b_ref[...]


def _ada(c, w, b):
    r = c.shape[0]
    tn = 768
    return pl.pallas_call(
        _ada_kernel,
        grid=(3 * D_MODEL // tn,),
        in_specs=[pl.BlockSpec((r, D_MODEL), lambda j: (0, 0)),
                  pl.BlockSpec((D_MODEL, tn), lambda j: (0, j)),
                  pl.BlockSpec((1, tn), lambda j: (0, j))],
        out_specs=pl.BlockSpec((r, tn), lambda j: (0, j)),
        out_shape=jax.ShapeDtypeStruct((r, 3 * D_MODEL), F32),
        compiler_params=_cparams(("arbitrary",)),
        name="ada_mod",
    )(c, w, b.reshape(1, -1))


def _in_proj_kernel(x_ref, mod_ref, w_ref, wkv_ref, tab_ref, s0_ref, dec_ref, qd_ref, kd_ref, ng_ref,
                    mr_ref, q_ref, k_ref, v_ref, so_ref, st_ref, *, chunk, chunk_decay):
    j = pl.program_id(1)

    @pl.when(j == 0)
    def _():
        st_ref[...] = s0_ref[0]

    m = mod_ref[0]
    h = _ln_plain(x_ref[0]) * (1.0 + m[:, D_MODEL:2 * D_MODEL]) + m[:, :D_MODEL]
    h = h.astype(BF16)
    tm = h.shape[0]
    tab = tab_ref[...]
    grp = []
    for c in range(4):
        p = _dot(h, w_ref[:, c * RET_WIDTH:(c + 1) * RET_WIDTH])
        if c < 2:
            cs = tab[:, (2 * c) * LANES:(2 * c + 1) * LANES]
            sn = tab[:, (2 * c + 1) * LANES:(2 * c + 2) * LANES]
            heads = []
            for hh in range(RET_HEADS):
                ph = p[:, hh * RET_DK:(hh + 1) * RET_DK]
                heads.append(ph * cs + pltpu.roll(ph, RET_DK // 2, 1) * sn)
            grp.append(heads)
        else:
            grp.append([p[:, hh * RET_DV:(hh + 1) * RET_DV] for hh in range(RET_HEADS)])
    for hh in range(RET_HEADS):
        sl = slice(hh * RET_DV, (hh + 1) * RET_DV)
        for ch in range(tm // chunk):
            rows = slice(ch * chunk, (ch + 1) * chunk)
            q, k, v, g = (grp[c][hh][rows] for c in range(4))
            vb = v.astype(BF16)
            st = st_ref[hh]
            scores = _dot_nt(q.astype(BF16), k.astype(BF16)) * dec_ref[hh]
            o = _dot(scores.astype(BF16), vb) + _dot((q * qd_ref[hh]).astype(BF16), st.astype(BF16))
            st_ref[hh] = st * chunk_decay[hh] + _dot_tn((k * kd_ref[hh]).astype(BF16), vb)
            o = _ln_plain(o) * ng_ref[:, sl] * _silu(g)
            mr_ref[0, rows, sl] = o.astype(mr_ref.dtype)

    @pl.when(j == pl.num_programs(1) - 1)
    def _():
        so_ref[0] = st_ref[...]

    base = 4 * RET_WIDTH
    p = _dot(h, w_ref[:, base:base + SB_WIDTH]) * (LOG2E * SB_HEAD_DIM ** -0.5)
    for hh in range(SB_HEADS):
        q_ref[0, hh] = p[:, hh * SB_HEAD_DIM:(hh + 1) * SB_HEAD_DIM].astype(q_ref.dtype)
    pt = _dot_nt(wkv_ref[...], h)
    for c, ref in enumerate((k_ref, v_ref)):
        for hh in range(SB_HEADS):
            r0 = c * SB_WIDTH + hh * SB_HEAD_DIM
            ref[0, hh] = pt[r0:r0 + SB_HEAD_DIM, :]


def _rope_table(pos0, t):
    half = RET_DK // 2
    inv = ROPE_BASE ** (-np.arange(half, dtype=np.float64) / half)
    ang = (pos0 + np.arange(t, dtype=np.float64))[:, None] * inv[None, :]
    cos, sin = np.cos(ang), np.sin(ang)
    cs = np.concatenate([cos, cos], axis=1)
    sn = np.concatenate([-sin, sin], axis=1)
    ks = RET_DK ** -0.5
    return jnp.asarray(np.concatenate([cs, sn, cs * ks, sn * ks], axis=1), dtype=F32)


def _in_proj(x, mod, w_a_bf, w_kvt_bf, pos0, state0, norm_g, tm):
    b, t, _ = x.shape
    tab = _rope_table(pos0, t)
    wa = w_a_bf.shape[1]
    chunk = min(256, tm)
    lg = np.log1p(-np.exp2(-5.0 - np.arange(RET_HEADS, dtype=np.float64)))
    idx = np.arange(chunk, dtype=np.float64)
    rel = idx[:, None] - idx[None, :]
    dec = np.where(rel >= 0, np.exp(lg[:, None, None] * np.maximum(rel, 0.0)), 0.0)
    qd = np.broadcast_to(np.exp(lg[:, None] * (idx + 1.0))[:, :, None], (RET_HEADS, chunk, RET_DK))
    kd = np.broadcast_to(np.exp(lg[:, None] * (chunk - 1.0 - idx))[:, :, None], (RET_HEADS, chunk, RET_DK))
    chunk_decay = tuple(float(v) for v in np.exp(lg * chunk))
    hs = jax.ShapeDtypeStruct((b, SB_HEADS, SB_HEAD_DIM, t), F32)
    q_spec = pl.BlockSpec((1, SB_HEADS, tm, SB_HEAD_DIM), lambda i, j: (i, 0, j, 0))
    kv_spec = pl.BlockSpec((1, SB_HEADS, SB_HEAD_DIM, tm), lambda i, j: (i, 0, 0, j))
    const3 = lambda shape: pl.BlockSpec(shape, lambda i, j: (0, 0, 0))
    state_spec = pl.BlockSpec((1, RET_HEADS, RET_DK, RET_DV), lambda i, j: (i, 0, 0, 0))
    return pl.pallas_call(
        functools.partial(_in_proj_kernel, chunk=chunk, chunk_decay=chunk_decay),
        grid=(b, t // tm),
        in_specs=[pl.BlockSpec((1, tm, D_MODEL), lambda i, j: (i, j, 0)),
                  pl.BlockSpec((1, 1, 3 * D_MODEL), lambda i, j: (i, 0, 0)),
                  pl.BlockSpec((D_MODEL, wa), lambda i, j: (0, 0)),
                  pl.BlockSpec((2 * SB_WIDTH, D_MODEL), lambda i, j: (0, 0)),
                  pl.BlockSpec((tm, 4 * LANES), lambda i, j: (j, 0)),
                  state_spec, const3((RET_HEADS, chunk, chunk)), const3((RET_HEADS, chunk, RET_DK)),
                  const3((RET_HEADS, chunk, RET_DK)),
                  pl.BlockSpec((1, RET_WIDTH), lambda i, j: (0, 0))],
        out_specs=[pl.BlockSpec((1, tm, RET_WIDTH), lambda i, j: (i, j, 0)),
                   q_spec, kv_spec, kv_spec, state_spec],
        out_shape=[jax.ShapeDtypeStruct((b, t, RET_WIDTH), BF16),
                   jax.ShapeDtypeStruct((b, SB_HEADS, t, SB_HEAD_DIM), BF16), hs, hs,
                   jax.ShapeDtypeStruct((b, RET_HEADS, RET_DK, RET_DV), F32)],
        scratch_shapes=[pltpu.VMEM((RET_HEADS, RET_DK, RET_DV), F32)],
        compiler_params=_cparams(("parallel", "arbitrary")),
        name="in_proj",
    )(x, mod.reshape(b, 1, -1), w_a_bf, w_kvt_bf, tab, state0,
      jnp.asarray(dec, F32), jnp.asarray(qd, F32), jnp.asarray(kd, F32), norm_g.reshape(1, -1))


SB_KEY_BLOCK = 256


LOG2E = 1.4426950408889634


MASKED_LOGIT = -1e30
EXP2_CLAMP = 126.0
SB_TERMS = 1


def _sb_stage1(qs, kts, z_ref, hl_ref, masked):
    tq, kb = qs[0].shape[0], kts[0].shape[1]
    if masked:
        valid = (lax.broadcasted_iota(jnp.int32, (tq, kb), 1)
                 < lax.broadcasted_iota(jnp.int32, (tq, kb), 0))
    for c, (q, kt) in enumerate(zip(qs, kts)):
        z = _dot(q, kt.astype(BF16))
        p = jnp.maximum(z, jnp.log2(1.0 + jnp.exp2(jnp.minimum(z, EXP2_CLAMP))))
        if masked:
            p = jnp.where(valid, p, 0.0)
            z = jnp.where(valid, z, MASKED_LOGIT)
        hi = p.astype(BF16)
        z_ref[c] = z
        hl_ref[c, :, :kb] = hi
        if SB_TERMS == 2:
            hl_ref[c, :, kb:] = (p - hi.astype(F32)).astype(BF16)


def _sb_stage2(z_ref, hl_ref, vts, tri2, acc_ref, car_ref):
    n, tq, kb = z_ref.shape
    r = _dot(hl_ref[...].reshape(n * tq, SB_TERMS * kb), tri2)
    for c in range(n):
        incl = r[c * tq:(c + 1) * tq]
        car = car_ref[c]
        w = jnp.exp2(z_ref[c] + incl + car)
        acc_ref[c] += _dot_nt(w.astype(BF16), vts[c].astype(BF16))
        car_ref[c] = car + incl[:, 0:1]


def _sb_finish(o, g):
    return o * lax.rsqrt(jnp.mean(o * o, axis=-1, keepdims=True) + LN_EPS) * g


def _tri_matrix():
    idx = np.arange(SB_KEY_BLOCK)
    t = -(idx[:, None] >= idx[None, :]).astype(np.float32)
    return jnp.asarray(np.concatenate([t] * SB_TERMS, axis=0), dtype=BF16)


def _sb_prompt_kernel(q_ref, kt_ref, vt_ref, tri_ref, g_ref, o_ref, acc_ref, car_ref,
                      za_ref, ha_ref, zb_ref, hb_ref, *, tq, nh):
    i = pl.program_id(2)
    qs = [q_ref[0, hh] for hh in range(nh)]
    acc_ref[...] = jnp.zeros_like(acc_ref)
    car_ref[...] = jnp.zeros_like(car_ref)

    def s1(blk, z_ref, hl_ref, masked=False):
        start = pl.multiple_of(blk * tq, tq)
        _sb_stage1(qs, [kt_ref[0, hh, :, pl.ds(start, tq)] for hh in range(nh)], z_ref, hl_ref, masked)

    def s2(blk, z_ref, hl_ref):
        start = pl.multiple_of(blk * tq, tq)
        _sb_stage2(z_ref, hl_ref, [vt_ref[0, hh, :, pl.ds(start, tq)] for hh in range(nh)],
                   tri_ref[...], acc_ref, car_ref)

    s1(i, za_ref, ha_ref, masked=True)

    def pair(p, _):
        blk = i - 2 * p
        s1(blk - 1, zb_ref, hb_ref)
        s2(blk, za_ref, ha_ref)
        s1(blk - 2, za_ref, ha_ref)
        s2(blk - 1, zb_ref, hb_ref)
        return 0

    lax.fori_loop(0, i // 2, pair, 0)

    @pl.when(i % 2 == 0)
    def _():
        s2(0, za_ref, ha_ref)

    @pl.when(i % 2 == 1)
    def _():
        s1(0, zb_ref, hb_ref)
        s2(1, za_ref, ha_ref)
        s2(0, zb_ref, hb_ref)

    d = SB_HEAD_DIM
    outs = [_sb_finish(acc_ref[hh], g_ref[0, :, hh * d:(hh + 1) * d]) for hh in range(nh)]
    o_ref[0] = jnp.concatenate(outs, axis=-1).astype(o_ref.dtype)


def _sb_prompt(q, kt, vt, norm_g):
    b, _, t, d = q.shape
    tq = SB_KEY_BLOCK
    nh = SB_HEADS
    kv_spec = pl.BlockSpec((1, nh, d, t), lambda i, h, j: (i, h, 0, 0))
    return pl.pallas_call(
        functools.partial(_sb_prompt_kernel, tq=tq, nh=nh),
        grid=(b, SB_HEADS // nh, t // tq),
        in_specs=[pl.BlockSpec((1, nh, tq, d), lambda i, h, j: (i, h, j, 0)), kv_spec, kv_spec,
                  pl.BlockSpec((SB_TERMS * tq, tq), lambda i, h, j: (0, 0)),
                  pl.BlockSpec((1, 1, nh * d), lambda i, h, j: (h, 0, 0))],
        out_specs=pl.BlockSpec((1, tq, nh * d), lambda i, h, j: (i, j, h)),
        out_shape=jax.ShapeDtypeStruct((b, t, SB_WIDTH), BF16),
        scratch_shapes=[pltpu.VMEM((nh, tq, d), F32), pltpu.VMEM((nh, tq, 1), F32),
                        pltpu.VMEM((nh, tq, tq), F32), pltpu.VMEM((nh, tq, SB_TERMS * tq), BF16),
                        pltpu.VMEM((nh, tq, tq), F32), pltpu.VMEM((nh, tq, SB_TERMS * tq), BF16)],
        compiler_params=_cparams(("parallel", "parallel", "arbitrary")),
        name="sb_prompt",
    )(q, kt, vt, _tri_matrix(), norm_g.reshape(SB_HEADS // nh, 1, nh * d))


def _sb_sample_kernel(q_ref, kt_ref, vt_ref, ktp_ref, vtp_ref, tri_ref, g_ref, o_ref, acc_ref, car_ref,
                      zd_ref, hd_ref, za_ref, ha_ref, zb_ref, hb_ref, *, nh):
    kb = SB_KEY_BLOCK
    n_past = ktp_ref.shape[3] // kb
    t = q_ref.shape[2]
    qs = [q_ref[0, hh] for hh in range(nh)]
    acc_ref[...] = jnp.zeros_like(acc_ref)
    car_ref[...] = jnp.zeros_like(car_ref)

    def s1(blk, z_ref, hl_ref):
        start = pl.multiple_of(blk * kb, kb)
        _sb_stage1(qs, [ktp_ref[0, hh, :, pl.ds(start, kb)] for hh in range(nh)], z_ref, hl_ref, False)

    def s2(blk, z_ref, hl_ref):
        start = pl.multiple_of(blk * kb, kb)
        _sb_stage2(z_ref, hl_ref, [vtp_ref[0, hh, :, pl.ds(start, kb)] for hh in range(nh)],
                   tri_ref[...], acc_ref, car_ref)

    _sb_stage1(qs, [kt_ref[0, hh] for hh in range(nh)], zd_ref, hd_ref, True)
    s1(n_past - 1, za_ref, ha_ref)
    _sb_stage2(zd_ref, hd_ref, [vt_ref[0, hh] for hh in range(nh)],
               jnp.concatenate([tri_ref[:t, :t]] * SB_TERMS, axis=0), acc_ref, car_ref)

    def pair(p, _):
        blk = n_past - 1 - 2 * p
        s1(blk - 1, zb_ref, hb_ref)
        s2(blk, za_ref, ha_ref)
        s1(blk - 2, za_ref, ha_ref)
        s2(blk - 1, zb_ref, hb_ref)
        return 0

    lax.fori_loop(0, (n_past - 1) // 2, pair, 0)
    if (n_past - 1) % 2 == 0:
        s2(0, za_ref, ha_ref)
    else:
        s1(0, zb_ref, hb_ref)
        s2(1, za_ref, ha_ref)
        s2(0, zb_ref, hb_ref)
    d = SB_HEAD_DIM
    outs = [_sb_finish(acc_ref[hh], g_ref[0, :, hh * d:(hh + 1) * d]) for hh in range(nh)]
    o_ref[0] = jnp.concatenate(outs, axis=-1).astype(o_ref.dtype)


def _sb_sample(q, kt, vt, kt_past, vt_past, norm_g):
    b, _, t, d = q.shape
    p = kt_past.shape[3]
    nh = SB_HEADS
    kb = SB_KEY_BLOCK
    new_spec = pl.BlockSpec((1, nh, d, t), lambda i, h: (i, h, 0, 0))
    past_spec = pl.BlockSpec((1, nh, d, p), lambda i, h: (i, h, 0, 0))
    return pl.pallas_call(
        functools.partial(_sb_sample_kernel, nh=nh),
        grid=(b, SB_HEADS // nh),
        in_specs=[pl.BlockSpec((1, nh, t, d), lambda i, h: (i, h, 0, 0)), new_spec, new_spec,
                  past_spec, past_spec,
                  pl.BlockSpec((SB_TERMS * kb, kb), lambda i, h: (0, 0)),
                  pl.BlockSpec((1, 1, nh * d), lambda i, h: (h, 0, 0))],
        out_specs=pl.BlockSpec((1, t, nh * d), lambda i, h: (i, 0, h)),
        out_shape=jax.ShapeDtypeStruct((b, t, SB_WIDTH), BF16),
        scratch_shapes=[pltpu.VMEM((nh, t, d), F32), pltpu.VMEM((nh, t, 1), F32),
                        pltpu.VMEM((nh, t, t), F32), pltpu.VMEM((nh, t, SB_TERMS * t), BF16),
                        pltpu.VMEM((nh, t, kb), F32), pltpu.VMEM((nh, t, SB_TERMS * kb), BF16),
                        pltpu.VMEM((nh, t, kb), F32), pltpu.VMEM((nh, t, SB_TERMS * kb), BF16)],
        compiler_params=_cparams(("parallel", "arbitrary")),
        name="sb_sample",
    )(q, kt, vt, kt_past, vt_past, _tri_matrix(), norm_g.reshape(SB_HEADS // nh, 1, nh * d))


N_PAIRS = 6
N_BUCKETS = N_GROUPS * N_PAIRS
MOE_TM = 512
XW = D_MODEL + LANES
_PAIRS = [(a, b) for a in range(EXPERTS_PER_GROUP) for b in range(a + 1, EXPERTS_PER_GROUP)]
BUCKET_E0 = np.array([g * EXPERTS_PER_GROUP + a for g in range(N_GROUPS) for a, _ in _PAIRS], np.int32)
BUCKET_E1 = np.array([g * EXPERTS_PER_GROUP + b for g in range(N_GROUPS) for _, b in _PAIRS], np.int32)


def _route(logits):
    lane = lax.broadcasted_iota(jnp.int32, logits.shape, 1)
    neg = -jnp.inf
    big = jnp.int32(2 * LANES)
    gl = jnp.where(lane < N_GROUPS, logits, neg)
    gmax = jnp.max(gl, axis=-1, keepdims=True)
    g_idx = jnp.min(jnp.where(gl == gmax, lane, big), axis=-1, keepdims=True)
    g_p = 1.0 / jnp.sum(jnp.exp(gl - gmax), axis=-1, keepdims=True)
    lo = N_GROUPS + g_idx * EXPERTS_PER_GROUP
    el = jnp.where((lane >= lo) & (lane < lo + EXPERTS_PER_GROUP), logits, neg)
    v1 = jnp.max(el, axis=-1, keepdims=True)
    i1 = jnp.min(jnp.where(el == v1, lane, big), axis=-1, keepdims=True)
    el2 = jnp.where(lane == i1, neg, el)
    v2 = jnp.max(el2, axis=-1, keepdims=True)
    i2 = jnp.min(jnp.where(el2 == v2, lane, big), axis=-1, keepdims=True)
    e21 = jnp.exp(v2 - v1)
    p1 = 1.0 / (1.0 + e21)
    p2 = e21 * p1
    first_lo = i1 < i2
    w_lo = jnp.where(first_lo, p1, p2) * g_p
    w_hi = jnp.where(first_lo, p2, p1) * g_p
    a = jnp.minimum(i1, i2) - lo
    b = jnp.maximum(i1, i2) - lo
    pair = jnp.where(a == 0, b - 1, jnp.where(a == 1, b + 1, 5))
    return g_idx * N_PAIRS + pair, w_lo, w_hi


def _out_proj_kernel(x_ref, mr_ref, ms_ref, wo_ref, mm_ref, mf_ref, lg_ref, lb_ref, wr_ref, br_ref,
                     tril_ref, cin_ref, x1_ref, hx_ref, rt_ref, cnt_ref, run_ref):
    @pl.when((pl.program_id(0) == 0) & (pl.program_id(1) == 0))
    def _():
        run_ref[...] = cin_ref[...]

    mix = _dot(mr_ref[0], wo_ref[:MIX_HALF]) + _dot(ms_ref[0], wo_ref[MIX_HALF:])
    gate = mm_ref[0][:, 2 * D_MODEL:]
    x1 = _ln_plain(DEEPNORM_ALPHA * x_ref[0] + gate * mix) * lg_ref[...] + lb_ref[...]
    x1_ref[0] = x1
    mf = mf_ref[0]
    h2 = _ln_plain(x1) * (1.0 + mf[:, D_MODEL:2 * D_MODEL]) + mf[:, :D_MODEL]
    bucket, w_lo, w_hi = _route(_dot(h2.astype(BF16), wr_ref[...]) + br_ref[...])
    tm = h2.shape[0]
    lane = lax.broadcasted_iota(jnp.int32, (tm, LANES), 1)
    hx_ref[0, :, :D_MODEL] = h2
    hx_ref[0, :, D_MODEL:] = jnp.where(lane == 0, w_lo, jnp.where(lane == 1, w_hi, 0.0))
    hit = lane == bucket
    onehot = hit.astype(BF16)
    before = _dot(tril_ref[...], onehot)
    run = run_ref[...]
    rank = jnp.sum(jnp.where(hit, before + run, 0.0), axis=-1, keepdims=True)
    run = run + before[tm - 1:tm] + onehot[tm - 1:tm].astype(F32)
    run_ref[...] = run
    cnt_ref[...] = run
    rt_ref[0] = jnp.where(lane == 0, bucket.astype(F32), jnp.where(lane == 1, rank, 0.0))


MIX_HALF = RET_WIDTH


def _out_proj(x, mr, ms, w_out_bf, mod_mix, mod_ffn, ln_g, ln_b, w_route_bf, b_route, counts_in, tm):
    b, t, _ = x.shape
    tok = lambda w: pl.BlockSpec((1, tm, w), lambda i, j: (i, j, 0))
    modspec = pl.BlockSpec((1, 1, 3 * D_MODEL), lambda i, j: (i, 0, 0))
    row = lambda w: pl.BlockSpec((1, w), lambda i, j: (0, 0))
    idx = np.arange(tm)
    tril = jnp.asarray(idx[:, None] > idx[None, :], dtype=BF16)
    return pl.pallas_call(
        _out_proj_kernel,
        grid=(b, t // tm),
        in_specs=[tok(D_MODEL), tok(RET_WIDTH), tok(SB_WIDTH),
                  pl.BlockSpec((D_MODEL, D_MODEL), lambda i, j: (0, 0)),
                  modspec, modspec, row(D_MODEL), row(D_MODEL),
                  pl.BlockSpec((D_MODEL, LANES), lambda i, j: (0, 0)), row(LANES),
                  pl.BlockSpec((tm, tm), lambda i, j: (0, 0)), row(LANES)],
        out_specs=[tok(D_MODEL), tok(XW), tok(LANES), row(LANES)],
        out_shape=[jax.ShapeDtypeStruct((b, t, D_MODEL), F32),
                   jax.ShapeDtypeStruct((b, t, XW), F32),
                   jax.ShapeDtypeStruct((b, t, LANES), F32),
                   jax.ShapeDtypeStruct((1, LANES), F32)],
        scratch_shapes=[pltpu.VMEM((1, LANES), F32)],
        compiler_params=_cparams(("arbitrary", "arbitrary")),
        name="out_proj",
    )(x, mr, ms, w_out_bf, mod_mix.reshape(b, 1, -1), mod_ffn.reshape(b, 1, -1),
      ln_g.reshape(1, -1), ln_b.reshape(1, -1), w_route_bf, b_route, tril, counts_in)


def _route_plan(routes, counts, n_tiles):
    bucket = jnp.concatenate([r[..., 0].astype(jnp.int32).reshape(-1) for r in routes])
    rank = jnp.concatenate([r[..., 1].astype(jnp.int32).reshape(-1) for r in routes])
    cnt = counts[0, :N_BUCKETS].astype(jnp.int32)
    padded = ((cnt + MOE_TM - 1) // MOE_TM) * MOE_TM
    ends = jnp.cumsum(padded)
    starts = ends - padded
    n_used = ends[-1] // MOE_TM
    tile = jnp.arange(n_tiles, dtype=jnp.int32)
    last = jnp.maximum(n_used - 1, 0)
    tile_idx = jnp.minimum(tile, last)
    tile_bucket = jnp.sum((ends[None, :] <= (tile_idx * MOE_TM)[:, None]).astype(jnp.int32), axis=1)
    tile_bucket = jnp.minimum(tile_bucket, N_BUCKETS - 1)
    e0 = jnp.asarray(BUCKET_E0)[tile_bucket]
    e1 = jnp.asarray(BUCKET_E1)[tile_bucket]
    in_bucket = bucket[:, None] == jnp.arange(N_BUCKETS, dtype=jnp.int32)[None, :]
    dest = rank + jnp.sum(jnp.where(in_bucket, starts[None, :], 0), axis=1)
    trailing = n_used + jnp.arange(N_BUCKETS, dtype=jnp.int32)
    ztiles = jnp.concatenate([jnp.where(padded > 0, ends - MOE_TM, -1),
                              jnp.where(trailing < n_tiles, trailing * MOE_TM, -1)])
    return (dest.astype(jnp.int32), ztiles.astype(jnp.int32), e0, e1, tile_idx,
            n_used.reshape(1).astype(jnp.int32))


def _row_copy(src_ref, s, dst_ref, d, sem):
    return pltpu.make_async_copy(src_ref.at[pl.ds(s, 1)], dst_ref.at[pl.ds(d, 1)], sem)


DMA_UNROLL = 8
DMA_THREADS = 2
FINAL_GROUPS = 8


SUBLANES = 8


def _scatter_kernel(dhi_ref, dlo_ref, ztile_ref, *refs, rows, steps):
    srcs = refs[:len(steps)]
    out_ref, zbuf, sem, zsem = refs[len(steps):]
    i = pl.program_id(0)

    @pl.when(i == 0)
    def _():
        zbuf[...] = jnp.zeros_like(zbuf)

        def fill(k, _):
            @pl.when(ztile_ref[k] >= 0)
            def _():
                g0 = ztile_ref[k] // SUBLANES
                pltpu.make_async_copy(zbuf, out_ref.at[pl.ds(g0, MOE_TM // SUBLANES)], zsem).start()
            return 0

        def drain(k, _):
            @pl.when(ztile_ref[k] >= 0)
            def _():
                pltpu.make_async_copy(zbuf, out_ref.at[pl.ds(0, MOE_TM // SUBLANES)], zsem).wait()
            return 0

        lax.fori_loop(0, ztile_ref.shape[0], fill, 0)
        lax.fori_loop(0, ztile_ref.shape[0], drain, 0)

    base = i * rows
    first = 0
    for src_ref, n_steps in zip(srcs, steps):
        @pl.when((i >= first) & (i < first + n_steps))
        def _(src_ref=src_ref):
            for r in range(rows):
                tok = base + r
                pltpu.make_async_copy(src_ref.at[r // SUBLANES, pl.ds(r % SUBLANES, 1)],
                                      out_ref.at[dhi_ref[tok], pl.ds(dlo_ref[tok], 1)], sem
                                      ).start(priority=r % DMA_THREADS)
            pltpu.make_async_copy(src_ref, out_ref.at[pl.ds(0, rows // SUBLANES)], sem).wait()
        first += n_steps


def _scatter_rows(hxs, dest, ztiles, n_rows):
    w = hxs[0].shape[1]
    rows = min([512] + [h.shape[0] for h in hxs])
    steps = tuple(h.shape[0] // rows for h in hxs)
    firsts = [sum(steps[:k]) for k in range(len(steps))]
    in_specs = [pl.BlockSpec((rows // SUBLANES, SUBLANES, w),
                             lambda i, dh, dl, z, f=f, s=s: (jnp.clip(i - f, 0, s - 1), 0, 0))
                for f, s in zip(firsts, steps)]
    out = pl.pallas_call(
        functools.partial(_scatter_kernel, rows=rows, steps=steps),
        grid_spec=pltpu.PrefetchScalarGridSpec(
            num_scalar_prefetch=3, grid=(sum(steps),),
            in_specs=in_specs,
            out_specs=pl.BlockSpec(memory_space=pl.ANY),
            scratch_shapes=[pltpu.VMEM((MOE_TM // SUBLANES, SUBLANES, w), F32),
                            pltpu.SemaphoreType.DMA(()), pltpu.SemaphoreType.DMA(())]),
        out_shape=jax.ShapeDtypeStruct((n_rows // SUBLANES, SUBLANES, w), F32),
        compiler_params=_cparams(("arbitrary",)),
        name="scatter_rows",
    )(dest // SUBLANES, dest % SUBLANES, ztiles,
      *[h.reshape(h.shape[0] // SUBLANES, SUBLANES, w) for h in hxs])
    return out.reshape(n_rows, w)


def _moe_kernel(e0_ref, e1_ref, ti_ref, nu_ref, x_ref, wg0, wu0, wd0, wg1, wu1, wd1, y_ref):
    del e0_ref, e1_ref, ti_ref

    @pl.when(pl.program_id(0) < nu_ref[0])
    def _():
        x = x_ref[...]
        h = x[:, :D_MODEL].astype(BF16)
        wx = x[:, D_MODEL:]
        lane = lax.broadcasted_iota(jnp.int32, wx.shape, 1)
        w_lo = jnp.sum(jnp.where(lane == 0, wx, 0.0), axis=-1, keepdims=True)
        w_hi = jnp.sum(jnp.where(lane == 1, wx, 0.0), axis=-1, keepdims=True)

        def expert(wg, wu, wd):
            a = _silu(_dot(h, wg[0])) * _dot(h, wu[0])
            return _dot(a.astype(BF16), wd[0])

        y_ref[...] = w_lo * expert(wg0, wu0, wd0) + w_hi * expert(wg1, wu1, wd1)

    @pl.when(pl.program_id(0) >= nu_ref[0])
    def _():
        y_ref[...] = jnp.zeros_like(y_ref)


def _moe(xs, e0, e1, tile_idx, n_used, wg_bf, wu_bf, wd_bf):
    n_rows = xs.shape[0]
    up = lambda sel: pl.BlockSpec((1, D_MODEL, D_EXPERT), lambda t, e0, e1, ti, nu: ((e0, e1)[sel][t], 0, 0))
    down = lambda sel: pl.BlockSpec((1, D_EXPERT, D_MODEL), lambda t, e0, e1, ti, nu: ((e0, e1)[sel][t], 0, 0))
    return pl.pallas_call(
        _moe_kernel,
        grid_spec=pltpu.PrefetchScalarGridSpec(
            num_scalar_prefetch=4, grid=(n_rows // MOE_TM,),
            in_specs=[pl.BlockSpec((MOE_TM, XW), lambda t, e0, e1, ti, nu: (ti[t], 0)),
                      up(0), up(0), down(0), up(1), up(1), down(1)],
            out_specs=pl.BlockSpec((MOE_TM, D_MODEL), lambda t, e0, e1, ti, nu: (t, 0))),
        out_shape=jax.ShapeDtypeStruct((n_rows, D_MODEL), F32),
        compiler_params=_cparams(("arbitrary",)),
        name="moe_routed",
    )(e0, e1, tile_idx, n_used, xs, wg_bf, wu_bf, wd_bf, wg_bf, wu_bf, wd_bf)


def _final_kernel(dest_ref, x1_ref, ys_ref, mf_ref, lg_ref, lb_ref, o_ref, buf, sem, *, tm):
    nj = pl.num_programs(1)
    step = pl.program_id(0) * nj + pl.program_id(1)
    n_steps = pl.num_programs(0) * nj
    slot = step % 2

    def slot_wait(sl):
        pltpu.make_async_copy(ys_ref.at[pl.ds(0, tm)], buf.at[sl], sem.at[sl]).wait()

    @pl.when(step == 0)
    def _():
        def issue(r, _):
            _row_copy(ys_ref, dest_ref[r], buf.at[slot], r, sem.at[slot]).start()
            return 0
        lax.fori_loop(0, tm, issue, 0, unroll=DMA_UNROLL)

    slot_wait(slot)
    nxt = jnp.minimum(step + 1, n_steps - 1) * tm
    gate = mf_ref[0][:, 2 * D_MODEL:]
    rg = tm // FINAL_GROUPS
    for c in range(FINAL_GROUPS):
        for r in range(c * rg, (c + 1) * rg):
            _row_copy(ys_ref, dest_ref[nxt + r], buf.at[1 - slot], r, sem.at[1 - slot]
                      ).start(priority=r % DMA_THREADS)
        rows = pl.ds(c * rg, rg)
        y = buf[slot, rows]
        o_ref[0, rows] = (_ln_plain(DEEPNORM_ALPHA * x1_ref[0, rows] + gate * y) * lg_ref[...]
                          + lb_ref[...])

    @pl.when(step == n_steps - 1)
    def _():
        slot_wait(1 - slot)


def _final(x1, ys, dest, mod_ffn, ln_g, ln_b, tm):
    b, t, _ = x1.shape
    tok = pl.BlockSpec((1, tm, D_MODEL), lambda i, j, *_: (i, j, 0))
    row = pl.BlockSpec((1, D_MODEL), lambda i, j, *_: (0, 0))
    return pl.pallas_call(
        functools.partial(_final_kernel, tm=tm),
        grid_spec=pltpu.PrefetchScalarGridSpec(
            num_scalar_prefetch=1, grid=(b, t // tm),
            in_specs=[tok, pl.BlockSpec(memory_space=pl.ANY),
                      pl.BlockSpec((1, 1, 3 * D_MODEL), lambda i, j, *_: (i, 0, 0)), row, row],
            out_specs=tok,
            scratch_shapes=[pltpu.VMEM((2, tm, D_MODEL), F32), pltpu.SemaphoreType.DMA((2,))]),
        out_shape=jax.ShapeDtypeStruct((b, t, D_MODEL), F32),
        compiler_params=_cparams(("arbitrary", "arbitrary")),
        name="final_ln",
    )(dest, x1, ys, mod_ffn.reshape(b, 1, -1), ln_g.reshape(1, -1), ln_b.reshape(1, -1))


def _mixer_half(x, mod_mix, mod_ffn, pos0, state0, k_past, v_past, counts_in, wts):
    b, t, _ = x.shape
    mr, sq, skt, svt, state = _in_proj(x, mod_mix, wts["w_in_a"], wts["w_in_kvt"], pos0, state0,
                                       wts["ret_norm_g"], min(512, t))
    if k_past is None:
        ms = _sb_prompt(sq, skt, svt, wts["sb_norm_g"])
    else:
        ms = _sb_sample(sq, skt, svt, jnp.swapaxes(k_past, 2, 3), jnp.swapaxes(v_past, 2, 3),
                        wts["sb_norm_g"])
    sk, sv = jnp.swapaxes(skt, 2, 3), jnp.swapaxes(svt, 2, 3)
    x1, hx, route, counts = _out_proj(x, mr, ms, wts["w_out"], mod_mix, mod_ffn, wts["ln_mix_g"],
                                      wts["ln_mix_b"], wts["w_route"], wts["b_route"], counts_in,
                                      min(1024, t))
    return dict(x1=x1, hx=hx.reshape(b * t, XW), route=route, counts=counts, mod_ffn=mod_ffn,
                sk=sk[None], sv=sv[None], state=state[None])


def _ffn_half(groups, wts):
    n = sum(g["hx"].shape[0] for g in groups)
    n_tiles = n // MOE_TM + N_BUCKETS
    dest, ztiles, e0, e1, tile_idx, n_used = _route_plan([g["route"] for g in groups],
                                                         groups[-1]["counts"], n_tiles)
    xs = _scatter_rows([g["hx"] for g in groups], dest, ztiles, n_tiles * MOE_TM)
    ys = _moe(xs, e0, e1, tile_idx, n_used, wts["w_e_gate"], wts["w_e_up"], wts["w_e_down"])
    outs, first = [], 0
    for g in groups:
        t = g["x1"].shape[1]
        n_g = g["hx"].shape[0]
        outs.append(_final(g["x1"], ys, dest[first:first + n_g], g["mod_ffn"], wts["ln_ffn_g"],
                           wts["ln_ffn_b"], min(512, t)))
        first += n_g
    return outs


def kernel(x_prompt, x_sample, cache_sb_k, cache_sb_v, state_ret, c_prompt, c_sample, w_in, w_out, ret_norm_g, sb_norm_g, w_ada_mix, b_ada_mix, ln_mix_g, ln_mix_b, w_ada_ffn, b_ada_ffn, ln_ffn_g, ln_ffn_b, w_group, b_group, w_router, b_router, w_e_gate, w_e_up, w_e_down):
    bp = x_prompt.shape[0]
    c_all = jnp.concatenate([c_prompt, c_sample], axis=0)
    mod_mix = _ada(c_all, w_ada_mix[0], b_ada_mix[0])
    mod_ffn = _ada(c_all, w_ada_ffn[0], b_ada_ffn[0])
    pad = LANES - N_GROUPS - N_EXPERTS
    w_route = jnp.concatenate([w_group[0], w_router[0], jnp.zeros((D_MODEL, pad), F32)], axis=1)
    b_route = jnp.concatenate([b_group[0], b_router[0], jnp.zeros((pad,), F32)]).reshape(1, LANES)
    n_a = 4 * RET_WIDTH + SB_WIDTH
    wts = dict(w_in_a=w_in[0, :, :n_a].astype(BF16), w_in_kvt=w_in[0, :, n_a:].T.astype(BF16),
               w_out=w_out[0].astype(BF16),
               ret_norm_g=ret_norm_g[0], sb_norm_g=sb_norm_g[0],
               ln_mix_g=ln_mix_g[0], ln_mix_b=ln_mix_b[0], ln_ffn_g=ln_ffn_g[0], ln_ffn_b=ln_ffn_b[0],
               w_route=w_route.astype(BF16), b_route=b_route,
               w_e_gate=w_e_gate[0].astype(BF16), w_e_up=w_e_up[0].astype(BF16),
               w_e_down=w_e_down[0].astype(BF16))
    ret_zero = jnp.zeros((bp, RET_HEADS, RET_DK, RET_DV), F32)
    gp = _mixer_half(x_prompt, mod_mix[:bp], mod_ffn[:bp], 0, ret_zero, None, None,
                     jnp.zeros((1, LANES), F32), wts)
    gs = _mixer_half(x_sample, mod_mix[bp:], mod_ffn[bp:], cache_sb_k.shape[3], state_ret[0],
                     cache_sb_k[0], cache_sb_v[0], gp["counts"], wts)
    y_p, y_s = _ffn_half([gp, gs], wts)
    return (y_p, y_s, gp["sk"], gp["sv"], gp["state"], gs["sk"], gs["sv"], gs["state"])
```

```python
import functools

import numpy as np
import jax
import jax.numpy as jnp
from jax import lax
from jax.experimental import pallas as pl
from jax.experimental.pallas import tpu as pltpu

D_MODEL = 1024
RET_HEADS = 4
RET_DK = 128
RET_DV = 128
RET_WIDTH = RET_HEADS * RET_DV
SB_HEADS = 8
SB_HEAD_DIM = 64
SB_WIDTH = SB_HEADS * SB_HEAD_DIM
IN_WIDTH = 2 * RET_HEADS * RET_DK + 2 * RET_WIDTH + 3 * SB_WIDTH
ROPE_BASE = 10000.0
N_GROUPS = 4
EXPERTS_PER_GROUP = 4
N_EXPERTS = N_GROUPS * EXPERTS_PER_GROUP
D_EXPERT = 512
DEPTH = 1
DEEPNORM_ALPHA = (2.0 * DEPTH) ** 0.25
LN_EPS = 1e-5

LANES = 128
VMEM_LIMIT = 48 * 1024 * 1024

F32 = jnp.float32
BF16 = jnp.bfloat16


def _cparams(sem):
    return pltpu.CompilerParams(dimension_semantics=sem, vmem_limit_bytes=VMEM_LIMIT)


def _dot(a, b):
    return jnp.dot(a, b, preferred_element_type=F32)


def _dot_nt(a, b):
    return lax.dot_general(a, b, (((1,), (1,)), ((), ())), preferred_element_type=F32)


def _dot_tn(a, b):
    return lax.dot_general(a, b, (((0,), (0,)), ((), ())), preferred_element_type=F32)


def _split_dot(a, w_hi, w_lo):
    a_hi = a.astype(BF16)
    a_lo = (a - a_hi.astype(F32)).astype(BF16)
    return _dot(a_hi, w_hi) + (_dot(a_hi, w_lo) + _dot(a_lo, w_hi))


def _ln_plain(x):
    mu = jnp.mean(x, axis=-1, keepdims=True)
    xc = x - mu
    var = jnp.mean(xc * xc, axis=-1, keepdims=True)
    return xc * lax.rsqrt(var + LN_EPS)


def _silu(x):
    return x * (1.0 / (1.0 + jnp.exp(-x)))


def _ada_kernel(c_ref, w_ref, b_ref, o_ref):
    c = c_ref[...]
    w = w_ref[...]
    w_hi = w.astype(BF16)
    w_lo = (w - w_hi.astype(F32)).astype(BF16)
    o_ref[...] = _split_dot(_silu(c), w_hi, w_lo) + b_ref[...]


def _ada(c, w, b):
    r = c.shape[0]
    tn = 768
    return pl.pallas_call(
        _ada_kernel,
        grid=(3 * D_MODEL // tn,),
        in_specs=[pl.BlockSpec((r, D_MODEL), lambda j: (0, 0)),
                  pl.BlockSpec((D_MODEL, tn), lambda j: (0, j)),
                  pl.BlockSpec((1, tn), lambda j: (0, j))],
        out_specs=pl.BlockSpec((r, tn), lambda j: (0, j)),
        out_shape=jax.ShapeDtypeStruct((r, 3 * D_MODEL), F32),
        compiler_params=_cparams(("arbitrary",)),
        name="ada_mod",
    )(c, w, b.reshape(1, -1))


def _in_proj_kernel(x_ref, mod_ref, w_ref, wkv_ref, tab_ref, s0_ref, dec_ref, qd_ref, kd_ref, ng_ref,
                    mr_ref, q_ref, k_ref, v_ref, so_ref, st_ref, *, chunk, chunk_decay):
    j = pl.program_id(1)

    @pl.when(j == 0)
    def _():
        st_ref[...] = s0_ref[0]

    m = mod_ref[0]
    h = _ln_plain(x_ref[0]) * (1.0 + m[:, D_MODEL:2 * D_MODEL]) + m[:, :D_MODEL]
    h = h.astype(BF16)
    tm = h.shape[0]
    tab = tab_ref[...]
    grp = []
    for c in range(4):
        p = _dot(h, w_ref[:, c * RET_WIDTH:(c + 1) * RET_WIDTH])
        if c < 2:
            cs = tab[:, (2 * c) * LANES:(2 * c + 1) * LANES]
            sn = tab[:, (2 * c + 1) * LANES:(2 * c + 2) * LANES]
            heads = []
            for hh in range(RET_HEADS):
                ph = p[:, hh * RET_DK:(hh + 1) * RET_DK]
                heads.append(ph * cs + pltpu.roll(ph, RET_DK // 2, 1) * sn)
            grp.append(heads)
        else:
            grp.append([p[:, hh * RET_DV:(hh + 1) * RET_DV] for hh in range(RET_HEADS)])
    for hh in range(RET_HEADS):
        sl = slice(hh * RET_DV, (hh + 1) * RET_DV)
        for ch in range(tm // chunk):
            rows = slice(ch * chunk, (ch + 1) * chunk)
            q, k, v, g = (grp[c][hh][rows] for c in range(4))
            vb = v.astype(BF16)
            st = st_ref[hh]
            scores = _dot_nt(q.astype(BF16), k.astype(BF16)) * dec_ref[hh]
            o = _dot(scores.astype(BF16), vb) + _dot((q * qd_ref[hh]).astype(BF16), st.astype(BF16))
            st_ref[hh] = st * chunk_decay[hh] + _dot_tn((k * kd_ref[hh]).astype(BF16), vb)
            o = _ln_plain(o) * ng_ref[:, sl] * _silu(g)
            mr_ref[0, rows, sl] = o.astype(mr_ref.dtype)

    @pl.when(j == pl.num_programs(1) - 1)
    def _():
        so_ref[0] = st_ref[...]

    base = 4 * RET_WIDTH
    p = _dot(h, w_ref[:, base:base + SB_WIDTH]) * (LOG2E * SB_HEAD_DIM ** -0.5)
    for hh in range(SB_HEADS):
        q_ref[0, hh] = p[:, hh * SB_HEAD_DIM:(hh + 1) * SB_HEAD_DIM].astype(q_ref.dtype)
    pt = _dot_nt(wkv_ref[...], h)
    for c, ref in enumerate((k_ref, v_ref)):
        for hh in range(SB_HEADS):
            r0 = c * SB_WIDTH + hh * SB_HEAD_DIM
            ref[0, hh] = pt[r0:r0 + SB_HEAD_DIM, :]


def _rope_table(pos0, t):
    half = RET_DK // 2
    inv = ROPE_BASE ** (-np.arange(half, dtype=np.float64) / half)
    ang = (pos0 + np.arange(t, dtype=np.float64))[:, None] * inv[None, :]
    cos, sin = np.cos(ang), np.sin(ang)
    cs = np.concatenate([cos, cos], axis=1)
    sn = np.concatenate([-sin, sin], axis=1)
    ks = RET_DK ** -0.5
    return jnp.asarray(np.concatenate([cs, sn, cs * ks, sn * ks], axis=1), dtype=F32)


def _in_proj(x, mod, w_a_bf, w_kvt_bf, pos0, state0, norm_g, tm):
    b, t, _ = x.shape
    tab = _rope_table(pos0, t)
    wa = w_a_bf.shape[1]
    chunk = min(256, tm)
    lg = np.log1p(-np.exp2(-5.0 - np.arange(RET_HEADS, dtype=np.float64)))
    idx = np.arange(chunk, dtype=np.float64)
    rel = idx[:, None] - idx[None, :]
    dec = np.where(rel >= 0, np.exp(lg[:, None, None] * np.maximum(rel, 0.0)), 0.0)
    qd = np.broadcast_to(np.exp(lg[:, None] * (idx + 1.0))[:, :, None], (RET_HEADS, chunk, RET_DK))
    kd = np.broadcast_to(np.exp(lg[:, None] * (chunk - 1.0 - idx))[:, :, None], (RET_HEADS, chunk, RET_DK))
    chunk_decay = tuple(float(v) for v in np.exp(lg * chunk))
    hs = jax.ShapeDtypeStruct((b, SB_HEADS, SB_HEAD_DIM, t), F32)
    q_spec = pl.BlockSpec((1, SB_HEADS, tm, SB_HEAD_DIM), lambda i, j: (i, 0, j, 0))
    kv_spec = pl.BlockSpec((1, SB_HEADS, SB_HEAD_DIM, tm), lambda i, j: (i, 0, 0, j))
    const3 = lambda shape: pl.BlockSpec(shape, lambda i, j: (0, 0, 0))
    state_spec = pl.BlockSpec((1, RET_HEADS, RET_DK, RET_DV), lambda i, j: (i, 0, 0, 0))
    return pl.pallas_call(
        functools.partial(_in_proj_kernel, chunk=chunk, chunk_decay=chunk_decay),
        grid=(b, t // tm),
        in_specs=[pl.BlockSpec((1, tm, D_MODEL), lambda i, j: (i, j, 0)),
                  pl.BlockSpec((1, 1, 3 * D_MODEL), lambda i, j: (i, 0, 0)),
                  pl.BlockSpec((D_MODEL, wa), lambda i, j: (0, 0)),
                  pl.BlockSpec((2 * SB_WIDTH, D_MODEL), lambda i, j: (0, 0)),
                  pl.BlockSpec((tm, 4 * LANES), lambda i, j: (j, 0)),
                  state_spec, const3((RET_HEADS, chunk, chunk)), const3((RET_HEADS, chunk, RET_DK)),
                  const3((RET_HEADS, chunk, RET_DK)),
                  pl.BlockSpec((1, RET_WIDTH), lambda i, j: (0, 0))],
        out_specs=[pl.BlockSpec((1, tm, RET_WIDTH), lambda i, j: (i, j, 0)),
                   q_spec, kv_spec, kv_spec, state_spec],
        out_shape=[jax.ShapeDtypeStruct((b, t, RET_WIDTH), BF16),
                   jax.ShapeDtypeStruct((b, SB_HEADS, t, SB_HEAD_DIM), BF16), hs, hs,
                   jax.ShapeDtypeStruct((b, RET_HEADS, RET_DK, RET_DV), F32)],
        scratch_shapes=[pltpu.VMEM((RET_HEADS, RET_DK, RET_DV), F32)],
        compiler_params=_cparams(("parallel", "arbitrary")),
        name="in_proj",
    )(x, mod.reshape(b, 1, -1), w_a_bf, w_kvt_bf, tab, state0,
      jnp.asarray(dec, F32), jnp.asarray(qd, F32), jnp.asarray(kd, F32), norm_g.reshape(1, -1))


SB_KEY_BLOCK = 256


LOG2E = 1.4426950408889634


MASKED_LOGIT = -1e30
EXP2_CLAMP = 126.0
SB_TERMS = 1


def _sb_stage1(qs, kts, z_ref, hl_ref, masked):
    tq, kb = qs[0].shape[0], kts[0].shape[1]
    if masked:
        valid = (lax.broadcasted_iota(jnp.int32, (tq, kb), 1)
                 < lax.broadcasted_iota(jnp.int32, (tq, kb), 0))
    for c, (q, kt) in enumerate(zip(qs, kts)):
        z = _dot(q, kt.astype(BF16))
        p = jnp.maximum(z, jnp.log2(1.0 + jnp.exp2(jnp.minimum(z, EXP2_CLAMP))))
        if masked:
            p = jnp.where(valid, p, 0.0)
            z = jnp.where(valid, z, MASKED_LOGIT)
        hi = p.astype(BF16)
        z_ref[c] = z
        hl_ref[c, :, :kb] = hi
        if SB_TERMS == 2:
            hl_ref[c, :, kb:] = (p - hi.astype(F32)).astype(BF16)


def _sb_stage2(z_ref, hl_ref, vts, tri2, acc_ref, car_ref):
    n, tq, kb = z_ref.shape
    r = _dot(hl_ref[...].reshape(n * tq, SB_TERMS * kb), tri2)
    for c in range(n):
        incl = r[c * tq:(c + 1) * tq]
        car = car_ref[c]
        w = jnp.exp2(z_ref[c] + incl + car)
        acc_ref[c] += _dot_nt(w.astype(BF16), vts[c].astype(BF16))
        car_ref[c] = car + incl[:, 0:1]


def _sb_finish(o, g):
    return o * lax.rsqrt(jnp.mean(o * o, axis=-1, keepdims=True) + LN_EPS) * g


def _tri_matrix():
    idx = np.arange(SB_KEY_BLOCK)
    t = -(idx[:, None] >= idx[None, :]).astype(np.float32)
    return jnp.asarray(np.concatenate([t] * SB_TERMS, axis=0), dtype=BF16)


def _sb_prompt_kernel(q_ref, kt_ref, vt_ref, tri_ref, g_ref, o_ref, acc_ref, car_ref,
                      za_ref, ha_ref, zb_ref, hb_ref, *, tq, nh):
    i = pl.program_id(2)
    qs = [q_ref[0, hh] for hh in range(nh)]
    acc_ref[...] = jnp.zeros_like(acc_ref)
    car_ref[...] = jnp.zeros_like(car_ref)

    def s1(blk, z_ref, hl_ref, masked=False):
        start = pl.multiple_of(blk * tq, tq)
        _sb_stage1(qs, [kt_ref[0, hh, :, pl.ds(start, tq)] for hh in range(nh)], z_ref, hl_ref, masked)

    def s2(blk, z_ref, hl_ref):
        start = pl.multiple_of(blk * tq, tq)
        _sb_stage2(z_ref, hl_ref, [vt_ref[0, hh, :, pl.ds(start, tq)] for hh in range(nh)],
                   tri_ref[...], acc_ref, car_ref)

    s1(i, za_ref, ha_ref, masked=True)

    def pair(p, _):
        blk = i - 2 * p
        s1(blk - 1, zb_ref, hb_ref)
        s2(blk, za_ref, ha_ref)
        s1(blk - 2, za_ref, ha_ref)
        s2(blk - 1, zb_ref, hb_ref)
        return 0

    lax.fori_loop(0, i // 2, pair, 0)

    @pl.when(i % 2 == 0)
    def _():
        s2(0, za_ref, ha_ref)

    @pl.when(i % 2 == 1)
    def _():
        s1(0, zb_ref, hb_ref)
        s2(1, za_ref, ha_ref)
        s2(0, zb_ref, hb_ref)

    d = SB_HEAD_DIM
    outs = [_sb_finish(acc_ref[hh], g_ref[0, :, hh * d:(hh + 1) * d]) for hh in range(nh)]
    o_ref[0] = jnp.concatenate(outs, axis=-1).astype(o_ref.dtype)


def _sb_prompt(q, kt, vt, norm_g):
    b, _, t, d = q.shape
    tq = SB_KEY_BLOCK
    nh = SB_HEADS
    kv_spec = pl.BlockSpec((1, nh, d, t), lambda i, h, j: (i, h, 0, 0))
    return pl.pallas_call(
        functools.partial(_sb_prompt_kernel, tq=tq, nh=nh),
        grid=(b, SB_HEADS // nh, t // tq),
        in_specs=[pl.BlockSpec((1, nh, tq, d), lambda i, h, j: (i, h, j, 0)), kv_spec, kv_spec,
                  pl.BlockSpec((SB_TERMS * tq, tq), lambda i, h, j: (0, 0)),
                  pl.BlockSpec((1, 1, nh * d), lambda i, h, j: (h, 0, 0))],
        out_specs=pl.BlockSpec((1, tq, nh * d), lambda i, h, j: (i, j, h)),
        out_shape=jax.ShapeDtypeStruct((b, t, SB_WIDTH), BF16),
        scratch_shapes=[pltpu.VMEM((nh, tq, d), F32), pltpu.VMEM((nh, tq, 1), F32),
                        pltpu.VMEM((nh, tq, tq), F32), pltpu.VMEM((nh, tq, SB_TERMS * tq), BF16),
                        pltpu.VMEM((nh, tq, tq), F32), pltpu.VMEM((nh, tq, SB_TERMS * tq), BF16)],
        compiler_params=_cparams(("parallel", "parallel", "arbitrary")),
        name="sb_prompt",
    )(q, kt, vt, _tri_matrix(), norm_g.reshape(SB_HEADS // nh, 1, nh * d))


def _sb_sample_kernel(q_ref, kt_ref, vt_ref, ktp_ref, vtp_ref, tri_ref, g_ref, o_ref, acc_ref, car_ref,
                      zd_ref, hd_ref, za_ref, ha_ref, zb_ref, hb_ref, *, nh):
    kb = SB_KEY_BLOCK
    n_past = ktp_ref.shape[3] // kb
    t = q_ref.shape[2]
    qs = [q_ref[0, hh] for hh in range(nh)]
    acc_ref[...] = jnp.zeros_like(acc_ref)
    car_ref[...] = jnp.zeros_like(car_ref)

    def s1(blk, z_ref, hl_ref):
        start = pl.multiple_of(blk * kb, kb)
        _sb_stage1(qs, [ktp_ref[0, hh, :, pl.ds(start, kb)] for hh in range(nh)], z_ref, hl_ref, False)

    def s2(blk, z_ref, hl_ref):
        start = pl.multiple_of(blk * kb, kb)
        _sb_stage2(z_ref, hl_ref, [vtp_ref[0, hh, :, pl.ds(start, kb)] for hh in range(nh)],
                   tri_ref[...], acc_ref, car_ref)

    _sb_stage1(qs, [kt_ref[0, hh] for hh in range(nh)], zd_ref, hd_ref, True)
    s1(n_past - 1, za_ref, ha_ref)
    _sb_stage2(zd_ref, hd_ref, [vt_ref[0, hh] for hh in range(nh)],
               jnp.concatenate([tri_ref[:t, :t]] * SB_TERMS, axis=0), acc_ref, car_ref)

    def pair(p, _):
        blk = n_past - 1 - 2 * p
        s1(blk - 1, zb_ref, hb_ref)
        s2(blk, za_ref, ha_ref)
        s1(blk - 2, za_ref, ha_ref)
        s2(blk - 1, zb_ref, hb_ref)
        return 0

    lax.fori_loop(0, (n_past - 1) // 2, pair, 0)
    if (n_past - 1) % 2 == 0:
        s2(0, za_ref, ha_ref)
    else:
        s1(0, zb_ref, hb_ref)
        s2(1, za_ref, ha_ref)
        s2(0, zb_ref, hb_ref)
    d = SB_HEAD_DIM
    outs = [_sb_finish(acc_ref[hh], g_ref[0, :, hh * d:(hh + 1) * d]) for hh in range(nh)]
    o_ref[0] = jnp.concatenate(outs, axis=-1).astype(o_ref.dtype)


def _sb_sample(q, kt, vt, kt_past, vt_past, norm_g):
    b, _, t, d = q.shape
    p = kt_past.shape[3]
    nh = SB_HEADS
    kb = SB_KEY_BLOCK
    new_spec = pl.BlockSpec((1, nh, d, t), lambda i, h: (i, h, 0, 0))
    past_spec = pl.BlockSpec((1, nh, d, p), lambda i, h: (i, h, 0, 0))
    return pl.pallas_call(
        functools.partial(_sb_sample_kernel, nh=nh),
        grid=(b, SB_HEADS // nh),
        in_specs=[pl.BlockSpec((1, nh, t, d), lambda i, h: (i, h, 0, 0)), new_spec, new_spec,
                  past_spec, past_spec,
                  pl.BlockSpec((SB_TERMS * kb, kb), lambda i, h: (0, 0)),
                  pl.BlockSpec((1, 1, nh * d), lambda i, h: (h, 0, 0))],
        out_specs=pl.BlockSpec((1, t, nh * d), lambda i, h: (i, 0, h)),
        out_shape=jax.ShapeDtypeStruct((b, t, SB_WIDTH), BF16),
        scratch_shapes=[pltpu.VMEM((nh, t, d), F32), pltpu.VMEM((nh, t, 1), F32),
                        pltpu.VMEM((nh, t, t), F32), pltpu.VMEM((nh, t, SB_TERMS * t), BF16),
                        pltpu.VMEM((nh, t, kb), F32), pltpu.VMEM((nh, t, SB_TERMS * kb), BF16),
                        pltpu.VMEM((nh, t, kb), F32), pltpu.VMEM((nh, t, SB_TERMS * kb), BF16)],
        compiler_params=_cparams(("parallel", "arbitrary")),
        name="sb_sample",
    )(q, kt, vt, kt_past, vt_past, _tri_matrix(), norm_g.reshape(SB_HEADS // nh, 1, nh * d))


N_PAIRS = 6
N_BUCKETS = N_GROUPS * N_PAIRS
MOE_TM = 512
XW = D_MODEL + LANES
_PAIRS = [(a, b) for a in range(EXPERTS_PER_GROUP) for b in range(a + 1, EXPERTS_PER_GROUP)]
BUCKET_E0 = np.array([g * EXPERTS_PER_GROUP + a for g in range(N_GROUPS) for a, _ in _PAIRS], np.int32)
BUCKET_E1 = np.array([g * EXPERTS_PER_GROUP + b for g in range(N_GROUPS) for _, b in _PAIRS], np.int32)


def _route(logits):
    lane = lax.broadcasted_iota(jnp.int32, logits.shape, 1)
    neg = -jnp.inf
    big = jnp.int32(2 * LANES)
    gl = jnp.where(lane < N_GROUPS, logits, neg)
    gmax = jnp.max(gl, axis=-1, keepdims=True)
    g_idx = jnp.min(jnp.where(gl == gmax, lane, big), axis=-1, keepdims=True)
    g_p = 1.0 / jnp.sum(jnp.exp(gl - gmax), axis=-1, keepdims=True)
    lo = N_GROUPS + g_idx * EXPERTS_PER_GROUP
    el = jnp.where((lane >= lo) & (lane < lo + EXPERTS_PER_GROUP), logits, neg)
    v1 = jnp.max(el, axis=-1, keepdims=True)
    i1 = jnp.min(jnp.where(el == v1, lane, big), axis=-1, keepdims=True)
    el2 = jnp.where(lane == i1, neg, el)
    v2 = jnp.max(el2, axis=-1, keepdims=True)
    i2 = jnp.min(jnp.where(el2 == v2, lane, big), axis=-1, keepdims=True)
    e21 = jnp.exp(v2 - v1)
    p1 = 1.0 / (1.0 + e21)
    p2 = e21 * p1
    first_lo = i1 < i2
    w_lo = jnp.where(first_lo, p1, p2) * g_p
    w_hi = jnp.where(first_lo, p2, p1) * g_p
    a = jnp.minimum(i1, i2) - lo
    b = jnp.maximum(i1, i2) - lo
    pair = jnp.where(a == 0, b - 1, jnp.where(a == 1, b + 1, 5))
    return g_idx * N_PAIRS + pair, w_lo, w_hi


def _out_proj_kernel(x_ref, mr_ref, ms_ref, wo_ref, mm_ref, mf_ref, lg_ref, lb_ref, wr_ref, br_ref,
                     tril_ref, cin_ref, x1_ref, hx_ref, rt_ref, cnt_ref, run_ref):
    @pl.when((pl.program_id(0) == 0) & (pl.program_id(1) == 0))
    def _():
        run_ref[...] = cin_ref[...]

    mix = _dot(mr_ref[0], wo_ref[:MIX_HALF]) + _dot(ms_ref[0], wo_ref[MIX_HALF:])
    gate = mm_ref[0][:, 2 * D_MODEL:]
    x1 = _ln_plain(DEEPNORM_ALPHA * x_ref[0] + gate * mix) * lg_ref[...] + lb_ref[...]
    x1_ref[0] = x1
    mf = mf_ref[0]
    h2 = _ln_plain(x1) * (1.0 + mf[:, D_MODEL:2 * D_MODEL]) + mf[:, :D_MODEL]
    bucket, w_lo, w_hi = _route(_dot(h2.astype(BF16), wr_ref[...]) + br_ref[...])
    tm = h2.shape[0]
    lane = lax.broadcasted_iota(jnp.int32, (tm, LANES), 1)
    hx_ref[0, :, :D_MODEL] = h2
    hx_ref[0, :, D_MODEL:] = jnp.where(lane == 0, w_lo, jnp.where(lane == 1, w_hi, 0.0))
    hit = lane == bucket
    onehot = hit.astype(BF16)
    before = _dot(tril_ref[...], onehot)
    run = run_ref[...]
    rank = jnp.sum(jnp.where(hit, before + run, 0.0), axis=-1, keepdims=True)
    run = run + before[tm - 1:tm] + onehot[tm - 1:tm].astype(F32)
    run_ref[...] = run
    cnt_ref[...] = run
    rt_ref[0] = jnp.where(lane == 0, bucket.astype(F32), jnp.where(lane == 1, rank, 0.0))


MIX_HALF = RET_WIDTH


def _out_proj(x, mr, ms, w_out_bf, mod_mix, mod_ffn, ln_g, ln_b, w_route_bf, b_route, counts_in, tm):
    b, t, _ = x.shape
    tok = lambda w: pl.BlockSpec((1, tm, w), lambda i, j: (i, j, 0))
    modspec = pl.BlockSpec((1, 1, 3 * D_MODEL), lambda i, j: (i, 0, 0))
    row = lambda w: pl.BlockSpec((1, w), lambda i, j: (0, 0))
    idx = np.arange(tm)
    tril = jnp.asarray(idx[:, None] > idx[None, :], dtype=BF16)
    return pl.pallas_call(
        _out_proj_kernel,
        grid=(b, t // tm),
        in_specs=[tok(D_MODEL), tok(RET_WIDTH), tok(SB_WIDTH),
                  pl.BlockSpec((D_MODEL, D_MODEL), lambda i, j: (0, 0)),
                  modspec, modspec, row(D_MODEL), row(D_MODEL),
                  pl.BlockSpec((D_MODEL, LANES), lambda i, j: (0, 0)), row(LANES),
                  pl.BlockSpec((tm, tm), lambda i, j: (0, 0)), row(LANES)],
        out_specs=[tok(D_MODEL), tok(XW), tok(LANES), row(LANES)],
        out_shape=[jax.ShapeDtypeStruct((b, t, D_MODEL), F32),
                   jax.ShapeDtypeStruct((b, t, XW), F32),
                   jax.ShapeDtypeStruct((b, t, LANES), F32),
                   jax.ShapeDtypeStruct((1, LANES), F32)],
        scratch_shapes=[pltpu.VMEM((1, LANES), F32)],
        compiler_params=_cparams(("arbitrary", "arbitrary")),
        name="out_proj",
    )(x, mr, ms, w_out_bf, mod_mix.reshape(b, 1, -1), mod_ffn.reshape(b, 1, -1),
      ln_g.reshape(1, -1), ln_b.reshape(1, -1), w_route_bf, b_route, tril, counts_in)


def _route_plan(routes, counts, n_tiles):
    bucket = jnp.concatenate([r[..., 0].astype(jnp.int32).reshape(-1) for r in routes])
    rank = jnp.concatenate([r[..., 1].astype(jnp.int32).reshape(-1) for r in routes])
    cnt = counts[0, :N_BUCKETS].astype(jnp.int32)
    padded = ((cnt + MOE_TM - 1) // MOE_TM) * MOE_TM
    ends = jnp.cumsum(padded)
    starts = ends - padded
    n_used = ends[-1] // MOE_TM
    tile = jnp.arange(n_tiles, dtype=jnp.int32)
    last = jnp.maximum(n_used - 1, 0)
    tile_idx = jnp.minimum(tile, last)
    tile_bucket = jnp.sum((ends[None, :] <= (tile_idx * MOE_TM)[:, None]).astype(jnp.int32), axis=1)
    tile_bucket = jnp.minimum(tile_bucket, N_BUCKETS - 1)
    e0 = jnp.asarray(BUCKET_E0)[tile_bucket]
    e1 = jnp.asarray(BUCKET_E1)[tile_bucket]
    in_bucket = bucket[:, None] == jnp.arange(N_BUCKETS, dtype=jnp.int32)[None, :]
    dest = rank + jnp.sum(jnp.where(in_bucket, starts[None, :], 0), axis=1)
    trailing = n_used + jnp.arange(N_BUCKETS, dtype=jnp.int32)
    ztiles = jnp.concatenate([jnp.where(padded > 0, ends - MOE_TM, -1),
                              jnp.where(trailing < n_tiles, trailing * MOE_TM, -1)])
    return (dest.astype(jnp.int32), ztiles.astype(jnp.int32), e0, e1, tile_idx,
            n_used.reshape(1).astype(jnp.int32))


def _row_copy(src_ref, s, dst_ref, d, sem):
    return pltpu.make_async_copy(src_ref.at[pl.ds(s, 1)], dst_ref.at[pl.ds(d, 1)], sem)


DMA_UNROLL = 8
DMA_THREADS = 2
FINAL_GROUPS = 8


SUBLANES = 8


def _scatter_kernel(dhi_ref, dlo_ref, ztile_ref, *refs, rows, steps):
    srcs = refs[:len(steps)]
    out_ref, zbuf, sem, zsem = refs[len(steps):]
    i = pl.program_id(0)

    @pl.when(i == 0)
    def _():
        zbuf[...] = jnp.zeros_like(zbuf)

        def fill(k, _):
            @pl.when(ztile_ref[k] >= 0)
            def _():
                g0 = ztile_ref[k] // SUBLANES
                pltpu.make_async_copy(zbuf, out_ref.at[pl.ds(g0, MOE_TM // SUBLANES)], zsem).start()
            return 0

        def drain(k, _):
            @pl.when(ztile_ref[k] >= 0)
            def _():
                pltpu.make_async_copy(zbuf, out_ref.at[pl.ds(0, MOE_TM // SUBLANES)], zsem).wait()
            return 0

        lax.fori_loop(0, ztile_ref.shape[0], fill, 0)
        lax.fori_loop(0, ztile_ref.shape[0], drain, 0)

    base = i * rows
    first = 0
    for src_ref, n_steps in zip(srcs, steps):
        @pl.when((i >= first) & (i < first + n_steps))
        def _(src_ref=src_ref):
            for r in range(rows):
                tok = base + r
                pltpu.make_async_copy(src_ref.at[r // SUBLANES, pl.ds(r % SUBLANES, 1)],
                                      out_ref.at[dhi_ref[tok], pl.ds(dlo_ref[tok], 1)], sem
                                      ).start(priority=r % DMA_THREADS)
            pltpu.make_async_copy(src_ref, out_ref.at[pl.ds(0, rows // SUBLANES)], sem).wait()
        first += n_steps


def _scatter_rows(hxs, dest, ztiles, n_rows):
    w = hxs[0].shape[1]
    rows = min([1024] + [h.shape[0] for h in hxs])
    steps = tuple(h.shape[0] // rows for h in hxs)
    firsts = [sum(steps[:k]) for k in range(len(steps))]
    in_specs = [pl.BlockSpec((rows // SUBLANES, SUBLANES, w),
                             lambda i, dh, dl, z, f=f, s=s: (jnp.clip(i - f, 0, s - 1), 0, 0))
                for f, s in zip(firsts, steps)]
    out = pl.pallas_call(
        functools.partial(_scatter_kernel, rows=rows, steps=steps),
        grid_spec=pltpu.PrefetchScalarGridSpec(
            num_scalar_prefetch=3, grid=(sum(steps),),
            in_specs=in_specs,
            out_specs=pl.BlockSpec(memory_space=pl.ANY),
            scratch_shapes=[pltpu.VMEM((MOE_TM // SUBLANES, SUBLANES, w), F32),
                            pltpu.SemaphoreType.DMA(()), pltpu.SemaphoreType.DMA(())]),
        out_shape=jax.ShapeDtypeStruct((n_rows // SUBLANES, SUBLANES, w), F32),
        compiler_params=_cparams(("arbitrary",)),
        name="scatter_rows",
    )(dest // SUBLANES, dest % SUBLANES, ztiles,
      *[h.reshape(h.shape[0] // SUBLANES, SUBLANES, w) for h in hxs])
    return out.reshape(n_rows, w)


def _moe_kernel(e0_ref, e1_ref, ti_ref, nu_ref, x_ref, wg0, wu0, wd0, wg1, wu1, wd1, y_ref):
    del e0_ref, e1_ref, ti_ref

    @pl.when(pl.program_id(0) < nu_ref[0])
    def _():
        x = x_ref[...]
        h = x[:, :D_MODEL].astype(BF16)
        wx = x[:, D_MODEL:]
        lane = lax.broadcasted_iota(jnp.int32, wx.shape, 1)
        w_lo = jnp.sum(jnp.where(lane == 0, wx, 0.0), axis=-1, keepdims=True)
        w_hi = jnp.sum(jnp.where(lane == 1, wx, 0.0), axis=-1, keepdims=True)

        def expert(wg, wu, wd):
            a = _silu(_dot(h, wg[0])) * _dot(h, wu[0])
            return _dot(a.astype(BF16), wd[0])

        y_ref[...] = w_lo * expert(wg0, wu0, wd0) + w_hi * expert(wg1, wu1, wd1)

    @pl.when(pl.program_id(0) >= nu_ref[0])
    def _():
        y_ref[...] = jnp.zeros_like(y_ref)


def _moe(xs, e0, e1, tile_idx, n_used, wg_bf, wu_bf, wd_bf):
    n_rows = xs.shape[0]
    up = lambda sel: pl.BlockSpec((1, D_MODEL, D_EXPERT), lambda t, e0, e1, ti, nu: ((e0, e1)[sel][t], 0, 0))
    down = lambda sel: pl.BlockSpec((1, D_EXPERT, D_MODEL), lambda t, e0, e1, ti, nu: ((e0, e1)[sel][t], 0, 0))
    return pl.pallas_call(
        _moe_kernel,
        grid_spec=pltpu.PrefetchScalarGridSpec(
            num_scalar_prefetch=4, grid=(n_rows // MOE_TM,),
            in_specs=[pl.BlockSpec((MOE_TM, XW), lambda t, e0, e1, ti, nu: (ti[t], 0)),
                      up(0), up(0), down(0), up(1), up(1), down(1)],
            out_specs=pl.BlockSpec((MOE_TM, D_MODEL), lambda t, e0, e1, ti, nu: (t, 0))),
        out_shape=jax.ShapeDtypeStruct((n_rows, D_MODEL), F32),
        compiler_params=_cparams(("arbitrary",)),
        name="moe_routed",
    )(e0, e1, tile_idx, n_used, xs, wg_bf, wu_bf, wd_bf, wg_bf, wu_bf, wd_bf)


def _final_kernel(dest_ref, x1_ref, ys_ref, mf_ref, lg_ref, lb_ref, o_ref, buf, sem, *, tm):
    nj = pl.num_programs(1)
    step = pl.program_id(0) * nj + pl.program_id(1)
    n_steps = pl.num_programs(0) * nj
    slot = step % 2

    def slot_wait(sl):
        pltpu.make_async_copy(ys_ref.at[pl.ds(0, tm)], buf.at[sl], sem.at[sl]).wait()

    @pl.when(step == 0)
    def _():
        def issue(r, _):
            _row_copy(ys_ref, dest_ref[r], buf.at[slot], r, sem.at[slot]).start()
            return 0
        lax.fori_loop(0, tm, issue, 0, unroll=DMA_UNROLL)

    slot_wait(slot)
    nxt = jnp.minimum(step + 1, n_steps - 1) * tm
    gate = mf_ref[0][:, 2 * D_MODEL:]
    rg = tm // FINAL_GROUPS
    for c in range(FINAL_GROUPS):
        for r in range(c * rg, (c + 1) * rg):
            _row_copy(ys_ref, dest_ref[nxt + r], buf.at[1 - slot], r, sem.at[1 - slot]
                      ).start(priority=r % DMA_THREADS)
        rows = pl.ds(c * rg, rg)
        y = buf[slot, rows]
        o_ref[0, rows] = (_ln_plain(DEEPNORM_ALPHA * x1_ref[0, rows] + gate * y) * lg_ref[...]
                          + lb_ref[...])

    @pl.when(step == n_steps - 1)
    def _():
        slot_wait(1 - slot)


def _final(x1, ys, dest, mod_ffn, ln_g, ln_b, tm):
    b, t, _ = x1.shape
    tok = pl.BlockSpec((1, tm, D_MODEL), lambda i, j, *_: (i, j, 0))
    row = pl.BlockSpec((1, D_MODEL), lambda i, j, *_: (0, 0))
    return pl.pallas_call(
        functools.partial(_final_kernel, tm=tm),
        grid_spec=pltpu.PrefetchScalarGridSpec(
            num_scalar_prefetch=1, grid=(b, t // tm),
            in_specs=[tok, pl.BlockSpec(memory_space=pl.ANY),
                      pl.BlockSpec((1, 1, 3 * D_MODEL), lambda i, j, *_: (i, 0, 0)), row, row],
            out_specs=tok,
            scratch_shapes=[pltpu.VMEM((2, tm, D_MODEL), F32), pltpu.SemaphoreType.DMA((2,))]),
        out_shape=jax.ShapeDtypeStruct((b, t, D_MODEL), F32),
        compiler_params=_cparams(("arbitrary", "arbitrary")),
        name="final_ln",
    )(dest, x1, ys, mod_ffn.reshape(b, 1, -1), ln_g.reshape(1, -1), ln_b.reshape(1, -1))


def _mixer_half(x, mod_mix, mod_ffn, pos0, state0, k_past, v_past, counts_in, wts):
    b, t, _ = x.shape
    mr, sq, skt, svt, state = _in_proj(x, mod_mix, wts["w_in_a"], wts["w_in_kvt"], pos0, state0,
                                       wts["ret_norm_g"], min(512, t))
    if k_past is None:
        ms = _sb_prompt(sq, skt, svt, wts["sb_norm_g"])
    else:
        ms = _sb_sample(sq, skt, svt, jnp.swapaxes(k_past, 2, 3), jnp.swapaxes(v_past, 2, 3),
                        wts["sb_norm_g"])
    sk, sv = jnp.swapaxes(skt, 2, 3), jnp.swapaxes(svt, 2, 3)
    x1, hx, route, counts = _out_proj(x, mr, ms, wts["w_out"], mod_mix, mod_ffn, wts["ln_mix_g"],
                                      wts["ln_mix_b"], wts["w_route"], wts["b_route"], counts_in,
                                      min(1024, t))
    return dict(x1=x1, hx=hx.reshape(b * t, XW), route=route, counts=counts, mod_ffn=mod_ffn,
                sk=sk[None], sv=sv[None], state=state[None])


def _ffn_half(groups, wts):
    n = sum(g["hx"].shape[0] for g in groups)
    n_tiles = n // MOE_TM + N_BUCKETS
    dest, ztiles, e0, e1, tile_idx, n_used = _route_plan([g["route"] for g in groups],
                                                         groups[-1]["counts"], n_tiles)
    xs = _scatter_rows([g["hx"] for g in groups], dest, ztiles, n_tiles * MOE_TM)
    ys = _moe(xs, e0, e1, tile_idx, n_used, wts["w_e_gate"], wts["w_e_up"], wts["w_e_down"])
    outs, first = [], 0
    for g in groups:
        t = g["x1"].shape[1]
        n_g = g["hx"].shape[0]
        outs.append(_final(g["x1"], ys, dest[first:first + n_g], g["mod_ffn"], wts["ln_ffn_g"],
                           wts["ln_ffn_b"], min(1024, t)))
        first += n_g
    return outs


def kernel(x_prompt, x_sample, cache_sb_k, cache_sb_v, state_ret, c_prompt, c_sample, w_in, w_out, ret_norm_g, sb_norm_g, w_ada_mix, b_ada_mix, ln_mix_g, ln_mix_b, w_ada_ffn, b_ada_ffn, ln_ffn_g, ln_ffn_b, w_group, b_group, w_router, b_router, w_e_gate, w_e_up, w_e_down):
    bp = x_prompt.shape[0]
    c_all = jnp.concatenate([c_prompt, c_sample], axis=0)
    mod_mix = _ada(c_all, w_ada_mix[0], b_ada_mix[0])
    mod_ffn = _ada(c_all, w_ada_ffn[0], b_ada_ffn[0])
    pad = LANES - N_GROUPS - N_EXPERTS
    w_route = jnp.concatenate([w_group[0], w_router[0], jnp.zeros((D_MODEL, pad), F32)], axis=1)
    b_route = jnp.concatenate([b_group[0], b_router[0], jnp.zeros((pad,), F32)]).reshape(1, LANES)
    n_a = 4 * RET_WIDTH + SB_WIDTH
    wts = dict(w_in_a=w_in[0, :, :n_a].astype(BF16), w_in_kvt=w_in[0, :, n_a:].T.astype(BF16),
               w_out=w_out[0].astype(BF16),
               ret_norm_g=ret_norm_g[0], sb_norm_g=sb_norm_g[0],
               ln_mix_g=ln_mix_g[0], ln_mix_b=ln_mix_b[0], ln_ffn_g=ln_ffn_g[0], ln_ffn_b=ln_ffn_b[0],
               w_route=w_route.astype(BF16), b_route=b_route,
               w_e_gate=w_e_gate[0].astype(BF16), w_e_up=w_e_up[0].astype(BF16),
               w_e_down=w_e_down[0].astype(BF16))
    ret_zero = jnp.zeros((bp, RET_HEADS, RET_DK, RET_DV), F32)
    gp = _mixer_half(x_prompt, mod_mix[:bp], mod_ffn[:bp], 0, ret_zero, None, None,
                     jnp.zeros((1, LANES), F32), wts)
    gs = _mixer_half(x_sample, mod_mix[bp:], mod_ffn[bp:], cache_sb_k.shape[3], state_ret[0],
                     cache_sb_k[0], cache_sb_v[0], gp["counts"], wts)
    y_p, y_s = _ffn_half([gp, gs], wts)
    return (y_p, y_s, gp["sk"], gp["sv"], gp["state"], gs["sk"], gs["sv"], gs["state"])
```
